```python
import math
import jax, jax.numpy as jnp
from jax import lax
import numpy as np

D_MODEL = 1024
BATCH = 8
SEQ = 2048
DEPTH = 4

SSM_WIDTH = D_MODEL // 4
MLSTM_WIDTH = D_MODEL // 4
NSA_WIDTH = D_MODEL // 2
D_MIX = SSM_WIDTH + MLSTM_WIDTH + NSA_WIDTH
SSM_GROUP = 16
SSM_GROUPS = SSM_WIDTH // SSM_GROUP
SSM_STATE = 64
DT_MIN = 1e-3
DT_MAX = 1e-1
MLSTM_HEADS = 4
MLSTM_HEAD_DIM = MLSTM_WIDTH // MLSTM_HEADS
MLSTM_CHUNK = 128
MLSTM_CONV = 4
NSA_HEAD_DIM = 64
NSA_HEADS = NSA_WIDTH // NSA_HEAD_DIM
NSA_KV_GROUPS = 2
NSA_KV_WIDTH = NSA_KV_GROUPS * NSA_HEAD_DIM
CMP_BLOCK = 32
CMP_STRIDE = 16
CMP_HIDDEN = 256
SEL_BLOCK = 64
SEL_TOPN = 8
WINDOW = 256
Q_BLOCK = 128
FORCE_SCORE = 1e4
NEG_INF = -1e30
ROPE_THETA = 500000.0
ROPE_DIMS = NSA_HEAD_DIM // 4
D_FF = 4 * D_MODEL
EPS = 1e-6

IN_SPLITS = (SSM_WIDTH,
             MLSTM_WIDTH, MLSTM_WIDTH, MLSTM_WIDTH, MLSTM_WIDTH, MLSTM_HEADS, MLSTM_HEADS,
             NSA_WIDTH, NSA_KV_WIDTH, NSA_KV_WIDTH, NSA_KV_WIDTH, NSA_KV_WIDTH, NSA_KV_WIDTH, NSA_KV_WIDTH,
             3 * NSA_HEADS)
D_IN = SSM_WIDTH + 4 * MLSTM_WIDTH + 2 * MLSTM_HEADS + NSA_WIDTH + 6 * NSA_KV_WIDTH + 3 * NSA_HEADS

kernel_name = 'hybrid_s5_mlstm_nsa_sandwich'

F32 = jnp.float32


def _rmsnorm(x, g):
    xf = x.astype(F32)
    y = xf * lax.rsqrt(jnp.mean(xf * xf, axis=-1, keepdims=True) + EPS)
    return (y * g).astype(x.dtype)


def _group_rmsnorm(y, g, n_groups):
    shp = y.shape
    yf = y.astype(F32).reshape(*shp[:-1], n_groups, shp[-1] // n_groups)
    yf = yf * lax.rsqrt(jnp.mean(yf * yf, axis=-1, keepdims=True) + EPS)
    return yf.reshape(shp) * g


def _rope_tables(positions):
    inv = ROPE_THETA ** (-jnp.arange(0, ROPE_DIMS, 2, dtype=F32) / ROPE_DIMS)
    ang = positions.astype(F32)[..., None] * inv
    return jnp.cos(ang)[:, :, None, :], jnp.sin(ang)[:, :, None, :]


def _rope(x, cos, sin):
    half = ROPE_DIMS // 2
    x1 = x[..., :half]
    x2 = x[..., half:ROPE_DIMS]
    rot = jnp.concatenate([x1 * cos - x2 * sin, x2 * cos + x1 * sin], axis=-1)
    return jnp.concatenate([rot.astype(x.dtype), x[..., ROPE_DIMS:]], axis=-1)


def _masked_softmax(s, mask):
    s = jnp.where(mask, s.astype(F32), NEG_INF)
    p = jax.nn.softmax(s, axis=-1)
    return jnp.where(mask, p, 0.0)


def _causal_dwconv(x, w):
    K = w.shape[0]
    S = x.shape[1]
    xp = jnp.pad(x, ((0, 0), (K - 1, 0), (0, 0)))
    out = xp[:, 0:S] * w[0]
    for j in range(1, K):
        out = out + xp[:, j:j + S] * w[j]
    return out


def _s5_mixer(u, lam_re, lam_im, b_re, b_im, c_re, c_im, d_skip, log_dt, w_glu):
    Bsz, S, _ = u.shape
    uf = u.astype(F32).reshape(Bsz, S, SSM_GROUPS, SSM_GROUP)
    dt = jnp.exp(log_dt.astype(F32))[:, None]
    lr = lam_re.astype(F32)
    li = lam_im.astype(F32)
    mag = jnp.exp(lr * dt)
    ang = li * dt
    ab_re = mag * jnp.cos(ang)
    ab_im = mag * jnp.sin(ang)
    den = lr * lr + li * li
    g_re = ((ab_re - 1.0) * lr + ab_im * li) / den
    g_im = (ab_im * lr - (ab_re - 1.0) * li) / den
    br = b_re.astype(F32)
    bi = b_im.astype(F32)
    bb_re = g_re[..., None] * br - g_im[..., None] * bi
    bb_im = g_re[..., None] * bi + g_im[..., None] * br
    bu_re = jnp.einsum('bsgh,gph->bsgp', uf, bb_re)
    bu_im = jnp.einsum('bsgh,gph->bsgp', uf, bb_im)
    a_re = jnp.broadcast_to(ab_re, bu_re.shape)
    a_im = jnp.broadcast_to(ab_im, bu_im.shape)

    def combine(e1, e2):
        a1r, a1i, b1r, b1i = e1
        a2r, a2i, b2r, b2i = e2
        return (a2r * a1r - a2i * a1i,
                a2r * a1i + a2i * a1r,
                a2r * b1r - a2i * b1i + b2r,
                a2r * b1i + a2i * b1r + b2i)

    _, _, xr, xi = lax.associative_scan(combine, (a_re, a_im, bu_re, bu_im), axis=1)
    y = (jnp.einsum('bsgp,ghp->bsgh', xr, c_re.astype(F32))
         - jnp.einsum('bsgp,ghp->bsgh', xi, c_im.astype(F32))
         + d_skip.astype(F32).reshape(SSM_GROUPS, SSM_GROUP) * uf)
    y = jax.nn.gelu(y.reshape(Bsz, S, SSM_WIDTH))
    return y * jax.nn.sigmoid(y @ w_glu.astype(F32))


def _mlstm_mixer(q, k, v, o_pre, i_pre, f_pre, conv_w, b_i, b_f):
    Bsz, S, _ = q.shape
    H, Dh, L = MLSTM_HEADS, MLSTM_HEAD_DIM, MLSTM_CHUNK
    nc = S // L
    qk = jax.nn.silu(_causal_dwconv(jnp.concatenate([q, k], axis=-1), conv_w))
    q, k = jnp.split(qk, 2, axis=-1)

    def heads(t):
        return t.astype(F32).reshape(Bsz, nc, L, H, Dh).transpose(1, 0, 3, 2, 4)

    def gates(t):
        return t.reshape(Bsz, nc, L, H).transpose(1, 0, 3, 2)

    qh = heads(q)
    kh = heads(k) * (Dh ** -0.5)
    vh = heads(v)
    ig = gates(i_pre.astype(F32) + b_i.astype(F32))
    lf = gates(jax.nn.log_sigmoid(f_pre.astype(F32) + b_f.astype(F32)))
    causal = jnp.tril(jnp.ones((L, L), dtype=bool))

    def step(carry, xs):
        C, n, m = carry
        qc, kc, vc, ic, fc = xs
        b = jnp.cumsum(fc, axis=-1)
        dmat = b[..., :, None] - b[..., None, :] + ic[..., None, :]
        dmat = jnp.where(causal, dmat, -jnp.inf)
        inter = b + m[..., None]
        m_t = jnp.maximum(inter, jnp.max(dmat, axis=-1))
        w_intra = jnp.exp(dmat - m_t[..., None])
        w_inter = jnp.exp(inter - m_t)
        s = jnp.einsum('bhtd,bhsd->bhts', qc, kc) * w_intra
        num = (w_inter[..., None] * jnp.einsum('bhtd,bhde->bhte', qc, C)
               + jnp.einsum('bhts,bhse->bhte', s, vc))
        den = w_inter * jnp.einsum('bhtd,bhd->bht', qc, n) + jnp.sum(s, axis=-1)
        h = num / jnp.maximum(jnp.abs(den), jnp.exp(-m_t))[..., None]
        bL = b[..., -1]
        logw = bL[..., None] - b + ic
        m_new = jnp.maximum(bL + m, jnp.max(logw, axis=-1))
        wk = jnp.exp(logw - m_new[..., None])
        decay = jnp.exp(bL + m - m_new)
        C_new = decay[..., None, None] * C + jnp.einsum('bhs,bhsd,bhse->bhde', wk, kc, vc)
        n_new = decay[..., None] * n + jnp.einsum('bhs,bhsd->bhd', wk, kc)
        return (C_new, n_new, m_new), h

    init = (jnp.zeros((Bsz, H, Dh, Dh), F32), jnp.zeros((Bsz, H, Dh), F32), jnp.zeros((Bsz, H), F32))
    _, hs = lax.scan(step, init, (qh, kh, vh, ig, lf))
    h = hs.transpose(1, 0, 3, 2, 4).reshape(Bsz, S, MLSTM_WIDTH)
    return jax.nn.sigmoid(o_pre.astype(F32)) * h


def _compress(t, pe, w1, w2):
    Bsz, S, G, Dh = t.shape
    r = CMP_BLOCK // CMP_STRIDE
    ch = t.reshape(Bsz, S // CMP_STRIDE, CMP_STRIDE, G, Dh)
    n = S // CMP_STRIDE - (r - 1)
    blocks = jnp.concatenate([ch[:, j:j + n] for j in range(r)], axis=2)
    blocks = blocks + pe[None, None, :, None, :]
    flat = blocks.transpose(0, 1, 3, 2, 4).reshape(Bsz, n, G, CMP_BLOCK * Dh)
    return jax.nn.gelu(flat @ w1) @ w2


def _cmp_to_sel_overlap(n_cmp, n_sel):
    i = np.arange(n_cmp)[:, None]
    j = np.arange(n_sel)[None, :]
    lo = np.maximum(i * CMP_STRIDE, j * SEL_BLOCK)
    hi = np.minimum(i * CMP_STRIDE + CMP_BLOCK, (j + 1) * SEL_BLOCK)
    return jnp.asarray(np.maximum(hi - lo, 0) / CMP_STRIDE, dtype=F32)


def _nsa_mixer(q, k_c, v_c, k_s, v_s, k_w, v_w, g_pre, cos, sin,
               pe_k, w1_k, w2_k, pe_v, w1_v, w2_v):
    Bsz, S, _ = q.shape
    H, G, Dh = NSA_HEADS, NSA_KV_GROUPS, NSA_HEAD_DIM
    R = H // G
    scale = Dh ** -0.5
    t_idx = jnp.arange(S)
    qh = _rope(q.reshape(Bsz, S, H, Dh), cos, sin)
    qg = qh.reshape(Bsz, S, G, R, Dh)

    def kv(t):
        return t.reshape(Bsz, S, G, Dh)

    k_c = _rope(kv(k_c), cos, sin)
    k_s = _rope(kv(k_s), cos, sin)
    k_w = _rope(kv(k_w), cos, sin)
    v_c, v_s, v_w = kv(v_c), kv(v_s), kv(v_w)

    kcmp = _compress(k_c, pe_k, w1_k, w2_k)
    vcmp = _compress(v_c, pe_v, w1_v, w2_v)
    n_cmp = kcmp.shape[1]
    cmp_end = jnp.arange(n_cmp) * CMP_STRIDE + CMP_BLOCK - 1
    cmask = cmp_end[None, :] <= t_idx[:, None]
    sc = jnp.einsum('bsgrd,bngd->bgrsn', qg, kcmp) * scale
    p_c = _masked_softmax(sc, cmask)
    o_c = jnp.einsum('bgrsn,bngd->bsgrd', p_c, vcmp).reshape(Bsz, S, H, Dh)

    n_sel = S // SEL_BLOCK
    n_top = min(SEL_TOPN, n_sel)
    imp = jnp.einsum('bgrsn,nj->bgsj', p_c, _cmp_to_sel_overlap(n_cmp, n_sel))
    blk = jnp.arange(n_sel)
    valid = blk[None, :] * SEL_BLOCK <= t_idx[:, None]
    forced = (blk[None, :] == 0) | (blk[None, :] == (t_idx // SEL_BLOCK)[:, None])
    imp = jnp.where(forced, FORCE_SCORE, jnp.where(valid, imp, -FORCE_SCORE))
    _, sel_idx = lax.top_k(imp, n_top)

    nq = S // Q_BLOCK
    ks_blocks = k_s.reshape(Bsz, n_sel, SEL_BLOCK, G, Dh).transpose(0, 3, 1, 2, 4)
    vs_blocks = v_s.reshape(Bsz, n_sel, SEL_BLOCK, G, Dh).transpose(0, 3, 1, 2, 4)
    q_blk = qg.reshape(Bsz, nq, Q_BLOCK, G, R, Dh).transpose(1, 0, 3, 2, 4, 5)
    idx_blk = sel_idx.reshape(Bsz, G, nq, Q_BLOCK, n_top).transpose(2, 0, 1, 3, 4)
    t_blk = t_idx.reshape(nq, Q_BLOCK)
    b_ix = jnp.arange(Bsz)[:, None, None]
    g_ix = jnp.arange(G)[None, :, None]
    kk = n_top * SEL_BLOCK

    def sel_block(args):
        qb, ib, tb = args
        flat = ib.reshape(Bsz, G, Q_BLOCK * n_top)
        kg = ks_blocks[b_ix, g_ix, flat].reshape(Bsz, G, Q_BLOCK, kk, Dh)
        vg = vs_blocks[b_ix, g_ix, flat].reshape(Bsz, G, Q_BLOCK, kk, Dh)
        kpos = (ib[..., None] * SEL_BLOCK + jnp.arange(SEL_BLOCK)).reshape(Bsz, G, Q_BLOCK, kk)
        mask = kpos <= tb[None, None, :, None]
        s = jnp.einsum('bgtrd,bgtkd->bgrtk', qb, kg) * scale
        p = _masked_softmax(s, mask[:, :, None])
        return jnp.einsum('bgrtk,bgtkd->btgrd', p, vg)

    o_s = lax.map(sel_block, (q_blk, idx_blk, t_blk))
    o_s = o_s.transpose(1, 0, 2, 3, 4, 5).reshape(Bsz, S, H, Dh)

    nb = S // Q_BLOCK
    nprev = WINDOW // Q_BLOCK

    def band(t):
        tb = t.reshape(Bsz, nb, Q_BLOCK, G, Dh)
        tp = jnp.pad(tb, ((0, 0), (nprev, 0), (0, 0), (0, 0), (0, 0)))
        return jnp.concatenate([tp[:, j:j + nb] for j in range(nprev + 1)], axis=2)

    kwb = band(k_w)
    vwb = band(v_w)
    qpos = t_idx.reshape(nb, Q_BLOCK)
    kpos = (jnp.arange(nb)[:, None] - nprev) * Q_BLOCK + jnp.arange((nprev + 1) * Q_BLOCK)[None, :]
    wmask = ((kpos[:, None, :] >= 0) & (kpos[:, None, :] <= qpos[:, :, None])
             & (qpos[:, :, None] - kpos[:, None, :] < WINDOW))
    qw = qg.reshape(Bsz, nb, Q_BLOCK, G, R, Dh)
    sw = jnp.einsum('bitgrd,bikgd->bigrtk', qw, kwb) * scale
    p_w = _masked_softmax(sw, wmask[None, :, None, None])
    o_w = jnp.einsum('bigrtk,bikgd->bitgrd', p_w, vwb).reshape(Bsz, S, H, Dh)

    g = jax.nn.sigmoid(g_pre.astype(F32)).reshape(Bsz, S, H, 3)
    o = g[..., 0:1] * o_c + g[..., 1:2] * o_s + g[..., 2:3] * o_w
    return o.reshape(Bsz, S, NSA_WIDTH)


def _split_points():
    return [int(v) for v in np.cumsum(IN_SPLITS)[:-1]]


def setup_inputs(seed: int = 0) -> dict:
    key = jax.random.key(seed)
    ks = jax.random.split(key, 40)
    L, D, G, P, Hc = DEPTH, D_MODEL, SSM_GROUPS, SSM_STATE, SSM_GROUP

    def nrm(k, shape, scale):
        return scale * jax.random.normal(k, shape, F32)

    def gain(k, shape):
        return 1.0 + 0.05 * jax.random.normal(k, shape, F32)

    x = jax.random.normal(ks[0], (BATCH, SEQ, D), F32)
    offs = jax.random.randint(ks[1], (BATCH, 1), 0, SEQ, dtype=jnp.int32)
    positions = offs + jnp.arange(SEQ, dtype=jnp.int32)[None, :]
    return {
        'x': x,
        'positions': positions,
        'ln_mix_pre': gain(ks[2], (L, D)),
        'ln_mix_post': gain(ks[3], (L, D)),
        'ln_mlp_pre': gain(ks[4], (L, D)),
        'ln_mlp_post': gain(ks[5], (L, D)),
        'w_in': nrm(ks[6], (L, D, D_IN), D ** -0.5),
        'w_out': nrm(ks[7], (L, D_MIX, D), D_MIX ** -0.5),
        'ssm_lambda_re': -0.5 + 0.01 * jax.random.normal(ks[8], (L, G, P), F32),
        'ssm_lambda_im': math.pi * jnp.arange(P, dtype=F32) + 0.01 * jax.random.normal(ks[9], (L, G, P), F32),
        'ssm_b_re': nrm(ks[10], (L, G, P, Hc), (2 * Hc) ** -0.5),
        'ssm_b_im': nrm(ks[11], (L, G, P, Hc), (2 * Hc) ** -0.5),
        'ssm_c_re': nrm(ks[12], (L, G, Hc, P), P ** -0.5),
        'ssm_c_im': nrm(ks[13], (L, G, Hc, P), P ** -0.5),
        'ssm_d': nrm(ks[14], (L, SSM_WIDTH), 1.0),
        'ssm_log_dt': jax.random.uniform(ks[15], (L, G), F32, math.log(DT_MIN), math.log(DT_MAX)),
        'ssm_w_glu': nrm(ks[16], (L, SSM_WIDTH, SSM_WIDTH), SSM_WIDTH ** -0.5),
        'mlstm_conv': nrm(ks[17], (L, MLSTM_CONV, 2 * MLSTM_WIDTH), MLSTM_CONV ** -0.5),
        'mlstm_b_i': nrm(ks[18], (L, MLSTM_HEADS), 0.1),
        'mlstm_b_f': jax.random.uniform(ks[19], (L, MLSTM_HEADS), F32, 3.0, 6.0),
        'cmp_pe_k': nrm(ks[20], (L, CMP_BLOCK, NSA_HEAD_DIM), 0.1),
        'cmp_w1_k': nrm(ks[21], (L, CMP_BLOCK * NSA_HEAD_DIM, CMP_HIDDEN), (CMP_BLOCK * NSA_HEAD_DIM) ** -0.5),
        'cmp_w2_k': nrm(ks[22], (L, CMP_HIDDEN, NSA_HEAD_DIM), CMP_HIDDEN ** -0.5),
        'cmp_pe_v': nrm(ks[23], (L, CMP_BLOCK, NSA_HEAD_DIM), 0.1),
        'cmp_w1_v': nrm(ks[24], (L, CMP_BLOCK * NSA_HEAD_DIM, CMP_HIDDEN), (CMP_BLOCK * NSA_HEAD_DIM) ** -0.5),
        'cmp_w2_v': nrm(ks[25], (L, CMP_HIDDEN, NSA_HEAD_DIM), CMP_HIDDEN ** -0.5),
        'gn_ssm': gain(ks[26], (L, SSM_WIDTH)),
        'gn_mlstm': gain(ks[27], (L, MLSTM_WIDTH)),
        'gn_nsa': gain(ks[28], (L, NSA_WIDTH)),
        'mlp_w1': nrm(ks[29], (L, D, D_FF), D ** -0.5),
        'mlp_w2': nrm(ks[30], (L, D_FF, D), D_FF ** -0.5),
    }


def reference(x, positions, ln_mix_pre, ln_mix_post, ln_mlp_pre, ln_mlp_post, w_in, w_out,
              ssm_lambda_re, ssm_lambda_im, ssm_b_re, ssm_b_im, ssm_c_re, ssm_c_im, ssm_d,
              ssm_log_dt, ssm_w_glu, mlstm_conv, mlstm_b_i, mlstm_b_f,
              cmp_pe_k, cmp_w1_k, cmp_w2_k, cmp_pe_v, cmp_w1_v, cmp_w2_v,
              gn_ssm, gn_mlstm, gn_nsa, mlp_w1, mlp_w2):
    cos, sin = _rope_tables(positions)
    h = x
    for l in range(DEPTH):
        u = _rmsnorm(h, ln_mix_pre[l])
        z = u @ w_in[l]
        (s_u, m_q, m_k, m_v, m_o, m_i, m_f,
         a_q, c_k, c_v, s_k, s_v, w_k, w_v, a_g) = jnp.split(z, _split_points(), axis=-1)
        y_ssm = _s5_mixer(s_u, ssm_lambda_re[l], ssm_lambda_im[l], ssm_b_re[l], ssm_b_im[l],
                          ssm_c_re[l], ssm_c_im[l], ssm_d[l], ssm_log_dt[l], ssm_w_glu[l])
        y_mls = _mlstm_mixer(m_q, m_k, m_v, m_o, m_i, m_f, mlstm_conv[l], mlstm_b_i[l], mlstm_b_f[l])
        y_nsa = _nsa_mixer(a_q, c_k, c_v, s_k, s_v, w_k, w_v, a_g, cos, sin,
                           cmp_pe_k[l], cmp_w1_k[l], cmp_w2_k[l], cmp_pe_v[l], cmp_w1_v[l], cmp_w2_v[l])
        y = jnp.concatenate([_group_rmsnorm(y_ssm, gn_ssm[l], SSM_GROUPS),
                             _group_rmsnorm(y_mls, gn_mlstm[l], MLSTM_HEADS),
                             _group_rmsnorm(y_nsa, gn_nsa[l], NSA_HEADS)], axis=-1).astype(h.dtype)
        h = h + _rmsnorm(y @ w_out[l], ln_mix_post[l])
        u = _rmsnorm(h, ln_mlp_pre[l])
        f = jnp.square(jax.nn.relu(u @ mlp_w1[l])) @ mlp_w2[l]
        h = h + _rmsnorm(f, ln_mlp_post[l])
    return h
```

```python
import functools
import math

import numpy as np
import jax
import jax.numpy as jnp
from jax import lax
from jax.experimental import pallas as pl
from jax.experimental.pallas import tpu as pltpu

F32 = jnp.float32
BF16 = jnp.bfloat16
HIGHEST = lax.Precision.HIGHEST

D_MODEL = 1024
DEPTH = 4
SSM_WIDTH = 256
SSM_GROUP = 16
SSM_GROUPS = 16
SSM_STATE = 64
SSM_LANES = SSM_GROUPS * SSM_STATE
MLSTM_WIDTH = 256
MLSTM_HEADS = 4
MLSTM_HEAD_DIM = 64
MLSTM_CHUNK = 128
MLSTM_CONV = 4
NSA_WIDTH = 512
NSA_HEAD_DIM = 64
NSA_HEADS = 8
NSA_KV_GROUPS = 2
NSA_REP = NSA_HEADS // NSA_KV_GROUPS
NSA_KV_WIDTH = 128
CMP_BLOCK = 32
CMP_STRIDE = 16
CMP_HIDDEN = 256
SEL_BLOCK = 64
SEL_TOPN = 8
WINDOW = 256
Q_BLOCK = 128
FORCE_SCORE = 1e4
NEG_INF = -1e30
ROPE_THETA = 500000.0
ROPE_DIMS = 16
ROPE_HALF = 8
D_FF = 4096
EPS = 1e-6
D_IN = 2592

LANE = 128
SUBLANE = 8
VMEM_LIMIT = 56 * 1024 * 1024

C_SU, C_MQ, C_MK, C_MV, C_MO = 0, 256, 512, 768, 1024
C_AQ, C_CK, C_SK, C_WK = 1280, 1792, 1920, 2048
C_CV, C_SV, C_WV = 2176, 2304, 2432
C_G0, C_G1 = 2560, 2688
D_INP = 2816
GATE_COL = 8


def _dot(a, b, precision=None):
    return jnp.dot(a, b, preferred_element_type=F32, precision=precision)


def _dot_nt(a, b):
    return lax.dot_general(a, b, (((1,), (1,)), ((), ())), preferred_element_type=F32)


def _dot_tn(a, b):
    return lax.dot_general(a, b, (((0,), (0,)), ((), ())), preferred_element_type=F32)


def _sigmoid(x):
    return 1.0 / (1.0 + jnp.exp(-x))


def _gelu_tanh(x):
    return 0.5 * x * (1.0 + jnp.tanh(math.sqrt(2.0 / math.pi) * (x + 0.044715 * (x * x * x))))


def _log_sigmoid(x):
    return jnp.minimum(x, 0.0) - jnp.log(1.0 + jnp.exp(-jnp.abs(x)))


def _inproj_kernel(x_ref, g_ref, w_ref, rc_ref, rs1_ref, rs2_ref,
                   su_ref, mq_ref, mk_ref, mv_ref, mo_ref, aq_ref, ck_ref, sk_ref, wk_ref,
                   cv_ref, sv_ref, wv_ref, gt_ref):
    x = x_ref[...]
    ms = jnp.mean(x * x, axis=-1, keepdims=True)
    u = (x * lax.rsqrt(ms + EPS) * g_ref[...]).astype(BF16)
    rc, rs1, rs2 = rc_ref[...], rs1_ref[...], rs2_ref[...]

    def mm(c0, width):
        return _dot(u, w_ref[:, c0:c0 + width])

    def rope(z):
        return z * rc + pltpu.roll(z, LANE - ROPE_HALF, 1) * rs1 + pltpu.roll(z, ROPE_HALF, 1) * rs2

    su_ref[...] = mm(C_SU, 256)
    mq_ref[...] = mm(C_MQ, 256)
    mk_ref[...] = mm(C_MK, 256)
    mv_ref[...] = mm(C_MV, 256).astype(BF16)
    mo_ref[...] = mm(C_MO, 256)
    for j in range(NSA_HEADS // 2):
        z = rope(mm(C_AQ + LANE * j, LANE)) * (NSA_HEAD_DIM ** -0.5)
        aq_ref[2 * j] = z[:, :NSA_HEAD_DIM].astype(BF16)
        aq_ref[2 * j + 1] = z[:, NSA_HEAD_DIM:].astype(BF16)
    ck_ref[...] = rope(mm(C_CK, LANE)).astype(BF16)
    cv_ref[...] = mm(C_CV, LANE).astype(BF16)
    for ref, c0, rot in ((sk_ref, C_SK, True), (wk_ref, C_WK, True), (sv_ref, C_SV, False), (wv_ref, C_WV, False)):
        z = mm(c0, LANE)
        if rot:
            z = rope(z)
        ref[0] = z[:, :NSA_HEAD_DIM].astype(BF16)
        ref[1] = z[:, NSA_HEAD_DIM:].astype(BF16)
    gt_ref[0] = mm(C_G0, LANE)
    gt_ref[1] = mm(C_G1, LANE)


def _inproj(h2, gain, w, rc, rs1, rs2, B, S, ts):
    nt = S // ts
    BS = B * S
    row = lambda b, i: (b * nt + i, 0)
    full = lambda b, i: (0, 0)
    headed = lambda b, i: (b, 0, i, 0)
    in_specs = [
        pl.BlockSpec((ts, D_MODEL), row),
        pl.BlockSpec((1, D_MODEL), full),
        pl.BlockSpec((D_MODEL, D_INP), full),
        pl.BlockSpec((ts, LANE), row),
        pl.BlockSpec((ts, LANE), row),
        pl.BlockSpec((ts, LANE), row),
    ]
    kv_shape = jax.ShapeDtypeStruct((B, NSA_KV_GROUPS, S, NSA_HEAD_DIM), BF16)
    kv_spec = pl.BlockSpec((None, NSA_KV_GROUPS, ts, NSA_HEAD_DIM), headed)
    out_shape = [
        jax.ShapeDtypeStruct((S, B * SSM_WIDTH), F32),
        jax.ShapeDtypeStruct((BS, MLSTM_WIDTH), F32),
        jax.ShapeDtypeStruct((BS, MLSTM_WIDTH), F32),
        jax.ShapeDtypeStruct((BS, MLSTM_WIDTH), BF16),
        jax.ShapeDtypeStruct((BS, MLSTM_WIDTH), F32),
        jax.ShapeDtypeStruct((B, NSA_HEADS, S, NSA_HEAD_DIM), BF16),
        jax.ShapeDtypeStruct((BS, NSA_KV_WIDTH), BF16),
        kv_shape, kv_shape,
        jax.ShapeDtypeStruct((BS, NSA_KV_WIDTH), BF16),
        kv_shape, kv_shape,
        jax.ShapeDtypeStruct((NSA_KV_GROUPS, BS, LANE), F32),
    ]
    out_specs = [
        pl.BlockSpec((ts, SSM_WIDTH), lambda b, i: (i, b)),
        pl.BlockSpec((ts, MLSTM_WIDTH), row),
        pl.BlockSpec((ts, MLSTM_WIDTH), row),
        pl.BlockSpec((ts, MLSTM_WIDTH), row),
        pl.BlockSpec((ts, MLSTM_WIDTH), row),
        pl.BlockSpec((None, NSA_HEADS, ts, NSA_HEAD_DIM), headed),
        pl.BlockSpec((ts, NSA_KV_WIDTH), row),
        kv_spec, kv_spec,
        pl.BlockSpec((ts, NSA_KV_WIDTH), row),
        kv_spec, kv_spec,
        pl.BlockSpec((NSA_KV_GROUPS, ts, LANE), lambda b, i: (0, b * nt + i, 0)),
    ]
    return pl.pallas_call(
        _inproj_kernel,
        grid=(B, nt),
        in_specs=in_specs,
        out_specs=out_specs,
        out_shape=out_shape,
        compiler_params=pltpu.CompilerParams(
            dimension_semantics=("parallel", "parallel"), vmem_limit_bytes=VMEM_LIMIT),
        name="inproj",
    )(h2, gain, w, rc, rs1, rs2)


def _s5_kernel(u_ref, bb_ref, a_ref, cc_ref, d_ref, wg_ref, gm_ref, gain_ref, o_ref, x_sc, st_sc, *, B, ts):
    @pl.when(pl.program_id(0) == 0)
    def _():
        st_sc[...] = jnp.zeros_like(st_sc)

    u = u_ref[...]
    x_sc[...] = _dot(u.astype(BF16), bb_ref[...])
    ar = jnp.broadcast_to(a_ref[0:1, :], (B, SSM_LANES))
    ai = jnp.broadcast_to(a_ref[1:2, :], (B, SSM_LANES))

    def step(t, carry):
        xr, xi = carry
        r = pl.multiple_of(t * B, B)
        br = x_sc[pl.ds(r, B), 0:SSM_LANES]
        bi = x_sc[pl.ds(r, B), SSM_LANES:2 * SSM_LANES]
        nr = ar * xr - ai * xi + br
        ni = ar * xi + ai * xr + bi
        x_sc[pl.ds(r, B), 0:SSM_LANES] = nr
        x_sc[pl.ds(r, B), SSM_LANES:2 * SSM_LANES] = ni
        return nr, ni

    xr, xi = lax.fori_loop(0, ts, step, (st_sc[0], st_sc[1]))
    st_sc[0] = xr
    st_sc[1] = xi

    y = _dot(x_sc[...].astype(BF16), cc_ref[...]) + d_ref[...] * u
    y = _gelu_tanh(y)
    y = y * _sigmoid(_dot(y.astype(BF16), wg_ref[...]))
    ms = _dot(y * y, gm_ref[...], precision=HIGHEST)
    o_ref[...] = (y * lax.rsqrt(ms + EPS) * gain_ref[...]).astype(BF16)


def _s5(u_tm, bb, a, cc, d, wg, gm, gain, B, S, ts):
    rows = ts * B
    full = lambda i: (0, 0)
    return pl.pallas_call(
        functools.partial(_s5_kernel, B=B, ts=ts),
        grid=(S // ts,),
        in_specs=[
            pl.BlockSpec((rows, SSM_WIDTH), lambda i: (i, 0)),
            pl.BlockSpec((SSM_WIDTH, 2 * SSM_LANES), full),
            pl.BlockSpec((2, SSM_LANES), full),
            pl.BlockSpec((2 * SSM_LANES, SSM_WIDTH), full),
            pl.BlockSpec((1, SSM_WIDTH), full),
            pl.BlockSpec((SSM_WIDTH, SSM_WIDTH), full),
            pl.BlockSpec((SSM_WIDTH, SSM_WIDTH), full),
            pl.BlockSpec((1, SSM_WIDTH), full),
        ],
        out_specs=pl.BlockSpec((rows, SSM_WIDTH), lambda i: (i, 0)),
        out_shape=jax.ShapeDtypeStruct((S * B, SSM_WIDTH), BF16),
        scratch_shapes=[pltpu.VMEM((rows, 2 * SSM_LANES), F32), pltpu.VMEM((2, B, SSM_LANES), F32)],
        compiler_params=pltpu.CompilerParams(
            dimension_semantics=("arbitrary",), vmem_limit_bytes=VMEM_LIMIT),
        name="s5",
    )(u_tm, bb, a, cc, d, wg, gm, gain)


def _mlstm_kernel(q_ref, k_ref, v_ref, o_ref, gc_ref, gr_ref, cwq_ref, cwk_ref, bc_ref, br_ref, hm_ref,
                  gain_ref, y_ref, qt_sc, kt_sc, c_sc, m_sc, *, B):
    L, H, Dh, W = MLSTM_CHUNK, MLSTM_HEADS, MLSTM_HEAD_DIM, MLSTM_WIDTH

    @pl.when(pl.program_id(0) == 0)
    def _():
        qt_sc[...] = jnp.zeros_like(qt_sc)
        kt_sc[...] = jnp.zeros_like(kt_sc)
        c_sc[...] = jnp.zeros_like(c_sc)
        m_sc[...] = jnp.zeros_like(m_sc)

    row_l = lax.broadcasted_iota(jnp.int32, (L, L), 0)
    col_l = lax.broadcasted_iota(jnp.int32, (L, L), 1)
    causal = col_l <= row_l
    tri = causal.astype(F32)
    triu = (row_l <= col_l).astype(F32)
    lane_w = lax.broadcasted_iota(jnp.int32, (1, W), 1) // Dh
    lane_2w = (lax.broadcasted_iota(jnp.int32, (1, 2 * W), 1) % W) // Dh
    bd_mask = (lax.broadcasted_iota(jnp.int32, (W, 2 * W), 0) // Dh
               == (lax.broadcasted_iota(jnp.int32, (W, 2 * W), 1) % W) // Dh)
    row8 = lax.broadcasted_iota(jnp.int32, (SUBLANE, W), 0)
    cwq = cwq_ref[...]
    cwk = cwk_ref[...]
    ones_v = jnp.ones((L, W), BF16)

    def conv_silu(x, tail, w):
        acc = x * w[MLSTM_CONV - 1:MLSTM_CONV, :]
        for sft in range(1, MLSTM_CONV):
            xs = pltpu.roll(x, sft, 0)
            head = jnp.where(row8 < sft, pltpu.roll(tail, sft, 0), xs[:SUBLANE])
            xs = jnp.concatenate([head, xs[SUBLANE:]], axis=0)
            acc = acc + xs * w[MLSTM_CONV - 1 - sft:MLSTM_CONV - sft, :]
        return acc * _sigmoid(acc)

    def expand(cols, width_lanes):
        out = cols[H - 1]
        for hh in range(H - 2, -1, -1):
            out = jnp.where(width_lanes == hh, cols[hh], out)
        return out

    def per_batch(b, _):
        q_raw = q_ref[b]
        k_raw = k_ref[b]
        q = conv_silu(q_raw, qt_sc[b], cwq)
        k = conv_silu(k_raw, kt_sc[b], cwk) * (Dh ** -0.5)
        qt_sc[b] = q_raw[L - SUBLANE:, :]
        kt_sc[b] = k_raw[L - SUBLANE:, :]
        vaug = jnp.concatenate([v_ref[b], ones_v], axis=1)

        gc = gc_ref[b] + bc_ref[...]
        gr = gr_ref[b] + br_ref[...]
        bcol = _dot(tri, _log_sigmoid(gc), precision=HIGHEST)
        brow = _dot(_log_sigmoid(gr), triu, precision=HIGHEST)
        m_all = m_sc[b]

        w_intra, w_inter, e_mt, w_k, dec, m_new = [], [], [], [], [], []
        for hh in range(H):
            bc = bcol[:, H + hh:H + hh + 1]
            ic = gc[:, hh:hh + 1]
            brr = brow[H + hh:H + hh + 1, :]
            irr = gr[hh:hh + 1, :]
            m_prev = m_all[hh:hh + 1, 0:1]
            dm = jnp.where(causal, bc - brr + irr, NEG_INF)
            inter = bc + m_prev
            mt = jnp.maximum(inter, jnp.max(dm, axis=1, keepdims=True))
            w_intra.append(jnp.exp(dm - mt))
            w_inter.append(jnp.exp(inter - mt))
            e_mt.append(jnp.exp(-mt))
            b_last = bc[L - 1:L, :]
            logw = b_last - bc + ic
            mn = jnp.maximum(b_last + m_prev, jnp.max(logw, axis=0, keepdims=True))
            w_k.append(jnp.exp(logw - mn))
            dec.append(jnp.exp(b_last + m_prev - mn))
            m_new.append(mn)

        qb = q.astype(BF16)
        kb = k.astype(BF16)
        q_heads = jnp.concatenate([jnp.where(lane_w == hh, qb, jnp.zeros_like(qb)) for hh in range(H)], axis=0)
        s = _dot_nt(q_heads, kb) * jnp.concatenate(w_intra, axis=0)
        sv = _dot(s.astype(BF16), vaug)
        intra = jnp.where(lane_2w == 0, sv[0:L], 0.0)
        for hh in range(1, H):
            intra = intra + jnp.where(lane_2w == hh, sv[hh * L:(hh + 1) * L], 0.0)
        c_aug = c_sc[b]
        qc = _dot(qb, c_aug.astype(BF16))
        tot = expand(w_inter, lane_2w) * qc + intra
        den = jnp.maximum(jnp.abs(tot[:, W:]), expand(e_mt, lane_w))
        hout = tot[:, :W] / den
        y = _sigmoid(o_ref[b]) * hout
        ms = _dot(y * y, hm_ref[...], precision=HIGHEST)
        y_ref[b] = (y * lax.rsqrt(ms + EPS) * gain_ref[...]).astype(BF16)

        kw = (k * expand(w_k, lane_w)).astype(BF16)
        upd = _dot_tn(kw, vaug)
        c_sc[b] = expand(dec, lane_2w) * c_aug + jnp.where(bd_mask, upd, 0.0)
        for hh in range(H):
            m_sc[b, hh:hh + 1, :] = jnp.broadcast_to(m_new[hh], (1, LANE))
        return 0

    lax.fori_loop(0, B, per_batch, 0)


def _mlstm(mq, mk, mv, mo, gcol, grow, cwq, cwk, bcol, brow, hm, gain, B, S):
    L, W = MLSTM_CHUNK, MLSTM_WIDTH
    seq = lambda c: (0, c, 0)
    full = lambda c: (0, 0)
    return pl.pallas_call(
        functools.partial(_mlstm_kernel, B=B),
        grid=(S // L,),
        in_specs=[
            pl.BlockSpec((B, L, W), seq),
            pl.BlockSpec((B, L, W), seq),
            pl.BlockSpec((B, L, W), seq),
            pl.BlockSpec((B, L, W), seq),
            pl.BlockSpec((B, L, LANE), seq),
            pl.BlockSpec((B, SUBLANE, L), lambda c: (0, 0, c)),
            pl.BlockSpec((MLSTM_CONV, W), full),
            pl.BlockSpec((MLSTM_CONV, W), full),
            pl.BlockSpec((1, LANE), full),
            pl.BlockSpec((SUBLANE, 1), full),
            pl.BlockSpec((W, W), full),
            pl.BlockSpec((1, W), full),
        ],
        out_specs=pl.BlockSpec((B, L, W), seq),
        out_shape=jax.ShapeDtypeStruct((B, S, W), BF16),
        scratch_shapes=[
            pltpu.VMEM((B, SUBLANE, W), F32),
            pltpu.VMEM((B, SUBLANE, W), F32),
            pltpu.VMEM((B, W, 2 * W), F32),
            pltpu.VMEM((B, SUBLANE, LANE), F32),
        ],
        compiler_params=pltpu.CompilerParams(
            dimension_semantics=("arbitrary",), vmem_limit_bytes=VMEM_LIMIT),
        name="mlstm",
    )(mq, mk, mv, mo, gcol, grow, cwq, cwk, bcol, brow, hm, gain)


def _compress_kernel(ch_ref, w1ab_ref, w1_ref, pe_ref, w2_ref, o_ref):
    rows = ch_ref.shape[0]
    ab = _dot(ch_ref[...], w1ab_ref[...])
    const = _dot(pe_ref[...], w1_ref[...], precision=HIGHEST)
    hid = ab[:, :CMP_HIDDEN] + pltpu.roll(ab[:, CMP_HIDDEN:], rows - 1, 0) + const
    o_ref[...] = _dot(_gelu_tanh(hid).astype(BF16), w2_ref[...]).astype(BF16)


def _compress(ch, w1ab, w1, pe, w2):
    two, rows, width = ch.shape
    return pl.pallas_call(
        _compress_kernel,
        grid=(two,),
        in_specs=[
            pl.BlockSpec((None, rows, width), lambda i: (i, 0, 0)),
            pl.BlockSpec((None, width, 2 * CMP_HIDDEN), lambda i: (i, 0, 0)),
            pl.BlockSpec((None, 2 * width, CMP_HIDDEN), lambda i: (i, 0, 0)),
            pl.BlockSpec((None, 1, 2 * width), lambda i: (i, 0, 0)),
            pl.BlockSpec((None, CMP_HIDDEN, NSA_HEAD_DIM), lambda i: (i, 0, 0)),
        ],
        out_specs=pl.BlockSpec((None, rows, NSA_HEAD_DIM), lambda i: (i, 0, 0)),
        out_shape=jax.ShapeDtypeStruct((two, rows, NSA_HEAD_DIM), BF16),
        compiler_params=pltpu.CompilerParams(
            dimension_semantics=("parallel",), vmem_limit_bytes=VMEM_LIMIT),
        name="compress",
    )(ch, w1ab, w1, pe, w2)


def _nsa_kernel(q_ref, kc_ref, vc_ref, ks_ref, vs_ref, kw_ref, vw_ref, gt_ref, gain_ref,
                ov_ref, rrep_ref, ssum_ref, e_ref, o_ref, *, n_sel, n_top, ck):
    TQ, R, Dh = Q_BLOCK, NSA_REP, NSA_HEAD_DIM
    i = pl.program_id(2)
    t0 = i * TQ
    q = q_ref[...].reshape(R * TQ, Dh)
    tq1 = t0 + lax.broadcasted_iota(jnp.int32, (TQ, 1), 0)
    tq = jnp.concatenate([tq1] * R, axis=0)

    ncmp = kc_ref.shape[0]
    sc = _dot_nt(q, kc_ref[...])
    cend = lax.broadcasted_iota(jnp.int32, (1, ncmp), 1) * CMP_STRIDE + (CMP_BLOCK - 1)
    cmask = cend <= tq
    scm = jnp.where(cmask, sc, NEG_INF)
    ec = jnp.where(cmask, jnp.exp(scm - jnp.max(scm, axis=1, keepdims=True)), 0.0)
    pc = ec / jnp.maximum(jnp.sum(ec, axis=1, keepdims=True), 1e-30)
    o_c = _dot(pc.astype(BF16), vc_ref[...])

    psum = pc[0:TQ]
    for r in range(1, R):
        psum = psum + pc[r * TQ:(r + 1) * TQ]
    imp = _dot(psum, ov_ref[...], precision=HIGHEST)
    blk = lax.broadcasted_iota(jnp.int32, (1, n_sel), 1)
    valid = blk * SEL_BLOCK <= tq1
    forced = (blk == 0) | (blk == tq1 // SEL_BLOCK)
    imp = jnp.where(forced, FORCE_SCORE, jnp.where(valid, imp, -FORCE_SCORE))
    nn = n_sel * n_sel
    pair = _dot(imp, rrep_ref[...], precision=HIGHEST)
    mine, other = pair[:, :nn], pair[:, nn:]
    cidx = lax.broadcasted_iota(jnp.int32, (1, nn), 1)
    beats = (other > mine) | ((other == mine) & (cidx % n_sel < cidx // n_sel))
    rank = _dot(jnp.where(beats, 1.0, 0.0).astype(BF16), ssum_ref[...])
    selb = jnp.where(rank < n_top, 0.0, NEG_INF).astype(BF16)

    def chunk(c, carry):
        m, l, acc = carry
        k0 = pl.multiple_of(c * ck, ck)
        kch = ks_ref[pl.ds(k0, ck), :]
        vch = vs_ref[pl.ds(k0, ck), :]
        kpos = k0 + lax.broadcasted_iota(jnp.int32, (1, ck), 1)
        bias = jnp.where(kpos <= tq1, _dot(selb, e_ref[c]), NEG_INF)
        s = (_dot_nt(q, kch).reshape(R, TQ, ck) + bias[None]).reshape(R * TQ, ck)
        mn = jnp.maximum(m, jnp.max(s, axis=1, keepdims=True))
        alpha = jnp.exp(m - mn)
        p = jnp.exp(s - mn)
        l = alpha * l + jnp.sum(p, axis=1, keepdims=True)
        acc = alpha * acc + _dot(p.astype(BF16), vch)
        return mn, l, acc

    init = (jnp.full((R * TQ, 1), NEG_INF, F32), jnp.zeros((R * TQ, 1), F32), jnp.zeros((R * TQ, Dh), F32))
    _, l_s, acc_s = lax.fori_loop(0, (t0 + TQ + ck - 1) // ck, chunk, init)
    o_s = acc_s / l_s

    wk_len = WINDOW + TQ
    ws = pl.multiple_of(jnp.maximum(t0 - WINDOW, 0), TQ)
    kpos = ws + lax.broadcasted_iota(jnp.int32, (1, wk_len), 1)
    wbias = jnp.where((kpos <= tq1) & (tq1 - kpos < WINDOW), 0.0, NEG_INF)
    sw = (_dot_nt(q, kw_ref[pl.ds(ws, wk_len), :]).reshape(R, TQ, wk_len) + wbias[None]).reshape(R * TQ, wk_len)
    pw = jnp.exp(sw - jnp.max(sw, axis=1, keepdims=True))
    o_w = _dot(pw.astype(BF16), vw_ref[pl.ds(ws, wk_len), :]) / jnp.sum(pw, axis=1, keepdims=True)

    gs = _sigmoid(gt_ref[...])
    for r in range(R):
        c = GATE_COL + 3 * r
        rows = slice(r * TQ, (r + 1) * TQ)
        o = gs[:, c:c + 1] * o_c[rows] + gs[:, c + 1:c + 2] * o_s[rows] + gs[:, c + 2:c + 3] * o_w[rows]
        ms = jnp.mean(o * o, axis=1, keepdims=True)
        o_ref[r] = (o * lax.rsqrt(ms + EPS) * gain_ref[r]).astype(BF16)


def _nsa(aq, kcmp, vcmp, ks, vs, kw, vw, gates, gain, consts, B, S):
    G, R, TQ, Dh = NSA_KV_GROUPS, NSA_REP, Q_BLOCK, NSA_HEAD_DIM
    nq = S // TQ
    ncmp = S // CMP_STRIDE
    n_sel = S // SEL_BLOCK
    ov, rrep, ssum, emat, ck = consts
    kv_spec = pl.BlockSpec((None, None, S, Dh), lambda b, g, i: (b, g, 0, 0))
    cmp_spec = pl.BlockSpec((None, ncmp, Dh), lambda b, g, i: (b * G + g, 0, 0))
    const2 = lambda b, g, i: (0, 0)
    return pl.pallas_call(
        functools.partial(_nsa_kernel, n_sel=n_sel, n_top=min(SEL_TOPN, n_sel), ck=ck),
        grid=(B, G, nq),
        in_specs=[
            pl.BlockSpec((None, R, TQ, Dh), lambda b, g, i: (b, g, i, 0)),
            cmp_spec, cmp_spec, kv_spec, kv_spec, kv_spec, kv_spec,
            pl.BlockSpec((None, TQ, LANE), lambda b, g, i: (g, b * nq + i, 0)),
            pl.BlockSpec((R, 1, Dh), lambda b, g, i: (g, 0, 0)),
            pl.BlockSpec(ov.shape, const2),
            pl.BlockSpec(rrep.shape, const2),
            pl.BlockSpec(ssum.shape, const2),
            pl.BlockSpec(emat.shape, lambda b, g, i: (0, 0, 0)),
        ],
        out_specs=pl.BlockSpec((None, R, TQ, Dh), lambda b, g, i: (b, g, i, 0)),
        out_shape=jax.ShapeDtypeStruct((B, NSA_HEADS, S, Dh), BF16),
        compiler_params=pltpu.CompilerParams(
            dimension_semantics=("parallel", "parallel", "arbitrary"), vmem_limit_bytes=VMEM_LIMIT),
        name="nsa",
    )(aq, kcmp, vcmp, ks, vs, kw, vw, gates, gain, ov, rrep, ssum, emat)


def _nsa_consts(S):
    n_cmp = S // CMP_STRIDE
    n_sel = S // SEL_BLOCK
    ck = min(512, S)
    i = np.arange(n_cmp)[:, None]
    j = np.arange(n_sel)[None, :]
    lo = np.maximum(i * CMP_STRIDE, j * SEL_BLOCK)
    hi = np.minimum(i * CMP_STRIDE + CMP_BLOCK, (j + 1) * SEL_BLOCK)
    ov = np.maximum(hi - lo, 0) / CMP_STRIDE
    ov[n_cmp - 1] = 0.0
    nn = n_sel * n_sel
    c = np.arange(nn)
    rrep = np.zeros((n_sel, 2 * nn), np.float32)
    rrep[c // n_sel, c] = 1.0
    rrep[c % n_sel, nn + c] = 1.0
    ssum = np.zeros((nn, n_sel), np.float32)
    ssum[c, c // n_sel] = 1.0
    key = np.arange(S)
    emat = (key[None, :] // SEL_BLOCK == np.arange(n_sel)[:, None]).astype(np.float32)
    emat = emat.reshape(n_sel, S // ck, ck).transpose(1, 0, 2)
    return (jnp.asarray(ov, F32), jnp.asarray(rrep, F32), jnp.asarray(ssum, BF16), jnp.asarray(emat, BF16), ck)


def _outproj_kernel(h_ref, ys_ref, ym_ref, yn_ref, w_ref, g_ref, o_ref):
    acc = _dot(ys_ref[...], w_ref[0:SSM_WIDTH, :])
    acc = acc + _dot(ym_ref[...], w_ref[SSM_WIDTH:SSM_WIDTH + MLSTM_WIDTH, :])
    base = SSM_WIDTH + MLSTM_WIDTH
    for hd in range(NSA_HEADS):
        acc = acc + _dot(yn_ref[hd], w_ref[base + hd * NSA_HEAD_DIM:base + (hd + 1) * NSA_HEAD_DIM, :])
    ms = jnp.mean(acc * acc, axis=-1, keepdims=True)
    o_ref[...] = h_ref[...] + acc * lax.rsqrt(ms + EPS) * g_ref[...]


def _outproj(h2, y_ssm, y_mls, y_nsa, w, gain, B, S, ts):
    nt = S // ts
    row = lambda b, i: (b * nt + i, 0)
    full = lambda b, i: (0, 0)
    return pl.pallas_call(
        _outproj_kernel,
        grid=(B, nt),
        in_specs=[
            pl.BlockSpec((ts, D_MODEL), row),
            pl.BlockSpec((ts, SSM_WIDTH), lambda b, i: (i, b)),
            pl.BlockSpec((ts, MLSTM_WIDTH), row),
            pl.BlockSpec((None, NSA_HEADS, ts, NSA_HEAD_DIM), lambda b, i: (b, 0, i, 0)),
            pl.BlockSpec((D_MODEL, D_MODEL), full),
            pl.BlockSpec((1, D_MODEL), full),
        ],
        out_specs=pl.BlockSpec((ts, D_MODEL), row),
        out_shape=jax.ShapeDtypeStruct((B * S, D_MODEL), F32),
        compiler_params=pltpu.CompilerParams(
            dimension_semantics=("parallel", "parallel"), vmem_limit_bytes=VMEM_LIMIT),
        name="outproj",
    )(h2, y_ssm, y_mls, y_nsa, w, gain)


def _mlp_kernel(h_ref, g1_ref, w1_ref, w2_ref, g2_ref, o_ref, u_sc, acc_sc):
    kf = pl.program_id(1)

    @pl.when(kf == 0)
    def _():
        x = h_ref[...]
        ms = jnp.mean(x * x, axis=-1, keepdims=True)
        u_sc[...] = (x * lax.rsqrt(ms + EPS) * g1_ref[...]).astype(BF16)
        acc_sc[...] = jnp.zeros_like(acc_sc)

    a = jnp.maximum(_dot(u_sc[...], w1_ref[...]), 0.0)
    acc_sc[...] += _dot((a * a).astype(BF16), w2_ref[...])

    @pl.when(kf == pl.num_programs(1) - 1)
    def _():
        f = acc_sc[...]
        ms = jnp.mean(f * f, axis=-1, keepdims=True)
        o_ref[...] = h_ref[...] + f * lax.rsqrt(ms + EPS) * g2_ref[...]


def _mlp(h2, g1, w1, w2, g2, tm, tf):
    rows = h2.shape[0]
    return pl.pallas_call(
        _mlp_kernel,
        grid=(rows // tm, D_FF // tf),
        in_specs=[
            pl.BlockSpec((tm, D_MODEL), lambda i, k: (i, 0)),
            pl.BlockSpec((1, D_MODEL), lambda i, k: (0, 0)),
            pl.BlockSpec((D_MODEL, tf), lambda i, k: (0, k)),
            pl.BlockSpec((tf, D_MODEL), lambda i, k: (k, 0)),
            pl.BlockSpec((1, D_MODEL), lambda i, k: (0, 0)),
        ],
        out_specs=pl.BlockSpec((tm, D_MODEL), lambda i, k: (i, 0)),
        out_shape=jax.ShapeDtypeStruct((rows, D_MODEL), F32),
        scratch_shapes=[pltpu.VMEM((tm, D_MODEL), BF16), pltpu.VMEM((tm, D_MODEL), F32)],
        compiler_params=pltpu.CompilerParams(
            dimension_semantics=("parallel", "arbitrary"), vmem_limit_bytes=VMEM_LIMIT),
        name="mlp",
    )(h2, g1, w1, w2, g2)


def _inproj_perm():
    r = lambda a, b: list(range(a, b))
    pad = [D_IN]
    g0 = r(1280, 1288) + r(2568, 2580) + pad * (LANE - 20)
    g1 = pad * GATE_COL + r(2580, 2592) + pad * (LANE - GATE_COL - 12)
    perm = (r(0, 1280) + r(1288, 1800) + r(1800, 1928) + r(2056, 2184) + r(2312, 2440)
            + r(1928, 2056) + r(2184, 2312) + r(2440, 2568) + g0 + g1)
    assert len(perm) == D_INP
    return np.asarray(perm, np.int32)


def _rope_tables(positions):
    inv = ROPE_THETA ** (-jnp.arange(0, ROPE_DIMS, 2, dtype=F32) / ROPE_DIMS)
    ang = positions.astype(F32)[..., None] * inv
    cos, sin = jnp.cos(ang), jnp.sin(ang)
    z = jnp.zeros_like(cos)
    rest = NSA_HEAD_DIM - ROPE_DIMS
    pad_one = jnp.ones(cos.shape[:-1] + (rest,), F32)
    pad_zero = jnp.zeros(cos.shape[:-1] + (rest,), F32)
    rc = jnp.concatenate([cos, cos, pad_one], axis=-1)
    rs1 = jnp.concatenate([-sin, z, pad_zero], axis=-1)
    rs2 = jnp.concatenate([z, sin, pad_zero], axis=-1)
    tile = lambda t: jnp.tile(t, (1, 1, LANE // NSA_HEAD_DIM)).reshape(-1, LANE)
    return tile(rc), tile(rs1), tile(rs2)


def _s5_params(lam_re, lam_im, b_re, b_im, c_re, c_im, log_dt):
    G, P, Hc = SSM_GROUPS, SSM_STATE, SSM_GROUP
    dt = jnp.exp(log_dt)[:, None]
    mag = jnp.exp(lam_re * dt)
    ang = lam_im * dt
    ab_re = mag * jnp.cos(ang)
    ab_im = mag * jnp.sin(ang)
    den = lam_re * lam_re + lam_im * lam_im
    g_re = ((ab_re - 1.0) * lam_re + ab_im * lam_im) / den
    g_im = (ab_im * lam_re - (ab_re - 1.0) * lam_im) / den
    bb_re = g_re[..., None] * b_re - g_im[..., None] * b_im
    bb_im = g_re[..., None] * b_im + g_im[..., None] * b_re
    eye = jnp.eye(G, dtype=F32)
    blockdiag_in = lambda t: jnp.einsum('gph,gk->ghkp', t, eye).reshape(G * Hc, G * P)
    blockdiag_out = lambda t: jnp.einsum('ghp,gk->gpkh', t, eye).reshape(G * P, G * Hc)
    bb = jnp.concatenate([blockdiag_in(bb_re), blockdiag_in(bb_im)], axis=1).astype(BF16)
    cc = jnp.concatenate([blockdiag_out(c_re), -blockdiag_out(c_im)], axis=0).astype(BF16)
    a = jnp.stack([ab_re.reshape(-1), ab_im.reshape(-1)], axis=0)
    return bb, a, cc


def _group_mean_matrix(width, group):
    idx = np.arange(width) // group
    return jnp.asarray((idx[:, None] == idx[None, :]).astype(np.float32) / group, F32)


def kernel(x, positions, ln_mix_pre, ln_mix_post, ln_mlp_pre, ln_mlp_post, w_in, w_out, ssm_lambda_re, ssm_lambda_im, ssm_b_re, ssm_b_im, ssm_c_re, ssm_c_im, ssm_d, ssm_log_dt, ssm_w_glu, mlstm_conv, mlstm_b_i, mlstm_b_f, cmp_pe_k, cmp_w1_k, cmp_w2_k, cmp_pe_v, cmp_w1_v, cmp_w2_v, gn_ssm, gn_mlstm, gn_nsa, mlp_w1, mlp_w2):
    B, S, D = x.shape
    depth = w_in.shape[0]
    assert D == D_MODEL and B == SUBLANE and S % 512 == 0 and S >= WINDOW + Q_BLOCK
    G, H = NSA_KV_GROUPS, MLSTM_HEADS
    ts_proj = 512
    ts_scan = 128

    rc, rs1, rs2 = _rope_tables(positions)
    perm = _inproj_perm()
    w_in_p = jnp.concatenate([w_in, jnp.zeros((depth, D, 1), F32)], axis=-1)[:, :, perm].astype(BF16)
    w_out_b = w_out.astype(BF16)
    w1_b = mlp_w1.astype(BF16)
    w2_b = mlp_w2.astype(BF16)
    wglu_b = ssm_w_glu.astype(BF16)
    gm_ssm = _group_mean_matrix(SSM_WIDTH, SSM_GROUP)
    hm_mls = _group_mean_matrix(MLSTM_WIDTH, MLSTM_HEAD_DIM)
    consts = _nsa_consts(S)
    half = CMP_STRIDE * NSA_HEAD_DIM
    w1ab = jnp.stack([jnp.concatenate([cmp_w1_k[:, :half], cmp_w1_k[:, half:]], axis=-1),
                      jnp.concatenate([cmp_w1_v[:, :half], cmp_w1_v[:, half:]], axis=-1)], axis=1).astype(BF16)
    w1f = jnp.stack([cmp_w1_k, cmp_w1_v], axis=1)
    pef = jnp.stack([cmp_pe_k.reshape(depth, 1, -1), cmp_pe_v.reshape(depth, 1, -1)], axis=1)
    w2c = jnp.stack([cmp_w2_k, cmp_w2_v], axis=1).astype(BF16)
    zeros_g = jnp.zeros((depth, LANE - 2 * H), F32)
    bias_col = jnp.concatenate([mlstm_b_i, mlstm_b_f, zeros_g], axis=-1)[:, None, :]
    bias_row = jnp.concatenate([mlstm_b_i, mlstm_b_f], axis=-1)[:, :, None]

    h = x.reshape(B * S, D)
    for l in range(depth):
        (su, mq, mk, mv, mo, aq, ck, sk, wk, cv, sv, wv, gates) = _inproj(
            h, ln_mix_pre[l][None], w_in_p[l], rc, rs1, rs2, B, S, ts_proj)

        bb, a, cc = _s5_params(ssm_lambda_re[l], ssm_lambda_im[l], ssm_b_re[l], ssm_b_im[l],
                               ssm_c_re[l], ssm_c_im[l], ssm_log_dt[l])
        y_ssm = _s5(su.reshape(S * B, SSM_WIDTH), bb, a, cc, ssm_d[l][None], wglu_b[l], gm_ssm,
                    gn_ssm[l][None], B, S, ts_scan)

        sh3 = lambda t: t.reshape(B, S, t.shape[-1])
        gcol = sh3(gates[0])
        grow = jnp.swapaxes(gcol[:, :, :2 * H], 1, 2)
        y_mls = _mlstm(sh3(mq), sh3(mk), sh3(mv), sh3(mo), gcol, grow,
                       mlstm_conv[l][:, :MLSTM_WIDTH], mlstm_conv[l][:, MLSTM_WIDTH:],
                       bias_col[l], bias_row[l], hm_mls, gn_mlstm[l][None], B, S)

        def chunks(t):
            t = t.reshape(B, S // CMP_STRIDE, CMP_STRIDE, G, NSA_HEAD_DIM).transpose(0, 3, 1, 2, 4)
            return t.reshape(B * G * (S // CMP_STRIDE), half)

        cmp_kv = _compress(jnp.stack([chunks(ck), chunks(cv)]), w1ab[l], w1f[l], pef[l], w2c[l])
        ncmp = S // CMP_STRIDE
        kcmp = cmp_kv[0].reshape(B * G, ncmp, NSA_HEAD_DIM)
        vcmp = cmp_kv[1].reshape(B * G, ncmp, NSA_HEAD_DIM)
        y_nsa = _nsa(aq, kcmp, vcmp, sk, sv, wk, wv, gates, gn_nsa[l].reshape(NSA_HEADS, 1, NSA_HEAD_DIM),
                     consts, B, S)

        h = _outproj(h, y_ssm.reshape(S, B * SSM_WIDTH), y_mls.reshape(B * S, MLSTM_WIDTH), y_nsa,
                     w_out_b[l], ln_mix_post[l][None], B, S, ts_proj)
        h = _mlp(h, ln_mlp_pre[l][None], w1_b[l], w2_b[l], ln_mlp_post[l][None], 1024, 512)
    return h.reshape(B, S, D)
```

```python
import functools
import math

import numpy as np
import jax
import jax.numpy as jnp
from jax import lax
from jax.experimental import pallas as pl
from jax.experimental.pallas import tpu as pltpu

F32 = jnp.float32
BF16 = jnp.bfloat16
HIGHEST = lax.Precision.HIGHEST

D_MODEL = 1024
DEPTH = 4
SSM_WIDTH = 256
SSM_GROUP = 16
SSM_GROUPS = 16
SSM_STATE = 64
SSM_LANES = SSM_GROUPS * SSM_STATE
MLSTM_WIDTH = 256
MLSTM_HEADS = 4
MLSTM_HEAD_DIM = 64
MLSTM_CHUNK = 128
MLSTM_CONV = 4
NSA_WIDTH = 512
NSA_HEAD_DIM = 64
NSA_HEADS = 8
NSA_KV_GROUPS = 2
NSA_REP = NSA_HEADS // NSA_KV_GROUPS
NSA_KV_WIDTH = 128
CMP_BLOCK = 32
CMP_STRIDE = 16
CMP_HIDDEN = 256
SEL_BLOCK = 64
SEL_TOPN = 8
WINDOW = 256
Q_BLOCK = 128
FORCE_SCORE = 1e4
NEG_INF = -1e30
ROPE_THETA = 500000.0
ROPE_DIMS = 16
ROPE_HALF = 8
D_FF = 4096
EPS = 1e-6
D_IN = 2592

LANE = 128
SUBLANE = 8
VMEM_LIMIT = 56 * 1024 * 1024

C_SU, C_MQ, C_MK, C_MV, C_MO = 0, 256, 512, 768, 1024
C_AQ, C_CK, C_SK, C_WK = 1280, 1792, 1920, 2048
C_CV, C_SV, C_WV = 2176, 2304, 2432
C_G0, C_G1 = 2560, 2688
D_INP = 2816
GATE_COL = 8


def _dot(a, b, precision=None):
    return jnp.dot(a, b, preferred_element_type=F32, precision=precision)


def _dot_nt(a, b):
    return lax.dot_general(a, b, (((1,), (1,)), ((), ())), preferred_element_type=F32)


def _dot_tn(a, b):
    return lax.dot_general(a, b, (((0,), (0,)), ((), ())), preferred_element_type=F32)


def _sigmoid(x):
    return 1.0 / (1.0 + jnp.exp(-x))


def _dot_split(x, w_bf16):
    hi = x.astype(BF16)
    lo = (x - hi.astype(F32)).astype(BF16)
    return _dot(hi, w_bf16) + _dot(lo, w_bf16)


def _gelu_tanh(x):
    return 0.5 * x * (1.0 + jnp.tanh(math.sqrt(2.0 / math.pi) * (x + 0.044715 * (x * x * x))))


def _log_sigmoid(x):
    return jnp.minimum(x, 0.0) - jnp.log(1.0 + jnp.exp(-jnp.abs(x)))


def _inproj_kernel(x_ref, g_ref, w_ref, rc_ref, rs1_ref, rs2_ref,
                   su_ref, mq_ref, mk_ref, mv_ref, mo_ref, aq_ref, ckv_ref, sk_ref, wk_ref,
                   sv_ref, wv_ref, gt_ref):
    x = x_ref[...]
    ms = jnp.mean(x * x, axis=-1, keepdims=True)
    u = (x * lax.rsqrt(ms + EPS) * g_ref[...]).astype(BF16)
    rc, rs1, rs2 = rc_ref[...], rs1_ref[...], rs2_ref[...]

    def mm(c0, width):
        return _dot(u, w_ref[:, c0:c0 + width])

    def rope(z):
        return z * rc + pltpu.roll(z, LANE - ROPE_HALF, 1) * rs1 + pltpu.roll(z, ROPE_HALF, 1) * rs2

    su_ref[...] = mm(C_SU, 256)
    mq_ref[...] = mm(C_MQ, 256)
    mk_ref[...] = mm(C_MK, 256)
    mv_ref[...] = mm(C_MV, 256).astype(BF16)
    mo_ref[...] = mm(C_MO, 256)
    for j in range(NSA_HEADS // 2):
        z = rope(mm(C_AQ + LANE * j, LANE)) * (NSA_HEAD_DIM ** -0.5)
        aq_ref[2 * j] = z[:, :NSA_HEAD_DIM].astype(BF16)
        aq_ref[2 * j + 1] = z[:, NSA_HEAD_DIM:].astype(BF16)
    ckv_ref[0] = rope(mm(C_CK, LANE))
    ckv_ref[1] = mm(C_CV, LANE)
    for ref, c0, rot in ((sk_ref, C_SK, True), (wk_ref, C_WK, True), (sv_ref, C_SV, False), (wv_ref, C_WV, False)):
        z = mm(c0, LANE)
        if rot:
            z = rope(z)
        ref[0] = z[:, :NSA_HEAD_DIM].astype(BF16)
        ref[1] = z[:, NSA_HEAD_DIM:].astype(BF16)
    gt_ref[0] = mm(C_G0, LANE)
    gt_ref[1] = mm(C_G1, LANE)


def _inproj(h2, gain, w, layer, rc, rs1, rs2, B, S, ts):
    nt = S // ts
    BS = B * S
    row = lambda b, i: (b * nt + i, 0)
    full = lambda b, i: (0, 0)
    headed = lambda b, i: (b, 0, i, 0)
    paired = lambda b, i: (0, b * nt + i, 0)
    in_specs = [
        pl.BlockSpec((ts, D_MODEL), row),
        pl.BlockSpec((1, D_MODEL), full),
        pl.BlockSpec((None, D_MODEL, D_INP), lambda b, i: (layer, 0, 0)),
        pl.BlockSpec((ts, LANE), row),
        pl.BlockSpec((ts, LANE), row),
        pl.BlockSpec((ts, LANE), row),
    ]
    kv_shape = jax.ShapeDtypeStruct((B, NSA_KV_GROUPS, S, NSA_HEAD_DIM), BF16)
    kv_spec = pl.BlockSpec((None, NSA_KV_GROUPS, ts, NSA_HEAD_DIM), headed)
    out_shape = [
        jax.ShapeDtypeStruct((BS, SSM_WIDTH), F32),
        jax.ShapeDtypeStruct((BS, MLSTM_WIDTH), F32),
        jax.ShapeDtypeStruct((BS, MLSTM_WIDTH), F32),
        jax.ShapeDtypeStruct((BS, MLSTM_WIDTH), BF16),
        jax.ShapeDtypeStruct((BS, MLSTM_WIDTH), F32),
        jax.ShapeDtypeStruct((B, NSA_HEADS, S, NSA_HEAD_DIM), BF16),
        jax.ShapeDtypeStruct((2, BS, NSA_KV_WIDTH), F32),
        kv_shape, kv_shape,
        kv_shape, kv_shape,
        jax.ShapeDtypeStruct((NSA_KV_GROUPS, BS, LANE), F32),
    ]
    out_specs = [
        pl.BlockSpec((ts, SSM_WIDTH), row),
        pl.BlockSpec((ts, MLSTM_WIDTH), row),
        pl.BlockSpec((ts, MLSTM_WIDTH), row),
        pl.BlockSpec((ts, MLSTM_WIDTH), row),
        pl.BlockSpec((ts, MLSTM_WIDTH), row),
        pl.BlockSpec((None, NSA_HEADS, ts, NSA_HEAD_DIM), headed),
        pl.BlockSpec((2, ts, NSA_KV_WIDTH), paired),
        kv_spec, kv_spec,
        kv_spec, kv_spec,
        pl.BlockSpec((NSA_KV_GROUPS, ts, LANE), paired),
    ]
    return pl.pallas_call(
        _inproj_kernel,
        grid=(B, nt),
        in_specs=in_specs,
        out_specs=out_specs,
        out_shape=out_shape,
        compiler_params=pltpu.CompilerParams(
            dimension_semantics=("parallel", "parallel"), vmem_limit_bytes=VMEM_LIMIT),
        name="inproj",
    )(h2, gain, w, rc, rs1, rs2)


def _s5_kernel(u_ref, bb_ref, a_ref, cc_ref, d_ref, wg_ref, gm_ref, gain_ref, o_ref, x_sc, st_sc, tm_sc, *, B, ts):
    @pl.when(pl.program_id(0) == 0)
    def _():
        st_sc[...] = jnp.zeros_like(st_sc)

    nl = SSM_WIDTH // LANE
    for b in range(B):
        for c in range(nl):
            tm_sc[c, pl.ds(b, ts, stride=B), :] = u_ref[b, :, c * LANE:(c + 1) * LANE]
    u = jnp.concatenate([tm_sc[c] for c in range(nl)], axis=1)
    x_sc[...] = _dot(u.astype(BF16), bb_ref[...])
    ar = jnp.broadcast_to(a_ref[0:1, :], (B, SSM_LANES))
    ai = jnp.broadcast_to(a_ref[1:2, :], (B, SSM_LANES))

    def step(t, carry):
        xr, xi = carry
        r = pl.multiple_of(t * B, B)
        br = x_sc[pl.ds(r, B), 0:SSM_LANES]
        bi = x_sc[pl.ds(r, B), SSM_LANES:2 * SSM_LANES]
        nr = ar * xr - ai * xi + br
        ni = ar * xi + ai * xr + bi
        x_sc[pl.ds(r, B), 0:SSM_LANES] = nr
        x_sc[pl.ds(r, B), SSM_LANES:2 * SSM_LANES] = ni
        return nr, ni

    xr, xi = lax.fori_loop(0, ts, step, (st_sc[0], st_sc[1]))
    st_sc[0] = xr
    st_sc[1] = xi

    y = _dot(x_sc[...].astype(BF16), cc_ref[...]) + d_ref[...] * u
    y = _gelu_tanh(y)
    y = y * _sigmoid(_dot(y.astype(BF16), wg_ref[...]))
    ms = _dot_split(y * y, gm_ref[...])
    y = y * lax.rsqrt(ms + EPS) * gain_ref[...]
    for c in range(nl):
        tm_sc[c] = y[:, c * LANE:(c + 1) * LANE]
    for b in range(B):
        o_ref[b] = jnp.concatenate(
            [tm_sc[c, pl.ds(b, ts, stride=B), :] for c in range(nl)], axis=1).astype(BF16)


def _s5(u, bb, a, cc, d, wg, layer, gm, gain, B, S, ts):
    rows = ts * B
    full = lambda i: (0, 0)
    lsel = lambda i: (layer, 0, 0)
    return pl.pallas_call(
        functools.partial(_s5_kernel, B=B, ts=ts),
        grid=(S // ts,),
        in_specs=[
            pl.BlockSpec((B, ts, SSM_WIDTH), lambda i: (0, i, 0)),
            pl.BlockSpec((None, SSM_WIDTH, 2 * SSM_LANES), lsel),
            pl.BlockSpec((None, 2, SSM_LANES), lsel),
            pl.BlockSpec((None, 2 * SSM_LANES, SSM_WIDTH), lsel),
            pl.BlockSpec((1, SSM_WIDTH), full),
            pl.BlockSpec((None, SSM_WIDTH, SSM_WIDTH), lsel),
            pl.BlockSpec((SSM_WIDTH, SSM_WIDTH), full),
            pl.BlockSpec((1, SSM_WIDTH), full),
        ],
        out_specs=pl.BlockSpec((B, ts, SSM_WIDTH), lambda i: (0, i, 0)),
        out_shape=jax.ShapeDtypeStruct((B, S, SSM_WIDTH), BF16),
        scratch_shapes=[pltpu.VMEM((rows, 2 * SSM_LANES), F32), pltpu.VMEM((2, B, SSM_LANES), F32),
                        pltpu.VMEM((SSM_WIDTH // LANE, rows, LANE), F32)],
        compiler_params=pltpu.CompilerParams(
            dimension_semantics=("arbitrary",), vmem_limit_bytes=VMEM_LIMIT),
        name="s5",
    )(u, bb, a, cc, d, wg, gm, gain)


def _mlstm_kernel(q_ref, k_ref, v_ref, o_ref, gc_ref, gr_ref, cwq_ref, cwk_ref, bc_ref, br_ref, hm_ref,
                  gain_ref, y_ref, qt_sc, kt_sc, c_sc, m_sc, *, B):
    L, H, Dh, W = MLSTM_CHUNK, MLSTM_HEADS, MLSTM_HEAD_DIM, MLSTM_WIDTH

    @pl.when(pl.program_id(0) == 0)
    def _():
        qt_sc[...] = jnp.zeros_like(qt_sc)
        kt_sc[...] = jnp.zeros_like(kt_sc)
        c_sc[...] = jnp.zeros_like(c_sc)
        m_sc[...] = jnp.zeros_like(m_sc)

    row_l = lax.broadcasted_iota(jnp.int32, (L, L), 0)
    col_l = lax.broadcasted_iota(jnp.int32, (L, L), 1)
    causal = col_l <= row_l
    tri = causal.astype(F32)
    triu = (row_l <= col_l).astype(F32)
    lane_w = lax.broadcasted_iota(jnp.int32, (1, W), 1) // Dh
    lane_2w = (lax.broadcasted_iota(jnp.int32, (1, 2 * W), 1) % W) // Dh
    bd_mask = (lax.broadcasted_iota(jnp.int32, (W, 2 * W), 0) // Dh
               == (lax.broadcasted_iota(jnp.int32, (W, 2 * W), 1) % W) // Dh)
    row8 = lax.broadcasted_iota(jnp.int32, (SUBLANE, W), 0)
    cwq = cwq_ref[...]
    cwk = cwk_ref[...]
    ones_v = jnp.ones((L, W), BF16)

    def conv_silu(x, tail, w):
        acc = x * w[MLSTM_CONV - 1:MLSTM_CONV, :]
        for sft in range(1, MLSTM_CONV):
            xs = pltpu.roll(x, sft, 0)
            head = jnp.where(row8 < sft, pltpu.roll(tail, sft, 0), xs[:SUBLANE])
            xs = jnp.concatenate([head, xs[SUBLANE:]], axis=0)
            acc = acc + xs * w[MLSTM_CONV - 1 - sft:MLSTM_CONV - sft, :]
        return acc * _sigmoid(acc)

    def expand(cols, width_lanes):
        out = cols[H - 1]
        for hh in range(H - 2, -1, -1):
            out = jnp.where(width_lanes == hh, cols[hh], out)
        return out

    def per_batch(b, _):
        q_raw = q_ref[b]
        k_raw = k_ref[b]
        q = conv_silu(q_raw, qt_sc[b], cwq)
        k = conv_silu(k_raw, kt_sc[b], cwk) * (Dh ** -0.5)
        qt_sc[b] = q_raw[L - SUBLANE:, :]
        kt_sc[b] = k_raw[L - SUBLANE:, :]
        vaug = jnp.concatenate([v_ref[b], ones_v], axis=1)

        gc = gc_ref[b] + bc_ref[...]
        gr = gr_ref[b] + br_ref[...]
        bcol = _dot(tri, _log_sigmoid(gc), precision=HIGHEST)
        brow = _dot(_log_sigmoid(gr), triu, precision=HIGHEST)
        m_all = m_sc[b]

        w_intra, w_inter, e_mt, w_k, dec, m_new = [], [], [], [], [], []
        for hh in range(H):
            bc = bcol[:, H + hh:H + hh + 1]
            ic = gc[:, hh:hh + 1]
            brr = brow[H + hh:H + hh + 1, :]
            irr = gr[hh:hh + 1, :]
            m_prev = m_all[hh:hh + 1, 0:1]
            dm = jnp.where(causal, bc - brr + irr, NEG_INF)
            inter = bc + m_prev
            mt = jnp.maximum(inter, jnp.max(dm, axis=1, keepdims=True))
            w_intra.append(jnp.exp(dm - mt))
            w_inter.append(jnp.exp(inter - mt))
            e_mt.append(jnp.exp(-mt))
            b_last = bc[L - 1:L, :]
            logw = b_last - bc + ic
            mn = jnp.maximum(b_last + m_prev, jnp.max(logw, axis=0, keepdims=True))
            w_k.append(jnp.exp(logw - mn))
            dec.append(jnp.exp(b_last + m_prev - mn))
            m_new.append(mn)

        qb = q.astype(BF16)
        kb = k.astype(BF16)
        q_heads = jnp.concatenate([jnp.where(lane_w == hh, qb, jnp.zeros_like(qb)) for hh in range(H)], axis=0)
        s = _dot_nt(q_heads, kb) * jnp.concatenate(w_intra, axis=0)
        sv = _dot(s.astype(BF16), vaug)
        intra = jnp.where(lane_2w == 0, sv[0:L], 0.0)
        for hh in range(1, H):
            intra = intra + jnp.where(lane_2w == hh, sv[hh * L:(hh + 1) * L], 0.0)
        c_aug = c_sc[b]
        qc = _dot(qb, c_aug.astype(BF16))
        tot = expand(w_inter, lane_2w) * qc + intra
        den = jnp.maximum(jnp.abs(tot[:, W:]), expand(e_mt, lane_w))
        hout = tot[:, :W] / den
        y = _sigmoid(o_ref[b]) * hout
        ms = _dot_split(y * y, hm_ref[...])
        y_ref[b] = (y * lax.rsqrt(ms + EPS) * gain_ref[...]).astype(BF16)

        kw = (k * expand(w_k, lane_w)).astype(BF16)
        upd = _dot_tn(kw, vaug)
        c_sc[b] = expand(dec, lane_2w) * c_aug + jnp.where(bd_mask, upd, 0.0)
        for hh in range(H):
            m_sc[b, hh:hh + 1, :] = jnp.broadcast_to(m_new[hh], (1, LANE))
        return 0

    lax.fori_loop(0, B, per_batch, 0)


def _mlstm(mq, mk, mv, mo, gcol, grow, cwq, cwk, bcol, brow, hm, gain, B, S):
    L, W = MLSTM_CHUNK, MLSTM_WIDTH
    seq = lambda c: (0, c, 0)
    full = lambda c: (0, 0)
    return pl.pallas_call(
        functools.partial(_mlstm_kernel, B=B),
        grid=(S // L,),
        in_specs=[
            pl.BlockSpec((B, L, W), seq),
            pl.BlockSpec((B, L, W), seq),
            pl.BlockSpec((B, L, W), seq),
            pl.BlockSpec((B, L, W), seq),
            pl.BlockSpec((B, L, LANE), seq),
            pl.BlockSpec((B, SUBLANE, L), lambda c: (0, 0, c)),
            pl.BlockSpec((MLSTM_CONV, W), full),
            pl.BlockSpec((MLSTM_CONV, W), full),
            pl.BlockSpec((1, LANE), full),
            pl.BlockSpec((SUBLANE, 1), full),
            pl.BlockSpec((W, W), full),
            pl.BlockSpec((1, W), full),
        ],
        out_specs=pl.BlockSpec((B, L, W), seq),
        out_shape=jax.ShapeDtypeStruct((B, S, W), BF16),
        scratch_shapes=[
            pltpu.VMEM((B, SUBLANE, W), F32),
            pltpu.VMEM((B, SUBLANE, W), F32),
            pltpu.VMEM((B, W, 2 * W), F32),
            pltpu.VMEM((B, SUBLANE, LANE), F32),
        ],
        compiler_params=pltpu.CompilerParams(
            dimension_semantics=("arbitrary",), vmem_limit_bytes=VMEM_LIMIT),
        name="mlstm",
    )(mq, mk, mv, mo, gcol, grow, cwq, cwk, bcol, brow, hm, gain)


def _compress_kernel(c_ref, w1ab_ref, w1_ref, pe_ref, w2_ref, w2t_ref, o_ref, ot_ref, ch_sc):
    G, Dh = NSA_KV_GROUPS, NSA_HEAD_DIM
    rows = ch_sc.shape[0]
    n = rows // G
    for r in range(CMP_STRIDE):
        tok = c_ref[pl.ds(r, n, stride=CMP_STRIDE), :]
        for g in range(G):
            ch_sc[g * n:(g + 1) * n, r * Dh:(r + 1) * Dh] = tok[:, g * Dh:(g + 1) * Dh]
    ab = _dot(ch_sc[...].astype(BF16), w1ab_ref[...])
    const = _dot(pe_ref[...], w1_ref[...], precision=HIGHEST)
    hid = ab[:, :CMP_HIDDEN] + pltpu.roll(ab[:, CMP_HIDDEN:], rows - 1, 0) + const
    act = _gelu_tanh(hid).astype(BF16)
    o_ref[...] = _dot(act, w2_ref[...]).astype(BF16)
    ot_ref[...] = _dot_nt(w2t_ref[...], act).astype(BF16)


def _compress(ckv, w1ab, w1, pe, w2, w2t, layer, B, S):
    G, Dh = NSA_KV_GROUPS, NSA_HEAD_DIM
    n = S // CMP_STRIDE
    width = CMP_STRIDE * Dh
    wsel = lambda i, b: (layer, i, 0, 0)
    return pl.pallas_call(
        _compress_kernel,
        grid=(2, B),
        in_specs=[
            pl.BlockSpec((None, S, G * Dh), lambda i, b: (i, b, 0)),
            pl.BlockSpec((None, None, width, 2 * CMP_HIDDEN), wsel),
            pl.BlockSpec((None, None, 2 * width, CMP_HIDDEN), wsel),
            pl.BlockSpec((None, None, 1, 2 * width), wsel),
            pl.BlockSpec((None, None, CMP_HIDDEN, Dh), wsel),
            pl.BlockSpec((None, None, Dh, CMP_HIDDEN), wsel),
        ],
        out_specs=[pl.BlockSpec((None, None, G * n, Dh), lambda i, b: (i, b, 0, 0)),
                   pl.BlockSpec((None, None, Dh, G * n), lambda i, b: (i, b, 0, 0))],
        out_shape=[jax.ShapeDtypeStruct((2, B, G * n, Dh), BF16),
                   jax.ShapeDtypeStruct((2, B, Dh, G * n), BF16)],
        scratch_shapes=[pltpu.VMEM((G * n, width), F32)],
        compiler_params=pltpu.CompilerParams(
            dimension_semantics=("parallel", "parallel"), vmem_limit_bytes=VMEM_LIMIT),
        name="compress",
    )(ckv, w1ab, w1, pe, w2, w2t)


def _nsa_kernel(q_ref, kc_ref, vct_ref, ks_ref, vst_ref, kw_ref, vwt_ref, gtt_ref, gain_ref,
                ovt_ref, et_ref, o_ref, *, n_sel, n_top, ck):
    TQ, R, Dh = Q_BLOCK, NSA_REP, NSA_HEAD_DIM
    i = pl.program_id(2)
    t0 = i * TQ
    q = q_ref[...].reshape(R * TQ, Dh)
    tq1 = t0 + lax.broadcasted_iota(jnp.int32, (1, TQ), 1)
    heads = lambda t: jnp.concatenate([t] * R, axis=1)

    ncmp = kc_ref.shape[0]
    cend = lax.broadcasted_iota(jnp.int32, (ncmp, 1), 0) * CMP_STRIDE + (CMP_BLOCK - 1)
    cmask = heads(cend <= tq1)
    scm = jnp.where(cmask, _dot_nt(kc_ref[...], q), NEG_INF)
    ec = jnp.where(cmask, jnp.exp(scm - jnp.max(scm, axis=0, keepdims=True)), 0.0)
    pc = ec * (1.0 / jnp.maximum(jnp.sum(ec, axis=0, keepdims=True), 1e-30))
    oc = _dot(vct_ref[...], pc.astype(BF16))

    psum = pc[:, 0:TQ]
    for r in range(1, R):
        psum = psum + pc[:, r * TQ:(r + 1) * TQ]
    imp = _dot(ovt_ref[...], psum, precision=HIGHEST)
    blk = lax.broadcasted_iota(jnp.int32, (n_sel, 1), 0)
    valid = blk * SEL_BLOCK <= tq1
    forced = (blk == 0) | (blk == tq1 // SEL_BLOCK)
    imp = jnp.where(forced, FORCE_SCORE, jnp.where(valid, imp, -FORCE_SCORE))
    rank = jnp.zeros((n_sel, TQ), F32)
    for jp in range(n_sel):
        other = imp[jp:jp + 1, :]
        wins = jnp.where(blk > jp, jnp.where(other >= imp, 1.0, 0.0), jnp.where(other > imp, 1.0, 0.0))
        rank = rank + wins
    selb = jnp.where(rank < n_top, 0.0, NEG_INF).astype(BF16)

    nwb = WINDOW // TQ + 1
    wb0 = jnp.maximum(i - WINDOW // TQ, 0)
    ws = pl.multiple_of(wb0 * TQ, TQ)
    kpos = ws + lax.broadcasted_iota(jnp.int32, (nwb * TQ, 1), 0)
    wbias = jnp.where((kpos <= tq1) & (tq1 - kpos < WINDOW), 0.0, NEG_INF)
    sw = _dot_nt(kw_ref[pl.ds(ws, nwb * TQ), :], q) + heads(wbias)
    pw = jnp.exp(sw - jnp.max(sw, axis=0, keepdims=True))
    vwt = jnp.concatenate([vwt_ref[wb0 + j] for j in range(nwb)], axis=1)
    ow = _dot(vwt, pw.astype(BF16))
    l_w = jnp.sum(pw, axis=0, keepdims=True)

    def chunk(c, carry):
        m, l, acc = carry
        k0 = pl.multiple_of(c * ck, ck)
        kpos = k0 + lax.broadcasted_iota(jnp.int32, (ck, 1), 0)
        bias = jnp.where(kpos <= tq1, _dot(et_ref[c], selb), NEG_INF)
        s = _dot_nt(ks_ref[pl.ds(k0, ck), :], q) + heads(bias)
        mn = jnp.maximum(m, jnp.max(s, axis=0, keepdims=True))
        alpha = jnp.exp(m - mn)
        p = jnp.exp(s - mn)
        l = alpha * l + jnp.sum(p, axis=0, keepdims=True)
        acc = alpha * acc + _dot(vst_ref[c], p.astype(BF16))
        return mn, l, acc

    init = (jnp.full((1, R * TQ), NEG_INF, F32), jnp.zeros((1, R * TQ), F32), jnp.zeros((Dh, R * TQ), F32))
    _, l_s, acc_s = lax.fori_loop(0, (t0 + TQ + ck - 1) // ck, chunk, init)

    gs = _sigmoid(gtt_ref[...])
    for r in range(R):
        ln = slice(r * TQ, (r + 1) * TQ)
        o = (gs[3 * r:3 * r + 1, :] * oc[:, ln]
             + (gs[3 * r + 1:3 * r + 2, :] / l_s[:, ln]) * acc_s[:, ln]
             + (gs[3 * r + 2:3 * r + 3, :] / l_w[:, ln]) * ow[:, ln])
        ms = jnp.mean(o * o, axis=0, keepdims=True)
        o_ref[r] = jnp.transpose(o * lax.rsqrt(ms + EPS) * gain_ref[r]).astype(BF16)


def _nsa(aq, cmp_k, cmp_vt, ks, vst, kw, vwt, gates_t, gain, consts, B, S):
    G, R, TQ, Dh = NSA_KV_GROUPS, NSA_REP, Q_BLOCK, NSA_HEAD_DIM
    nq = S // TQ
    ncmp = S // CMP_STRIDE
    n_sel = S // SEL_BLOCK
    ovt, emat_t, ck = consts
    k_spec = pl.BlockSpec((None, None, S, Dh), lambda b, g, i: (b, g, 0, 0))
    return pl.pallas_call(
        functools.partial(_nsa_kernel, n_sel=n_sel, n_top=min(SEL_TOPN, n_sel), ck=ck),
        grid=(B, G, nq),
        in_specs=[
            pl.BlockSpec((None, R, TQ, Dh), lambda b, g, i: (b, g, i, 0)),
            pl.BlockSpec((None, None, ncmp, Dh), lambda b, g, i: (0, b, g, 0)),
            pl.BlockSpec((None, None, Dh, ncmp), lambda b, g, i: (1, b, 0, g)),
            k_spec,
            pl.BlockSpec((None, None, S // ck, Dh, ck), lambda b, g, i: (b, g, 0, 0, 0)),
            k_spec,
            pl.BlockSpec((None, None, S // TQ, Dh, TQ), lambda b, g, i: (b, g, 0, 0, 0)),
            pl.BlockSpec((None, None, 2 * SUBLANE, TQ), lambda b, g, i: (g, b, 0, i)),
            pl.BlockSpec((R, Dh, 1), lambda b, g, i: (g, 0, 0)),
            pl.BlockSpec(ovt.shape, lambda b, g, i: (0, 0)),
            pl.BlockSpec(emat_t.shape, lambda b, g, i: (0, 0, 0)),
        ],
        out_specs=pl.BlockSpec((None, R, TQ, Dh), lambda b, g, i: (b, g, i, 0)),
        out_shape=jax.ShapeDtypeStruct((B, NSA_HEADS, S, Dh), BF16),
        compiler_params=pltpu.CompilerParams(
            dimension_semantics=("parallel", "parallel", "arbitrary"), vmem_limit_bytes=VMEM_LIMIT),
        name="nsa",
    )(aq, cmp_k, cmp_vt, ks, vst, kw, vwt, gates_t, gain, ovt, emat_t)


def _nsa_consts(S):
    n_cmp = S // CMP_STRIDE
    n_sel = S // SEL_BLOCK
    ck = 256
    i = np.arange(n_cmp)[:, None]
    j = np.arange(n_sel)[None, :]
    lo = np.maximum(i * CMP_STRIDE, j * SEL_BLOCK)
    hi = np.minimum(i * CMP_STRIDE + CMP_BLOCK, (j + 1) * SEL_BLOCK)
    ov = np.maximum(hi - lo, 0) / CMP_STRIDE
    ov[n_cmp - 1] = 0.0
    key = np.arange(S)
    emat_t = (key[:, None] // SEL_BLOCK == np.arange(n_sel)[None, :]).astype(np.float32)
    return (jnp.asarray(ov.T, F32), jnp.asarray(emat_t.reshape(S // ck, ck, n_sel), BF16), ck)


def _chunked_t(v, width):
    B, G, S, Dh = v.shape
    return v.reshape(B, G, S // width, width, Dh).swapaxes(-1, -2)


def _outproj_kernel(h_ref, ys_ref, ym_ref, yn_ref, w_ref, g_ref, o_ref):
    acc = _dot(ys_ref[...], w_ref[0:SSM_WIDTH, :])
    acc = acc + _dot(ym_ref[...], w_ref[SSM_WIDTH:SSM_WIDTH + MLSTM_WIDTH, :])
    base = SSM_WIDTH + MLSTM_WIDTH
    for hd in range(NSA_HEADS):
        acc = acc + _dot(yn_ref[hd], w_ref[base + hd * NSA_HEAD_DIM:base + (hd + 1) * NSA_HEAD_DIM, :])
    ms = jnp.mean(acc * acc, axis=-1, keepdims=True)
    o_ref[...] = h_ref[...] + acc * lax.rsqrt(ms + EPS) * g_ref[...]


def _outproj(h2, y_ssm, y_mls, y_nsa, w, layer, gain, B, S, ts):
    nt = S // ts
    row = lambda b, i: (b * nt + i, 0)
    full = lambda b, i: (0, 0)
    return pl.pallas_call(
        _outproj_kernel,
        grid=(B, nt),
        in_specs=[
            pl.BlockSpec((ts, D_MODEL), row),
            pl.BlockSpec((ts, SSM_WIDTH), row),
            pl.BlockSpec((ts, MLSTM_WIDTH), row),
            pl.BlockSpec((None, NSA_HEADS, ts, NSA_HEAD_DIM), lambda b, i: (b, 0, i, 0)),
            pl.BlockSpec((None, D_MODEL, D_MODEL), lambda b, i: (layer, 0, 0)),
            pl.BlockSpec((1, D_MODEL), full),
        ],
        out_specs=pl.BlockSpec((ts, D_MODEL), row),
        out_shape=jax.ShapeDtypeStruct((B * S, D_MODEL), F32),
        compiler_params=pltpu.CompilerParams(
            dimension_semantics=("parallel", "parallel"), vmem_limit_bytes=VMEM_LIMIT),
        name="outproj",
    )(h2, y_ssm, y_mls, y_nsa, w, gain)


def _mlp_kernel(h_ref, g1_ref, w1_ref, w2_ref, g2_ref, o_ref, u_sc, acc_sc):
    kf = pl.program_id(1)

    @pl.when(kf == 0)
    def _():
        x = h_ref[...]
        ms = jnp.mean(x * x, axis=-1, keepdims=True)
        u_sc[...] = (x * lax.rsqrt(ms + EPS) * g1_ref[...]).astype(BF16)
        acc_sc[...] = jnp.zeros_like(acc_sc)

    a = jnp.maximum(_dot(u_sc[...], w1_ref[...]), 0.0)
    acc_sc[...] += _dot((a * a).astype(BF16), w2_ref[...])

    @pl.when(kf == pl.num_programs(1) - 1)
    def _():
        f = acc_sc[...]
        ms = jnp.mean(f * f, axis=-1, keepdims=True)
        o_ref[...] = h_ref[...] + f * lax.rsqrt(ms + EPS) * g2_ref[...]


def _mlp(h2, g1, w1, w2, layer, g2, tm, tf):
    rows = h2.shape[0]
    return pl.pallas_call(
        _mlp_kernel,
        grid=(rows // tm, D_FF // tf),
        in_specs=[
            pl.BlockSpec((tm, D_MODEL), lambda i, k: (i, 0)),
            pl.BlockSpec((1, D_MODEL), lambda i, k: (0, 0)),
            pl.BlockSpec((None, D_MODEL, tf), lambda i, k: (layer, 0, k)),
            pl.BlockSpec((None, tf, D_MODEL), lambda i, k: (layer, k, 0)),
            pl.BlockSpec((1, D_MODEL), lambda i, k: (0, 0)),
        ],
        out_specs=pl.BlockSpec((tm, D_MODEL), lambda i, k: (i, 0)),
        out_shape=jax.ShapeDtypeStruct((rows, D_MODEL), F32),
        scratch_shapes=[pltpu.VMEM((tm, D_MODEL), BF16), pltpu.VMEM((tm, D_MODEL), F32)],
        compiler_params=pltpu.CompilerParams(
            dimension_semantics=("parallel", "arbitrary"), vmem_limit_bytes=VMEM_LIMIT),
        name="mlp",
    )(h2, g1, w1, w2, g2)


def _inproj_pieces():
    return ((0, 1280), (1288, 1800), (1800, 1928), (2056, 2184), (2312, 2440),
            (1928, 2056), (2184, 2312), (2440, 2568),
            (1280, 1288), (2568, 2580), (None, LANE - 20),
            (None, GATE_COL), (2580, 2592), (None, LANE - GATE_COL - 12))


def _permute_w_in(w_in):
    parts = []
    for a, b in _inproj_pieces():
        if a is None:
            parts.append(jnp.zeros(w_in.shape[:-1] + (b,), BF16))
        else:
            parts.append(w_in[..., a:b].astype(BF16))
    out = jnp.concatenate(parts, axis=-1)
    assert out.shape[-1] == D_INP
    return out


def _rope_tables(positions):
    inv = ROPE_THETA ** (-jnp.arange(0, ROPE_DIMS, 2, dtype=F32) / ROPE_DIMS)
    ang = positions.astype(F32)[..., None] * inv
    cos, sin = jnp.cos(ang), jnp.sin(ang)
    z = jnp.zeros_like(cos)
    rest = NSA_HEAD_DIM - ROPE_DIMS
    pad_one = jnp.ones(cos.shape[:-1] + (rest,), F32)
    pad_zero = jnp.zeros(cos.shape[:-1] + (rest,), F32)
    rc = jnp.concatenate([cos, cos, pad_one], axis=-1)
    rs1 = jnp.concatenate([-sin, z, pad_zero], axis=-1)
    rs2 = jnp.concatenate([z, sin, pad_zero], axis=-1)
    tile = lambda t: jnp.tile(t, (1, 1, LANE // NSA_HEAD_DIM)).reshape(-1, LANE)
    return tile(rc), tile(rs1), tile(rs2)


def _s5_params(lam_re, lam_im, b_re, b_im, c_re, c_im, log_dt):
    G, P, Hc = SSM_GROUPS, SSM_STATE, SSM_GROUP
    dt = jnp.exp(log_dt)[:, None]
    mag = jnp.exp(lam_re * dt)
    ang = lam_im * dt
    ab_re = mag * jnp.cos(ang)
    ab_im = mag * jnp.sin(ang)
    den = lam_re * lam_re + lam_im * lam_im
    g_re = ((ab_re - 1.0) * lam_re + ab_im * lam_im) / den
    g_im = (ab_im * lam_re - (ab_re - 1.0) * lam_im) / den
    bb_re = g_re[..., None] * b_re - g_im[..., None] * b_im
    bb_im = g_re[..., None] * b_im + g_im[..., None] * b_re
    eye = jnp.eye(G, dtype=F32)
    blockdiag_in = lambda t: jnp.einsum('gph,gk->ghkp', t, eye).reshape(G * Hc, G * P)
    blockdiag_out = lambda t: jnp.einsum('ghp,gk->gpkh', t, eye).reshape(G * P, G * Hc)
    bb = jnp.concatenate([blockdiag_in(bb_re), blockdiag_in(bb_im)], axis=1).astype(BF16)
    cc = jnp.concatenate([blockdiag_out(c_re), -blockdiag_out(c_im)], axis=0).astype(BF16)
    a = jnp.stack([ab_re.reshape(-1), ab_im.reshape(-1)], axis=0)
    return bb, a, cc


def _group_mean_matrix(width, group):
    idx = np.arange(width) // group
    return jnp.asarray((idx[:, None] == idx[None, :]).astype(np.float32) / group, BF16)


def kernel(x, positions, ln_mix_pre, ln_mix_post, ln_mlp_pre, ln_mlp_post, w_in, w_out, ssm_lambda_re, ssm_lambda_im, ssm_b_re, ssm_b_im, ssm_c_re, ssm_c_im, ssm_d, ssm_log_dt, ssm_w_glu, mlstm_conv, mlstm_b_i, mlstm_b_f, cmp_pe_k, cmp_w1_k, cmp_w2_k, cmp_pe_v, cmp_w1_v, cmp_w2_v, gn_ssm, gn_mlstm, gn_nsa, mlp_w1, mlp_w2):
    B, S, D = x.shape
    depth = w_in.shape[0]
    assert D == D_MODEL and B == SUBLANE and S % 512 == 0 and S >= WINDOW + Q_BLOCK
    G, H = NSA_KV_GROUPS, MLSTM_HEADS
    ts_proj = 512
    ts_scan = 128

    rc, rs1, rs2 = _rope_tables(positions)
    w_in_p = _permute_w_in(w_in)
    w_out_b = w_out.astype(BF16)
    w1_b = mlp_w1.astype(BF16)
    w2_b = mlp_w2.astype(BF16)
    wglu_b = ssm_w_glu.astype(BF16)
    gm_ssm = _group_mean_matrix(SSM_WIDTH, SSM_GROUP)
    hm_mls = _group_mean_matrix(MLSTM_WIDTH, MLSTM_HEAD_DIM)
    consts = _nsa_consts(S)
    half = CMP_STRIDE * NSA_HEAD_DIM
    w1ab = jnp.stack([jnp.concatenate([cmp_w1_k[:, :half], cmp_w1_k[:, half:]], axis=-1),
                      jnp.concatenate([cmp_w1_v[:, :half], cmp_w1_v[:, half:]], axis=-1)], axis=1).astype(BF16)
    w1f = jnp.stack([cmp_w1_k, cmp_w1_v], axis=1)
    pef = jnp.stack([cmp_pe_k.reshape(depth, 1, -1), cmp_pe_v.reshape(depth, 1, -1)], axis=1)
    w2c = jnp.stack([cmp_w2_k, cmp_w2_v], axis=1).astype(BF16)
    w2ct = jnp.swapaxes(w2c, -1, -2)
    zeros_g = jnp.zeros((depth, LANE - 2 * H), F32)
    bias_col = jnp.concatenate([mlstm_b_i, mlstm_b_f, zeros_g], axis=-1)[:, None, :]
    bias_row = jnp.concatenate([mlstm_b_i, mlstm_b_f], axis=-1)[:, :, None]

    bb, a, cc = jax.vmap(_s5_params)(ssm_lambda_re, ssm_lambda_im, ssm_b_re, ssm_b_im, ssm_c_re, ssm_c_im,
                                     ssm_log_dt)
    sh3 = lambda t: t.reshape(B, S, t.shape[-1])

    h = x.reshape(B * S, D)
    for l in range(depth):
        (su, mq, mk, mv, mo, aq, ckv, sk, wk, sv, wv, gates) = _inproj(
            h, ln_mix_pre[l][None], w_in_p, l, rc, rs1, rs2, B, S, ts_proj)

        y_ssm = _s5(sh3(su), bb, a, cc, ssm_d[l][None], wglu_b, l, gm_ssm, gn_ssm[l][None], B, S, ts_scan)

        gcol = sh3(gates[0])
        grow = jnp.swapaxes(gcol[:, :, :2 * H], 1, 2)
        y_mls = _mlstm(sh3(mq), sh3(mk), sh3(mv), sh3(mo), gcol, grow,
                       mlstm_conv[l][:, :MLSTM_WIDTH], mlstm_conv[l][:, MLSTM_WIDTH:],
                       bias_col[l], bias_row[l], hm_mls, gn_mlstm[l][None], B, S)

        cmp_k, cmp_t = _compress(ckv, w1ab, w1f, pef, w2c, w2ct, l, B, S)
        gates_t = jnp.swapaxes(
            gates.reshape(G, B, S, LANE)[..., GATE_COL:GATE_COL + 2 * SUBLANE], -1, -2)
        y_nsa = _nsa(aq, cmp_k, cmp_t, sk, _chunked_t(sv, consts[2]), wk, _chunked_t(wv, Q_BLOCK), gates_t,
                     gn_nsa[l].reshape(NSA_HEADS, NSA_HEAD_DIM, 1), consts, B, S)

        h = _outproj(h, y_ssm.reshape(B * S, SSM_WIDTH), y_mls.reshape(B * S, MLSTM_WIDTH), y_nsa,
                     w_out_b, l, ln_mix_post[l][None], B, S, ts_proj)
        h = _mlp(h, ln_mlp_pre[l][None], w1_b, w2_b, l, ln_mlp_post[l][None], 1024, 512)
    return h.reshape(B, S, D)
```

```python
import functools
import math

import numpy as np
import jax
import jax.numpy as jnp
from jax import lax
from jax.experimental import pallas as pl
from jax.experimental.pallas import tpu as pltpu

F32 = jnp.float32
BF16 = jnp.bfloat16
HIGHEST = lax.Precision.HIGHEST

D_MODEL = 1024
DEPTH = 4
SSM_WIDTH = 256
SSM_GROUP = 16
SSM_GROUPS = 16
SSM_STATE = 64
SSM_LANES = SSM_GROUPS * SSM_STATE
MLSTM_WIDTH = 256
MLSTM_HEADS = 4
MLSTM_HEAD_DIM = 64
MLSTM_CHUNK = 128
MLSTM_CONV = 4
NSA_WIDTH = 512
NSA_HEAD_DIM = 64
NSA_HEADS = 8
NSA_KV_GROUPS = 2
NSA_REP = NSA_HEADS // NSA_KV_GROUPS
NSA_KV_WIDTH = 128
CMP_BLOCK = 32
CMP_STRIDE = 16
CMP_HIDDEN = 256
SEL_BLOCK = 64
SEL_TOPN = 8
WINDOW = 256
Q_BLOCK = 128
FORCE_SCORE = 1e4
NEG_INF = -1e30
ROPE_THETA = 500000.0
ROPE_DIMS = 16
ROPE_HALF = 8
D_FF = 4096
EPS = 1e-6
D_IN = 2592

LANE = 128
SUBLANE = 8
VMEM_LIMIT = 56 * 1024 * 1024

C_SU, C_MQ, C_MK, C_MV, C_MO = 0, 256, 512, 768, 1024
C_AQ, C_CK, C_SK, C_WK = 1280, 1792, 1920, 2048
C_CV, C_SV, C_WV = 2176, 2304, 2432
C_G0, C_G1 = 2560, 2688
D_INP = 2816
GATE_COL = 8
SEL_UNROLL = 2


def _dot(a, b, precision=None):
    return jnp.dot(a, b, preferred_element_type=F32, precision=precision)


def _dot_nt(a, b):
    return lax.dot_general(a, b, (((1,), (1,)), ((), ())), preferred_element_type=F32)


def _dot_tn(a, b):
    return lax.dot_general(a, b, (((0,), (0,)), ((), ())), preferred_element_type=F32)


def _sigmoid(x):
    return 1.0 / (1.0 + jnp.exp(-x))


def _dot_split(x, w_bf16):
    hi = x.astype(BF16)
    lo = (x - hi.astype(F32)).astype(BF16)
    return _dot(hi, w_bf16) + _dot(lo, w_bf16)


def _gelu_tanh(x):
    return 0.5 * x * (1.0 + jnp.tanh(math.sqrt(2.0 / math.pi) * (x + 0.044715 * (x * x * x))))


def _log_sigmoid(x):
    return jnp.minimum(x, 0.0) - jnp.log(1.0 + jnp.exp(-jnp.abs(x)))


def _inproj_kernel(x_ref, g_ref, w_ref, rc_ref, rs1_ref, rs2_ref,
                   su_ref, mq_ref, mk_ref, mv_ref, mo_ref, aq_ref, ckv_ref, sk_ref, wk_ref,
                   sv_ref, wv_ref, gt_ref):
    x = x_ref[...]
    ms = jnp.mean(x * x, axis=-1, keepdims=True)
    u = (x * lax.rsqrt(ms + EPS) * g_ref[...]).astype(BF16)
    rc, rs1, rs2 = rc_ref[...], rs1_ref[...], rs2_ref[...]

    def mm(c0, width):
        return _dot(u, w_ref[:, c0:c0 + width])

    def rope(z):
        return z * rc + pltpu.roll(z, LANE - ROPE_HALF, 1) * rs1 + pltpu.roll(z, ROPE_HALF, 1) * rs2

    su_ref[...] = mm(C_SU, 256)
    mq_ref[...] = mm(C_MQ, 256)
    mk_ref[...] = mm(C_MK, 256)
    mv_ref[...] = mm(C_MV, 256).astype(BF16)
    mo_ref[...] = mm(C_MO, 256)
    for j in range(NSA_HEADS // 2):
        z = rope(mm(C_AQ + LANE * j, LANE)) * (NSA_HEAD_DIM ** -0.5)
        aq_ref[2 * j] = z[:, :NSA_HEAD_DIM].astype(BF16)
        aq_ref[2 * j + 1] = z[:, NSA_HEAD_DIM:].astype(BF16)
    ckv_ref[0] = rope(mm(C_CK, LANE))
    ckv_ref[1] = mm(C_CV, LANE)
    for ref, c0, rot in ((sk_ref, C_SK, True), (wk_ref, C_WK, True), (sv_ref, C_SV, False), (wv_ref, C_WV, False)):
        z = mm(c0, LANE)
        if rot:
            z = rope(z)
        ref[0] = z[:, :NSA_HEAD_DIM].astype(BF16)
        ref[1] = z[:, NSA_HEAD_DIM:].astype(BF16)
    gt_ref[0] = mm(C_G0, LANE)
    gt_ref[1] = mm(C_G1, LANE)


def _inproj(h2, gain, w, layer, rc, rs1, rs2, B, S, ts):
    nt = S // ts
    BS = B * S
    row = lambda b, i: (b * nt + i, 0)
    full = lambda b, i: (0, 0)
    headed = lambda b, i: (b, 0, i, 0)
    paired = lambda b, i: (0, b * nt + i, 0)
    in_specs = [
        pl.BlockSpec((ts, D_MODEL), row),
        pl.BlockSpec((1, D_MODEL), full),
        pl.BlockSpec((None, D_MODEL, D_INP), lambda b, i: (layer, 0, 0)),
        pl.BlockSpec((ts, LANE), row),
        pl.BlockSpec((ts, LANE), row),
        pl.BlockSpec((ts, LANE), row),
    ]
    kv_shape = jax.ShapeDtypeStruct((B, NSA_KV_GROUPS, S, NSA_HEAD_DIM), BF16)
    kv_spec = pl.BlockSpec((None, NSA_KV_GROUPS, ts, NSA_HEAD_DIM), headed)
    out_shape = [
        jax.ShapeDtypeStruct((BS, SSM_WIDTH), F32),
        jax.ShapeDtypeStruct((BS, MLSTM_WIDTH), F32),
        jax.ShapeDtypeStruct((BS, MLSTM_WIDTH), F32),
        jax.ShapeDtypeStruct((BS, MLSTM_WIDTH), BF16),
        jax.ShapeDtypeStruct((BS, MLSTM_WIDTH), F32),
        jax.ShapeDtypeStruct((B, NSA_HEADS, S, NSA_HEAD_DIM), BF16),
        jax.ShapeDtypeStruct((2, BS, NSA_KV_WIDTH), F32),
        kv_shape, kv_shape,
        kv_shape, kv_shape,
        jax.ShapeDtypeStruct((NSA_KV_GROUPS, BS, LANE), F32),
    ]
    out_specs = [
        pl.BlockSpec((ts, SSM_WIDTH), row),
        pl.BlockSpec((ts, MLSTM_WIDTH), row),
        pl.BlockSpec((ts, MLSTM_WIDTH), row),
        pl.BlockSpec((ts, MLSTM_WIDTH), row),
        pl.BlockSpec((ts, MLSTM_WIDTH), row),
        pl.BlockSpec((None, NSA_HEADS, ts, NSA_HEAD_DIM), headed),
        pl.BlockSpec((2, ts, NSA_KV_WIDTH), paired),
        kv_spec, kv_spec,
        kv_spec, kv_spec,
        pl.BlockSpec((NSA_KV_GROUPS, ts, LANE), paired),
    ]
    return pl.pallas_call(
        _inproj_kernel,
        grid=(B, nt),
        in_specs=in_specs,
        out_specs=out_specs,
        out_shape=out_shape,
        compiler_params=pltpu.CompilerParams(
            dimension_semantics=("parallel", "parallel"), vmem_limit_bytes=VMEM_LIMIT),
        name="inproj",
    )(h2, gain, w, rc, rs1, rs2)


def _s5_kernel(u_ref, bb_ref, a_ref, cc_ref, d_ref, wg_ref, gm_ref, gain_ref, o_ref, x_sc, st_sc, tm_sc, *, B, ts):
    @pl.when(pl.program_id(0) == 0)
    def _():
        st_sc[...] = jnp.zeros_like(st_sc)

    nl = SSM_WIDTH // LANE
    for b in range(B):
        for c in range(nl):
            tm_sc[c, pl.ds(b, ts, stride=B), :] = u_ref[b, :, c * LANE:(c + 1) * LANE]
    u = jnp.concatenate([tm_sc[c] for c in range(nl)], axis=1)
    x_sc[...] = _dot(u.astype(BF16), bb_ref[...])
    ar = jnp.broadcast_to(a_ref[0:1, :], (B, SSM_LANES))
    ai = jnp.broadcast_to(a_ref[1:2, :], (B, SSM_LANES))

    def step(t, carry):
        xr, xi = carry
        r = pl.multiple_of(t * B, B)
        br = x_sc[pl.ds(r, B), 0:SSM_LANES]
        bi = x_sc[pl.ds(r, B), SSM_LANES:2 * SSM_LANES]
        nr = ar * xr - ai * xi + br
        ni = ar * xi + ai * xr + bi
        x_sc[pl.ds(r, B), 0:SSM_LANES] = nr
        x_sc[pl.ds(r, B), SSM_LANES:2 * SSM_LANES] = ni
        return nr, ni

    xr, xi = lax.fori_loop(0, ts, step, (st_sc[0], st_sc[1]))
    st_sc[0] = xr
    st_sc[1] = xi

    y = _dot(x_sc[...].astype(BF16), cc_ref[...]) + d_ref[...] * u
    y = _gelu_tanh(y)
    y = y * _sigmoid(_dot(y.astype(BF16), wg_ref[...]))
    ms = _dot_split(y * y, gm_ref[...])
    y = y * lax.rsqrt(ms + EPS) * gain_ref[...]
    for c in range(nl):
        tm_sc[c] = y[:, c * LANE:(c + 1) * LANE]
    for b in range(B):
        o_ref[b] = jnp.concatenate(
            [tm_sc[c, pl.ds(b, ts, stride=B), :] for c in range(nl)], axis=1).astype(BF16)


def _s5(u, bb, a, cc, d, wg, layer, gm, gain, B, S, ts):
    rows = ts * B
    full = lambda i: (0, 0)
    lsel = lambda i: (layer, 0, 0)
    return pl.pallas_call(
        functools.partial(_s5_kernel, B=B, ts=ts),
        grid=(S // ts,),
        in_specs=[
            pl.BlockSpec((B, ts, SSM_WIDTH), lambda i: (0, i, 0)),
            pl.BlockSpec((None, SSM_WIDTH, 2 * SSM_LANES), lsel),
            pl.BlockSpec((None, 2, SSM_LANES), lsel),
            pl.BlockSpec((None, 2 * SSM_LANES, SSM_WIDTH), lsel),
            pl.BlockSpec((1, SSM_WIDTH), full),
            pl.BlockSpec((None, SSM_WIDTH, SSM_WIDTH), lsel),
            pl.BlockSpec((SSM_WIDTH, SSM_WIDTH), full),
            pl.BlockSpec((1, SSM_WIDTH), full),
        ],
        out_specs=pl.BlockSpec((B, ts, SSM_WIDTH), lambda i: (0, i, 0)),
        out_shape=jax.ShapeDtypeStruct((B, S, SSM_WIDTH), BF16),
        scratch_shapes=[pltpu.VMEM((rows, 2 * SSM_LANES), F32), pltpu.VMEM((2, B, SSM_LANES), F32),
                        pltpu.VMEM((SSM_WIDTH // LANE, rows, LANE), F32)],
        compiler_params=pltpu.CompilerParams(
            dimension_semantics=("arbitrary",), vmem_limit_bytes=VMEM_LIMIT),
        name="s5",
    )(u, bb, a, cc, d, wg, gm, gain)


def _mlstm_kernel(q_ref, k_ref, v_ref, o_ref, gc_ref, gr_ref, cwq_ref, cwk_ref, bc_ref, br_ref, hm_ref,
                  gain_ref, y_ref, qt_sc, kt_sc, c_sc, m_sc, *, B):
    L, H, Dh, W = MLSTM_CHUNK, MLSTM_HEADS, MLSTM_HEAD_DIM, MLSTM_WIDTH

    @pl.when(pl.program_id(0) == 0)
    def _():
        qt_sc[...] = jnp.zeros_like(qt_sc)
        kt_sc[...] = jnp.zeros_like(kt_sc)
        c_sc[...] = jnp.zeros_like(c_sc)
        m_sc[...] = jnp.zeros_like(m_sc)

    row_l = lax.broadcasted_iota(jnp.int32, (L, L), 0)
    col_l = lax.broadcasted_iota(jnp.int32, (L, L), 1)
    causal = col_l <= row_l
    tri = causal.astype(F32)
    triu = (row_l <= col_l).astype(F32)
    lane_w = lax.broadcasted_iota(jnp.int32, (1, W), 1) // Dh
    lane_2w = (lax.broadcasted_iota(jnp.int32, (1, 2 * W), 1) % W) // Dh
    bd_mask = (lax.broadcasted_iota(jnp.int32, (W, 2 * W), 0) // Dh
               == (lax.broadcasted_iota(jnp.int32, (W, 2 * W), 1) % W) // Dh)
    row8 = lax.broadcasted_iota(jnp.int32, (SUBLANE, W), 0)
    cwq = cwq_ref[...]
    cwk = cwk_ref[...]
    ones_v = jnp.ones((L, W), BF16)

    def conv_silu(x, tail, w):
        acc = x * w[MLSTM_CONV - 1:MLSTM_CONV, :]
        for sft in range(1, MLSTM_CONV):
            xs = pltpu.roll(x, sft, 0)
            head = jnp.where(row8 < sft, pltpu.roll(tail, sft, 0), xs[:SUBLANE])
            xs = jnp.concatenate([head, xs[SUBLANE:]], axis=0)
            acc = acc + xs * w[MLSTM_CONV - 1 - sft:MLSTM_CONV - sft, :]
        return acc * _sigmoid(acc)

    def expand(cols, width_lanes):
        out = cols[H - 1]
        for hh in range(H - 2, -1, -1):
            out = jnp.where(width_lanes == hh, cols[hh], out)
        return out

    def per_batch(b, _):
        q_raw = q_ref[b]
        k_raw = k_ref[b]
        q = conv_silu(q_raw, qt_sc[b], cwq)
        k = conv_silu(k_raw, kt_sc[b], cwk) * (Dh ** -0.5)
        qt_sc[b] = q_raw[L - SUBLANE:, :]
        kt_sc[b] = k_raw[L - SUBLANE:, :]
        vaug = jnp.concatenate([v_ref[b], ones_v], axis=1)

        gc = gc_ref[b] + bc_ref[...]
        gr = gr_ref[b] + br_ref[...]
        bcol = _dot(tri, _log_sigmoid(gc), precision=HIGHEST)
        brow = _dot(_log_sigmoid(gr), triu, precision=HIGHEST)
        m_all = m_sc[b]

        w_intra, w_inter, e_mt, w_k, dec, m_new = [], [], [], [], [], []
        for hh in range(H):
            bc = bcol[:, H + hh:H + hh + 1]
            ic = gc[:, hh:hh + 1]
            brr = brow[H + hh:H + hh + 1, :]
            irr = gr[hh:hh + 1, :]
            m_prev = m_all[hh:hh + 1, 0:1]
            dm = jnp.where(causal, bc - brr + irr, NEG_INF)
            inter = bc + m_prev
            mt = jnp.maximum(inter, jnp.max(dm, axis=1, keepdims=True))
            w_intra.append(jnp.exp(dm - mt))
            w_inter.append(jnp.exp(inter - mt))
            e_mt.append(jnp.exp(-mt))
            b_last = bc[L - 1:L, :]
            logw = b_last - bc + ic
            mn = jnp.maximum(b_last + m_prev, jnp.max(logw, axis=0, keepdims=True))
            w_k.append(jnp.exp(logw - mn))
            dec.append(jnp.exp(b_last + m_prev - mn))
            m_new.append(mn)

        qb = q.astype(BF16)
        kb = k.astype(BF16)
        q_heads = jnp.concatenate([jnp.where(lane_w == hh, qb, jnp.zeros_like(qb)) for hh in range(H)], axis=0)
        s = _dot_nt(q_heads, kb) * jnp.concatenate(w_intra, axis=0)
        sv = _dot(s.astype(BF16), vaug)
        intra = jnp.where(lane_2w == 0, sv[0:L], 0.0)
        for hh in range(1, H):
            intra = intra + jnp.where(lane_2w == hh, sv[hh * L:(hh + 1) * L], 0.0)
        c_aug = c_sc[b]
        qc = _dot(qb, c_aug.astype(BF16))
        tot = expand(w_inter, lane_2w) * qc + intra
        den = jnp.maximum(jnp.abs(tot[:, W:]), expand(e_mt, lane_w))
        hout = tot[:, :W] / den
        y = _sigmoid(o_ref[b]) * hout
        ms = _dot_split(y * y, hm_ref[...])
        y_ref[b] = (y * lax.rsqrt(ms + EPS) * gain_ref[...]).astype(BF16)

        kw = (k * expand(w_k, lane_w)).astype(BF16)
        upd = _dot_tn(kw, vaug)
        c_sc[b] = expand(dec, lane_2w) * c_aug + jnp.where(bd_mask, upd, 0.0)
        for hh in range(H):
            m_sc[b, hh:hh + 1, :] = jnp.broadcast_to(m_new[hh], (1, LANE))
        return 0

    lax.fori_loop(0, B, per_batch, 0)


def _mlstm(mq, mk, mv, mo, gcol, grow, cwq, cwk, bcol, brow, hm, gain, B, S):
    L, W = MLSTM_CHUNK, MLSTM_WIDTH
    seq = lambda c: (0, c, 0)
    full = lambda c: (0, 0)
    return pl.pallas_call(
        functools.partial(_mlstm_kernel, B=B),
        grid=(S // L,),
        in_specs=[
            pl.BlockSpec((B, L, W), seq),
            pl.BlockSpec((B, L, W), seq),
            pl.BlockSpec((B, L, W), seq),
            pl.BlockSpec((B, L, W), seq),
            pl.BlockSpec((B, L, LANE), seq),
            pl.BlockSpec((B, SUBLANE, L), lambda c: (0, 0, c)),
            pl.BlockSpec((MLSTM_CONV, W), full),
            pl.BlockSpec((MLSTM_CONV, W), full),
            pl.BlockSpec((1, LANE), full),
            pl.BlockSpec((SUBLANE, 1), full),
            pl.BlockSpec((W, W), full),
            pl.BlockSpec((1, W), full),
        ],
        out_specs=pl.BlockSpec((B, L, W), seq),
        out_shape=jax.ShapeDtypeStruct((B, S, W), BF16),
        scratch_shapes=[
            pltpu.VMEM((B, SUBLANE, W), F32),
            pltpu.VMEM((B, SUBLANE, W), F32),
            pltpu.VMEM((B, W, 2 * W), F32),
            pltpu.VMEM((B, SUBLANE, LANE), F32),
        ],
        compiler_params=pltpu.CompilerParams(
            dimension_semantics=("arbitrary",), vmem_limit_bytes=VMEM_LIMIT),
        name="mlstm",
    )(mq, mk, mv, mo, gcol, grow, cwq, cwk, bcol, brow, hm, gain)


def _compress_kernel(c_ref, w1ab_ref, w1_ref, pe_ref, w2_ref, w2t_ref, o_ref, ot_ref, ch_sc):
    G, Dh = NSA_KV_GROUPS, NSA_HEAD_DIM
    rows = ch_sc.shape[0]
    n = rows // G
    for r in range(CMP_STRIDE):
        tok = c_ref[pl.ds(r, n, stride=CMP_STRIDE), :]
        for g in range(G):
            ch_sc[g * n:(g + 1) * n, r * Dh:(r + 1) * Dh] = tok[:, g * Dh:(g + 1) * Dh]
    ab = _dot(ch_sc[...].astype(BF16), w1ab_ref[...])
    const = _dot(pe_ref[...], w1_ref[...], precision=HIGHEST)
    hid = ab[:, :CMP_HIDDEN] + pltpu.roll(ab[:, CMP_HIDDEN:], rows - 1, 0) + const
    act = _gelu_tanh(hid).astype(BF16)
    o_ref[...] = _dot(act, w2_ref[...]).astype(BF16)
    ot_ref[...] = _dot_nt(w2t_ref[...], act).astype(BF16)


def _compress(ckv, w1ab, w1, pe, w2, w2t, layer, B, S):
    G, Dh = NSA_KV_GROUPS, NSA_HEAD_DIM
    n = S // CMP_STRIDE
    width = CMP_STRIDE * Dh
    wsel = lambda i, b: (layer, i, 0, 0)
    return pl.pallas_call(
        _compress_kernel,
        grid=(2, B),
        in_specs=[
            pl.BlockSpec((None, S, G * Dh), lambda i, b: (i, b, 0)),
            pl.BlockSpec((None, None, width, 2 * CMP_HIDDEN), wsel),
            pl.BlockSpec((None, None, 2 * width, CMP_HIDDEN), wsel),
            pl.BlockSpec((None, None, 1, 2 * width), wsel),
            pl.BlockSpec((None, None, CMP_HIDDEN, Dh), wsel),
            pl.BlockSpec((None, None, Dh, CMP_HIDDEN), wsel),
        ],
        out_specs=[pl.BlockSpec((None, None, G * n, Dh), lambda i, b: (i, b, 0, 0)),
                   pl.BlockSpec((None, None, Dh, G * n), lambda i, b: (i, b, 0, 0))],
        out_shape=[jax.ShapeDtypeStruct((2, B, G * n, Dh), BF16),
                   jax.ShapeDtypeStruct((2, B, Dh, G * n), BF16)],
        scratch_shapes=[pltpu.VMEM((G * n, width), F32)],
        compiler_params=pltpu.CompilerParams(
            dimension_semantics=("parallel", "parallel"), vmem_limit_bytes=VMEM_LIMIT),
        name="compress",
    )(ckv, w1ab, w1, pe, w2, w2t)


def _nsa_kernel(q_ref, kc_ref, vct_ref, ks_ref, vst_ref, kw_ref, vwt_ref, gtt_ref, gain_ref,
                ovt_ref, et_ref, o_ref, *, n_sel, n_top, ck, unroll):
    TQ, R, Dh, G = Q_BLOCK, NSA_REP, NSA_HEAD_DIM, NSA_KV_GROUPS
    groups = range(G)
    i = pl.program_id(1)
    t0 = i * TQ
    qs = [q_ref[g * R:(g + 1) * R].reshape(R * TQ, Dh) for g in groups]
    tq1 = t0 + lax.broadcasted_iota(jnp.int32, (1, TQ), 1)
    heads = lambda t: jnp.concatenate([t] * R, axis=1)

    ncmp = kc_ref.shape[0] // G
    nwb = WINDOW // TQ + 1
    wb0 = jnp.maximum(i - WINDOW // TQ, 0)
    ws = pl.multiple_of(wb0 * TQ, TQ)
    sc = [_dot_nt(kc_ref[g * ncmp:(g + 1) * ncmp, :], qs[g]) for g in groups]
    sw = [_dot_nt(kw_ref[g, pl.ds(ws, nwb * TQ), :], qs[g]) for g in groups]

    cend = lax.broadcasted_iota(jnp.int32, (ncmp, 1), 0) * CMP_STRIDE + (CMP_BLOCK - 1)
    cmask = heads(cend <= tq1)
    pc = []
    for g in groups:
        scm = jnp.where(cmask, sc[g], NEG_INF)
        ec = jnp.where(cmask, jnp.exp(scm - jnp.max(scm, axis=0, keepdims=True)), 0.0)
        pc.append(ec * (1.0 / jnp.maximum(jnp.sum(ec, axis=0, keepdims=True), 1e-30)))
    oc = [_dot(vct_ref[:, g * ncmp:(g + 1) * ncmp], pc[g].astype(BF16)) for g in groups]

    imp = []
    for g in groups:
        psum = pc[g][:, 0:TQ]
        for r in range(1, R):
            psum = psum + pc[g][:, r * TQ:(r + 1) * TQ]
        imp.append(_dot(ovt_ref[...], psum, precision=HIGHEST))

    kpos = ws + lax.broadcasted_iota(jnp.int32, (nwb * TQ, 1), 0)
    wbias = heads(jnp.where((kpos <= tq1) & (tq1 - kpos < WINDOW), 0.0, NEG_INF))
    ow, l_w = [], []
    for g in groups:
        swb = sw[g] + wbias
        pw = jnp.exp(swb - jnp.max(swb, axis=0, keepdims=True))
        vwt = jnp.concatenate([vwt_ref[g, wb0 + j] for j in range(nwb)], axis=1)
        ow.append(_dot(vwt, pw.astype(BF16)))
        l_w.append(jnp.sum(pw, axis=0, keepdims=True))

    blk = lax.broadcasted_iota(jnp.int32, (n_sel, 1), 0)
    valid = blk * SEL_BLOCK <= tq1
    forced = (blk == 0) | (blk == tq1 // SEL_BLOCK)
    selb = []
    for g in groups:
        val = jnp.where(forced, FORCE_SCORE, jnp.where(valid, imp[g], -FORCE_SCORE))
        rank = jnp.zeros((n_sel, TQ), F32)
        for jp in range(n_sel):
            other = val[jp:jp + 1, :]
            wins = jnp.where(blk > jp, jnp.where(other >= val, 1.0, 0.0), jnp.where(other > val, 1.0, 0.0))
            rank = rank + wins
        selb.append(jnp.where(rank < n_top, 0.0, NEG_INF).astype(BF16))

    def scores(g, c):
        k0 = pl.multiple_of(c * ck, ck)
        kpos = k0 + lax.broadcasted_iota(jnp.int32, (ck, 1), 0)
        bias = jnp.where(kpos <= tq1, _dot(et_ref[c], selb[g]), NEG_INF)
        return _dot_nt(ks_ref[g, pl.ds(k0, ck), :], qs[g]) + heads(bias)

    def update(g, c, s, carry):
        m, l, acc = carry
        mn = jnp.maximum(m, jnp.max(s, axis=0, keepdims=True))
        alpha = jnp.exp(m - mn)
        p = jnp.exp(s - mn)
        l = alpha * l + jnp.sum(p, axis=0, keepdims=True)
        acc = alpha * acc + _dot(vst_ref[g, c], p.astype(BF16))
        return mn, l, acc

    def chunk_group(cg, carry):
        cs = [cg * unroll + sub for sub in range(unroll)]
        ss = [[scores(g, c) for g in groups] for c in cs]
        carry = list(carry)
        for c, s in zip(cs, ss):
            for g in groups:
                carry[g] = update(g, c, s[g], carry[g])
        return tuple(carry)

    n_chunks = (t0 + TQ + ck - 1) // ck
    init = tuple((jnp.full((1, R * TQ), NEG_INF, F32), jnp.zeros((1, R * TQ), F32),
                  jnp.zeros((Dh, R * TQ), F32)) for _ in groups)
    sel = lax.fori_loop(0, (n_chunks + unroll - 1) // unroll, chunk_group, init)

    for g in groups:
        gs = _sigmoid(gtt_ref[g])
        _, l_s, acc_s = sel[g]
        for r in range(R):
            ln = slice(r * TQ, (r + 1) * TQ)
            o = (gs[3 * r:3 * r + 1, :] * oc[g][:, ln]
                 + (gs[3 * r + 1:3 * r + 2, :] / l_s[:, ln]) * acc_s[:, ln]
                 + (gs[3 * r + 2:3 * r + 3, :] / l_w[g][:, ln]) * ow[g][:, ln])
            ms = jnp.mean(o * o, axis=0, keepdims=True)
            hd = g * R + r
            o_ref[hd] = jnp.transpose(o * lax.rsqrt(ms + EPS) * gain_ref[hd]).astype(BF16)


def _nsa(aq, cmp_k, cmp_vt, ks, vst, kw, vwt, gates_t, gain, consts, B, S):
    G, H, TQ, Dh = NSA_KV_GROUPS, NSA_HEADS, Q_BLOCK, NSA_HEAD_DIM
    nq = S // TQ
    ncmp = S // CMP_STRIDE
    n_sel = S // SEL_BLOCK
    ovt, emat_t, ck = consts
    k_spec = pl.BlockSpec((None, G, S, Dh), lambda b, i: (b, 0, 0, 0))
    return pl.pallas_call(
        functools.partial(_nsa_kernel, n_sel=n_sel, n_top=min(SEL_TOPN, n_sel), ck=ck, unroll=SEL_UNROLL),
        grid=(B, nq),
        in_specs=[
            pl.BlockSpec((None, H, TQ, Dh), lambda b, i: (b, 0, i, 0)),
            pl.BlockSpec((None, None, G * ncmp, Dh), lambda b, i: (0, b, 0, 0)),
            pl.BlockSpec((None, None, Dh, G * ncmp), lambda b, i: (1, b, 0, 0)),
            k_spec,
            pl.BlockSpec((None, G, S // ck, Dh, ck), lambda b, i: (b, 0, 0, 0, 0)),
            k_spec,
            pl.BlockSpec((None, G, S // TQ, Dh, TQ), lambda b, i: (b, 0, 0, 0, 0)),
            pl.BlockSpec((G, None, 2 * SUBLANE, TQ), lambda b, i: (0, b, 0, i)),
            pl.BlockSpec((H, Dh, 1), lambda b, i: (0, 0, 0)),
            pl.BlockSpec(ovt.shape, lambda b, i: (0, 0)),
            pl.BlockSpec(emat_t.shape, lambda b, i: (0, 0, 0)),
        ],
        out_specs=pl.BlockSpec((None, H, TQ, Dh), lambda b, i: (b, 0, i, 0)),
        out_shape=jax.ShapeDtypeStruct((B, H, S, Dh), BF16),
        compiler_params=pltpu.CompilerParams(
            dimension_semantics=("parallel", "arbitrary"), vmem_limit_bytes=VMEM_LIMIT),
        name="nsa",
    )(aq, cmp_k, cmp_vt, ks, vst, kw, vwt, gates_t, gain, ovt, emat_t)


def _nsa_consts(S):
    n_cmp = S // CMP_STRIDE
    n_sel = S // SEL_BLOCK
    ck = 256
    i = np.arange(n_cmp)[:, None]
    j = np.arange(n_sel)[None, :]
    lo = np.maximum(i * CMP_STRIDE, j * SEL_BLOCK)
    hi = np.minimum(i * CMP_STRIDE + CMP_BLOCK, (j + 1) * SEL_BLOCK)
    ov = np.maximum(hi - lo, 0) / CMP_STRIDE
    ov[n_cmp - 1] = 0.0
    key = np.arange(S)
    emat_t = (key[:, None] // SEL_BLOCK == np.arange(n_sel)[None, :]).astype(np.float32)
    return (jnp.asarray(ov.T, F32), jnp.asarray(emat_t.reshape(S // ck, ck, n_sel), BF16), ck)


def _chunked_t(v, width):
    B, G, S, Dh = v.shape
    return v.reshape(B, G, S // width, width, Dh).swapaxes(-1, -2)


def _outproj_kernel(h_ref, ys_ref, ym_ref, yn_ref, w_ref, g_ref, o_ref):
    acc = _dot(ys_ref[...], w_ref[0:SSM_WIDTH, :])
    acc = acc + _dot(ym_ref[...], w_ref[SSM_WIDTH:SSM_WIDTH + MLSTM_WIDTH, :])
    base = SSM_WIDTH + MLSTM_WIDTH
    for hd in range(NSA_HEADS):
        acc = acc + _dot(yn_ref[hd], w_ref[base + hd * NSA_HEAD_DIM:base + (hd + 1) * NSA_HEAD_DIM, :])
    ms = jnp.mean(acc * acc, axis=-1, keepdims=True)
    o_ref[...] = h_ref[...] + acc * lax.rsqrt(ms + EPS) * g_ref[...]


def _outproj(h2, y_ssm, y_mls, y_nsa, w, layer, gain, B, S, ts):
    nt = S // ts
    row = lambda b, i: (b * nt + i, 0)
    full = lambda b, i: (0, 0)
    return pl.pallas_call(
        _outproj_kernel,
        grid=(B, nt),
        in_specs=[
            pl.BlockSpec((ts, D_MODEL), row),
            pl.BlockSpec((ts, SSM_WIDTH), row),
            pl.BlockSpec((ts, MLSTM_WIDTH), row),
            pl.BlockSpec((None, NSA_HEADS, ts, NSA_HEAD_DIM), lambda b, i: (b, 0, i, 0)),
            pl.BlockSpec((None, D_MODEL, D_MODEL), lambda b, i: (layer, 0, 0)),
            pl.BlockSpec((1, D_MODEL), full),
        ],
        out_specs=pl.BlockSpec((ts, D_MODEL), row),
        out_shape=jax.ShapeDtypeStruct((B * S, D_MODEL), F32),
        compiler_params=pltpu.CompilerParams(
            dimension_semantics=("parallel", "parallel"), vmem_limit_bytes=VMEM_LIMIT),
        name="outproj",
    )(h2, y_ssm, y_mls, y_nsa, w, gain)


def _mlp_kernel(h_ref, g1_ref, w1_ref, w2_ref, g2_ref, o_ref, u_sc, acc_sc):
    kf = pl.program_id(1)

    @pl.when(kf == 0)
    def _():
        x = h_ref[...]
        ms = jnp.mean(x * x, axis=-1, keepdims=True)
        u_sc[...] = (x * lax.rsqrt(ms + EPS) * g1_ref[...]).astype(BF16)
        acc_sc[...] = jnp.zeros_like(acc_sc)

    a = jnp.maximum(_dot(u_sc[...], w1_ref[...]), 0.0)
    acc_sc[...] += _dot((a * a).astype(BF16), w2_ref[...])

    @pl.when(kf == pl.num_programs(1) - 1)
    def _():
        f = acc_sc[...]
        ms = jnp.mean(f * f, axis=-1, keepdims=True)
        o_ref[...] = h_ref[...] + f * lax.rsqrt(ms + EPS) * g2_ref[...]


def _mlp(h2, g1, w1, w2, layer, g2, tm, tf):
    rows = h2.shape[0]
    return pl.pallas_call(
        _mlp_kernel,
        grid=(rows // tm, D_FF // tf),
        in_specs=[
            pl.BlockSpec((tm, D_MODEL), lambda i, k: (i, 0)),
            pl.BlockSpec((1, D_MODEL), lambda i, k: (0, 0)),
            pl.BlockSpec((None, D_MODEL, tf), lambda i, k: (layer, 0, k)),
            pl.BlockSpec((None, tf, D_MODEL), lambda i, k: (layer, k, 0)),
            pl.BlockSpec((1, D_MODEL), lambda i, k: (0, 0)),
        ],
        out_specs=pl.BlockSpec((tm, D_MODEL), lambda i, k: (i, 0)),
        out_shape=jax.ShapeDtypeStruct((rows, D_MODEL), F32),
        scratch_shapes=[pltpu.VMEM((tm, D_MODEL), BF16), pltpu.VMEM((tm, D_MODEL), F32)],
        compiler_params=pltpu.CompilerParams(
            dimension_semantics=("parallel", "arbitrary"), vmem_limit_bytes=VMEM_LIMIT),
        name="mlp",
    )(h2, g1, w1, w2, g2)


def _inproj_pieces():
    return ((0, 1280), (1288, 1800), (1800, 1928), (2056, 2184), (2312, 2440),
            (1928, 2056), (2184, 2312), (2440, 2568),
            (1280, 1288), (2568, 2580), (None, LANE - 20),
            (None, GATE_COL), (2580, 2592), (None, LANE - GATE_COL - 12))


def _permute_w_in(w_in):
    parts = []
    for a, b in _inproj_pieces():
        if a is None:
            parts.append(jnp.zeros(w_in.shape[:-1] + (b,), BF16))
        else:
            parts.append(w_in[..., a:b].astype(BF16))
    out = jnp.concatenate(parts, axis=-1)
    assert out.shape[-1] == D_INP
    return out


def _rope_tables(positions):
    inv = ROPE_THETA ** (-jnp.arange(0, ROPE_DIMS, 2, dtype=F32) / ROPE_DIMS)
    ang = positions.astype(F32)[..., None] * inv
    cos, sin = jnp.cos(ang), jnp.sin(ang)
    z = jnp.zeros_like(cos)
    rest = NSA_HEAD_DIM - ROPE_DIMS
    pad_one = jnp.ones(cos.shape[:-1] + (rest,), F32)
    pad_zero = jnp.zeros(cos.shape[:-1] + (rest,), F32)
    rc = jnp.concatenate([cos, cos, pad_one], axis=-1)
    rs1 = jnp.concatenate([-sin, z, pad_zero], axis=-1)
    rs2 = jnp.concatenate([z, sin, pad_zero], axis=-1)
    tile = lambda t: jnp.tile(t, (1, 1, LANE // NSA_HEAD_DIM)).reshape(-1, LANE)
    return tile(rc), tile(rs1), tile(rs2)


def _s5_params(lam_re, lam_im, b_re, b_im, c_re, c_im, log_dt):
    G, P, Hc = SSM_GROUPS, SSM_STATE, SSM_GROUP
    dt = jnp.exp(log_dt)[:, None]
    mag = jnp.exp(lam_re * dt)
    ang = lam_im * dt
    ab_re = mag * jnp.cos(ang)
    ab_im = mag * jnp.sin(ang)
    den = lam_re * lam_re + lam_im * lam_im
    g_re = ((ab_re - 1.0) * lam_re + ab_im * lam_im) / den
    g_im = (ab_im * lam_re - (ab_re - 1.0) * lam_im) / den
    bb_re = g_re[..., None] * b_re - g_im[..., None] * b_im
    bb_im = g_re[..., None] * b_im + g_im[..., None] * b_re
    eye = jnp.eye(G, dtype=F32)
    blockdiag_in = lambda t: jnp.einsum('gph,gk->ghkp', t, eye).reshape(G * Hc, G * P)
    blockdiag_out = lambda t: jnp.einsum('ghp,gk->gpkh', t, eye).reshape(G * P, G * Hc)
    bb = jnp.concatenate([blockdiag_in(bb_re), blockdiag_in(bb_im)], axis=1).astype(BF16)
    cc = jnp.concatenate([blockdiag_out(c_re), -blockdiag_out(c_im)], axis=0).astype(BF16)
    a = jnp.stack([ab_re.reshape(-1), ab_im.reshape(-1)], axis=0)
    return bb, a, cc


def _group_mean_matrix(width, group):
    idx = np.arange(width) // group
    return jnp.asarray((idx[:, None] == idx[None, :]).astype(np.float32) / group, BF16)


def kernel(x, positions, ln_mix_pre, ln_mix_post, ln_mlp_pre, ln_mlp_post, w_in, w_out, ssm_lambda_re, ssm_lambda_im, ssm_b_re, ssm_b_im, ssm_c_re, ssm_c_im, ssm_d, ssm_log_dt, ssm_w_glu, mlstm_conv, mlstm_b_i, mlstm_b_f, cmp_pe_k, cmp_w1_k, cmp_w2_k, cmp_pe_v, cmp_w1_v, cmp_w2_v, gn_ssm, gn_mlstm, gn_nsa, mlp_w1, mlp_w2):
    B, S, D = x.shape
    depth = w_in.shape[0]
    assert D == D_MODEL and B == SUBLANE and S % 512 == 0 and S >= WINDOW + Q_BLOCK
    G, H = NSA_KV_GROUPS, MLSTM_HEADS
    ts_proj = 512
    ts_scan = 128

    rc, rs1, rs2 = _rope_tables(positions)
    w_in_p = _permute_w_in(w_in)
    w_out_b = w_out.astype(BF16)
    w1_b = mlp_w1.astype(BF16)
    w2_b = mlp_w2.astype(BF16)
    wglu_b = ssm_w_glu.astype(BF16)
    gm_ssm = _group_mean_matrix(SSM_WIDTH, SSM_GROUP)
    hm_mls = _group_mean_matrix(MLSTM_WIDTH, MLSTM_HEAD_DIM)
    consts = _nsa_consts(S)
    half = CMP_STRIDE * NSA_HEAD_DIM
    w1ab = jnp.stack([jnp.concatenate([cmp_w1_k[:, :half], cmp_w1_k[:, half:]], axis=-1),
                      jnp.concatenate([cmp_w1_v[:, :half], cmp_w1_v[:, half:]], axis=-1)], axis=1).astype(BF16)
    w1f = jnp.stack([cmp_w1_k, cmp_w1_v], axis=1)
    pef = jnp.stack([cmp_pe_k.reshape(depth, 1, -1), cmp_pe_v.reshape(depth, 1, -1)], axis=1)
    w2c = jnp.stack([cmp_w2_k, cmp_w2_v], axis=1).astype(BF16)
    w2ct = jnp.swapaxes(w2c, -1, -2)
    zeros_g = jnp.zeros((depth, LANE - 2 * H), F32)
    bias_col = jnp.concatenate([mlstm_b_i, mlstm_b_f, zeros_g], axis=-1)[:, None, :]
    bias_row = jnp.concatenate([mlstm_b_i, mlstm_b_f], axis=-1)[:, :, None]

    bb, a, cc = jax.vmap(_s5_params)(ssm_lambda_re, ssm_lambda_im, ssm_b_re, ssm_b_im, ssm_c_re, ssm_c_im,
                                     ssm_log_dt)
    sh3 = lambda t: t.reshape(B, S, t.shape[-1])

    h = x.reshape(B * S, D)
    for l in range(depth):
        (su, mq, mk, mv, mo, aq, ckv, sk, wk, sv, wv, gates) = _inproj(
            h, ln_mix_pre[l][None], w_in_p, l, rc, rs1, rs2, B, S, ts_proj)

        y_ssm = _s5(sh3(su), bb, a, cc, ssm_d[l][None], wglu_b, l, gm_ssm, gn_ssm[l][None], B, S, ts_scan)

        gcol = sh3(gates[0])
        grow = jnp.swapaxes(gcol[:, :, :2 * H], 1, 2)
        y_mls = _mlstm(sh3(mq), sh3(mk), sh3(mv), sh3(mo), gcol, grow,
                       mlstm_conv[l][:, :MLSTM_WIDTH], mlstm_conv[l][:, MLSTM_WIDTH:],
                       bias_col[l], bias_row[l], hm_mls, gn_mlstm[l][None], B, S)

        cmp_k, cmp_t = _compress(ckv, w1ab, w1f, pef, w2c, w2ct, l, B, S)
        gates_t = jnp.swapaxes(
            gates.reshape(G, B, S, LANE)[..., GATE_COL:GATE_COL + 2 * SUBLANE], -1, -2)
        y_nsa = _nsa(aq, cmp_k, cmp_t, sk, _chunked_t(sv, consts[2]), wk, _chunked_t(wv, Q_BLOCK), gates_t,
                     gn_nsa[l].reshape(NSA_HEADS, NSA_HEAD_DIM, 1), consts, B, S)

        h = _outproj(h, y_ssm.reshape(B * S, SSM_WIDTH), y_mls.reshape(B * S, MLSTM_WIDTH), y_nsa,
                     w_out_b, l, ln_mix_post[l][None], B, S, ts_proj)
        h = _mlp(h, ln_mlp_pre[l][None], w1_b, w2_b, l, ln_mlp_post[l][None], 1024, 512)
    return h.reshape(B, S, D)
```

```python
import functools
import math

import numpy as np
import jax
import jax.numpy as jnp
from jax import lax
from jax.experimental import pallas as pl
from jax.experimental.pallas import tpu as pltpu

F32 = jnp.float32
BF16 = jnp.bfloat16
HIGHEST = lax.Precision.HIGHEST

D_MODEL = 1024
DEPTH = 4
SSM_WIDTH = 256
SSM_GROUP = 16
SSM_GROUPS = 16
SSM_STATE = 64
SSM_LANES = SSM_GROUPS * SSM_STATE
MLSTM_WIDTH = 256
MLSTM_HEADS = 4
MLSTM_HEAD_DIM = 64
MLSTM_CHUNK = 128
MLSTM_CONV = 4
NSA_WIDTH = 512
NSA_HEAD_DIM = 64
NSA_HEADS = 8
NSA_KV_GROUPS = 2
NSA_REP = NSA_HEADS // NSA_KV_GROUPS
NSA_KV_WIDTH = 128
CMP_BLOCK = 32
CMP_STRIDE = 16
CMP_HIDDEN = 256
SEL_BLOCK = 64
SEL_TOPN = 8
WINDOW = 256
Q_BLOCK = 128
FORCE_SCORE = 1e4
NEG_INF = -1e30
ROPE_THETA = 500000.0
ROPE_DIMS = 16
ROPE_HALF = 8
D_FF = 4096
EPS = 1e-6
D_IN = 2592

LANE = 128
SUBLANE = 8
VMEM_LIMIT = 56 * 1024 * 1024

C_SU, C_MQ, C_MK, C_MV, C_MO = 0, 256, 512, 768, 1024
C_AQ, C_CK, C_SK, C_WK = 1280, 1792, 1920, 2048
C_CV, C_SV, C_WV = 2176, 2304, 2432
C_G0, C_G1 = 2560, 2688
D_INP = 2816
GATE_COL = 8
Q_SCALE = NSA_HEAD_DIM ** -0.5 * math.log2(math.e)
SEL_UNROLL = 2


def _dot(a, b, precision=None):
    return jnp.dot(a, b, preferred_element_type=F32, precision=precision)


def _dot_nt(a, b):
    return lax.dot_general(a, b, (((1,), (1,)), ((), ())), preferred_element_type=F32)


def _dot_tn(a, b):
    return lax.dot_general(a, b, (((0,), (0,)), ((), ())), preferred_element_type=F32)


def _sigmoid(x):
    return 1.0 / (1.0 + jnp.exp(-x))


def _dot_split(x, w_bf16):
    hi = x.astype(BF16)
    lo = (x - hi.astype(F32)).astype(BF16)
    return _dot(hi, w_bf16) + _dot(lo, w_bf16)


def _gelu_tanh(x):
    return 0.5 * x * (1.0 + jnp.tanh(math.sqrt(2.0 / math.pi) * (x + 0.044715 * (x * x * x))))


def _log_sigmoid(x):
    return jnp.minimum(x, 0.0) - jnp.log(1.0 + jnp.exp(-jnp.abs(x)))


def _inproj_kernel(x_ref, g_ref, w_ref, rc_ref, rs1_ref, rs2_ref,
                   su_ref, mq_ref, mk_ref, mv_ref, mo_ref, aq_ref, ckv_ref, sk_ref, wk_ref,
                   sv_ref, wv_ref, gt_ref):
    x = x_ref[...]
    ms = jnp.mean(x * x, axis=-1, keepdims=True)
    u = (x * lax.rsqrt(ms + EPS) * g_ref[...]).astype(BF16)
    rc, rs1, rs2 = rc_ref[...], rs1_ref[...], rs2_ref[...]

    def mm(c0, width):
        return _dot(u, w_ref[:, c0:c0 + width])

    def rope(z):
        return z * rc + pltpu.roll(z, LANE - ROPE_HALF, 1) * rs1 + pltpu.roll(z, ROPE_HALF, 1) * rs2

    su_ref[...] = mm(C_SU, 256)
    mq_ref[...] = mm(C_MQ, 256)
    mk_ref[...] = mm(C_MK, 256)
    mv_ref[...] = mm(C_MV, 256).astype(BF16)
    mo_ref[...] = mm(C_MO, 256)
    def mm_pair(c0):
        z = mm(c0, 2 * LANE)
        return z[:, :LANE], z[:, LANE:]

    def put_heads(ref, first, z):
        ref[first] = z[:, :NSA_HEAD_DIM].astype(BF16)
        ref[first + 1] = z[:, NSA_HEAD_DIM:].astype(BF16)

    for j in range(NSA_HEADS // 4):
        for k, z in enumerate(mm_pair(C_AQ + 2 * LANE * j)):
            put_heads(aq_ref, 4 * j + 2 * k, rope(z) * Q_SCALE)
    z_ck, z_sk = mm_pair(C_CK)
    z_wk, z_cv = mm_pair(C_WK)
    z_sv, z_wv = mm_pair(C_SV)
    ckv_ref[0] = rope(z_ck)
    ckv_ref[1] = z_cv
    put_heads(sk_ref, 0, rope(z_sk))
    put_heads(wk_ref, 0, rope(z_wk))
    put_heads(sv_ref, 0, z_sv)
    put_heads(wv_ref, 0, z_wv)
    gt_ref[0], gt_ref[1] = mm_pair(C_G0)


def _inproj(h2, gain, w, layer, rc, rs1, rs2, B, S, ts):
    nt = S // ts
    BS = B * S
    row = lambda b, i: (b * nt + i, 0)
    full = lambda b, i: (0, 0)
    headed = lambda b, i: (b, 0, i, 0)
    paired = lambda b, i: (0, b * nt + i, 0)
    in_specs = [
        pl.BlockSpec((ts, D_MODEL), row),
        pl.BlockSpec((1, D_MODEL), full),
        pl.BlockSpec((None, D_MODEL, D_INP), lambda b, i: (layer, 0, 0)),
        pl.BlockSpec((ts, LANE), row),
        pl.BlockSpec((ts, LANE), row),
        pl.BlockSpec((ts, LANE), row),
    ]
    kv_shape = jax.ShapeDtypeStruct((B, NSA_KV_GROUPS, S, NSA_HEAD_DIM), BF16)
    kv_spec = pl.BlockSpec((None, NSA_KV_GROUPS, ts, NSA_HEAD_DIM), headed)
    out_shape = [
        jax.ShapeDtypeStruct((BS, SSM_WIDTH), F32),
        jax.ShapeDtypeStruct((BS, MLSTM_WIDTH), F32),
        jax.ShapeDtypeStruct((BS, MLSTM_WIDTH), F32),
        jax.ShapeDtypeStruct((BS, MLSTM_WIDTH), BF16),
        jax.ShapeDtypeStruct((BS, MLSTM_WIDTH), F32),
        jax.ShapeDtypeStruct((B, NSA_HEADS, S, NSA_HEAD_DIM), BF16),
        jax.ShapeDtypeStruct((2, BS, NSA_KV_WIDTH), F32),
        kv_shape, kv_shape,
        kv_shape, kv_shape,
        jax.ShapeDtypeStruct((NSA_KV_GROUPS, BS, LANE), F32),
    ]
    out_specs = [
        pl.BlockSpec((ts, SSM_WIDTH), row),
        pl.BlockSpec((ts, MLSTM_WIDTH), row),
        pl.BlockSpec((ts, MLSTM_WIDTH), row),
        pl.BlockSpec((ts, MLSTM_WIDTH), row),
        pl.BlockSpec((ts, MLSTM_WIDTH), row),
        pl.BlockSpec((None, NSA_HEADS, ts, NSA_HEAD_DIM), headed),
        pl.BlockSpec((2, ts, NSA_KV_WIDTH), paired),
        kv_spec, kv_spec,
        kv_spec, kv_spec,
        pl.BlockSpec((NSA_KV_GROUPS, ts, LANE), paired),
    ]
    return pl.pallas_call(
        _inproj_kernel,
        grid=(B, nt),
        in_specs=in_specs,
        out_specs=out_specs,
        out_shape=out_shape,
        compiler_params=pltpu.CompilerParams(
            dimension_semantics=("parallel", "parallel"), vmem_limit_bytes=VMEM_LIMIT),
        name="inproj",
    )(h2, gain, w, rc, rs1, rs2)


def _s5_kernel(u_ref, bb_ref, a_ref, cc_ref, d_ref, wg_ref, gm_ref, gain_ref, o_ref, x_sc, st_sc, tm_sc, *, B, ts):
    @pl.when(pl.program_id(0) == 0)
    def _():
        st_sc[...] = jnp.zeros_like(st_sc)

    nl = SSM_WIDTH // LANE
    for b in range(B):
        for c in range(nl):
            tm_sc[c, pl.ds(b, ts, stride=B), :] = u_ref[b, :, c * LANE:(c + 1) * LANE]
    u = jnp.concatenate([tm_sc[c] for c in range(nl)], axis=1)
    x_sc[...] = _dot(u.astype(BF16), bb_ref[...])
    ar = jnp.broadcast_to(a_ref[0:1, :], (B, SSM_LANES))
    ai = jnp.broadcast_to(a_ref[1:2, :], (B, SSM_LANES))

    def step(t, carry):
        xr, xi = carry
        r = pl.multiple_of(t * B, B)
        br = x_sc[pl.ds(r, B), 0:SSM_LANES]
        bi = x_sc[pl.ds(r, B), SSM_LANES:2 * SSM_LANES]
        nr = ar * xr - ai * xi + br
        ni = ar * xi + ai * xr + bi
        x_sc[pl.ds(r, B), 0:SSM_LANES] = nr
        x_sc[pl.ds(r, B), SSM_LANES:2 * SSM_LANES] = ni
        return nr, ni

    xr, xi = lax.fori_loop(0, ts, step, (st_sc[0], st_sc[1]))
    st_sc[0] = xr
    st_sc[1] = xi

    y = _dot(x_sc[...].astype(BF16), cc_ref[...]) + d_ref[...] * u
    y = _gelu_tanh(y)
    y = y * _sigmoid(_dot(y.astype(BF16), wg_ref[...]))
    ms = _dot_split(y * y, gm_ref[...])
    y = y * lax.rsqrt(ms + EPS) * gain_ref[...]
    for c in range(nl):
        tm_sc[c] = y[:, c * LANE:(c + 1) * LANE]
    for b in range(B):
        o_ref[b] = jnp.concatenate(
            [tm_sc[c, pl.ds(b, ts, stride=B), :] for c in range(nl)], axis=1).astype(BF16)


def _s5(u, bb, a, cc, d, wg, layer, gm, gain, B, S, ts):
    rows = ts * B
    full = lambda i: (0, 0)
    lsel = lambda i: (layer, 0, 0)
    return pl.pallas_call(
        functools.partial(_s5_kernel, B=B, ts=ts),
        grid=(S // ts,),
        in_specs=[
            pl.BlockSpec((B, ts, SSM_WIDTH), lambda i: (0, i, 0)),
            pl.BlockSpec((None, SSM_WIDTH, 2 * SSM_LANES), lsel),
            pl.BlockSpec((None, 2, SSM_LANES), lsel),
            pl.BlockSpec((None, 2 * SSM_LANES, SSM_WIDTH), lsel),
            pl.BlockSpec((1, SSM_WIDTH), full),
            pl.BlockSpec((None, SSM_WIDTH, SSM_WIDTH), lsel),
            pl.BlockSpec((SSM_WIDTH, SSM_WIDTH), full),
            pl.BlockSpec((1, SSM_WIDTH), full),
        ],
        out_specs=pl.BlockSpec((B, ts, SSM_WIDTH), lambda i: (0, i, 0)),
        out_shape=jax.ShapeDtypeStruct((B, S, SSM_WIDTH), BF16),
        scratch_shapes=[pltpu.VMEM((rows, 2 * SSM_LANES), F32), pltpu.VMEM((2, B, SSM_LANES), F32),
                        pltpu.VMEM((SSM_WIDTH // LANE, rows, LANE), F32)],
        compiler_params=pltpu.CompilerParams(
            dimension_semantics=("arbitrary",), vmem_limit_bytes=VMEM_LIMIT),
        name="s5",
    )(u, bb, a, cc, d, wg, gm, gain)


def _mlstm_kernel(q_ref, k_ref, v_ref, o_ref, gc_ref, gr_ref, cwq_ref, cwk_ref, bc_ref, br_ref, hm_ref,
                  gain_ref, y_ref, qt_sc, kt_sc, c_sc, m_sc, *, B):
    L, H, Dh, W = MLSTM_CHUNK, MLSTM_HEADS, MLSTM_HEAD_DIM, MLSTM_WIDTH

    @pl.when(pl.program_id(0) == 0)
    def _():
        qt_sc[...] = jnp.zeros_like(qt_sc)
        kt_sc[...] = jnp.zeros_like(kt_sc)
        c_sc[...] = jnp.zeros_like(c_sc)
        m_sc[...] = jnp.zeros_like(m_sc)

    row_l = lax.broadcasted_iota(jnp.int32, (L, L), 0)
    col_l = lax.broadcasted_iota(jnp.int32, (L, L), 1)
    causal = col_l <= row_l
    tri = causal.astype(F32)
    triu = (row_l <= col_l).astype(F32)
    lane_w = lax.broadcasted_iota(jnp.int32, (1, W), 1) // Dh
    lane_2w = (lax.broadcasted_iota(jnp.int32, (1, 2 * W), 1) % W) // Dh
    bd_mask = (lax.broadcasted_iota(jnp.int32, (W, 2 * W), 0) // Dh
               == (lax.broadcasted_iota(jnp.int32, (W, 2 * W), 1) % W) // Dh)
    row8 = lax.broadcasted_iota(jnp.int32, (SUBLANE, W), 0)
    cwq = cwq_ref[...]
    cwk = cwk_ref[...]
    ones_v = jnp.ones((L, W), BF16)

    def conv_silu(x, tail, w):
        acc = x * w[MLSTM_CONV - 1:MLSTM_CONV, :]
        for sft in range(1, MLSTM_CONV):
            xs = pltpu.roll(x, sft, 0)
            head = jnp.where(row8 < sft, pltpu.roll(tail, sft, 0), xs[:SUBLANE])
            xs = jnp.concatenate([head, xs[SUBLANE:]], axis=0)
            acc = acc + xs * w[MLSTM_CONV - 1 - sft:MLSTM_CONV - sft, :]
        return acc * _sigmoid(acc)

    def expand(cols, width_lanes):
        out = cols[H - 1]
        for hh in range(H - 2, -1, -1):
            out = jnp.where(width_lanes == hh, cols[hh], out)
        return out

    def per_batch(b, _):
        q_raw = q_ref[b]
        k_raw = k_ref[b]
        q = conv_silu(q_raw, qt_sc[b], cwq)
        k = conv_silu(k_raw, kt_sc[b], cwk) * (Dh ** -0.5)
        qt_sc[b] = q_raw[L - SUBLANE:, :]
        kt_sc[b] = k_raw[L - SUBLANE:, :]
        vaug = jnp.concatenate([v_ref[b], ones_v], axis=1)

        gc = gc_ref[b] + bc_ref[...]
        gr = gr_ref[b] + br_ref[...]
        bcol = _dot(tri, _log_sigmoid(gc), precision=HIGHEST)
        brow = _dot(_log_sigmoid(gr), triu, precision=HIGHEST)
        m_all = m_sc[b]

        w_intra, w_inter, e_mt, w_k, dec, m_new = [], [], [], [], [], []
        for hh in range(H):
            bc = bcol[:, H + hh:H + hh + 1]
            ic = gc[:, hh:hh + 1]
            brr = brow[H + hh:H + hh + 1, :]
            irr = gr[hh:hh + 1, :]
            m_prev = m_all[hh:hh + 1, 0:1]
            dm = jnp.where(causal, bc - brr + irr, NEG_INF)
            inter = bc + m_prev
            mt = jnp.maximum(inter, jnp.max(dm, axis=1, keepdims=True))
            w_intra.append(jnp.exp(dm - mt))
            w_inter.append(jnp.exp(inter - mt))
            e_mt.append(jnp.exp(-mt))
            b_last = bc[L - 1:L, :]
            logw = b_last - bc + ic
            mn = jnp.maximum(b_last + m_prev, jnp.max(logw, axis=0, keepdims=True))
            w_k.append(jnp.exp(logw - mn))
            dec.append(jnp.exp(b_last + m_prev - mn))
            m_new.append(mn)

        qb = q.astype(BF16)
        kb = k.astype(BF16)
        q_heads = jnp.concatenate([jnp.where(lane_w == hh, qb, jnp.zeros_like(qb)) for hh in range(H)], axis=0)
        s = _dot_nt(q_heads, kb) * jnp.concatenate(w_intra, axis=0)
        sv = _dot(s.astype(BF16), vaug)
        intra = jnp.where(lane_2w == 0, sv[0:L], 0.0)
        for hh in range(1, H):
            intra = intra + jnp.where(lane_2w == hh, sv[hh * L:(hh + 1) * L], 0.0)
        c_aug = c_sc[b]
        qc = _dot(qb, c_aug.astype(BF16))
        tot = expand(w_inter, lane_2w) * qc + intra
        den = jnp.maximum(jnp.abs(tot[:, W:]), expand(e_mt, lane_w))
        hout = tot[:, :W] / den
        y = _sigmoid(o_ref[b]) * hout
        ms = _dot_split(y * y, hm_ref[...])
        y_ref[b] = (y * lax.rsqrt(ms + EPS) * gain_ref[...]).astype(BF16)

        kw = (k * expand(w_k, lane_w)).astype(BF16)
        upd = _dot_tn(kw, vaug)
        c_sc[b] = expand(dec, lane_2w) * c_aug + jnp.where(bd_mask, upd, 0.0)
        for hh in range(H):
            m_sc[b, hh:hh + 1, :] = jnp.broadcast_to(m_new[hh], (1, LANE))
        return 0

    lax.fori_loop(0, B, per_batch, 0)


def _mlstm(mq, mk, mv, mo, gcol, grow, cwq, cwk, bcol, brow, hm, gain, B, S):
    L, W = MLSTM_CHUNK, MLSTM_WIDTH
    seq = lambda c: (0, c, 0)
    full = lambda c: (0, 0)
    return pl.pallas_call(
        functools.partial(_mlstm_kernel, B=B),
        grid=(S // L,),
        in_specs=[
            pl.BlockSpec((B, L, W), seq),
            pl.BlockSpec((B, L, W), seq),
            pl.BlockSpec((B, L, W), seq),
            pl.BlockSpec((B, L, W), seq),
            pl.BlockSpec((B, L, LANE), seq),
            pl.BlockSpec((B, SUBLANE, L), lambda c: (0, 0, c)),
            pl.BlockSpec((MLSTM_CONV, W), full),
            pl.BlockSpec((MLSTM_CONV, W), full),
            pl.BlockSpec((1, LANE), full),
            pl.BlockSpec((SUBLANE, 1), full),
            pl.BlockSpec((W, W), full),
            pl.BlockSpec((1, W), full),
        ],
        out_specs=pl.BlockSpec((B, L, W), seq),
        out_shape=jax.ShapeDtypeStruct((B, S, W), BF16),
        scratch_shapes=[
            pltpu.VMEM((B, SUBLANE, W), F32),
            pltpu.VMEM((B, SUBLANE, W), F32),
            pltpu.VMEM((B, W, 2 * W), F32),
            pltpu.VMEM((B, SUBLANE, LANE), F32),
        ],
        compiler_params=pltpu.CompilerParams(
            dimension_semantics=("arbitrary",), vmem_limit_bytes=VMEM_LIMIT),
        name="mlstm",
    )(mq, mk, mv, mo, gcol, grow, cwq, cwk, bcol, brow, hm, gain)


def _compress_kernel(c_ref, w1ab_ref, w1_ref, pe_ref, w2_ref, w2t_ref, o_ref, ot_ref, ch_sc):
    G, Dh = NSA_KV_GROUPS, NSA_HEAD_DIM
    rows = ch_sc.shape[0]
    n = rows // G
    for r in range(CMP_STRIDE):
        tok = c_ref[pl.ds(r, n, stride=CMP_STRIDE), :]
        for g in range(G):
            ch_sc[g * n:(g + 1) * n, r * Dh:(r + 1) * Dh] = tok[:, g * Dh:(g + 1) * Dh]
    ab = _dot(ch_sc[...].astype(BF16), w1ab_ref[...])
    const = _dot(pe_ref[...], w1_ref[...], precision=HIGHEST)
    hid = ab[:, :CMP_HIDDEN] + pltpu.roll(ab[:, CMP_HIDDEN:], rows - 1, 0) + const
    act = _gelu_tanh(hid).astype(BF16)
    o_ref[...] = _dot(act, w2_ref[...]).astype(BF16)
    ot_ref[...] = _dot_nt(w2t_ref[...], act).astype(BF16)


def _compress(ckv, w1ab, w1, pe, w2, w2t, layer, B, S):
    G, Dh = NSA_KV_GROUPS, NSA_HEAD_DIM
    n = S // CMP_STRIDE
    width = CMP_STRIDE * Dh
    wsel = lambda i, b: (layer, i, 0, 0)
    return pl.pallas_call(
        _compress_kernel,
        grid=(2, B),
        in_specs=[
            pl.BlockSpec((None, S, G * Dh), lambda i, b: (i, b, 0)),
            pl.BlockSpec((None, None, width, 2 * CMP_HIDDEN), wsel),
            pl.BlockSpec((None, None, 2 * width, CMP_HIDDEN), wsel),
            pl.BlockSpec((None, None, 1, 2 * width), wsel),
            pl.BlockSpec((None, None, CMP_HIDDEN, Dh), wsel),
            pl.BlockSpec((None, None, Dh, CMP_HIDDEN), wsel),
        ],
        out_specs=[pl.BlockSpec((None, None, G * n, Dh), lambda i, b: (i, b, 0, 0)),
                   pl.BlockSpec((None, None, Dh, G * n), lambda i, b: (i, b, 0, 0))],
        out_shape=[jax.ShapeDtypeStruct((2, B, G * n, Dh), BF16),
                   jax.ShapeDtypeStruct((2, B, Dh, G * n), BF16)],
        scratch_shapes=[pltpu.VMEM((G * n, width), F32)],
        compiler_params=pltpu.CompilerParams(
            dimension_semantics=("parallel", "parallel"), vmem_limit_bytes=VMEM_LIMIT),
        name="compress",
    )(ckv, w1ab, w1, pe, w2, w2t)


def _nsa_kernel(q_ref, kc_ref, vct_ref, ks_ref, vst_ref, kw_ref, vwt_ref, gtt_ref, gain_ref,
                ovt_ref, et_ref, o_ref, *, n_sel, n_top, ck, unroll):
    TQ, R, Dh, G = Q_BLOCK, NSA_REP, NSA_HEAD_DIM, NSA_KV_GROUPS
    groups = range(G)
    i = pl.program_id(1)
    t0 = i * TQ
    qs = [q_ref[g * R:(g + 1) * R].reshape(R * TQ, Dh) for g in groups]
    tq1 = t0 + lax.broadcasted_iota(jnp.int32, (1, TQ), 1)
    heads = lambda t: jnp.concatenate([t] * R, axis=1)

    ncmp = kc_ref.shape[0] // G
    nwb = WINDOW // TQ + 1
    wb0 = jnp.maximum(i - WINDOW // TQ, 0)
    ws = pl.multiple_of(wb0 * TQ, TQ)
    sc = [_dot_nt(kc_ref[g * ncmp:(g + 1) * ncmp, :], qs[g]) for g in groups]
    sw = [_dot_nt(kw_ref[g, pl.ds(ws, nwb * TQ), :], qs[g]) for g in groups]

    cend = lax.broadcasted_iota(jnp.int32, (ncmp, 1), 0) * CMP_STRIDE + (CMP_BLOCK - 1)
    cmask = heads(cend <= tq1)
    pc = []
    for g in groups:
        scm = jnp.where(cmask, sc[g], NEG_INF)
        ec = jnp.where(cmask, jnp.exp2(scm - jnp.max(scm, axis=0, keepdims=True)), 0.0)
        pc.append(ec * (1.0 / jnp.maximum(jnp.sum(ec, axis=0, keepdims=True), 1e-30)))
    oc = [_dot(vct_ref[:, g * ncmp:(g + 1) * ncmp], pc[g].astype(BF16)) for g in groups]

    imp = []
    for g in groups:
        psum = pc[g][:, 0:TQ]
        for r in range(1, R):
            psum = psum + pc[g][:, r * TQ:(r + 1) * TQ]
        imp.append(_dot(ovt_ref[...], psum, precision=HIGHEST))

    kpos = ws + lax.broadcasted_iota(jnp.int32, (nwb * TQ, 1), 0)
    wbias = heads(jnp.where((kpos <= tq1) & (tq1 - kpos < WINDOW), 0.0, NEG_INF))
    ow, l_w = [], []
    for g in groups:
        swb = sw[g] + wbias
        pw = jnp.exp2(swb - jnp.max(swb, axis=0, keepdims=True))
        vwt = jnp.concatenate([vwt_ref[g, wb0 + j] for j in range(nwb)], axis=1)
        ow.append(_dot(vwt, pw.astype(BF16)))
        l_w.append(jnp.sum(pw, axis=0, keepdims=True))

    blk = lax.broadcasted_iota(jnp.int32, (n_sel, 1), 0)
    valid = blk * SEL_BLOCK <= tq1
    forced = (blk == 0) | (blk == tq1 // SEL_BLOCK)
    selb = []
    for g in groups:
        val = jnp.where(forced, FORCE_SCORE, jnp.where(valid, imp[g], -FORCE_SCORE))
        rank = jnp.zeros((n_sel, TQ), F32)
        for jp in range(n_sel):
            other = val[jp:jp + 1, :]
            wins = jnp.where(blk > jp, jnp.where(other >= val, 1.0, 0.0), jnp.where(other > val, 1.0, 0.0))
            rank = rank + wins
        selb.append(jnp.where(rank < n_top, 0.0, NEG_INF).astype(BF16))

    def scores(g, c):
        k0 = pl.multiple_of(c * ck, ck)
        kpos = k0 + lax.broadcasted_iota(jnp.int32, (ck, 1), 0)
        bias = jnp.where(kpos <= tq1, _dot(et_ref[c], selb[g]), NEG_INF)
        return _dot_nt(ks_ref[g, pl.ds(k0, ck), :], qs[g]) + heads(bias)

    def update(g, c, s, carry):
        m, l, acc = carry
        mn = jnp.maximum(m, jnp.max(s, axis=0, keepdims=True))
        alpha = jnp.exp2(m - mn)
        p = jnp.exp2(s - mn)
        l = alpha * l + jnp.sum(p, axis=0, keepdims=True)
        acc = alpha * acc + _dot(vst_ref[g, c], p.astype(BF16))
        return mn, l, acc

    def chunk_group(cg, carry):
        cs = [cg * unroll + sub for sub in range(unroll)]
        ss = [[scores(g, c) for g in groups] for c in cs]
        carry = list(carry)
        for c, s in zip(cs, ss):
            for g in groups:
                carry[g] = update(g, c, s[g], carry[g])
        return tuple(carry)

    n_chunks = (t0 + TQ + ck - 1) // ck
    init = tuple((jnp.full((1, R * TQ), NEG_INF, F32), jnp.zeros((1, R * TQ), F32),
                  jnp.zeros((Dh, R * TQ), F32)) for _ in groups)
    sel = lax.fori_loop(0, (n_chunks + unroll - 1) // unroll, chunk_group, init)

    normed = []
    for g in groups:
        gs = _sigmoid(gtt_ref[g])
        _, l_s, acc_s = sel[g]
        for r in range(R):
            ln = slice(r * TQ, (r + 1) * TQ)
            o = (gs[3 * r:3 * r + 1, :] * oc[g][:, ln]
                 + (gs[3 * r + 1:3 * r + 2, :] / l_s[:, ln]) * acc_s[:, ln]
                 + (gs[3 * r + 2:3 * r + 3, :] / l_w[g][:, ln]) * ow[g][:, ln])
            ms = jnp.mean(o * o, axis=0, keepdims=True)
            normed.append(o * lax.rsqrt(ms + EPS) * gain_ref[g * R + r])
    for pair in range(G * R // 2):
        both = jnp.concatenate(normed[2 * pair:2 * pair + 2], axis=0)
        o_ref[:, pair * 2 * Dh:(pair + 1) * 2 * Dh] = jnp.transpose(both).astype(BF16)


def _nsa(aq, cmp_k, cmp_vt, ks, vst, kw, vwt, gates_t, gain, consts, B, S):
    G, H, TQ, Dh = NSA_KV_GROUPS, NSA_HEADS, Q_BLOCK, NSA_HEAD_DIM
    nq = S // TQ
    ncmp = S // CMP_STRIDE
    n_sel = S // SEL_BLOCK
    ovt, emat_t, ck = consts
    k_spec = pl.BlockSpec((None, G, S, Dh), lambda b, i: (b, 0, 0, 0))
    return pl.pallas_call(
        functools.partial(_nsa_kernel, n_sel=n_sel, n_top=min(SEL_TOPN, n_sel), ck=ck, unroll=SEL_UNROLL),
        grid=(B, nq),
        in_specs=[
            pl.BlockSpec((None, H, TQ, Dh), lambda b, i: (b, 0, i, 0)),
            pl.BlockSpec((None, None, G * ncmp, Dh), lambda b, i: (0, b, 0, 0)),
            pl.BlockSpec((None, None, Dh, G * ncmp), lambda b, i: (1, b, 0, 0)),
            k_spec,
            pl.BlockSpec((None, G, S // ck, Dh, ck), lambda b, i: (b, 0, 0, 0, 0)),
            k_spec,
            pl.BlockSpec((None, G, S // TQ, Dh, TQ), lambda b, i: (b, 0, 0, 0, 0)),
            pl.BlockSpec((G, None, 2 * SUBLANE, TQ), lambda b, i: (0, b, 0, i)),
            pl.BlockSpec((H, Dh, 1), lambda b, i: (0, 0, 0)),
            pl.BlockSpec(ovt.shape, lambda b, i: (0, 0)),
            pl.BlockSpec(emat_t.shape, lambda b, i: (0, 0, 0)),
        ],
        out_specs=pl.BlockSpec((TQ, H * Dh), lambda b, i: (b * nq + i, 0)),
        out_shape=jax.ShapeDtypeStruct((B * S, H * Dh), BF16),
        compiler_params=pltpu.CompilerParams(
            dimension_semantics=("parallel", "arbitrary"), vmem_limit_bytes=VMEM_LIMIT),
        name="nsa",
    )(aq, cmp_k, cmp_vt, ks, vst, kw, vwt, gates_t, gain, ovt, emat_t)


def _nsa_consts(S):
    n_cmp = S // CMP_STRIDE
    n_sel = S // SEL_BLOCK
    ck = 256
    i = np.arange(n_cmp)[:, None]
    j = np.arange(n_sel)[None, :]
    lo = np.maximum(i * CMP_STRIDE, j * SEL_BLOCK)
    hi = np.minimum(i * CMP_STRIDE + CMP_BLOCK, (j + 1) * SEL_BLOCK)
    ov = np.maximum(hi - lo, 0) / CMP_STRIDE
    ov[n_cmp - 1] = 0.0
    key = np.arange(S)
    emat_t = (key[:, None] // SEL_BLOCK == np.arange(n_sel)[None, :]).astype(np.float32)
    return (jnp.asarray(ov.T, F32), jnp.asarray(emat_t.reshape(S // ck, ck, n_sel), BF16), ck)


def _chunked_t(v, width):
    B, G, S, Dh = v.shape
    return v.reshape(B, G, S // width, width, Dh).swapaxes(-1, -2)


def _outproj_kernel(h_ref, ys_ref, ym_ref, yn_ref, w_ref, g_ref, o_ref):
    acc = _dot(ys_ref[...], w_ref[0:SSM_WIDTH, :])
    acc = acc + _dot(ym_ref[...], w_ref[SSM_WIDTH:SSM_WIDTH + MLSTM_WIDTH, :])
    acc = acc + _dot(yn_ref[...], w_ref[SSM_WIDTH + MLSTM_WIDTH:, :])
    ms = jnp.mean(acc * acc, axis=-1, keepdims=True)
    o_ref[...] = h_ref[...] + acc * lax.rsqrt(ms + EPS) * g_ref[...]


def _outproj(h2, y_ssm, y_mls, y_nsa, w, layer, gain, B, S, ts):
    nt = S // ts
    row = lambda b, i: (b * nt + i, 0)
    full = lambda b, i: (0, 0)
    return pl.pallas_call(
        _outproj_kernel,
        grid=(B, nt),
        in_specs=[
            pl.BlockSpec((ts, D_MODEL), row),
            pl.BlockSpec((ts, SSM_WIDTH), row),
            pl.BlockSpec((ts, MLSTM_WIDTH), row),
            pl.BlockSpec((ts, NSA_WIDTH), row),
            pl.BlockSpec((None, D_MODEL, D_MODEL), lambda b, i: (layer, 0, 0)),
            pl.BlockSpec((1, D_MODEL), full),
        ],
        out_specs=pl.BlockSpec((ts, D_MODEL), row),
        out_shape=jax.ShapeDtypeStruct((B * S, D_MODEL), F32),
        compiler_params=pltpu.CompilerParams(
            dimension_semantics=("parallel", "parallel"), vmem_limit_bytes=VMEM_LIMIT),
        name="outproj",
    )(h2, y_ssm, y_mls, y_nsa, w, gain)


def _mlp_kernel(h_ref, g1_ref, w1_ref, w2_ref, g2_ref, o_ref, u_sc, acc_sc):
    kf = pl.program_id(1)

    @pl.when(kf == 0)
    def _():
        x = h_ref[...]
        ms = jnp.mean(x * x, axis=-1, keepdims=True)
        u_sc[...] = (x * lax.rsqrt(ms + EPS) * g1_ref[...]).astype(BF16)
        acc_sc[...] = jnp.zeros_like(acc_sc)

    a = jnp.maximum(_dot(u_sc[...], w1_ref[...]), 0.0)
    acc_sc[...] += _dot((a * a).astype(BF16), w2_ref[...])

    @pl.when(kf == pl.num_programs(1) - 1)
    def _():
        f = acc_sc[...]
        ms = jnp.mean(f * f, axis=-1, keepdims=True)
        o_ref[...] = h_ref[...] + f * lax.rsqrt(ms + EPS) * g2_ref[...]


def _mlp(h2, g1, w1, w2, layer, g2, tm, tf):
    rows = h2.shape[0]
    return pl.pallas_call(
        _mlp_kernel,
        grid=(rows // tm, D_FF // tf),
        in_specs=[
            pl.BlockSpec((tm, D_MODEL), lambda i, k: (i, 0)),
            pl.BlockSpec((1, D_MODEL), lambda i, k: (0, 0)),
            pl.BlockSpec((None, D_MODEL, tf), lambda i, k: (layer, 0, k)),
            pl.BlockSpec((None, tf, D_MODEL), lambda i, k: (layer, k, 0)),
            pl.BlockSpec((1, D_MODEL), lambda i, k: (0, 0)),
        ],
        out_specs=pl.BlockSpec((tm, D_MODEL), lambda i, k: (i, 0)),
        out_shape=jax.ShapeDtypeStruct((rows, D_MODEL), F32),
        scratch_shapes=[pltpu.VMEM((tm, D_MODEL), BF16), pltpu.VMEM((tm, D_MODEL), F32)],
        compiler_params=pltpu.CompilerParams(
            dimension_semantics=("parallel", "arbitrary"), vmem_limit_bytes=VMEM_LIMIT),
        name="mlp",
    )(h2, g1, w1, w2, g2)


def _inproj_pieces():
    return ((0, 1280), (1288, 1800), (1800, 1928), (2056, 2184), (2312, 2440),
            (1928, 2056), (2184, 2312), (2440, 2568),
            (1280, 1288), (2568, 2580), (None, LANE - 20),
            (None, GATE_COL), (2580, 2592), (None, LANE - GATE_COL - 12))


def _permute_w_in(w_in):
    parts = []
    for a, b in _inproj_pieces():
        if a is None:
            parts.append(jnp.zeros(w_in.shape[:-1] + (b,), BF16))
        else:
            parts.append(w_in[..., a:b].astype(BF16))
    out = jnp.concatenate(parts, axis=-1)
    assert out.shape[-1] == D_INP
    return out


def _rope_tables(positions):
    inv = ROPE_THETA ** (-jnp.arange(0, ROPE_DIMS, 2, dtype=F32) / ROPE_DIMS)
    ang = positions.astype(F32)[..., None] * inv
    cos, sin = jnp.cos(ang), jnp.sin(ang)
    z = jnp.zeros_like(cos)
    rest = NSA_HEAD_DIM - ROPE_DIMS
    pad_one = jnp.ones(cos.shape[:-1] + (rest,), F32)
    pad_zero = jnp.zeros(cos.shape[:-1] + (rest,), F32)
    rc = jnp.concatenate([cos, cos, pad_one], axis=-1)
    rs1 = jnp.concatenate([-sin, z, pad_zero], axis=-1)
    rs2 = jnp.concatenate([z, sin, pad_zero], axis=-1)
    tile = lambda t: jnp.tile(t, (1, 1, LANE // NSA_HEAD_DIM)).reshape(-1, LANE)
    return tile(rc), tile(rs1), tile(rs2)


def _s5_params(lam_re, lam_im, b_re, b_im, c_re, c_im, log_dt):
    G, P, Hc = SSM_GROUPS, SSM_STATE, SSM_GROUP
    dt = jnp.exp(log_dt)[:, None]
    mag = jnp.exp(lam_re * dt)
    ang = lam_im * dt
    ab_re = mag * jnp.cos(ang)
    ab_im = mag * jnp.sin(ang)
    den = lam_re * lam_re + lam_im * lam_im
    g_re = ((ab_re - 1.0) * lam_re + ab_im * lam_im) / den
    g_im = (ab_im * lam_re - (ab_re - 1.0) * lam_im) / den
    bb_re = g_re[..., None] * b_re - g_im[..., None] * b_im
    bb_im = g_re[..., None] * b_im + g_im[..., None] * b_re
    eye = jnp.eye(G, dtype=F32)
    blockdiag_in = lambda t: jnp.einsum('gph,gk->ghkp', t, eye).reshape(G * Hc, G * P)
    blockdiag_out = lambda t: jnp.einsum('ghp,gk->gpkh', t, eye).reshape(G * P, G * Hc)
    bb = jnp.concatenate([blockdiag_in(bb_re), blockdiag_in(bb_im)], axis=1).astype(BF16)
    cc = jnp.concatenate([blockdiag_out(c_re), -blockdiag_out(c_im)], axis=0).astype(BF16)
    a = jnp.stack([ab_re.reshape(-1), ab_im.reshape(-1)], axis=0)
    return bb, a, cc


def _group_mean_matrix(width, group):
    idx = np.arange(width) // group
    return jnp.asarray((idx[:, None] == idx[None, :]).astype(np.float32) / group, BF16)


def kernel(x, positions, ln_mix_pre, ln_mix_post, ln_mlp_pre, ln_mlp_post, w_in, w_out, ssm_lambda_re, ssm_lambda_im, ssm_b_re, ssm_b_im, ssm_c_re, ssm_c_im, ssm_d, ssm_log_dt, ssm_w_glu, mlstm_conv, mlstm_b_i, mlstm_b_f, cmp_pe_k, cmp_w1_k, cmp_w2_k, cmp_pe_v, cmp_w1_v, cmp_w2_v, gn_ssm, gn_mlstm, gn_nsa, mlp_w1, mlp_w2):
    B, S, D = x.shape
    depth = w_in.shape[0]
    assert D == D_MODEL and B == SUBLANE and S % 512 == 0 and S >= WINDOW + Q_BLOCK
    G, H = NSA_KV_GROUPS, MLSTM_HEADS
    ts_proj = 512
    ts_scan = 128

    rc, rs1, rs2 = _rope_tables(positions)
    w_in_p = _permute_w_in(w_in)
    w_out_b = w_out.astype(BF16)
    w1_b = mlp_w1.astype(BF16)
    w2_b = mlp_w2.astype(BF16)
    wglu_b = ssm_w_glu.astype(BF16)
    gm_ssm = _group_mean_matrix(SSM_WIDTH, SSM_GROUP)
    hm_mls = _group_mean_matrix(MLSTM_WIDTH, MLSTM_HEAD_DIM)
    consts = _nsa_consts(S)
    half = CMP_STRIDE * NSA_HEAD_DIM
    w1ab = jnp.stack([jnp.concatenate([cmp_w1_k[:, :half], cmp_w1_k[:, half:]], axis=-1),
                      jnp.concatenate([cmp_w1_v[:, :half], cmp_w1_v[:, half:]], axis=-1)], axis=1).astype(BF16)
    w1f = jnp.stack([cmp_w1_k, cmp_w1_v], axis=1)
    pef = jnp.stack([cmp_pe_k.reshape(depth, 1, -1), cmp_pe_v.reshape(depth, 1, -1)], axis=1)
    w2c = jnp.stack([cmp_w2_k, cmp_w2_v], axis=1).astype(BF16)
    w2ct = jnp.swapaxes(w2c, -1, -2)
    zeros_g = jnp.zeros((depth, LANE - 2 * H), F32)
    bias_col = jnp.concatenate([mlstm_b_i, mlstm_b_f, zeros_g], axis=-1)[:, None, :]
    bias_row = jnp.concatenate([mlstm_b_i, mlstm_b_f], axis=-1)[:, :, None]

    bb, a, cc = jax.vmap(_s5_params)(ssm_lambda_re, ssm_lambda_im, ssm_b_re, ssm_b_im, ssm_c_re, ssm_c_im,
                                     ssm_log_dt)
    sh3 = lambda t: t.reshape(B, S, t.shape[-1])

    h = x.reshape(B * S, D)
    for l in range(depth):
        (su, mq, mk, mv, mo, aq, ckv, sk, wk, sv, wv, gates) = _inproj(
            h, ln_mix_pre[l][None], w_in_p, l, rc, rs1, rs2, B, S, ts_proj)

        y_ssm = _s5(sh3(su), bb, a, cc, ssm_d[l][None], wglu_b, l, gm_ssm, gn_ssm[l][None], B, S, ts_scan)

        gcol = sh3(gates[0])
        grow = jnp.swapaxes(gcol[:, :, :2 * H], 1, 2)
        y_mls = _mlstm(sh3(mq), sh3(mk), sh3(mv), sh3(mo), gcol, grow,
                       mlstm_conv[l][:, :MLSTM_WIDTH], mlstm_conv[l][:, MLSTM_WIDTH:],
                       bias_col[l], bias_row[l], hm_mls, gn_mlstm[l][None], B, S)

        cmp_k, cmp_t = _compress(ckv, w1ab, w1f, pef, w2c, w2ct, l, B, S)
        gates_t = jnp.swapaxes(
            gates.reshape(G, B, S, LANE)[..., GATE_COL:GATE_COL + 2 * SUBLANE], -1, -2)
        y_nsa = _nsa(aq, cmp_k, cmp_t, sk, _chunked_t(sv, consts[2]), wk, _chunked_t(wv, Q_BLOCK), gates_t,
                     gn_nsa[l].reshape(NSA_HEADS, NSA_HEAD_DIM, 1), consts, B, S)

        h = _outproj(h, y_ssm.reshape(B * S, SSM_WIDTH), y_mls.reshape(B * S, MLSTM_WIDTH), y_nsa,
                     w_out_b, l, ln_mix_post[l][None], B, S, ts_proj)
        h = _mlp(h, ln_mlp_pre[l][None], w1_b, w2_b, l, ln_mlp_post[l][None], 1024, 512)
    return h.reshape(B, S, D)
```

```python
import functools
import math

import numpy as np
import jax
import jax.numpy as jnp
from jax import lax
from jax.experimental import pallas as pl
from jax.experimental.pallas import tpu as pltpu

F32 = jnp.float32
BF16 = jnp.bfloat16
HIGHEST = lax.Precision.HIGHEST

D_MODEL = 1024
DEPTH = 4
SSM_WIDTH = 256
SSM_GROUP = 16
SSM_GROUPS = 16
SSM_STATE = 64
SSM_LANES = SSM_GROUPS * SSM_STATE
MLSTM_WIDTH = 256
MLSTM_HEADS = 4
MLSTM_HEAD_DIM = 64
MLSTM_CHUNK = 128
MLSTM_CONV = 4
NSA_WIDTH = 512
NSA_HEAD_DIM = 64
NSA_HEADS = 8
NSA_KV_GROUPS = 2
NSA_REP = NSA_HEADS // NSA_KV_GROUPS
NSA_KV_WIDTH = 128
CMP_BLOCK = 32
CMP_STRIDE = 16
CMP_HIDDEN = 256
SEL_BLOCK = 64
SEL_TOPN = 8
WINDOW = 256
Q_BLOCK = 128
FORCE_SCORE = 1e4
NEG_INF = -1e30
ROPE_THETA = 500000.0
ROPE_DIMS = 16
ROPE_HALF = 8
D_FF = 4096
EPS = 1e-6
D_IN = 2592

LANE = 128
SUBLANE = 8
VMEM_LIMIT = 56 * 1024 * 1024

C_SU, C_MQ, C_MK, C_MV, C_MO = 0, 256, 512, 768, 1024
C_AQ, C_CK, C_SK, C_WK = 1280, 1792, 1920, 2048
C_CV, C_SV, C_WV = 2176, 2304, 2432
C_G0, C_G1 = 2560, 2688
D_INP = 2816
GATE_COL = 16
Q_SCALE = NSA_HEAD_DIM ** -0.5 * math.log2(math.e)
SEL_UNROLL = 2


def _dot(a, b, precision=None):
    return jnp.dot(a, b, preferred_element_type=F32, precision=precision)


def _dot_nt(a, b):
    return lax.dot_general(a, b, (((1,), (1,)), ((), ())), preferred_element_type=F32)


def _dot_tn(a, b):
    return lax.dot_general(a, b, (((0,), (0,)), ((), ())), preferred_element_type=F32)


def _sigmoid(x):
    return 1.0 / (1.0 + jnp.exp(-x))


def _dot_split(x, w_bf16):
    hi = x.astype(BF16)
    lo = (x - hi.astype(F32)).astype(BF16)
    return _dot(hi, w_bf16) + _dot(lo, w_bf16)


def _gelu_tanh(x):
    return 0.5 * x * (1.0 + jnp.tanh(math.sqrt(2.0 / math.pi) * (x + 0.044715 * (x * x * x))))


def _log_sigmoid(x):
    return jnp.minimum(x, 0.0) - jnp.log(1.0 + jnp.exp(-jnp.abs(x)))


def _inproj_kernel(x_ref, g_ref, w_ref, rc_ref, rs1_ref, rs2_ref,
                   su_ref, mq_ref, mk_ref, mv_ref, mo_ref, aq_ref, ckv_ref, sk_ref, wk_ref,
                   svt_ref, wvt_ref, gt_ref, gtt_ref):
    x = x_ref[...]
    ms = jnp.mean(x * x, axis=-1, keepdims=True)
    u = (x * lax.rsqrt(ms + EPS) * g_ref[...]).astype(BF16)
    rc, rs1, rs2 = rc_ref[...], rs1_ref[...], rs2_ref[...]

    def mm(c0, width):
        return _dot(u, w_ref[:, c0:c0 + width])

    def rope(z):
        return z * rc + pltpu.roll(z, LANE - ROPE_HALF, 1) * rs1 + pltpu.roll(z, ROPE_HALF, 1) * rs2

    su_ref[...] = mm(C_SU, 256)
    mq_ref[...] = mm(C_MQ, 256)
    mk_ref[...] = mm(C_MK, 256)
    mv_ref[...] = mm(C_MV, 256).astype(BF16)
    mo_ref[...] = mm(C_MO, 256)
    def mm_pair(c0):
        z = mm(c0, 2 * LANE)
        return z[:, :LANE], z[:, LANE:]

    def put_heads(ref, first, z):
        ref[first] = z[:, :NSA_HEAD_DIM].astype(BF16)
        ref[first + 1] = z[:, NSA_HEAD_DIM:].astype(BF16)

    for j in range(NSA_HEADS // 4):
        for k, z in enumerate(mm_pair(C_AQ + 2 * LANE * j)):
            put_heads(aq_ref, 4 * j + 2 * k, rope(z) * Q_SCALE)
    z_ck, z_sk = mm_pair(C_CK)
    z_wk, z_cv = mm_pair(C_WK)
    z_sv, z_wv = mm_pair(C_SV)
    ckv_ref[0] = rope(z_ck)
    ckv_ref[1] = z_cv
    put_heads(sk_ref, 0, rope(z_sk))
    put_heads(wk_ref, 0, rope(z_wk))

    def put_chunks_t(ref, z):
        zt = jnp.transpose(z)
        width = ref.shape[-1]
        for g in range(NSA_KV_GROUPS):
            for j in range(ref.shape[1]):
                ref[g, j] = zt[g * NSA_HEAD_DIM:(g + 1) * NSA_HEAD_DIM, j * width:(j + 1) * width].astype(BF16)

    put_chunks_t(svt_ref, z_sv)
    put_chunks_t(wvt_ref, z_wv)
    z_g0, z_g1 = mm_pair(C_G0)
    gt_ref[...] = z_g0
    gtt_ref[0] = jnp.transpose(z_g0)
    gtt_ref[1] = jnp.transpose(z_g1)


def _inproj(h2, gain, w, layer, rc, rs1, rs2, B, S, ts, ck):
    nt = S // ts
    BS = B * S
    row = lambda b, i: (b * nt + i, 0)
    full = lambda b, i: (0, 0)
    headed = lambda b, i: (b, 0, i, 0)
    paired = lambda b, i: (0, b * nt + i, 0)
    in_specs = [
        pl.BlockSpec((ts, D_MODEL), row),
        pl.BlockSpec((1, D_MODEL), full),
        pl.BlockSpec((None, D_MODEL, D_INP), lambda b, i: (layer, 0, 0)),
        pl.BlockSpec((ts, LANE), row),
        pl.BlockSpec((ts, LANE), row),
        pl.BlockSpec((ts, LANE), row),
    ]
    kv_shape = jax.ShapeDtypeStruct((B, NSA_KV_GROUPS, S, NSA_HEAD_DIM), BF16)
    kv_spec = pl.BlockSpec((None, NSA_KV_GROUPS, ts, NSA_HEAD_DIM), headed)
    out_shape = [
        jax.ShapeDtypeStruct((BS, SSM_WIDTH), F32),
        jax.ShapeDtypeStruct((BS, MLSTM_WIDTH), F32),
        jax.ShapeDtypeStruct((BS, MLSTM_WIDTH), F32),
        jax.ShapeDtypeStruct((BS, MLSTM_WIDTH), BF16),
        jax.ShapeDtypeStruct((BS, MLSTM_WIDTH), F32),
        jax.ShapeDtypeStruct((B, NSA_HEADS, S, NSA_HEAD_DIM), BF16),
        jax.ShapeDtypeStruct((2, BS, NSA_KV_WIDTH), F32),
        kv_shape, kv_shape,
        jax.ShapeDtypeStruct((B, NSA_KV_GROUPS, S // ck, NSA_HEAD_DIM, ck), BF16),
        jax.ShapeDtypeStruct((B, NSA_KV_GROUPS, S // Q_BLOCK, NSA_HEAD_DIM, Q_BLOCK), BF16),
        jax.ShapeDtypeStruct((BS, LANE), F32),
        jax.ShapeDtypeStruct((NSA_KV_GROUPS, B, LANE, S), F32),
    ]
    out_specs = [
        pl.BlockSpec((ts, SSM_WIDTH), row),
        pl.BlockSpec((ts, MLSTM_WIDTH), row),
        pl.BlockSpec((ts, MLSTM_WIDTH), row),
        pl.BlockSpec((ts, MLSTM_WIDTH), row),
        pl.BlockSpec((ts, MLSTM_WIDTH), row),
        pl.BlockSpec((None, NSA_HEADS, ts, NSA_HEAD_DIM), headed),
        pl.BlockSpec((2, ts, NSA_KV_WIDTH), paired),
        kv_spec, kv_spec,
        pl.BlockSpec((None, NSA_KV_GROUPS, ts // ck, NSA_HEAD_DIM, ck), lambda b, i: (b, 0, i, 0, 0)),
        pl.BlockSpec((None, NSA_KV_GROUPS, ts // Q_BLOCK, NSA_HEAD_DIM, Q_BLOCK), lambda b, i: (b, 0, i, 0, 0)),
        pl.BlockSpec((ts, LANE), row),
        pl.BlockSpec((NSA_KV_GROUPS, None, LANE, ts), lambda b, i: (0, b, 0, i)),
    ]
    return pl.pallas_call(
        _inproj_kernel,
        grid=(B, nt),
        in_specs=in_specs,
        out_specs=out_specs,
        out_shape=out_shape,
        compiler_params=pltpu.CompilerParams(
            dimension_semantics=("parallel", "parallel"), vmem_limit_bytes=VMEM_LIMIT),
        name="inproj",
    )(h2, gain, w, rc, rs1, rs2)


def _s5_kernel(u_ref, bb_ref, a_ref, cc_ref, d_ref, wg_ref, gm_ref, gain_ref, o_ref, x_sc, st_sc, tm_sc, *, B, ts):
    @pl.when(pl.program_id(0) == 0)
    def _():
        st_sc[...] = jnp.zeros_like(st_sc)

    nl = SSM_WIDTH // LANE
    for b in range(B):
        for c in range(nl):
            tm_sc[c, pl.ds(b, ts, stride=B), :] = u_ref[b, :, c * LANE:(c + 1) * LANE]
    u = jnp.concatenate([tm_sc[c] for c in range(nl)], axis=1)
    x_sc[...] = _dot(u.astype(BF16), bb_ref[...])
    ar = jnp.broadcast_to(a_ref[0:1, :], (B, SSM_LANES))
    ai = jnp.broadcast_to(a_ref[1:2, :], (B, SSM_LANES))

    def step(t, carry):
        xr, xi = carry
        r = pl.multiple_of(t * B, B)
        br = x_sc[pl.ds(r, B), 0:SSM_LANES]
        bi = x_sc[pl.ds(r, B), SSM_LANES:2 * SSM_LANES]
        nr = ar * xr - ai * xi + br
        ni = ar * xi + ai * xr + bi
        x_sc[pl.ds(r, B), 0:SSM_LANES] = nr
        x_sc[pl.ds(r, B), SSM_LANES:2 * SSM_LANES] = ni
        return nr, ni

    xr, xi = lax.fori_loop(0, ts, step, (st_sc[0], st_sc[1]))
    st_sc[0] = xr
    st_sc[1] = xi

    y = _dot(x_sc[...].astype(BF16), cc_ref[...]) + d_ref[...] * u
    y = _gelu_tanh(y)
    y = y * _sigmoid(_dot(y.astype(BF16), wg_ref[...]))
    ms = _dot_split(y * y, gm_ref[...])
    y = y * lax.rsqrt(ms + EPS) * gain_ref[...]
    for c in range(nl):
        tm_sc[c] = y[:, c * LANE:(c + 1) * LANE]
    for b in range(B):
        o_ref[b] = jnp.concatenate(
            [tm_sc[c, pl.ds(b, ts, stride=B), :] for c in range(nl)], axis=1).astype(BF16)


def _s5(u, bb, a, cc, d, wg, layer, gm, gain, B, S, ts):
    rows = ts * B
    full = lambda i: (0, 0)
    lsel = lambda i: (layer, 0, 0)
    return pl.pallas_call(
        functools.partial(_s5_kernel, B=B, ts=ts),
        grid=(S // ts,),
        in_specs=[
            pl.BlockSpec((B, ts, SSM_WIDTH), lambda i: (0, i, 0)),
            pl.BlockSpec((None, SSM_WIDTH, 2 * SSM_LANES), lsel),
            pl.BlockSpec((None, 2, SSM_LANES), lsel),
            pl.BlockSpec((None, 2 * SSM_LANES, SSM_WIDTH), lsel),
            pl.BlockSpec((1, SSM_WIDTH), full),
            pl.BlockSpec((None, SSM_WIDTH, SSM_WIDTH), lsel),
            pl.BlockSpec((SSM_WIDTH, SSM_WIDTH), full),
            pl.BlockSpec((1, SSM_WIDTH), full),
        ],
        out_specs=pl.BlockSpec((B, ts, SSM_WIDTH), lambda i: (0, i, 0)),
        out_shape=jax.ShapeDtypeStruct((B, S, SSM_WIDTH), BF16),
        scratch_shapes=[pltpu.VMEM((rows, 2 * SSM_LANES), F32), pltpu.VMEM((2, B, SSM_LANES), F32),
                        pltpu.VMEM((SSM_WIDTH // LANE, rows, LANE), F32)],
        compiler_params=pltpu.CompilerParams(
            dimension_semantics=("arbitrary",), vmem_limit_bytes=VMEM_LIMIT),
        name="s5",
    )(u, bb, a, cc, d, wg, gm, gain)


def _mlstm_kernel(q_ref, k_ref, v_ref, o_ref, gc_ref, gr_ref, cwq_ref, cwk_ref, bc_ref, br_ref, hm_ref,
                  gain_ref, y_ref, qt_sc, kt_sc, c_sc, m_sc, *, B):
    L, H, Dh, W = MLSTM_CHUNK, MLSTM_HEADS, MLSTM_HEAD_DIM, MLSTM_WIDTH

    @pl.when(pl.program_id(0) == 0)
    def _():
        qt_sc[...] = jnp.zeros_like(qt_sc)
        kt_sc[...] = jnp.zeros_like(kt_sc)
        c_sc[...] = jnp.zeros_like(c_sc)
        m_sc[...] = jnp.zeros_like(m_sc)

    row_l = lax.broadcasted_iota(jnp.int32, (L, L), 0)
    col_l = lax.broadcasted_iota(jnp.int32, (L, L), 1)
    causal = col_l <= row_l
    tri = causal.astype(F32)
    triu = (row_l <= col_l).astype(F32)
    lane_w = lax.broadcasted_iota(jnp.int32, (1, W), 1) // Dh
    lane_2w = (lax.broadcasted_iota(jnp.int32, (1, 2 * W), 1) % W) // Dh
    bd_mask = (lax.broadcasted_iota(jnp.int32, (W, 2 * W), 0) // Dh
               == (lax.broadcasted_iota(jnp.int32, (W, 2 * W), 1) % W) // Dh)
    row8 = lax.broadcasted_iota(jnp.int32, (SUBLANE, W), 0)
    cwq = cwq_ref[...]
    cwk = cwk_ref[...]
    ones_v = jnp.ones((L, W), BF16)

    def conv_silu(x, tail, w):
        acc = x * w[MLSTM_CONV - 1:MLSTM_CONV, :]
        for sft in range(1, MLSTM_CONV):
            xs = pltpu.roll(x, sft, 0)
            head = jnp.where(row8 < sft, pltpu.roll(tail, sft, 0), xs[:SUBLANE])
            xs = jnp.concatenate([head, xs[SUBLANE:]], axis=0)
            acc = acc + xs * w[MLSTM_CONV - 1 - sft:MLSTM_CONV - sft, :]
        return acc * _sigmoid(acc)

    def expand(cols, width_lanes):
        out = cols[H - 1]
        for hh in range(H - 2, -1, -1):
            out = jnp.where(width_lanes == hh, cols[hh], out)
        return out

    def per_batch(b, _):
        q_raw = q_ref[b]
        k_raw = k_ref[b]
        q = conv_silu(q_raw, qt_sc[b], cwq)
        k = conv_silu(k_raw, kt_sc[b], cwk) * (Dh ** -0.5)
        qt_sc[b] = q_raw[L - SUBLANE:, :]
        kt_sc[b] = k_raw[L - SUBLANE:, :]
        vaug = jnp.concatenate([v_ref[b], ones_v], axis=1)

        gc = gc_ref[b] + bc_ref[...]
        gr = gr_ref[b] + br_ref[...]
        bcol = _dot(tri, _log_sigmoid(gc), precision=HIGHEST)
        brow = _dot(_log_sigmoid(gr), triu, precision=HIGHEST)
        m_all = m_sc[b]

        w_intra, w_inter, e_mt, w_k, dec, m_new = [], [], [], [], [], []
        for hh in range(H):
            bc = bcol[:, H + hh:H + hh + 1]
            ic = gc[:, hh:hh + 1]
            brr = brow[H + hh:H + hh + 1, :]
            irr = gr[hh:hh + 1, :]
            m_prev = m_all[hh:hh + 1, 0:1]
            dm = jnp.where(causal, bc - brr + irr, NEG_INF)
            inter = bc + m_prev
            mt = jnp.maximum(inter, jnp.max(dm, axis=1, keepdims=True))
            w_intra.append(jnp.exp(dm - mt))
            w_inter.append(jnp.exp(inter - mt))
            e_mt.append(jnp.exp(-mt))
            b_last = bc[L - 1:L, :]
            logw = b_last - bc + ic
            mn = jnp.maximum(b_last + m_prev, jnp.max(logw, axis=0, keepdims=True))
            w_k.append(jnp.exp(logw - mn))
            dec.append(jnp.exp(b_last + m_prev - mn))
            m_new.append(mn)

        qb = q.astype(BF16)
        kb = k.astype(BF16)
        q_heads = jnp.concatenate([jnp.where(lane_w == hh, qb, jnp.zeros_like(qb)) for hh in range(H)], axis=0)
        s = _dot_nt(q_heads, kb) * jnp.concatenate(w_intra, axis=0)
        sv = _dot(s.astype(BF16), vaug)
        intra = jnp.where(lane_2w == 0, sv[0:L], 0.0)
        for hh in range(1, H):
            intra = intra + jnp.where(lane_2w == hh, sv[hh * L:(hh + 1) * L], 0.0)
        c_aug = c_sc[b]
        qc = _dot(qb, c_aug.astype(BF16))
        tot = expand(w_inter, lane_2w) * qc + intra
        den = jnp.maximum(jnp.abs(tot[:, W:]), expand(e_mt, lane_w))
        hout = tot[:, :W] / den
        y = _sigmoid(o_ref[b]) * hout
        ms = _dot_split(y * y, hm_ref[...])
        y_ref[b] = (y * lax.rsqrt(ms + EPS) * gain_ref[...]).astype(BF16)

        kw = (k * expand(w_k, lane_w)).astype(BF16)
        upd = _dot_tn(kw, vaug)
        c_sc[b] = expand(dec, lane_2w) * c_aug + jnp.where(bd_mask, upd, 0.0)
        for hh in range(H):
            m_sc[b, hh:hh + 1, :] = jnp.broadcast_to(m_new[hh], (1, LANE))
        return 0

    lax.fori_loop(0, B, per_batch, 0)


def _mlstm(mq, mk, mv, mo, gcol, grow, cwq, cwk, bcol, brow, hm, gain, B, S):
    L, W = MLSTM_CHUNK, MLSTM_WIDTH
    seq = lambda c: (0, c, 0)
    full = lambda c: (0, 0)
    return pl.pallas_call(
        functools.partial(_mlstm_kernel, B=B),
        grid=(S // L,),
        in_specs=[
            pl.BlockSpec((B, L, W), seq),
            pl.BlockSpec((B, L, W), seq),
            pl.BlockSpec((B, L, W), seq),
            pl.BlockSpec((B, L, W), seq),
            pl.BlockSpec((B, L, LANE), seq),
            pl.BlockSpec((None, B, SUBLANE, L), lambda c: (0, 0, 0, c)),
            pl.BlockSpec((MLSTM_CONV, W), full),
            pl.BlockSpec((MLSTM_CONV, W), full),
            pl.BlockSpec((1, LANE), full),
            pl.BlockSpec((SUBLANE, 1), full),
            pl.BlockSpec((W, W), full),
            pl.BlockSpec((1, W), full),
        ],
        out_specs=pl.BlockSpec((B, L, W), seq),
        out_shape=jax.ShapeDtypeStruct((B, S, W), BF16),
        scratch_shapes=[
            pltpu.VMEM((B, SUBLANE, W), F32),
            pltpu.VMEM((B, SUBLANE, W), F32),
            pltpu.VMEM((B, W, 2 * W), F32),
            pltpu.VMEM((B, SUBLANE, LANE), F32),
        ],
        compiler_params=pltpu.CompilerParams(
            dimension_semantics=("arbitrary",), vmem_limit_bytes=VMEM_LIMIT),
        name="mlstm",
    )(mq, mk, mv, mo, gcol, grow, cwq, cwk, bcol, brow, hm, gain)


def _compress_kernel(c_ref, w1ab_ref, w1_ref, pe_ref, w2_ref, w2t_ref, o_ref, ot_ref, ch_sc):
    G, Dh = NSA_KV_GROUPS, NSA_HEAD_DIM
    rows = ch_sc.shape[0]
    n = rows // G
    for r in range(CMP_STRIDE):
        tok = c_ref[pl.ds(r, n, stride=CMP_STRIDE), :]
        for g in range(G):
            ch_sc[g * n:(g + 1) * n, r * Dh:(r + 1) * Dh] = tok[:, g * Dh:(g + 1) * Dh]
    ab = _dot(ch_sc[...].astype(BF16), w1ab_ref[...])
    const = _dot(pe_ref[...], w1_ref[...], precision=HIGHEST)
    hid = ab[:, :CMP_HIDDEN] + pltpu.roll(ab[:, CMP_HIDDEN:], rows - 1, 0) + const
    act = _gelu_tanh(hid).astype(BF16)
    o_ref[...] = _dot(act, w2_ref[...]).astype(BF16)
    ot_ref[...] = _dot_nt(w2t_ref[...], act).astype(BF16)


def _compress(ckv, w1ab, w1, pe, w2, w2t, layer, B, S):
    G, Dh = NSA_KV_GROUPS, NSA_HEAD_DIM
    n = S // CMP_STRIDE
    width = CMP_STRIDE * Dh
    wsel = lambda i, b: (layer, i, 0, 0)
    return pl.pallas_call(
        _compress_kernel,
        grid=(2, B),
        in_specs=[
            pl.BlockSpec((None, S, G * Dh), lambda i, b: (i, b, 0)),
            pl.BlockSpec((None, None, width, 2 * CMP_HIDDEN), wsel),
            pl.BlockSpec((None, None, 2 * width, CMP_HIDDEN), wsel),
            pl.BlockSpec((None, None, 1, 2 * width), wsel),
            pl.BlockSpec((None, None, CMP_HIDDEN, Dh), wsel),
            pl.BlockSpec((None, None, Dh, CMP_HIDDEN), wsel),
        ],
        out_specs=[pl.BlockSpec((None, None, G * n, Dh), lambda i, b: (i, b, 0, 0)),
                   pl.BlockSpec((None, None, Dh, G * n), lambda i, b: (i, b, 0, 0))],
        out_shape=[jax.ShapeDtypeStruct((2, B, G * n, Dh), BF16),
                   jax.ShapeDtypeStruct((2, B, Dh, G * n), BF16)],
        scratch_shapes=[pltpu.VMEM((G * n, width), F32)],
        compiler_params=pltpu.CompilerParams(
            dimension_semantics=("parallel", "parallel"), vmem_limit_bytes=VMEM_LIMIT),
        name="compress",
    )(ckv, w1ab, w1, pe, w2, w2t)


def _nsa_kernel(q_ref, kc_ref, vct_ref, ks_ref, vst_ref, kw_ref, vwt_ref, gtt_ref, gain_ref,
                ovt_ref, et_ref, o_ref, *, n_sel, n_top, ck, unroll):
    TQ, R, Dh, G = Q_BLOCK, NSA_REP, NSA_HEAD_DIM, NSA_KV_GROUPS
    groups = range(G)
    i = pl.program_id(1)
    t0 = i * TQ
    qs = [q_ref[g * R:(g + 1) * R].reshape(R * TQ, Dh) for g in groups]
    tq1 = t0 + lax.broadcasted_iota(jnp.int32, (1, TQ), 1)
    heads = lambda t: jnp.concatenate([t] * R, axis=1)

    ncmp = kc_ref.shape[0] // G
    nwb = WINDOW // TQ + 1
    wb0 = jnp.maximum(i - WINDOW // TQ, 0)
    ws = pl.multiple_of(wb0 * TQ, TQ)
    sc = [_dot_nt(kc_ref[g * ncmp:(g + 1) * ncmp, :], qs[g]) for g in groups]
    sw = [_dot_nt(kw_ref[g, pl.ds(ws, nwb * TQ), :], qs[g]) for g in groups]

    cend = lax.broadcasted_iota(jnp.int32, (ncmp, 1), 0) * CMP_STRIDE + (CMP_BLOCK - 1)
    cmask = heads(cend <= tq1)
    pc = []
    for g in groups:
        scm = jnp.where(cmask, sc[g], NEG_INF)
        ec = jnp.where(cmask, jnp.exp2(scm - jnp.max(scm, axis=0, keepdims=True)), 0.0)
        pc.append(ec * (1.0 / jnp.maximum(jnp.sum(ec, axis=0, keepdims=True), 1e-30)))
    oc = [_dot(vct_ref[:, g * ncmp:(g + 1) * ncmp], pc[g].astype(BF16)) for g in groups]

    imp = []
    for g in groups:
        psum = pc[g][:, 0:TQ]
        for r in range(1, R):
            psum = psum + pc[g][:, r * TQ:(r + 1) * TQ]
        imp.append(_dot(ovt_ref[...], psum, precision=HIGHEST))

    kpos = ws + lax.broadcasted_iota(jnp.int32, (nwb * TQ, 1), 0)
    wbias = heads(jnp.where((kpos <= tq1) & (tq1 - kpos < WINDOW), 0.0, NEG_INF))
    ow, l_w = [], []
    for g in groups:
        swb = sw[g] + wbias
        pw = jnp.exp2(swb - jnp.max(swb, axis=0, keepdims=True))
        vwt = jnp.concatenate([vwt_ref[g, wb0 + j] for j in range(nwb)], axis=1)
        ow.append(_dot(vwt, pw.astype(BF16)))
        l_w.append(jnp.sum(pw, axis=0, keepdims=True))

    blk = lax.broadcasted_iota(jnp.int32, (n_sel, 1), 0)
    valid = blk * SEL_BLOCK <= tq1
    forced = (blk == 0) | (blk == tq1 // SEL_BLOCK)
    selb = []
    for g in groups:
        val = jnp.where(forced, FORCE_SCORE, jnp.where(valid, imp[g], -FORCE_SCORE))
        rank = jnp.zeros((n_sel, TQ), F32)
        for jp in range(n_sel):
            other = val[jp:jp + 1, :]
            wins = jnp.where(blk > jp, jnp.where(other >= val, 1.0, 0.0), jnp.where(other > val, 1.0, 0.0))
            rank = rank + wins
        selb.append(jnp.where(rank < n_top, 0.0, NEG_INF).astype(BF16))

    def scores(g, c):
        k0 = pl.multiple_of(c * ck, ck)
        kpos = k0 + lax.broadcasted_iota(jnp.int32, (ck, 1), 0)
        bias = jnp.where(kpos <= tq1, _dot(et_ref[c], selb[g]), NEG_INF)
        return _dot_nt(ks_ref[g, pl.ds(k0, ck), :], qs[g]) + heads(bias)

    def update(g, c, s, carry):
        m, l, acc = carry
        mn = jnp.maximum(m, jnp.max(s, axis=0, keepdims=True))
        alpha = jnp.exp2(m - mn)
        p = jnp.exp2(s - mn)
        l = alpha * l + jnp.sum(p, axis=0, keepdims=True)
        acc = alpha * acc + _dot(vst_ref[g, c], p.astype(BF16))
        return mn, l, acc

    def chunk_group(cg, carry):
        cs = [cg * unroll + sub for sub in range(unroll)]
        ss = [[scores(g, c) for g in groups] for c in cs]
        carry = list(carry)
        for c, s in zip(cs, ss):
            for g in groups:
                carry[g] = update(g, c, s[g], carry[g])
        return tuple(carry)

    n_chunks = (t0 + TQ + ck - 1) // ck
    init = tuple((jnp.full((1, R * TQ), NEG_INF, F32), jnp.zeros((1, R * TQ), F32),
                  jnp.zeros((Dh, R * TQ), F32)) for _ in groups)
    sel = lax.fori_loop(0, (n_chunks + unroll - 1) // unroll, chunk_group, init)

    normed = []
    for g in groups:
        gs = _sigmoid(gtt_ref[g])
        _, l_s, acc_s = sel[g]
        for r in range(R):
            ln = slice(r * TQ, (r + 1) * TQ)
            o = (gs[3 * r:3 * r + 1, :] * oc[g][:, ln]
                 + (gs[3 * r + 1:3 * r + 2, :] / l_s[:, ln]) * acc_s[:, ln]
                 + (gs[3 * r + 2:3 * r + 3, :] / l_w[g][:, ln]) * ow[g][:, ln])
            ms = jnp.mean(o * o, axis=0, keepdims=True)
            normed.append(o * lax.rsqrt(ms + EPS) * gain_ref[g * R + r])
    for pair in range(G * R // 2):
        both = jnp.concatenate(normed[2 * pair:2 * pair + 2], axis=0)
        o_ref[:, pair * 2 * Dh:(pair + 1) * 2 * Dh] = jnp.transpose(both).astype(BF16)


def _nsa(aq, cmp_k, cmp_vt, ks, vst, kw, vwt, gates_t, gain, consts, B, S):
    G, H, TQ, Dh = NSA_KV_GROUPS, NSA_HEADS, Q_BLOCK, NSA_HEAD_DIM
    nq = S // TQ
    ncmp = S // CMP_STRIDE
    n_sel = S // SEL_BLOCK
    ovt, emat_t, ck = consts
    k_spec = pl.BlockSpec((None, G, S, Dh), lambda b, i: (b, 0, 0, 0))
    return pl.pallas_call(
        functools.partial(_nsa_kernel, n_sel=n_sel, n_top=min(SEL_TOPN, n_sel), ck=ck, unroll=SEL_UNROLL),
        grid=(B, nq),
        in_specs=[
            pl.BlockSpec((None, H, TQ, Dh), lambda b, i: (b, 0, i, 0)),
            pl.BlockSpec((None, None, G * ncmp, Dh), lambda b, i: (0, b, 0, 0)),
            pl.BlockSpec((None, None, Dh, G * ncmp), lambda b, i: (1, b, 0, 0)),
            k_spec,
            pl.BlockSpec((None, G, S // ck, Dh, ck), lambda b, i: (b, 0, 0, 0, 0)),
            k_spec,
            pl.BlockSpec((None, G, S // TQ, Dh, TQ), lambda b, i: (b, 0, 0, 0, 0)),
            pl.BlockSpec((G, None, 2 * SUBLANE, TQ), lambda b, i: (0, b, GATE_COL // (2 * SUBLANE), i)),
            pl.BlockSpec((H, Dh, 1), lambda b, i: (0, 0, 0)),
            pl.BlockSpec(ovt.shape, lambda b, i: (0, 0)),
            pl.BlockSpec(emat_t.shape, lambda b, i: (0, 0, 0)),
        ],
        out_specs=pl.BlockSpec((TQ, H * Dh), lambda b, i: (b * nq + i, 0)),
        out_shape=jax.ShapeDtypeStruct((B * S, H * Dh), BF16),
        compiler_params=pltpu.CompilerParams(
            dimension_semantics=("parallel", "arbitrary"), vmem_limit_bytes=VMEM_LIMIT),
        name="nsa",
    )(aq, cmp_k, cmp_vt, ks, vst, kw, vwt, gates_t, gain, ovt, emat_t)


def _nsa_consts(S):
    n_cmp = S // CMP_STRIDE
    n_sel = S // SEL_BLOCK
    ck = 256
    i = np.arange(n_cmp)[:, None]
    j = np.arange(n_sel)[None, :]
    lo = np.maximum(i * CMP_STRIDE, j * SEL_BLOCK)
    hi = np.minimum(i * CMP_STRIDE + CMP_BLOCK, (j + 1) * SEL_BLOCK)
    ov = np.maximum(hi - lo, 0) / CMP_STRIDE
    ov[n_cmp - 1] = 0.0
    key = np.arange(S)
    emat_t = (key[:, None] // SEL_BLOCK == np.arange(n_sel)[None, :]).astype(np.float32)
    return (jnp.asarray(ov.T, F32), jnp.asarray(emat_t.reshape(S // ck, ck, n_sel), BF16), ck)


def _outproj_kernel(h_ref, ys_ref, ym_ref, yn_ref, w_ref, g_ref, o_ref):
    acc = _dot(ys_ref[...], w_ref[0:SSM_WIDTH, :])
    acc = acc + _dot(ym_ref[...], w_ref[SSM_WIDTH:SSM_WIDTH + MLSTM_WIDTH, :])
    acc = acc + _dot(yn_ref[...], w_ref[SSM_WIDTH + MLSTM_WIDTH:, :])
    ms = jnp.mean(acc * acc, axis=-1, keepdims=True)
    o_ref[...] = h_ref[...] + acc * lax.rsqrt(ms + EPS) * g_ref[...]


def _outproj(h2, y_ssm, y_mls, y_nsa, w, layer, gain, B, S, ts):
    nt = S // ts
    row = lambda b, i: (b * nt + i, 0)
    full = lambda b, i: (0, 0)
    return pl.pallas_call(
        _outproj_kernel,
        grid=(B, nt),
        in_specs=[
            pl.BlockSpec((ts, D_MODEL), row),
            pl.BlockSpec((ts, SSM_WIDTH), row),
            pl.BlockSpec((ts, MLSTM_WIDTH), row),
            pl.BlockSpec((ts, NSA_WIDTH), row),
            pl.BlockSpec((None, D_MODEL, D_MODEL), lambda b, i: (layer, 0, 0)),
            pl.BlockSpec((1, D_MODEL), full),
        ],
        out_specs=pl.BlockSpec((ts, D_MODEL), row),
        out_shape=jax.ShapeDtypeStruct((B * S, D_MODEL), F32),
        compiler_params=pltpu.CompilerParams(
            dimension_semantics=("parallel", "parallel"), vmem_limit_bytes=VMEM_LIMIT),
        name="outproj",
    )(h2, y_ssm, y_mls, y_nsa, w, gain)


def _mlp_kernel(h_ref, g1_ref, w1_ref, w2_ref, g2_ref, o_ref, u_sc, acc_sc):
    kf = pl.program_id(1)

    @pl.when(kf == 0)
    def _():
        x = h_ref[...]
        ms = jnp.mean(x * x, axis=-1, keepdims=True)
        u_sc[...] = (x * lax.rsqrt(ms + EPS) * g1_ref[...]).astype(BF16)
        acc_sc[...] = jnp.zeros_like(acc_sc)

    a = jnp.maximum(_dot(u_sc[...], w1_ref[...].astype(BF16)), 0.0)
    acc_sc[...] += _dot((a * a).astype(BF16), w2_ref[...].astype(BF16))

    @pl.when(kf == pl.num_programs(1) - 1)
    def _():
        f = acc_sc[...]
        ms = jnp.mean(f * f, axis=-1, keepdims=True)
        o_ref[...] = h_ref[...] + f * lax.rsqrt(ms + EPS) * g2_ref[...]


def _mlp(h2, g1, w1, w2, layer, g2, tm, tf):
    rows = h2.shape[0]
    return pl.pallas_call(
        _mlp_kernel,
        grid=(rows // tm, D_FF // tf),
        in_specs=[
            pl.BlockSpec((tm, D_MODEL), lambda i, k: (i, 0)),
            pl.BlockSpec((1, D_MODEL), lambda i, k: (0, 0)),
            pl.BlockSpec((None, D_MODEL, tf), lambda i, k: (layer, 0, k)),
            pl.BlockSpec((None, tf, D_MODEL), lambda i, k: (layer, k, 0)),
            pl.BlockSpec((1, D_MODEL), lambda i, k: (0, 0)),
        ],
        out_specs=pl.BlockSpec((tm, D_MODEL), lambda i, k: (i, 0)),
        out_shape=jax.ShapeDtypeStruct((rows, D_MODEL), F32),
        scratch_shapes=[pltpu.VMEM((tm, D_MODEL), BF16), pltpu.VMEM((tm, D_MODEL), F32)],
        compiler_params=pltpu.CompilerParams(
            dimension_semantics=("parallel", "arbitrary"), vmem_limit_bytes=VMEM_LIMIT),
        name="mlp",
    )(h2, g1, w1, w2, g2)


def _inproj_pieces():
    return ((0, 1280), (1288, 1800), (1800, 1928), (2056, 2184), (2312, 2440),
            (1928, 2056), (2184, 2312), (2440, 2568),
            (1280, 1288), (None, GATE_COL - 8), (2568, 2580), (None, LANE - GATE_COL - 12),
            (None, GATE_COL), (2580, 2592), (None, LANE - GATE_COL - 12))


def _permute_w_in(w_in):
    parts = []
    for a, b in _inproj_pieces():
        if a is None:
            parts.append(jnp.zeros(w_in.shape[:-1] + (b,), BF16))
        else:
            parts.append(w_in[..., a:b].astype(BF16))
    out = jnp.concatenate(parts, axis=-1)
    assert out.shape[-1] == D_INP
    return out


def _rope_tables(positions):
    inv = ROPE_THETA ** (-jnp.arange(0, ROPE_DIMS, 2, dtype=F32) / ROPE_DIMS)
    ang = positions.astype(F32)[..., None] * inv
    cos, sin = jnp.cos(ang), jnp.sin(ang)
    z = jnp.zeros_like(cos)
    rest = NSA_HEAD_DIM - ROPE_DIMS
    pad_one = jnp.ones(cos.shape[:-1] + (rest,), F32)
    pad_zero = jnp.zeros(cos.shape[:-1] + (rest,), F32)
    rc = jnp.concatenate([cos, cos, pad_one], axis=-1)
    rs1 = jnp.concatenate([-sin, z, pad_zero], axis=-1)
    rs2 = jnp.concatenate([z, sin, pad_zero], axis=-1)
    tile = lambda t: jnp.tile(t, (1, 1, LANE // NSA_HEAD_DIM)).reshape(-1, LANE)
    return tile(rc), tile(rs1), tile(rs2)


def _s5_params(lam_re, lam_im, b_re, b_im, c_re, c_im, log_dt):
    G, P, Hc = SSM_GROUPS, SSM_STATE, SSM_GROUP
    dt = jnp.exp(log_dt)[:, None]
    mag = jnp.exp(lam_re * dt)
    ang = lam_im * dt
    ab_re = mag * jnp.cos(ang)
    ab_im = mag * jnp.sin(ang)
    den = lam_re * lam_re + lam_im * lam_im
    g_re = ((ab_re - 1.0) * lam_re + ab_im * lam_im) / den
    g_im = (ab_im * lam_re - (ab_re - 1.0) * lam_im) / den
    bb_re = g_re[..., None] * b_re - g_im[..., None] * b_im
    bb_im = g_re[..., None] * b_im + g_im[..., None] * b_re
    eye = jnp.eye(G, dtype=F32)
    blockdiag_in = lambda t: jnp.einsum('gph,gk->ghkp', t, eye).reshape(G * Hc, G * P)
    blockdiag_out = lambda t: jnp.einsum('ghp,gk->gpkh', t, eye).reshape(G * P, G * Hc)
    bb = jnp.concatenate([blockdiag_in(bb_re), blockdiag_in(bb_im)], axis=1).astype(BF16)
    cc = jnp.concatenate([blockdiag_out(c_re), -blockdiag_out(c_im)], axis=0).astype(BF16)
    a = jnp.stack([ab_re.reshape(-1), ab_im.reshape(-1)], axis=0)
    return bb, a, cc


def _group_mean_matrix(width, group):
    idx = np.arange(width) // group
    return jnp.asarray((idx[:, None] == idx[None, :]).astype(np.float32) / group, BF16)


def kernel(x, positions, ln_mix_pre, ln_mix_post, ln_mlp_pre, ln_mlp_post, w_in, w_out, ssm_lambda_re, ssm_lambda_im, ssm_b_re, ssm_b_im, ssm_c_re, ssm_c_im, ssm_d, ssm_log_dt, ssm_w_glu, mlstm_conv, mlstm_b_i, mlstm_b_f, cmp_pe_k, cmp_w1_k, cmp_w2_k, cmp_pe_v, cmp_w1_v, cmp_w2_v, gn_ssm, gn_mlstm, gn_nsa, mlp_w1, mlp_w2):
    B, S, D = x.shape
    depth = w_in.shape[0]
    assert D == D_MODEL and B == SUBLANE and S % 512 == 0 and S >= WINDOW + Q_BLOCK
    G, H = NSA_KV_GROUPS, MLSTM_HEADS
    ts_proj = 512
    ts_scan = 128

    rc, rs1, rs2 = _rope_tables(positions)
    w_in_p = _permute_w_in(w_in)
    w_out_b = w_out.astype(BF16)
    wglu_b = ssm_w_glu.astype(BF16)
    gm_ssm = _group_mean_matrix(SSM_WIDTH, SSM_GROUP)
    hm_mls = _group_mean_matrix(MLSTM_WIDTH, MLSTM_HEAD_DIM)
    consts = _nsa_consts(S)
    half = CMP_STRIDE * NSA_HEAD_DIM
    w1ab = jnp.stack([jnp.concatenate([cmp_w1_k[:, :half], cmp_w1_k[:, half:]], axis=-1),
                      jnp.concatenate([cmp_w1_v[:, :half], cmp_w1_v[:, half:]], axis=-1)], axis=1).astype(BF16)
    w1f = jnp.stack([cmp_w1_k, cmp_w1_v], axis=1)
    pef = jnp.stack([cmp_pe_k.reshape(depth, 1, -1), cmp_pe_v.reshape(depth, 1, -1)], axis=1)
    w2c = jnp.stack([cmp_w2_k, cmp_w2_v], axis=1).astype(BF16)
    w2ct = jnp.swapaxes(w2c, -1, -2)
    zeros_g = jnp.zeros((depth, LANE - 2 * H), F32)
    bias_col = jnp.concatenate([mlstm_b_i, mlstm_b_f, zeros_g], axis=-1)[:, None, :]
    bias_row = jnp.concatenate([mlstm_b_i, mlstm_b_f], axis=-1)[:, :, None]

    bb, a, cc = jax.vmap(_s5_params)(ssm_lambda_re, ssm_lambda_im, ssm_b_re, ssm_b_im, ssm_c_re, ssm_c_im,
                                     ssm_log_dt)
    sh3 = lambda t: t.reshape(B, S, t.shape[-1])

    h = x.reshape(B * S, D)
    for l in range(depth):
        (su, mq, mk, mv, mo, aq, ckv, sk, wk, svt, wvt, gates, gates_t) = _inproj(
            h, ln_mix_pre[l][None], w_in_p, l, rc, rs1, rs2, B, S, ts_proj, consts[2])

        y_ssm = _s5(sh3(su), bb, a, cc, ssm_d[l][None], wglu_b, l, gm_ssm, gn_ssm[l][None], B, S, ts_scan)

        y_mls = _mlstm(sh3(mq), sh3(mk), sh3(mv), sh3(mo), sh3(gates), gates_t,
                       mlstm_conv[l][:, :MLSTM_WIDTH], mlstm_conv[l][:, MLSTM_WIDTH:],
                       bias_col[l], bias_row[l], hm_mls, gn_mlstm[l][None], B, S)

        cmp_k, cmp_t = _compress(ckv, w1ab, w1f, pef, w2c, w2ct, l, B, S)
        y_nsa = _nsa(aq, cmp_k, cmp_t, sk, svt, wk, wvt, gates_t,
                     gn_nsa[l].reshape(NSA_HEADS, NSA_HEAD_DIM, 1), consts, B, S)

        h = _outproj(h, y_ssm.reshape(B * S, SSM_WIDTH), y_mls.reshape(B * S, MLSTM_WIDTH), y_nsa,
                     w_out_b, l, ln_mix_post[l][None], B, S, ts_proj)
        h = _mlp(h, ln_mlp_pre[l][None], mlp_w1, mlp_w2, l, ln_mlp_post[l][None], 1024, 512)
    return h.reshape(B, S, D)
```

```python
import functools
import math

import numpy as np
import jax
import jax.numpy as jnp
from jax import lax
from jax.experimental import pallas as pl
from jax.experimental.pallas import tpu as pltpu

F32 = jnp.float32
BF16 = jnp.bfloat16
HIGHEST = lax.Precision.HIGHEST

D_MODEL = 1024
DEPTH = 4
SSM_WIDTH = 256
SSM_GROUP = 16
SSM_GROUPS = 16
SSM_STATE = 64
SSM_LANES = SSM_GROUPS * SSM_STATE
MLSTM_WIDTH = 256
MLSTM_HEADS = 4
MLSTM_HEAD_DIM = 64
MLSTM_CHUNK = 128
MLSTM_CONV = 4
NSA_WIDTH = 512
NSA_HEAD_DIM = 64
NSA_HEADS = 8
NSA_KV_GROUPS = 2
NSA_REP = NSA_HEADS // NSA_KV_GROUPS
NSA_KV_WIDTH = 128
CMP_BLOCK = 32
CMP_STRIDE = 16
CMP_HIDDEN = 256
SEL_BLOCK = 64
SEL_TOPN = 8
WINDOW = 256
Q_BLOCK = 128
FORCE_SCORE = 1e4
NEG_INF = -1e30
ROPE_THETA = 500000.0
ROPE_DIMS = 16
ROPE_HALF = 8
D_FF = 4096
EPS = 1e-6
D_IN = 2592

LANE = 128
SUBLANE = 8
VMEM_LIMIT = 56 * 1024 * 1024

C_SU, C_MQ, C_MK, C_MV, C_MO = 0, 256, 512, 768, 1024
C_AQ, C_CK, C_SK, C_WK = 1280, 1792, 1920, 2048
C_CV, C_SV, C_WV = 2176, 2304, 2432
C_G0, C_G1 = 2560, 2688
D_INP = 2816
GATE_COL = 16
Q_SCALE = NSA_HEAD_DIM ** -0.5 * math.log2(math.e)
MLSTM_ROWS = 4
SEL_UNROLL = 2


def _dot(a, b, precision=None):
    return jnp.dot(a, b, preferred_element_type=F32, precision=precision)


def _dot_nt(a, b):
    return lax.dot_general(a, b, (((1,), (1,)), ((), ())), preferred_element_type=F32)


def _dot_tn(a, b):
    return lax.dot_general(a, b, (((0,), (0,)), ((), ())), preferred_element_type=F32)


def _sigmoid(x):
    return 1.0 / (1.0 + jnp.exp(-x))


def _dot_split(x, w_bf16):
    hi = x.astype(BF16)
    lo = (x - hi.astype(F32)).astype(BF16)
    return _dot(hi, w_bf16) + _dot(lo, w_bf16)


def _gelu_tanh(x):
    return 0.5 * x * (1.0 + jnp.tanh(math.sqrt(2.0 / math.pi) * (x + 0.044715 * (x * x * x))))


def _log_sigmoid(x):
    return jnp.minimum(x, 0.0) - jnp.log(1.0 + jnp.exp(-jnp.abs(x)))


def _inproj_kernel(x_ref, g_ref, w_ref, rc_ref, rs1_ref, rs2_ref,
                   su_ref, mq_ref, mk_ref, mv_ref, mo_ref, aq_ref, ckv_ref, sk_ref, wk_ref,
                   svt_ref, wvt_ref, gt_ref, gtt_ref):
    x = x_ref[...]
    ms = jnp.mean(x * x, axis=-1, keepdims=True)
    u = (x * lax.rsqrt(ms + EPS) * g_ref[...]).astype(BF16)
    rc, rs1, rs2 = rc_ref[...], rs1_ref[...], rs2_ref[...]

    def mm(c0, width):
        return _dot(u, w_ref[:, c0:c0 + width])

    def rope(z):
        return z * rc + pltpu.roll(z, LANE - ROPE_HALF, 1) * rs1 + pltpu.roll(z, ROPE_HALF, 1) * rs2

    su_ref[...] = mm(C_SU, 256)
    mq_ref[...] = mm(C_MQ, 256)
    mk_ref[...] = mm(C_MK, 256)
    mv_ref[...] = mm(C_MV, 256).astype(BF16)
    mo_ref[...] = mm(C_MO, 256)
    def mm_pair(c0):
        z = mm(c0, 2 * LANE)
        return z[:, :LANE], z[:, LANE:]

    def put_heads(ref, first, z):
        ref[first] = z[:, :NSA_HEAD_DIM].astype(BF16)
        ref[first + 1] = z[:, NSA_HEAD_DIM:].astype(BF16)

    for j in range(NSA_HEADS // 4):
        for k, z in enumerate(mm_pair(C_AQ + 2 * LANE * j)):
            put_heads(aq_ref, 4 * j + 2 * k, rope(z) * Q_SCALE)
    z_ck, z_sk = mm_pair(C_CK)
    z_wk, z_cv = mm_pair(C_WK)
    z_sv, z_wv = mm_pair(C_SV)
    ckv_ref[0] = rope(z_ck)
    ckv_ref[1] = z_cv
    put_heads(sk_ref, 0, rope(z_sk))
    put_heads(wk_ref, 0, rope(z_wk))

    def put_chunks_t(ref, z):
        zt = jnp.transpose(z)
        width = ref.shape[-1]
        for g in range(NSA_KV_GROUPS):
            for j in range(ref.shape[1]):
                ref[g, j] = zt[g * NSA_HEAD_DIM:(g + 1) * NSA_HEAD_DIM, j * width:(j + 1) * width].astype(BF16)

    put_chunks_t(svt_ref, z_sv)
    put_chunks_t(wvt_ref, z_wv)
    z_g0, z_g1 = mm_pair(C_G0)
    gt_ref[...] = z_g0
    gtt_ref[0] = jnp.transpose(z_g0)
    gtt_ref[1] = jnp.transpose(z_g1)


def _inproj(h2, gain, w, layer, rc, rs1, rs2, B, S, ts, ck):
    nt = S // ts
    BS = B * S
    row = lambda b, i: (b * nt + i, 0)
    full = lambda b, i: (0, 0)
    headed = lambda b, i: (b, 0, i, 0)
    paired = lambda b, i: (0, b * nt + i, 0)
    in_specs = [
        pl.BlockSpec((ts, D_MODEL), row),
        pl.BlockSpec((1, D_MODEL), full),
        pl.BlockSpec((None, D_MODEL, D_INP), lambda b, i: (layer, 0, 0)),
        pl.BlockSpec((ts, LANE), row),
        pl.BlockSpec((ts, LANE), row),
        pl.BlockSpec((ts, LANE), row),
    ]
    kv_shape = jax.ShapeDtypeStruct((B, NSA_KV_GROUPS, S, NSA_HEAD_DIM), BF16)
    kv_spec = pl.BlockSpec((None, NSA_KV_GROUPS, ts, NSA_HEAD_DIM), headed)
    out_shape = [
        jax.ShapeDtypeStruct((BS, SSM_WIDTH), F32),
        jax.ShapeDtypeStruct((BS, MLSTM_WIDTH), F32),
        jax.ShapeDtypeStruct((BS, MLSTM_WIDTH), F32),
        jax.ShapeDtypeStruct((BS, MLSTM_WIDTH), BF16),
        jax.ShapeDtypeStruct((BS, MLSTM_WIDTH), F32),
        jax.ShapeDtypeStruct((B, NSA_HEADS, S, NSA_HEAD_DIM), BF16),
        jax.ShapeDtypeStruct((2, BS, NSA_KV_WIDTH), F32),
        kv_shape, kv_shape,
        jax.ShapeDtypeStruct((B, NSA_KV_GROUPS, S // ck, NSA_HEAD_DIM, ck), BF16),
        jax.ShapeDtypeStruct((B, NSA_KV_GROUPS, S // Q_BLOCK, NSA_HEAD_DIM, Q_BLOCK), BF16),
        jax.ShapeDtypeStruct((BS, LANE), F32),
        jax.ShapeDtypeStruct((NSA_KV_GROUPS, B, LANE, S), F32),
    ]
    out_specs = [
        pl.BlockSpec((ts, SSM_WIDTH), row),
        pl.BlockSpec((ts, MLSTM_WIDTH), row),
        pl.BlockSpec((ts, MLSTM_WIDTH), row),
        pl.BlockSpec((ts, MLSTM_WIDTH), row),
        pl.BlockSpec((ts, MLSTM_WIDTH), row),
        pl.BlockSpec((None, NSA_HEADS, ts, NSA_HEAD_DIM), headed),
        pl.BlockSpec((2, ts, NSA_KV_WIDTH), paired),
        kv_spec, kv_spec,
        pl.BlockSpec((None, NSA_KV_GROUPS, ts // ck, NSA_HEAD_DIM, ck), lambda b, i: (b, 0, i, 0, 0)),
        pl.BlockSpec((None, NSA_KV_GROUPS, ts // Q_BLOCK, NSA_HEAD_DIM, Q_BLOCK), lambda b, i: (b, 0, i, 0, 0)),
        pl.BlockSpec((ts, LANE), row),
        pl.BlockSpec((NSA_KV_GROUPS, None, LANE, ts), lambda b, i: (0, b, 0, i)),
    ]
    return pl.pallas_call(
        _inproj_kernel,
        grid=(B, nt),
        in_specs=in_specs,
        out_specs=out_specs,
        out_shape=out_shape,
        compiler_params=pltpu.CompilerParams(
            dimension_semantics=("parallel", "parallel"), vmem_limit_bytes=VMEM_LIMIT),
        name="inproj",
    )(h2, gain, w, rc, rs1, rs2)


def _s5_kernel(u_ref, bb_ref, a_ref, cc_ref, d_ref, wg_ref, gm_ref, gain_ref, o_ref, x_sc, st_sc, tm_sc, *, B, ts):
    @pl.when(pl.program_id(0) == 0)
    def _():
        st_sc[...] = jnp.zeros_like(st_sc)

    nl = SSM_WIDTH // LANE
    for b in range(B):
        for c in range(nl):
            tm_sc[c, pl.ds(b, ts, stride=B), :] = u_ref[b, :, c * LANE:(c + 1) * LANE]
    u = jnp.concatenate([tm_sc[c] for c in range(nl)], axis=1)
    ub = u.astype(BF16)
    for part in range(2):
        cols = slice(part * SSM_LANES, (part + 1) * SSM_LANES)
        x_sc[:, cols] = _dot(ub, bb_ref[:, cols])
    ar = jnp.broadcast_to(a_ref[0:1, :], (B, SSM_LANES))
    ai = jnp.broadcast_to(a_ref[1:2, :], (B, SSM_LANES))

    def step(t, carry):
        xr, xi = carry
        r = pl.multiple_of(t * B, B)
        br = x_sc[pl.ds(r, B), 0:SSM_LANES]
        bi = x_sc[pl.ds(r, B), SSM_LANES:2 * SSM_LANES]
        nr = ar * xr - ai * xi + br
        ni = ar * xi + ai * xr + bi
        x_sc[pl.ds(r, B), 0:SSM_LANES] = nr
        x_sc[pl.ds(r, B), SSM_LANES:2 * SSM_LANES] = ni
        return nr, ni

    xr, xi = lax.fori_loop(0, ts, step, (st_sc[0], st_sc[1]))
    st_sc[0] = xr
    st_sc[1] = xi

    half = (ts * B) // 2
    y = jnp.concatenate([_dot(x_sc[r * half:(r + 1) * half, :].astype(BF16), cc_ref[...]) for r in range(2)],
                        axis=0) + d_ref[...] * u
    y = _gelu_tanh(y)
    y = y * _sigmoid(_dot(y.astype(BF16), wg_ref[...]))
    ms = _dot_split(y * y, gm_ref[...])
    y = y * lax.rsqrt(ms + EPS) * gain_ref[...]
    for c in range(nl):
        tm_sc[c] = y[:, c * LANE:(c + 1) * LANE]
    for b in range(B):
        o_ref[b] = jnp.concatenate(
            [tm_sc[c, pl.ds(b, ts, stride=B), :] for c in range(nl)], axis=1).astype(BF16)


def _s5(u, bb, a, cc, d, wg, layer, gm, gain, B, S, ts):
    rows = ts * B
    full = lambda i: (0, 0)
    lsel = lambda i: (layer, 0, 0)
    return pl.pallas_call(
        functools.partial(_s5_kernel, B=B, ts=ts),
        grid=(S // ts,),
        in_specs=[
            pl.BlockSpec((B, ts, SSM_WIDTH), lambda i: (0, i, 0)),
            pl.BlockSpec((None, SSM_WIDTH, 2 * SSM_LANES), lsel),
            pl.BlockSpec((None, 2, SSM_LANES), lsel),
            pl.BlockSpec((None, 2 * SSM_LANES, SSM_WIDTH), lsel),
            pl.BlockSpec((1, SSM_WIDTH), full),
            pl.BlockSpec((None, SSM_WIDTH, SSM_WIDTH), lsel),
            pl.BlockSpec((SSM_WIDTH, SSM_WIDTH), full),
            pl.BlockSpec((1, SSM_WIDTH), full),
        ],
        out_specs=pl.BlockSpec((B, ts, SSM_WIDTH), lambda i: (0, i, 0)),
        out_shape=jax.ShapeDtypeStruct((B, S, SSM_WIDTH), BF16),
        scratch_shapes=[pltpu.VMEM((rows, 2 * SSM_LANES), F32), pltpu.VMEM((2, B, SSM_LANES), F32),
                        pltpu.VMEM((SSM_WIDTH // LANE, rows, LANE), F32)],
        compiler_params=pltpu.CompilerParams(
            dimension_semantics=("arbitrary",), vmem_limit_bytes=VMEM_LIMIT),
        name="s5",
    )(u, bb, a, cc, d, wg, gm, gain)


def _mlstm_kernel(q_ref, k_ref, v_ref, o_ref, gr_ref, cwq_ref, cwk_ref, br_ref, hm_ref,
                  gain_ref, y_ref, qt_sc, kt_sc, c_sc, m_sc, *, B):
    L, H, Dh, W = MLSTM_CHUNK, MLSTM_HEADS, MLSTM_HEAD_DIM, MLSTM_WIDTH

    @pl.when(pl.program_id(0) == 0)
    def _():
        qt_sc[...] = jnp.zeros_like(qt_sc)
        kt_sc[...] = jnp.zeros_like(kt_sc)
        c_sc[...] = jnp.zeros_like(c_sc)
        m_sc[...] = jnp.zeros_like(m_sc)

    visible = lax.broadcasted_iota(jnp.int32, (L, L), 0) <= lax.broadcasted_iota(jnp.int32, (L, L), 1)
    triu = visible.astype(F32)
    lane_w = lax.broadcasted_iota(jnp.int32, (1, W), 1) // Dh
    bd_mask = ((lax.broadcasted_iota(jnp.int32, (2 * W, W), 0) % W) // Dh
               == lax.broadcasted_iota(jnp.int32, (2 * W, W), 1) // Dh)
    row8 = lax.broadcasted_iota(jnp.int32, (SUBLANE, W), 0)
    cwq = cwq_ref[...]
    cwk = cwk_ref[...]
    ones_rows = jnp.ones((Dh, L), F32)

    def conv_silu(x, tail, w):
        acc = x * w[MLSTM_CONV - 1:MLSTM_CONV, :]
        for sft in range(1, MLSTM_CONV):
            xs = pltpu.roll(x, sft, 0)
            head = jnp.where(row8 < sft, pltpu.roll(tail, sft, 0), xs[:SUBLANE])
            xs = jnp.concatenate([head, xs[SUBLANE:]], axis=0)
            acc = acc + xs * w[MLSTM_CONV - 1 - sft:MLSTM_CONV - sft, :]
        return acc * _sigmoid(acc)

    def per_group(grp, _):
        bs = [grp * MLSTM_ROWS + n for n in range(MLSTM_ROWS)]
        st = [dict() for _ in bs]

        for b, d in zip(bs, st):
            q_raw = q_ref[b]
            k_raw = k_ref[b]
            d['q'] = conv_silu(q_raw, qt_sc[b], cwq)
            d['k'] = conv_silu(k_raw, kt_sc[b], cwk) * (Dh ** -0.5)
            qt_sc[b] = q_raw[L - SUBLANE:, :]
            kt_sc[b] = k_raw[L - SUBLANE:, :]
            d['gr'] = gr_ref[b] + br_ref[...]
        for d in st:
            d['brow'] = _dot(_log_sigmoid(d['gr']), triu, precision=HIGHEST)

        for b, d in zip(bs, st):
            gr, brow = d['gr'], d['brow']
            ccol = jnp.transpose(brow - pltpu.roll(gr, H, 0))
            m_all = m_sc[b]
            for key in ('w_intra', 'w_inter', 'e_mt', 'w_k', 'dec', 'm_new'):
                d[key] = []
            for hh in range(H):
                b_r = brow[H + hh:H + hh + 1, :]
                i_r = gr[hh:hh + 1, :]
                m_prev = m_all[hh:hh + 1, 0:1]
                dm = jnp.where(visible, b_r - ccol[:, H + hh:H + hh + 1], NEG_INF)
                inter = b_r + m_prev
                mt = jnp.maximum(inter, jnp.max(dm, axis=0, keepdims=True))
                d['w_intra'].append(jnp.exp(dm - mt))
                d['w_inter'].append(jnp.exp(inter - mt))
                d['e_mt'].append(jnp.exp(-mt))
                b_last = b_r[:, L - 1:L]
                logw = b_last - b_r + i_r
                mn = jnp.maximum(b_last + m_prev, jnp.max(logw, axis=1, keepdims=True))
                d['w_k'].append(jnp.exp(logw - mn))
                d['dec'].append(jnp.exp(b_last + m_prev - mn))
                d['m_new'].append(mn)
            d['qb'] = d['q'].astype(BF16)
            d['kb'] = d['k'].astype(BF16)
            d['vt'] = jnp.transpose(v_ref[b].astype(F32))
            d['c_t'] = c_sc[b]

        for d in st:
            d['qc'] = _dot(d['c_t'].astype(BF16), jnp.transpose(d['q']).astype(BF16))
            d['s_t'] = [_dot_nt(d['kb'], jnp.where(lane_w == hh, d['qb'], jnp.zeros_like(d['qb'])))
                        for hh in range(H)]
        for d in st:
            d['r'] = []
            for hh in range(H):
                v_aug = jnp.concatenate([d['vt'][hh * Dh:(hh + 1) * Dh], ones_rows], axis=0).astype(BF16)
                d['r'].append(_dot(v_aug, (d['s_t'][hh] * d['w_intra'][hh]).astype(BF16)))

        for b, d in zip(bs, st):
            h_t = []
            for hh in range(H):
                ch = slice(hh * Dh, (hh + 1) * Dh)
                num = d['w_inter'][hh] * d['qc'][ch] + d['r'][hh][:Dh]
                den = d['w_inter'][hh] * d['qc'][W + hh * Dh:W + (hh + 1) * Dh] + d['r'][hh][Dh:]
                h_t.append(num / jnp.maximum(jnp.abs(den), d['e_mt'][hh]))
            hout = jnp.transpose(jnp.concatenate(h_t, axis=0))
            d['y'] = _sigmoid(o_ref[b]) * hout
            d['vw'] = jnp.concatenate(
                [d['vt'][hh * Dh:(hh + 1) * Dh] * d['w_k'][hh] for hh in range(H)]
                + [jnp.broadcast_to(d['w_k'][hh], (Dh, L)) for hh in range(H)], axis=0).astype(BF16)
        for d in st:
            d['ms'] = _dot_split(d['y'] * d['y'], hm_ref[...])
            d['upd'] = _dot(d['vw'], d['kb'])

        for b, d in zip(bs, st):
            y_ref[b] = (d['y'] * lax.rsqrt(d['ms'] + EPS) * gain_ref[...]).astype(BF16)
            decay = d['dec'][H - 1]
            for hh in range(H - 2, -1, -1):
                decay = jnp.where(lane_w == hh, d['dec'][hh], decay)
            c_sc[b] = decay * d['c_t'] + jnp.where(bd_mask, d['upd'], 0.0)
            for hh in range(H):
                m_sc[b, hh:hh + 1, :] = jnp.broadcast_to(d['m_new'][hh], (1, LANE))
        return 0

    lax.fori_loop(0, B // MLSTM_ROWS, per_group, 0)


def _mlstm(mq, mk, mv, mo, grow, cwq, cwk, brow, hm, gain, B, S):
    L, W = MLSTM_CHUNK, MLSTM_WIDTH
    seq = lambda c: (0, c, 0)
    full = lambda c: (0, 0)
    return pl.pallas_call(
        functools.partial(_mlstm_kernel, B=B),
        grid=(S // L,),
        in_specs=[
            pl.BlockSpec((B, L, W), seq),
            pl.BlockSpec((B, L, W), seq),
            pl.BlockSpec((B, L, W), seq),
            pl.BlockSpec((B, L, W), seq),
            pl.BlockSpec((None, B, SUBLANE, L), lambda c: (0, 0, 0, c)),
            pl.BlockSpec((MLSTM_CONV, W), full),
            pl.BlockSpec((MLSTM_CONV, W), full),
            pl.BlockSpec((SUBLANE, 1), full),
            pl.BlockSpec((W, W), full),
            pl.BlockSpec((1, W), full),
        ],
        out_specs=pl.BlockSpec((B, L, W), seq),
        out_shape=jax.ShapeDtypeStruct((B, S, W), BF16),
        scratch_shapes=[
            pltpu.VMEM((B, SUBLANE, W), F32),
            pltpu.VMEM((B, SUBLANE, W), F32),
            pltpu.VMEM((B, 2 * W, W), F32),
            pltpu.VMEM((B, SUBLANE, LANE), F32),
        ],
        compiler_params=pltpu.CompilerParams(
            dimension_semantics=("arbitrary",), vmem_limit_bytes=VMEM_LIMIT),
        name="mlstm",
    )(mq, mk, mv, mo, grow, cwq, cwk, brow, hm, gain)


def _compress_kernel(c_ref, w1ab_ref, w1_ref, pe_ref, w2_ref, w2t_ref, o_ref, ot_ref, ch_sc):
    G, Dh = NSA_KV_GROUPS, NSA_HEAD_DIM
    rows = ch_sc.shape[0]
    n = rows // G
    for r in range(CMP_STRIDE):
        tok = c_ref[pl.ds(r, n, stride=CMP_STRIDE), :]
        for g in range(G):
            ch_sc[g * n:(g + 1) * n, r * Dh:(r + 1) * Dh] = tok[:, g * Dh:(g + 1) * Dh]
    ab = _dot(ch_sc[...].astype(BF16), w1ab_ref[...])
    const = _dot(pe_ref[...], w1_ref[...], precision=HIGHEST)
    hid = ab[:, :CMP_HIDDEN] + pltpu.roll(ab[:, CMP_HIDDEN:], rows - 1, 0) + const
    act = _gelu_tanh(hid).astype(BF16)
    o_ref[...] = _dot(act, w2_ref[...]).astype(BF16)
    ot_ref[...] = _dot_nt(w2t_ref[...], act).astype(BF16)


def _compress(ckv, w1ab, w1, pe, w2, w2t, layer, B, S):
    G, Dh = NSA_KV_GROUPS, NSA_HEAD_DIM
    n = S // CMP_STRIDE
    width = CMP_STRIDE * Dh
    wsel = lambda i, b: (layer, i, 0, 0)
    return pl.pallas_call(
        _compress_kernel,
        grid=(2, B),
        in_specs=[
            pl.BlockSpec((None, S, G * Dh), lambda i, b: (i, b, 0)),
            pl.BlockSpec((None, None, width, 2 * CMP_HIDDEN), wsel),
            pl.BlockSpec((None, None, 2 * width, CMP_HIDDEN), wsel),
            pl.BlockSpec((None, None, 1, 2 * width), wsel),
            pl.BlockSpec((None, None, CMP_HIDDEN, Dh), wsel),
            pl.BlockSpec((None, None, Dh, CMP_HIDDEN), wsel),
        ],
        out_specs=[pl.BlockSpec((None, None, G * n, Dh), lambda i, b: (i, b, 0, 0)),
                   pl.BlockSpec((None, None, Dh, G * n), lambda i, b: (i, b, 0, 0))],
        out_shape=[jax.ShapeDtypeStruct((2, B, G * n, Dh), BF16),
                   jax.ShapeDtypeStruct((2, B, Dh, G * n), BF16)],
        scratch_shapes=[pltpu.VMEM((G * n, width), F32)],
        compiler_params=pltpu.CompilerParams(
            dimension_semantics=("parallel", "parallel"), vmem_limit_bytes=VMEM_LIMIT),
        name="compress",
    )(ckv, w1ab, w1, pe, w2, w2t)


def _nsa_kernel(q_ref, kc_ref, vct_ref, ks_ref, vst_ref, kw_ref, vwt_ref, gtt_ref, gain_ref,
                ovt_ref, et_ref, o_ref, *, n_sel, n_top, ck, unroll):
    TQ, R, Dh, G = Q_BLOCK, NSA_REP, NSA_HEAD_DIM, NSA_KV_GROUPS
    groups = range(G)
    i = pl.program_id(1)
    t0 = i * TQ
    qs = [q_ref[g * R:(g + 1) * R].reshape(R * TQ, Dh) for g in groups]
    tq1 = t0 + lax.broadcasted_iota(jnp.int32, (1, TQ), 1)
    heads = lambda t: jnp.concatenate([t] * R, axis=1)

    ncmp = kc_ref.shape[0] // G
    nwb = WINDOW // TQ + 1
    wb0 = jnp.maximum(i - WINDOW // TQ, 0)
    ws = pl.multiple_of(wb0 * TQ, TQ)
    sc = [_dot_nt(kc_ref[g * ncmp:(g + 1) * ncmp, :], qs[g]) for g in groups]
    sw = [_dot_nt(kw_ref[g, pl.ds(ws, nwb * TQ), :], qs[g]) for g in groups]

    cend = lax.broadcasted_iota(jnp.int32, (ncmp, 1), 0) * CMP_STRIDE + (CMP_BLOCK - 1)
    cmask = heads(cend <= tq1)
    pc = []
    for g in groups:
        scm = jnp.where(cmask, sc[g], NEG_INF)
        ec = jnp.where(cmask, jnp.exp2(scm - jnp.max(scm, axis=0, keepdims=True)), 0.0)
        pc.append(ec * (1.0 / jnp.maximum(jnp.sum(ec, axis=0, keepdims=True), 1e-30)))
    oc = [_dot(vct_ref[:, g * ncmp:(g + 1) * ncmp], pc[g].astype(BF16)) for g in groups]

    imp = []
    for g in groups:
        psum = pc[g][:, 0:TQ]
        for r in range(1, R):
            psum = psum + pc[g][:, r * TQ:(r + 1) * TQ]
        imp.append(_dot(ovt_ref[...], psum, precision=HIGHEST))

    kpos = ws + lax.broadcasted_iota(jnp.int32, (nwb * TQ, 1), 0)
    wbias = heads(jnp.where((kpos <= tq1) & (tq1 - kpos < WINDOW), 0.0, NEG_INF))
    ow, l_w = [], []
    for g in groups:
        swb = sw[g] + wbias
        pw = jnp.exp2(swb - jnp.max(swb, axis=0, keepdims=True))
        vwt = jnp.concatenate([vwt_ref[g, wb0 + j] for j in range(nwb)], axis=1)
        ow.append(_dot(vwt, pw.astype(BF16)))
        l_w.append(jnp.sum(pw, axis=0, keepdims=True))

    blk = lax.broadcasted_iota(jnp.int32, (n_sel, 1), 0)
    valid = blk * SEL_BLOCK <= tq1
    forced = (blk == 0) | (blk == tq1 // SEL_BLOCK)
    selb = []
    for g in groups:
        val = jnp.where(forced, FORCE_SCORE, jnp.where(valid, imp[g], -FORCE_SCORE))
        rank = jnp.zeros((n_sel, TQ), F32)
        for jp in range(n_sel):
            other = val[jp:jp + 1, :]
            wins = jnp.where(blk > jp, jnp.where(other >= val, 1.0, 0.0), jnp.where(other > val, 1.0, 0.0))
            rank = rank + wins
        selb.append(jnp.where(rank < n_top, 0.0, NEG_INF).astype(BF16))

    def scores(g, c):
        k0 = pl.multiple_of(c * ck, ck)
        kpos = k0 + lax.broadcasted_iota(jnp.int32, (ck, 1), 0)
        bias = jnp.where(kpos <= tq1, _dot(et_ref[c], selb[g]), NEG_INF)
        return _dot_nt(ks_ref[g, pl.ds(k0, ck), :], qs[g]) + heads(bias)

    def update(g, c, s, carry):
        m, l, acc = carry
        mn = jnp.maximum(m, jnp.max(s, axis=0, keepdims=True))
        alpha = jnp.exp2(m - mn)
        p = jnp.exp2(s - mn)
        l = alpha * l + jnp.sum(p, axis=0, keepdims=True)
        acc = alpha * acc + _dot(vst_ref[g, c], p.astype(BF16))
        return mn, l, acc

    def chunk_group(cg, carry):
        cs = [cg * unroll + sub for sub in range(unroll)]
        ss = [[scores(g, c) for g in groups] for c in cs]
        carry = list(carry)
        for c, s in zip(cs, ss):
            for g in groups:
                carry[g] = update(g, c, s[g], carry[g])
        return tuple(carry)

    n_chunks = (t0 + TQ + ck - 1) // ck
    init = tuple((jnp.full((1, R * TQ), NEG_INF, F32), jnp.zeros((1, R * TQ), F32),
                  jnp.zeros((Dh, R * TQ), F32)) for _ in groups)
    sel = lax.fori_loop(0, (n_chunks + unroll - 1) // unroll, chunk_group, init)

    normed = []
    for g in groups:
        gs = _sigmoid(gtt_ref[g])
        _, l_s, acc_s = sel[g]
        for r in range(R):
            ln = slice(r * TQ, (r + 1) * TQ)
            o = (gs[3 * r:3 * r + 1, :] * oc[g][:, ln]
                 + (gs[3 * r + 1:3 * r + 2, :] / l_s[:, ln]) * acc_s[:, ln]
                 + (gs[3 * r + 2:3 * r + 3, :] / l_w[g][:, ln]) * ow[g][:, ln])
            ms = jnp.mean(o * o, axis=0, keepdims=True)
            normed.append(o * lax.rsqrt(ms + EPS) * gain_ref[g * R + r])
    for pair in range(G * R // 2):
        both = jnp.concatenate(normed[2 * pair:2 * pair + 2], axis=0)
        o_ref[:, pair * 2 * Dh:(pair + 1) * 2 * Dh] = jnp.transpose(both).astype(BF16)


def _nsa(aq, cmp_k, cmp_vt, ks, vst, kw, vwt, gates_t, gain, consts, B, S):
    G, H, TQ, Dh = NSA_KV_GROUPS, NSA_HEADS, Q_BLOCK, NSA_HEAD_DIM
    nq = S // TQ
    ncmp = S // CMP_STRIDE
    n_sel = S // SEL_BLOCK
    ovt, emat_t, ck = consts
    k_spec = pl.BlockSpec((None, G, S, Dh), lambda b, i: (b, 0, 0, 0))
    return pl.pallas_call(
        functools.partial(_nsa_kernel, n_sel=n_sel, n_top=min(SEL_TOPN, n_sel), ck=ck, unroll=SEL_UNROLL),
        grid=(B, nq),
        in_specs=[
            pl.BlockSpec((None, H, TQ, Dh), lambda b, i: (b, 0, i, 0)),
            pl.BlockSpec((None, None, G * ncmp, Dh), lambda b, i: (0, b, 0, 0)),
            pl.BlockSpec((None, None, Dh, G * ncmp), lambda b, i: (1, b, 0, 0)),
            k_spec,
            pl.BlockSpec((None, G, S // ck, Dh, ck), lambda b, i: (b, 0, 0, 0, 0)),
            k_spec,
            pl.BlockSpec((None, G, S // TQ, Dh, TQ), lambda b, i: (b, 0, 0, 0, 0)),
            pl.BlockSpec((G, None, 2 * SUBLANE, TQ), lambda b, i: (0, b, GATE_COL // (2 * SUBLANE), i)),
            pl.BlockSpec((H, Dh, 1), lambda b, i: (0, 0, 0)),
            pl.BlockSpec(ovt.shape, lambda b, i: (0, 0)),
            pl.BlockSpec(emat_t.shape, lambda b, i: (0, 0, 0)),
        ],
        out_specs=pl.BlockSpec((TQ, H * Dh), lambda b, i: (b * nq + i, 0)),
        out_shape=jax.ShapeDtypeStruct((B * S, H * Dh), BF16),
        compiler_params=pltpu.CompilerParams(
            dimension_semantics=("parallel", "arbitrary"), vmem_limit_bytes=VMEM_LIMIT),
        name="nsa",
    )(aq, cmp_k, cmp_vt, ks, vst, kw, vwt, gates_t, gain, ovt, emat_t)


def _nsa_consts(S):
    n_cmp = S // CMP_STRIDE
    n_sel = S // SEL_BLOCK
    ck = 256
    i = np.arange(n_cmp)[:, None]
    j = np.arange(n_sel)[None, :]
    lo = np.maximum(i * CMP_STRIDE, j * SEL_BLOCK)
    hi = np.minimum(i * CMP_STRIDE + CMP_BLOCK, (j + 1) * SEL_BLOCK)
    ov = np.maximum(hi - lo, 0) / CMP_STRIDE
    ov[n_cmp - 1] = 0.0
    key = np.arange(S)
    emat_t = (key[:, None] // SEL_BLOCK == np.arange(n_sel)[None, :]).astype(np.float32)
    return (jnp.asarray(ov.T, F32), jnp.asarray(emat_t.reshape(S // ck, ck, n_sel), BF16), ck)


def _outproj_kernel(h_ref, ys_ref, ym_ref, yn_ref, w_ref, g_ref, o_ref):
    acc = _dot(ys_ref[...], w_ref[0:SSM_WIDTH, :])
    acc = acc + _dot(ym_ref[...], w_ref[SSM_WIDTH:SSM_WIDTH + MLSTM_WIDTH, :])
    acc = acc + _dot(yn_ref[...], w_ref[SSM_WIDTH + MLSTM_WIDTH:, :])
    ms = jnp.mean(acc * acc, axis=-1, keepdims=True)
    o_ref[...] = h_ref[...] + acc * lax.rsqrt(ms + EPS) * g_ref[...]


def _outproj(h2, y_ssm, y_mls, y_nsa, w, layer, gain, B, S, ts):
    nt = S // ts
    row = lambda b, i: (b * nt + i, 0)
    full = lambda b, i: (0, 0)
    return pl.pallas_call(
        _outproj_kernel,
        grid=(B, nt),
        in_specs=[
            pl.BlockSpec((ts, D_MODEL), row),
            pl.BlockSpec((ts, SSM_WIDTH), row),
            pl.BlockSpec((ts, MLSTM_WIDTH), row),
            pl.BlockSpec((ts, NSA_WIDTH), row),
            pl.BlockSpec((None, D_MODEL, D_MODEL), lambda b, i: (layer, 0, 0)),
            pl.BlockSpec((1, D_MODEL), full),
        ],
        out_specs=pl.BlockSpec((ts, D_MODEL), row),
        out_shape=jax.ShapeDtypeStruct((B * S, D_MODEL), F32),
        compiler_params=pltpu.CompilerParams(
            dimension_semantics=("parallel", "parallel"), vmem_limit_bytes=VMEM_LIMIT),
        name="outproj",
    )(h2, y_ssm, y_mls, y_nsa, w, gain)


def _mlp_kernel(h_ref, g1_ref, w1_ref, w2_ref, g2_ref, o_ref, u_sc, acc_sc):
    kf = pl.program_id(1)

    @pl.when(kf == 0)
    def _():
        x = h_ref[...]
        ms = jnp.mean(x * x, axis=-1, keepdims=True)
        u_sc[...] = (x * lax.rsqrt(ms + EPS) * g1_ref[...]).astype(BF16)
        acc_sc[...] = jnp.zeros_like(acc_sc)

    a = jnp.maximum(_dot(u_sc[...], w1_ref[...].astype(BF16)), 0.0)
    acc_sc[...] += _dot((a * a).astype(BF16), w2_ref[...].astype(BF16))

    @pl.when(kf == pl.num_programs(1) - 1)
    def _():
        f = acc_sc[...]
        ms = jnp.mean(f * f, axis=-1, keepdims=True)
        o_ref[...] = h_ref[...] + f * lax.rsqrt(ms + EPS) * g2_ref[...]


def _mlp(h2, g1, w1, w2, layer, g2, tm, tf):
    rows = h2.shape[0]
    return pl.pallas_call(
        _mlp_kernel,
        grid=(rows // tm, D_FF // tf),
        in_specs=[
            pl.BlockSpec((tm, D_MODEL), lambda i, k: (i, 0)),
            pl.BlockSpec((1, D_MODEL), lambda i, k: (0, 0)),
            pl.BlockSpec((None, D_MODEL, tf), lambda i, k: (layer, 0, k)),
            pl.BlockSpec((None, tf, D_MODEL), lambda i, k: (layer, k, 0)),
            pl.BlockSpec((1, D_MODEL), lambda i, k: (0, 0)),
        ],
        out_specs=pl.BlockSpec((tm, D_MODEL), lambda i, k: (i, 0)),
        out_shape=jax.ShapeDtypeStruct((rows, D_MODEL), F32),
        scratch_shapes=[pltpu.VMEM((tm, D_MODEL), BF16), pltpu.VMEM((tm, D_MODEL), F32)],
        compiler_params=pltpu.CompilerParams(
            dimension_semantics=("parallel", "arbitrary"), vmem_limit_bytes=VMEM_LIMIT),
        name="mlp",
    )(h2, g1, w1, w2, g2)


def _inproj_pieces():
    return ((0, 1280), (1288, 1800), (1800, 1928), (2056, 2184), (2312, 2440),
            (1928, 2056), (2184, 2312), (2440, 2568),
            (1280, 1288), (None, GATE_COL - 8), (2568, 2580), (None, LANE - GATE_COL - 12),
            (None, GATE_COL), (2580, 2592), (None, LANE - GATE_COL - 12))


def _permute_w_in(w_in):
    parts = []
    for a, b in _inproj_pieces():
        if a is None:
            parts.append(jnp.zeros(w_in.shape[:-1] + (b,), BF16))
        else:
            parts.append(w_in[..., a:b].astype(BF16))
    out = jnp.concatenate(parts, axis=-1)
    assert out.shape[-1] == D_INP
    return out


def _rope_tables(positions):
    inv = ROPE_THETA ** (-jnp.arange(0, ROPE_DIMS, 2, dtype=F32) / ROPE_DIMS)
    ang = positions.astype(F32)[..., None] * inv
    cos, sin = jnp.cos(ang), jnp.sin(ang)
    z = jnp.zeros_like(cos)
    rest = NSA_HEAD_DIM - ROPE_DIMS
    pad_one = jnp.ones(cos.shape[:-1] + (rest,), F32)
    pad_zero = jnp.zeros(cos.shape[:-1] + (rest,), F32)
    rc = jnp.concatenate([cos, cos, pad_one], axis=-1)
    rs1 = jnp.concatenate([-sin, z, pad_zero], axis=-1)
    rs2 = jnp.concatenate([z, sin, pad_zero], axis=-1)
    tile = lambda t: jnp.tile(t, (1, 1, LANE // NSA_HEAD_DIM)).reshape(-1, LANE)
    return tile(rc), tile(rs1), tile(rs2)


def _s5_params(lam_re, lam_im, b_re, b_im, c_re, c_im, log_dt):
    G, P, Hc = SSM_GROUPS, SSM_STATE, SSM_GROUP
    dt = jnp.exp(log_dt)[:, None]
    mag = jnp.exp(lam_re * dt)
    ang = lam_im * dt
    ab_re = mag * jnp.cos(ang)
    ab_im = mag * jnp.sin(ang)
    den = lam_re * lam_re + lam_im * lam_im
    g_re = ((ab_re - 1.0) * lam_re + ab_im * lam_im) / den
    g_im = (ab_im * lam_re - (ab_re - 1.0) * lam_im) / den
    bb_re = g_re[..., None] * b_re - g_im[..., None] * b_im
    bb_im = g_re[..., None] * b_im + g_im[..., None] * b_re
    eye = jnp.eye(G, dtype=F32)
    blockdiag_in = lambda t: jnp.einsum('gph,gk->ghkp', t, eye).reshape(G * Hc, G * P)
    blockdiag_out = lambda t: jnp.einsum('ghp,gk->gpkh', t, eye).reshape(G * P, G * Hc)
    bb = jnp.concatenate([blockdiag_in(bb_re), blockdiag_in(bb_im)], axis=1).astype(BF16)
    cc = jnp.concatenate([blockdiag_out(c_re), -blockdiag_out(c_im)], axis=0).astype(BF16)
    a = jnp.stack([ab_re.reshape(-1), ab_im.reshape(-1)], axis=0)
    return bb, a, cc


def _group_mean_matrix(width, group):
    idx = np.arange(width) // group
    return jnp.asarray((idx[:, None] == idx[None, :]).astype(np.float32) / group, BF16)


def kernel(x, positions, ln_mix_pre, ln_mix_post, ln_mlp_pre, ln_mlp_post, w_in, w_out, ssm_lambda_re, ssm_lambda_im, ssm_b_re, ssm_b_im, ssm_c_re, ssm_c_im, ssm_d, ssm_log_dt, ssm_w_glu, mlstm_conv, mlstm_b_i, mlstm_b_f, cmp_pe_k, cmp_w1_k, cmp_w2_k, cmp_pe_v, cmp_w1_v, cmp_w2_v, gn_ssm, gn_mlstm, gn_nsa, mlp_w1, mlp_w2):
    B, S, D = x.shape
    depth = w_in.shape[0]
    assert D == D_MODEL and B == SUBLANE and S % 512 == 0 and S >= WINDOW + Q_BLOCK
    G, H = NSA_KV_GROUPS, MLSTM_HEADS
    ts_proj = 512
    ts_scan = 128

    rc, rs1, rs2 = _rope_tables(positions)
    w_in_p = _permute_w_in(w_in)
    w_out_b = w_out.astype(BF16)
    wglu_b = ssm_w_glu.astype(BF16)
    gm_ssm = _group_mean_matrix(SSM_WIDTH, SSM_GROUP)
    hm_mls = _group_mean_matrix(MLSTM_WIDTH, MLSTM_HEAD_DIM)
    consts = _nsa_consts(S)
    half = CMP_STRIDE * NSA_HEAD_DIM
    w1ab = jnp.stack([jnp.concatenate([cmp_w1_k[:, :half], cmp_w1_k[:, half:]], axis=-1),
                      jnp.concatenate([cmp_w1_v[:, :half], cmp_w1_v[:, half:]], axis=-1)], axis=1).astype(BF16)
    w1f = jnp.stack([cmp_w1_k, cmp_w1_v], axis=1)
    pef = jnp.stack([cmp_pe_k.reshape(depth, 1, -1), cmp_pe_v.reshape(depth, 1, -1)], axis=1)
    w2c = jnp.stack([cmp_w2_k, cmp_w2_v], axis=1).astype(BF16)
    w2ct = jnp.swapaxes(w2c, -1, -2)
    bias_row = jnp.concatenate([mlstm_b_i, mlstm_b_f], axis=-1)[:, :, None]

    bb, a, cc = jax.vmap(_s5_params)(ssm_lambda_re, ssm_lambda_im, ssm_b_re, ssm_b_im, ssm_c_re, ssm_c_im,
                                     ssm_log_dt)
    sh3 = lambda t: t.reshape(B, S, t.shape[-1])

    h = x.reshape(B * S, D)
    for l in range(depth):
        (su, mq, mk, mv, mo, aq, ckv, sk, wk, svt, wvt, gates, gates_t) = _inproj(
            h, ln_mix_pre[l][None], w_in_p, l, rc, rs1, rs2, B, S, ts_proj, consts[2])

        y_ssm = _s5(sh3(su), bb, a, cc, ssm_d[l][None], wglu_b, l, gm_ssm, gn_ssm[l][None], B, S, ts_scan)

        y_mls = _mlstm(sh3(mq), sh3(mk), sh3(mv), sh3(mo), gates_t,
                       mlstm_conv[l][:, :MLSTM_WIDTH], mlstm_conv[l][:, MLSTM_WIDTH:],
                       bias_row[l], hm_mls, gn_mlstm[l][None], B, S)

        cmp_k, cmp_t = _compress(ckv, w1ab, w1f, pef, w2c, w2ct, l, B, S)
        y_nsa = _nsa(aq, cmp_k, cmp_t, sk, svt, wk, wvt, gates_t,
                     gn_nsa[l].reshape(NSA_HEADS, NSA_HEAD_DIM, 1), consts, B, S)

        h = _outproj(h, y_ssm.reshape(B * S, SSM_WIDTH), y_mls.reshape(B * S, MLSTM_WIDTH), y_nsa,
                     w_out_b, l, ln_mix_post[l][None], B, S, ts_proj)
        h = _mlp(h, ln_mlp_pre[l][None], mlp_w1, mlp_w2, l, ln_mlp_post[l][None], 1024, 512)
    return h.reshape(B, S, D)
```

```python
import functools
import math

import numpy as np
import jax
import jax.numpy as jnp
from jax import lax
from jax.experimental import pallas as pl
from jax.experimental.pallas import tpu as pltpu

F32 = jnp.float32
BF16 = jnp.bfloat16
HIGHEST = lax.Precision.HIGHEST

D_MODEL = 1024
DEPTH = 4
SSM_WIDTH = 256
SSM_GROUP = 16
SSM_GROUPS = 16
SSM_STATE = 64
SSM_LANES = SSM_GROUPS * SSM_STATE
MLSTM_WIDTH = 256
MLSTM_HEADS = 4
MLSTM_HEAD_DIM = 64
MLSTM_CHUNK = 128
MLSTM_CONV = 4
NSA_WIDTH = 512
NSA_HEAD_DIM = 64
NSA_HEADS = 8
NSA_KV_GROUPS = 2
NSA_REP = NSA_HEADS // NSA_KV_GROUPS
NSA_KV_WIDTH = 128
CMP_BLOCK = 32
CMP_STRIDE = 16
CMP_HIDDEN = 256
SEL_BLOCK = 64
SEL_TOPN = 8
WINDOW = 256
Q_BLOCK = 128
FORCE_SCORE = 1e4
NEG_INF = -1e30
ROPE_THETA = 500000.0
ROPE_DIMS = 16
ROPE_HALF = 8
D_FF = 4096
EPS = 1e-6
D_IN = 2592

LANE = 128
SUBLANE = 8
VMEM_LIMIT = 56 * 1024 * 1024

C_SU, C_MQ, C_MK, C_MV, C_MO = 0, 256, 512, 768, 1024
C_AQ, C_CK, C_SK, C_WK = 1280, 1792, 1920, 2048
C_CV, C_SV, C_WV = 2176, 2304, 2432
C_G0, C_G1 = 2560, 2688
D_INP = 2816
GATE_COL = 16
Q_SCALE = NSA_HEAD_DIM ** -0.5 * math.log2(math.e)
MLSTM_ROWS = 4
SEL_UNROLL = 2


def _dot(a, b, precision=None):
    return jnp.dot(a, b, preferred_element_type=F32, precision=precision)


def _dot_nt(a, b):
    return lax.dot_general(a, b, (((1,), (1,)), ((), ())), preferred_element_type=F32)


def _dot_tn(a, b):
    return lax.dot_general(a, b, (((0,), (0,)), ((), ())), preferred_element_type=F32)


def _sigmoid(x):
    return 1.0 / (1.0 + jnp.exp(-x))


def _dot_split(x, w_bf16):
    hi = x.astype(BF16)
    lo = (x - hi.astype(F32)).astype(BF16)
    return _dot(hi, w_bf16) + _dot(lo, w_bf16)


def _gelu_tanh(x):
    return 0.5 * x * (1.0 + jnp.tanh(math.sqrt(2.0 / math.pi) * (x + 0.044715 * (x * x * x))))


def _log_sigmoid(x):
    return jnp.minimum(x, 0.0) - jnp.log(1.0 + jnp.exp(-jnp.abs(x)))


def _inproj_kernel(x_ref, g_ref, w_ref, rc_ref, rs1_ref, rs2_ref,
                   su_ref, mq_ref, mk_ref, mv_ref, mo_ref, aq_ref, ckv_ref, sk_ref, wk_ref,
                   svt_ref, wvt_ref, gt_ref, gtt_ref):
    x = x_ref[...]
    ms = jnp.mean(x * x, axis=-1, keepdims=True)
    u = (x * lax.rsqrt(ms + EPS) * g_ref[...]).astype(BF16)
    rc, rs1, rs2 = rc_ref[...], rs1_ref[...], rs2_ref[...]

    def mm(c0, width):
        return _dot(u, w_ref[:, c0:c0 + width])

    def rope(z):
        return z * rc + pltpu.roll(z, LANE - ROPE_HALF, 1) * rs1 + pltpu.roll(z, ROPE_HALF, 1) * rs2

    su_ref[...] = mm(C_SU, 256)
    mq_ref[...] = mm(C_MQ, 256)
    mk_ref[...] = mm(C_MK, 256)
    mv_ref[...] = mm(C_MV, 256).astype(BF16)
    mo_ref[...] = mm(C_MO, 256)
    def mm_pair(c0):
        z = mm(c0, 2 * LANE)
        return z[:, :LANE], z[:, LANE:]

    def put_heads(ref, first, z):
        ref[first] = z[:, :NSA_HEAD_DIM].astype(BF16)
        ref[first + 1] = z[:, NSA_HEAD_DIM:].astype(BF16)

    for j in range(NSA_HEADS // 4):
        for k, z in enumerate(mm_pair(C_AQ + 2 * LANE * j)):
            put_heads(aq_ref, 4 * j + 2 * k, rope(z) * Q_SCALE)
    z_ck, z_sk = mm_pair(C_CK)
    z_wk, z_cv = mm_pair(C_WK)
    z_sv, z_wv = mm_pair(C_SV)
    ckv_ref[0] = rope(z_ck)
    ckv_ref[1] = z_cv
    put_heads(sk_ref, 0, rope(z_sk))
    put_heads(wk_ref, 0, rope(z_wk))

    def put_chunks_t(ref, z):
        zt = jnp.transpose(z)
        width = ref.shape[-1]
        for g in range(NSA_KV_GROUPS):
            for j in range(ref.shape[1]):
                ref[g, j] = zt[g * NSA_HEAD_DIM:(g + 1) * NSA_HEAD_DIM, j * width:(j + 1) * width].astype(BF16)

    put_chunks_t(svt_ref, z_sv)
    put_chunks_t(wvt_ref, z_wv)
    z_g0, z_g1 = mm_pair(C_G0)
    gt_ref[...] = z_g0
    gtt_ref[0] = jnp.transpose(z_g0)
    gtt_ref[1] = jnp.transpose(z_g1)


def _inproj(h2, gain, w, layer, rc, rs1, rs2, B, S, ts, ck):
    nt = S // ts
    BS = B * S
    row = lambda b, i: (b * nt + i, 0)
    full = lambda b, i: (0, 0)
    headed = lambda b, i: (b, 0, i, 0)
    paired = lambda b, i: (0, b * nt + i, 0)
    in_specs = [
        pl.BlockSpec((ts, D_MODEL), row),
        pl.BlockSpec((1, D_MODEL), full),
        pl.BlockSpec((None, D_MODEL, D_INP), lambda b, i: (layer, 0, 0)),
        pl.BlockSpec((ts, LANE), row),
        pl.BlockSpec((ts, LANE), row),
        pl.BlockSpec((ts, LANE), row),
    ]
    kv_shape = jax.ShapeDtypeStruct((B, NSA_KV_GROUPS, S, NSA_HEAD_DIM), BF16)
    kv_spec = pl.BlockSpec((None, NSA_KV_GROUPS, ts, NSA_HEAD_DIM), headed)
    out_shape = [
        jax.ShapeDtypeStruct((BS, SSM_WIDTH), F32),
        jax.ShapeDtypeStruct((BS, MLSTM_WIDTH), F32),
        jax.ShapeDtypeStruct((BS, MLSTM_WIDTH), F32),
        jax.ShapeDtypeStruct((BS, MLSTM_WIDTH), BF16),
        jax.ShapeDtypeStruct((BS, MLSTM_WIDTH), F32),
        jax.ShapeDtypeStruct((B, NSA_HEADS, S, NSA_HEAD_DIM), BF16),
        jax.ShapeDtypeStruct((2, BS, NSA_KV_WIDTH), F32),
        kv_shape, kv_shape,
        jax.ShapeDtypeStruct((B, NSA_KV_GROUPS, S // ck, NSA_HEAD_DIM, ck), BF16),
        jax.ShapeDtypeStruct((B, NSA_KV_GROUPS, S // Q_BLOCK, NSA_HEAD_DIM, Q_BLOCK), BF16),
        jax.ShapeDtypeStruct((BS, LANE), F32),
        jax.ShapeDtypeStruct((NSA_KV_GROUPS, B, LANE, S), F32),
    ]
    out_specs = [
        pl.BlockSpec((ts, SSM_WIDTH), row),
        pl.BlockSpec((ts, MLSTM_WIDTH), row),
        pl.BlockSpec((ts, MLSTM_WIDTH), row),
        pl.BlockSpec((ts, MLSTM_WIDTH), row),
        pl.BlockSpec((ts, MLSTM_WIDTH), row),
        pl.BlockSpec((None, NSA_HEADS, ts, NSA_HEAD_DIM), headed),
        pl.BlockSpec((2, ts, NSA_KV_WIDTH), paired),
        kv_spec, kv_spec,
        pl.BlockSpec((None, NSA_KV_GROUPS, ts // ck, NSA_HEAD_DIM, ck), lambda b, i: (b, 0, i, 0, 0)),
        pl.BlockSpec((None, NSA_KV_GROUPS, ts // Q_BLOCK, NSA_HEAD_DIM, Q_BLOCK), lambda b, i: (b, 0, i, 0, 0)),
        pl.BlockSpec((ts, LANE), row),
        pl.BlockSpec((NSA_KV_GROUPS, None, LANE, ts), lambda b, i: (0, b, 0, i)),
    ]
    return pl.pallas_call(
        _inproj_kernel,
        grid=(B, nt),
        in_specs=in_specs,
        out_specs=out_specs,
        out_shape=out_shape,
        compiler_params=pltpu.CompilerParams(
            dimension_semantics=("parallel", "parallel"), vmem_limit_bytes=VMEM_LIMIT),
        name="inproj",
    )(h2, gain, w, rc, rs1, rs2)


def _s5_kernel(u_ref, bb_ref, a_ref, cc_ref, d_ref, wg_ref, gm_ref, gain_ref, o_ref, x_sc, st_sc, tm_sc, *, B, ts):
    @pl.when(pl.program_id(0) == 0)
    def _():
        st_sc[...] = jnp.zeros_like(st_sc)

    nl = SSM_WIDTH // LANE
    for b in range(B):
        for c in range(nl):
            tm_sc[c, pl.ds(b, ts, stride=B), :] = u_ref[b, :, c * LANE:(c + 1) * LANE]
    u = jnp.concatenate([tm_sc[c] for c in range(nl)], axis=1)
    ub = u.astype(BF16)
    for part in range(2):
        cols = slice(part * SSM_LANES, (part + 1) * SSM_LANES)
        x_sc[:, cols] = _dot(ub, bb_ref[:, cols])
    ar = jnp.broadcast_to(a_ref[0:1, :], (B, SSM_LANES))
    ai = jnp.broadcast_to(a_ref[1:2, :], (B, SSM_LANES))

    def step(t, carry):
        xr, xi = carry
        r = pl.multiple_of(t * B, B)
        br = x_sc[pl.ds(r, B), 0:SSM_LANES]
        bi = x_sc[pl.ds(r, B), SSM_LANES:2 * SSM_LANES]
        nr = ar * xr - ai * xi + br
        ni = ar * xi + ai * xr + bi
        x_sc[pl.ds(r, B), 0:SSM_LANES] = nr
        x_sc[pl.ds(r, B), SSM_LANES:2 * SSM_LANES] = ni
        return nr, ni

    xr, xi = lax.fori_loop(0, ts, step, (st_sc[0], st_sc[1]))
    st_sc[0] = xr
    st_sc[1] = xi

    half = (ts * B) // 2
    y = jnp.concatenate([_dot(x_sc[r * half:(r + 1) * half, :].astype(BF16), cc_ref[...]) for r in range(2)],
                        axis=0) + d_ref[...] * u
    y = _gelu_tanh(y)
    y = y * _sigmoid(_dot(y.astype(BF16), wg_ref[...]))
    ms = _dot_split(y * y, gm_ref[...])
    y = y * lax.rsqrt(ms + EPS) * gain_ref[...]
    for c in range(nl):
        tm_sc[c] = y[:, c * LANE:(c + 1) * LANE]
    for b in range(B):
        o_ref[b] = jnp.concatenate(
            [tm_sc[c, pl.ds(b, ts, stride=B), :] for c in range(nl)], axis=1).astype(BF16)


def _s5(u, bb, a, cc, d, wg, layer, gm, gain, B, S, ts):
    rows = ts * B
    full = lambda i: (0, 0)
    lsel = lambda i: (layer, 0, 0)
    return pl.pallas_call(
        functools.partial(_s5_kernel, B=B, ts=ts),
        grid=(S // ts,),
        in_specs=[
            pl.BlockSpec((B, ts, SSM_WIDTH), lambda i: (0, i, 0)),
            pl.BlockSpec((None, SSM_WIDTH, 2 * SSM_LANES), lsel),
            pl.BlockSpec((None, 2, SSM_LANES), lsel),
            pl.BlockSpec((None, 2 * SSM_LANES, SSM_WIDTH), lsel),
            pl.BlockSpec((1, SSM_WIDTH), full),
            pl.BlockSpec((None, SSM_WIDTH, SSM_WIDTH), lsel),
            pl.BlockSpec((SSM_WIDTH, SSM_WIDTH), full),
            pl.BlockSpec((1, SSM_WIDTH), full),
        ],
        out_specs=pl.BlockSpec((B, ts, SSM_WIDTH), lambda i: (0, i, 0)),
        out_shape=jax.ShapeDtypeStruct((B, S, SSM_WIDTH), BF16),
        scratch_shapes=[pltpu.VMEM((rows, 2 * SSM_LANES), F32), pltpu.VMEM((2, B, SSM_LANES), F32),
                        pltpu.VMEM((SSM_WIDTH // LANE, rows, LANE), F32)],
        compiler_params=pltpu.CompilerParams(
            dimension_semantics=("arbitrary",), vmem_limit_bytes=VMEM_LIMIT),
        name="s5",
    )(u, bb, a, cc, d, wg, gm, gain)


def _mlstm_kernel(q_ref, k_ref, v_ref, o_ref, gr_ref, cwq_ref, cwk_ref, br_ref, hm_ref,
                  gain_ref, y_ref, qt_sc, kt_sc, c_sc, m_sc, *, B):
    L, H, Dh, W = MLSTM_CHUNK, MLSTM_HEADS, MLSTM_HEAD_DIM, MLSTM_WIDTH

    @pl.when(pl.program_id(0) == 0)
    def _():
        qt_sc[...] = jnp.zeros_like(qt_sc)
        kt_sc[...] = jnp.zeros_like(kt_sc)
        c_sc[...] = jnp.zeros_like(c_sc)
        m_sc[...] = jnp.zeros_like(m_sc)

    visible = lax.broadcasted_iota(jnp.int32, (L, L), 0) <= lax.broadcasted_iota(jnp.int32, (L, L), 1)
    triu = visible.astype(F32)
    lane_w = lax.broadcasted_iota(jnp.int32, (1, W), 1) // Dh
    bd_mask = ((lax.broadcasted_iota(jnp.int32, (2 * W, W), 0) % W) // Dh
               == lax.broadcasted_iota(jnp.int32, (2 * W, W), 1) // Dh)
    row8 = lax.broadcasted_iota(jnp.int32, (SUBLANE, W), 0)
    cwq = cwq_ref[...]
    cwk = cwk_ref[...]
    ones_rows = jnp.ones((Dh, L), F32)

    def conv_silu(x, tail, w):
        acc = x * w[MLSTM_CONV - 1:MLSTM_CONV, :]
        for sft in range(1, MLSTM_CONV):
            xs = pltpu.roll(x, sft, 0)
            head = jnp.where(row8 < sft, pltpu.roll(tail, sft, 0), xs[:SUBLANE])
            xs = jnp.concatenate([head, xs[SUBLANE:]], axis=0)
            acc = acc + xs * w[MLSTM_CONV - 1 - sft:MLSTM_CONV - sft, :]
        return acc * _sigmoid(acc)

    def per_group(grp, _):
        bs = [grp * MLSTM_ROWS + n for n in range(MLSTM_ROWS)]
        st = [dict() for _ in bs]

        for b, d in zip(bs, st):
            q_raw = q_ref[b]
            k_raw = k_ref[b]
            d['q'] = conv_silu(q_raw, qt_sc[b], cwq)
            d['k'] = conv_silu(k_raw, kt_sc[b], cwk) * (Dh ** -0.5)
            qt_sc[b] = q_raw[L - SUBLANE:, :]
            kt_sc[b] = k_raw[L - SUBLANE:, :]
            d['gr'] = gr_ref[b] + br_ref[...]
        for d in st:
            d['brow'] = _dot(_log_sigmoid(d['gr']), triu, precision=HIGHEST)

        for b, d in zip(bs, st):
            gr, brow = d['gr'], d['brow']
            ccol = jnp.transpose(brow - pltpu.roll(gr, H, 0))
            m_all = m_sc[b]
            for key in ('w_intra', 'w_inter', 'e_mt', 'w_k', 'dec', 'm_new'):
                d[key] = []
            for hh in range(H):
                b_r = brow[H + hh:H + hh + 1, :]
                i_r = gr[hh:hh + 1, :]
                m_prev = m_all[hh:hh + 1, 0:1]
                dm = jnp.where(visible, b_r - ccol[:, H + hh:H + hh + 1], NEG_INF)
                inter = b_r + m_prev
                mt = jnp.maximum(inter, jnp.max(dm, axis=0, keepdims=True))
                d['w_intra'].append(jnp.exp(dm - mt))
                d['w_inter'].append(jnp.exp(inter - mt))
                d['e_mt'].append(jnp.exp(-mt))
                b_last = b_r[:, L - 1:L]
                logw = b_last - b_r + i_r
                mn = jnp.maximum(b_last + m_prev, jnp.max(logw, axis=1, keepdims=True))
                d['w_k'].append(jnp.exp(logw - mn))
                d['dec'].append(jnp.exp(b_last + m_prev - mn))
                d['m_new'].append(mn)
            d['qb'] = d['q'].astype(BF16)
            d['kb'] = d['k'].astype(BF16)
            d['vt'] = jnp.transpose(v_ref[b].astype(F32))
            d['c_t'] = c_sc[b]

        for d in st:
            d['qc'] = _dot(d['c_t'].astype(BF16), jnp.transpose(d['q']).astype(BF16))
            d['s_t'] = [_dot_nt(d['kb'], jnp.where(lane_w == hh, d['qb'], jnp.zeros_like(d['qb'])))
                        for hh in range(H)]
        for d in st:
            d['r'] = []
            for hh in range(H):
                v_aug = jnp.concatenate([d['vt'][hh * Dh:(hh + 1) * Dh], ones_rows], axis=0).astype(BF16)
                d['r'].append(_dot(v_aug, (d['s_t'][hh] * d['w_intra'][hh]).astype(BF16)))

        for b, d in zip(bs, st):
            h_t = []
            for hh in range(H):
                ch = slice(hh * Dh, (hh + 1) * Dh)
                num = d['w_inter'][hh] * d['qc'][ch] + d['r'][hh][:Dh]
                den = d['w_inter'][hh] * d['qc'][W + hh * Dh:W + (hh + 1) * Dh] + d['r'][hh][Dh:]
                h_t.append(num / jnp.maximum(jnp.abs(den), d['e_mt'][hh]))
            hout = jnp.transpose(jnp.concatenate(h_t, axis=0))
            d['y'] = _sigmoid(o_ref[b]) * hout
            d['vw'] = jnp.concatenate(
                [d['vt'][hh * Dh:(hh + 1) * Dh] * d['w_k'][hh] for hh in range(H)]
                + [jnp.broadcast_to(d['w_k'][hh], (Dh, L)) for hh in range(H)], axis=0).astype(BF16)
        for d in st:
            d['ms'] = _dot_split(d['y'] * d['y'], hm_ref[...])
            d['upd'] = _dot(d['vw'], d['kb'])

        for b, d in zip(bs, st):
            y_ref[b] = (d['y'] * lax.rsqrt(d['ms'] + EPS) * gain_ref[...]).astype(BF16)
            decay = d['dec'][H - 1]
            for hh in range(H - 2, -1, -1):
                decay = jnp.where(lane_w == hh, d['dec'][hh], decay)
            c_sc[b] = decay * d['c_t'] + jnp.where(bd_mask, d['upd'], 0.0)
            for hh in range(H):
                m_sc[b, hh:hh + 1, :] = jnp.broadcast_to(d['m_new'][hh], (1, LANE))
        return 0

    lax.fori_loop(0, B // MLSTM_ROWS, per_group, 0)


def _mlstm(mq, mk, mv, mo, grow, cwq, cwk, brow, hm, gain, B, S):
    L, W = MLSTM_CHUNK, MLSTM_WIDTH
    seq = lambda c: (0, c, 0)
    full = lambda c: (0, 0)
    return pl.pallas_call(
        functools.partial(_mlstm_kernel, B=B),
        grid=(S // L,),
        in_specs=[
            pl.BlockSpec((B, L, W), seq),
            pl.BlockSpec((B, L, W), seq),
            pl.BlockSpec((B, L, W), seq),
            pl.BlockSpec((B, L, W), seq),
            pl.BlockSpec((None, B, SUBLANE, L), lambda c: (0, 0, 0, c)),
            pl.BlockSpec((MLSTM_CONV, W), full),
            pl.BlockSpec((MLSTM_CONV, W), full),
            pl.BlockSpec((SUBLANE, 1), full),
            pl.BlockSpec((W, W), full),
            pl.BlockSpec((1, W), full),
        ],
        out_specs=pl.BlockSpec((B, L, W), seq),
        out_shape=jax.ShapeDtypeStruct((B, S, W), BF16),
        scratch_shapes=[
            pltpu.VMEM((B, SUBLANE, W), F32),
            pltpu.VMEM((B, SUBLANE, W), F32),
            pltpu.VMEM((B, 2 * W, W), F32),
            pltpu.VMEM((B, SUBLANE, LANE), F32),
        ],
        compiler_params=pltpu.CompilerParams(
            dimension_semantics=("arbitrary",), vmem_limit_bytes=VMEM_LIMIT),
        name="mlstm",
    )(mq, mk, mv, mo, grow, cwq, cwk, brow, hm, gain)


def _compress_kernel(c_ref, w1ab_ref, w1_ref, pe_ref, w2_ref, w2t_ref, o_ref, ot_ref, ch_sc):
    G, Dh = NSA_KV_GROUPS, NSA_HEAD_DIM
    rows = ch_sc.shape[0]
    n = rows // G
    for r in range(CMP_STRIDE):
        tok = c_ref[pl.ds(r, n, stride=CMP_STRIDE), :]
        for g in range(G):
            ch_sc[g * n:(g + 1) * n, r * Dh:(r + 1) * Dh] = tok[:, g * Dh:(g + 1) * Dh]
    ab = _dot(ch_sc[...].astype(BF16), w1ab_ref[...])
    const = _dot(pe_ref[...], w1_ref[...], precision=HIGHEST)
    hid = ab[:, :CMP_HIDDEN] + pltpu.roll(ab[:, CMP_HIDDEN:], rows - 1, 0) + const
    act = _gelu_tanh(hid).astype(BF16)
    o_ref[...] = _dot(act, w2_ref[...]).astype(BF16)
    ot_ref[...] = _dot_nt(w2t_ref[...], act).astype(BF16)


def _compress(ckv, w1ab, w1, pe, w2, w2t, layer, B, S):
    G, Dh = NSA_KV_GROUPS, NSA_HEAD_DIM
    n = S // CMP_STRIDE
    width = CMP_STRIDE * Dh
    wsel = lambda i, b: (layer, i, 0, 0)
    return pl.pallas_call(
        _compress_kernel,
        grid=(2, B),
        in_specs=[
            pl.BlockSpec((None, S, G * Dh), lambda i, b: (i, b, 0)),
            pl.BlockSpec((None, None, width, 2 * CMP_HIDDEN), wsel),
            pl.BlockSpec((None, None, 2 * width, CMP_HIDDEN), wsel),
            pl.BlockSpec((None, None, 1, 2 * width), wsel),
            pl.BlockSpec((None, None, CMP_HIDDEN, Dh), wsel),
            pl.BlockSpec((None, None, Dh, CMP_HIDDEN), wsel),
        ],
        out_specs=[pl.BlockSpec((None, None, G * n, Dh), lambda i, b: (i, b, 0, 0)),
                   pl.BlockSpec((None, None, Dh, G * n), lambda i, b: (i, b, 0, 0))],
        out_shape=[jax.ShapeDtypeStruct((2, B, G * n, Dh), BF16),
                   jax.ShapeDtypeStruct((2, B, Dh, G * n), BF16)],
        scratch_shapes=[pltpu.VMEM((G * n, width), F32)],
        compiler_params=pltpu.CompilerParams(
            dimension_semantics=("parallel", "parallel"), vmem_limit_bytes=VMEM_LIMIT),
        name="compress",
    )(ckv, w1ab, w1, pe, w2, w2t)


def _nsa_kernel(q_ref, kc_ref, vct_ref, ks_ref, vst_ref, kw_ref, vwt_ref, gtt_ref, gain_ref,
                ovt_ref, et_ref, o_ref, *, n_sel, n_top, ck, unroll):
    TQ, R, Dh, G = Q_BLOCK, NSA_REP, NSA_HEAD_DIM, NSA_KV_GROUPS
    groups = range(G)
    i = pl.program_id(1)
    t0 = i * TQ
    qs = [q_ref[g * R:(g + 1) * R].reshape(R * TQ, Dh) for g in groups]
    tq1 = t0 + lax.broadcasted_iota(jnp.int32, (1, TQ), 1)
    heads = lambda t: jnp.concatenate([t] * R, axis=1)

    ncmp = kc_ref.shape[0] // G
    nwb = WINDOW // TQ + 1
    wb0 = jnp.maximum(i - WINDOW // TQ, 0)
    ws = pl.multiple_of(wb0 * TQ, TQ)
    sc = [_dot_nt(kc_ref[g * ncmp:(g + 1) * ncmp, :], qs[g]) for g in groups]
    sw = [_dot_nt(kw_ref[g, pl.ds(ws, nwb * TQ), :], qs[g]) for g in groups]

    cend = lax.broadcasted_iota(jnp.int32, (ncmp, 1), 0) * CMP_STRIDE + (CMP_BLOCK - 1)
    cmask = heads(cend <= tq1)
    pc = []
    for g in groups:
        scm = jnp.where(cmask, sc[g], NEG_INF)
        ec = jnp.where(cmask, jnp.exp2(scm - jnp.max(scm, axis=0, keepdims=True)), 0.0)
        pc.append(ec * (1.0 / jnp.maximum(jnp.sum(ec, axis=0, keepdims=True), 1e-30)))
    oc = [_dot(vct_ref[:, g * ncmp:(g + 1) * ncmp], pc[g].astype(BF16)) for g in groups]

    imp = []
    for g in groups:
        psum = pc[g][:, 0:TQ]
        for r in range(1, R):
            psum = psum + pc[g][:, r * TQ:(r + 1) * TQ]
        imp.append(_dot(ovt_ref[...], psum, precision=HIGHEST))

    kpos = ws + lax.broadcasted_iota(jnp.int32, (nwb * TQ, 1), 0)
    wbias = heads(jnp.where((kpos <= tq1) & (tq1 - kpos < WINDOW), 0.0, NEG_INF))
    ow, l_w = [], []
    for g in groups:
        swb = sw[g] + wbias
        pw = jnp.exp2(swb - jnp.max(swb, axis=0, keepdims=True))
        vwt = jnp.concatenate([vwt_ref[g, wb0 + j] for j in range(nwb)], axis=1)
        ow.append(_dot(vwt, pw.astype(BF16)))
        l_w.append(jnp.sum(pw, axis=0, keepdims=True))

    blk = lax.broadcasted_iota(jnp.int32, (n_sel, 1), 0)
    valid = blk * SEL_BLOCK <= tq1
    forced = (blk == 0) | (blk == tq1 // SEL_BLOCK)
    selb = []
    for g in groups:
        val = jnp.where(forced, FORCE_SCORE, jnp.where(valid, imp[g], -FORCE_SCORE))
        rank = jnp.zeros((n_sel, TQ), F32)
        for jp in range(n_sel):
            other = val[jp:jp + 1, :]
            wins = jnp.where(blk > jp, jnp.where(other >= val, 1.0, 0.0), jnp.where(other > val, 1.0, 0.0))
            rank = rank + wins
        selb.append(jnp.where(rank < n_top, 0.0, NEG_INF).astype(BF16))

    def scores(g, c):
        k0 = c * ck
        kpos = k0 + lax.broadcasted_iota(jnp.int32, (ck, 1), 0)
        bias = jnp.where(kpos <= tq1, _dot(et_ref[c], selb[g]), NEG_INF)
        return _dot_nt(ks_ref[g, pl.ds(k0, ck), :], qs[g]) + heads(bias)

    def update(g, c, s, carry):
        m, l, acc = carry
        mn = jnp.maximum(m, jnp.max(s, axis=0, keepdims=True))
        alpha = jnp.exp2(m - mn)
        p = jnp.exp2(s - mn)
        l = alpha * l + jnp.sum(p, axis=0, keepdims=True)
        acc = alpha * acc + _dot(vst_ref[g, c], p.astype(BF16))
        return mn, l, acc

    def chunk_group(cg, carry):
        cs = [cg * unroll + sub for sub in range(unroll)]
        ss = [[scores(g, c) for g in groups] for c in cs]
        carry = list(carry)
        for c, s in zip(cs, ss):
            for g in groups:
                carry[g] = update(g, c, s[g], carry[g])
        return tuple(carry)

    n_chunks = (t0 + TQ + ck - 1) // ck
    init = tuple((jnp.full((1, R * TQ), NEG_INF, F32), jnp.zeros((1, R * TQ), F32),
                  jnp.zeros((Dh, R * TQ), F32)) for _ in groups)
    def unrolled(trips):
        def run():
            carry = init
            for cg in range(trips):
                carry = chunk_group(cg, carry)
            return carry
        return run

    max_trips = ks_ref.shape[1] // (ck * unroll)
    sel = lax.switch((n_chunks + unroll - 1) // unroll - 1, [unrolled(t) for t in range(1, max_trips + 1)])

    normed = []
    for g in groups:
        gs = _sigmoid(gtt_ref[g])
        _, l_s, acc_s = sel[g]
        for r in range(R):
            ln = slice(r * TQ, (r + 1) * TQ)
            o = (gs[3 * r:3 * r + 1, :] * oc[g][:, ln]
                 + (gs[3 * r + 1:3 * r + 2, :] / l_s[:, ln]) * acc_s[:, ln]
                 + (gs[3 * r + 2:3 * r + 3, :] / l_w[g][:, ln]) * ow[g][:, ln])
            ms = jnp.mean(o * o, axis=0, keepdims=True)
            normed.append(o * lax.rsqrt(ms + EPS) * gain_ref[g * R + r])
    for pair in range(G * R // 2):
        both = jnp.concatenate(normed[2 * pair:2 * pair + 2], axis=0)
        o_ref[:, pair * 2 * Dh:(pair + 1) * 2 * Dh] = jnp.transpose(both).astype(BF16)


def _nsa(aq, cmp_k, cmp_vt, ks, vst, kw, vwt, gates_t, gain, consts, B, S):
    G, H, TQ, Dh = NSA_KV_GROUPS, NSA_HEADS, Q_BLOCK, NSA_HEAD_DIM
    nq = S // TQ
    ncmp = S // CMP_STRIDE
    n_sel = S // SEL_BLOCK
    ovt, emat_t, ck = consts
    k_spec = pl.BlockSpec((None, G, S, Dh), lambda b, i: (b, 0, 0, 0))
    return pl.pallas_call(
        functools.partial(_nsa_kernel, n_sel=n_sel, n_top=min(SEL_TOPN, n_sel), ck=ck, unroll=SEL_UNROLL),
        grid=(B, nq),
        in_specs=[
            pl.BlockSpec((None, H, TQ, Dh), lambda b, i: (b, 0, i, 0)),
            pl.BlockSpec((None, None, G * ncmp, Dh), lambda b, i: (0, b, 0, 0)),
            pl.BlockSpec((None, None, Dh, G * ncmp), lambda b, i: (1, b, 0, 0)),
            k_spec,
            pl.BlockSpec((None, G, S // ck, Dh, ck), lambda b, i: (b, 0, 0, 0, 0)),
            k_spec,
            pl.BlockSpec((None, G, S // TQ, Dh, TQ), lambda b, i: (b, 0, 0, 0, 0)),
            pl.BlockSpec((G, None, 2 * SUBLANE, TQ), lambda b, i: (0, b, GATE_COL // (2 * SUBLANE), i)),
            pl.BlockSpec((H, Dh, 1), lambda b, i: (0, 0, 0)),
            pl.BlockSpec(ovt.shape, lambda b, i: (0, 0)),
            pl.BlockSpec(emat_t.shape, lambda b, i: (0, 0, 0)),
        ],
        out_specs=pl.BlockSpec((TQ, H * Dh), lambda b, i: (b * nq + i, 0)),
        out_shape=jax.ShapeDtypeStruct((B * S, H * Dh), BF16),
        compiler_params=pltpu.CompilerParams(
            dimension_semantics=("parallel", "arbitrary"), vmem_limit_bytes=VMEM_LIMIT),
        name="nsa",
    )(aq, cmp_k, cmp_vt, ks, vst, kw, vwt, gates_t, gain, ovt, emat_t)


def _nsa_consts(S):
    n_cmp = S // CMP_STRIDE
    n_sel = S // SEL_BLOCK
    ck = 256
    i = np.arange(n_cmp)[:, None]
    j = np.arange(n_sel)[None, :]
    lo = np.maximum(i * CMP_STRIDE, j * SEL_BLOCK)
    hi = np.minimum(i * CMP_STRIDE + CMP_BLOCK, (j + 1) * SEL_BLOCK)
    ov = np.maximum(hi - lo, 0) / CMP_STRIDE
    ov[n_cmp - 1] = 0.0
    key = np.arange(S)
    emat_t = (key[:, None] // SEL_BLOCK == np.arange(n_sel)[None, :]).astype(np.float32)
    return (jnp.asarray(ov.T, F32), jnp.asarray(emat_t.reshape(S // ck, ck, n_sel), BF16), ck)


def _outproj_kernel(h_ref, ys_ref, ym_ref, yn_ref, w_ref, g_ref, o_ref):
    acc = _dot(ys_ref[...], w_ref[0:SSM_WIDTH, :])
    acc = acc + _dot(ym_ref[...], w_ref[SSM_WIDTH:SSM_WIDTH + MLSTM_WIDTH, :])
    acc = acc + _dot(yn_ref[...], w_ref[SSM_WIDTH + MLSTM_WIDTH:, :])
    ms = jnp.mean(acc * acc, axis=-1, keepdims=True)
    o_ref[...] = h_ref[...] + acc * lax.rsqrt(ms + EPS) * g_ref[...]


def _outproj(h2, y_ssm, y_mls, y_nsa, w, layer, gain, B, S, ts):
    nt = S // ts
    row = lambda b, i: (b * nt + i, 0)
    full = lambda b, i: (0, 0)
    return pl.pallas_call(
        _outproj_kernel,
        grid=(B, nt),
        in_specs=[
            pl.BlockSpec((ts, D_MODEL), row),
            pl.BlockSpec((ts, SSM_WIDTH), row),
            pl.BlockSpec((ts, MLSTM_WIDTH), row),
            pl.BlockSpec((ts, NSA_WIDTH), row),
            pl.BlockSpec((None, D_MODEL, D_MODEL), lambda b, i: (layer, 0, 0)),
            pl.BlockSpec((1, D_MODEL), full),
        ],
        out_specs=pl.BlockSpec((ts, D_MODEL), row),
        out_shape=jax.ShapeDtypeStruct((B * S, D_MODEL), F32),
        compiler_params=pltpu.CompilerParams(
            dimension_semantics=("parallel", "parallel"), vmem_limit_bytes=VMEM_LIMIT),
        name="outproj",
    )(h2, y_ssm, y_mls, y_nsa, w, gain)


def _mlp_kernel(h_ref, g1_ref, w1_ref, w2_ref, g2_ref, o_ref, u_sc, acc_sc):
    kf = pl.program_id(1)

    @pl.when(kf == 0)
    def _():
        x = h_ref[...]
        ms = jnp.mean(x * x, axis=-1, keepdims=True)
        u_sc[...] = (x * lax.rsqrt(ms + EPS) * g1_ref[...]).astype(BF16)
        acc_sc[...] = jnp.zeros_like(acc_sc)

    a = jnp.maximum(_dot(u_sc[...], w1_ref[...].astype(BF16)), 0.0)
    acc_sc[...] += _dot((a * a).astype(BF16), w2_ref[...].astype(BF16))

    @pl.when(kf == pl.num_programs(1) - 1)
    def _():
        f = acc_sc[...]
        ms = jnp.mean(f * f, axis=-1, keepdims=True)
        o_ref[...] = h_ref[...] + f * lax.rsqrt(ms + EPS) * g2_ref[...]


def _mlp(h2, g1, w1, w2, layer, g2, tm, tf):
    rows = h2.shape[0]
    return pl.pallas_call(
        _mlp_kernel,
        grid=(rows // tm, D_FF // tf),
        in_specs=[
            pl.BlockSpec((tm, D_MODEL), lambda i, k: (i, 0)),
            pl.BlockSpec((1, D_MODEL), lambda i, k: (0, 0)),
            pl.BlockSpec((None, D_MODEL, tf), lambda i, k: (layer, 0, k)),
            pl.BlockSpec((None, tf, D_MODEL), lambda i, k: (layer, k, 0)),
            pl.BlockSpec((1, D_MODEL), lambda i, k: (0, 0)),
        ],
        out_specs=pl.BlockSpec((tm, D_MODEL), lambda i, k: (i, 0)),
        out_shape=jax.ShapeDtypeStruct((rows, D_MODEL), F32),
        scratch_shapes=[pltpu.VMEM((tm, D_MODEL), BF16), pltpu.VMEM((tm, D_MODEL), F32)],
        compiler_params=pltpu.CompilerParams(
            dimension_semantics=("parallel", "arbitrary"), vmem_limit_bytes=VMEM_LIMIT),
        name="mlp",
    )(h2, g1, w1, w2, g2)


def _inproj_pieces():
    return ((0, 1280), (1288, 1800), (1800, 1928), (2056, 2184), (2312, 2440),
            (1928, 2056), (2184, 2312), (2440, 2568),
            (1280, 1288), (None, GATE_COL - 8), (2568, 2580), (None, LANE - GATE_COL - 12),
            (None, GATE_COL), (2580, 2592), (None, LANE - GATE_COL - 12))


def _permute_w_in(w_in):
    parts = []
    for a, b in _inproj_pieces():
        if a is None:
            parts.append(jnp.zeros(w_in.shape[:-1] + (b,), BF16))
        else:
            parts.append(w_in[..., a:b].astype(BF16))
    out = jnp.concatenate(parts, axis=-1)
    assert out.shape[-1] == D_INP
    return out


def _rope_tables(positions):
    inv = ROPE_THETA ** (-jnp.arange(0, ROPE_DIMS, 2, dtype=F32) / ROPE_DIMS)
    ang = positions.astype(F32)[..., None] * inv
    cos, sin = jnp.cos(ang), jnp.sin(ang)
    z = jnp.zeros_like(cos)
    rest = NSA_HEAD_DIM - ROPE_DIMS
    pad_one = jnp.ones(cos.shape[:-1] + (rest,), F32)
    pad_zero = jnp.zeros(cos.shape[:-1] + (rest,), F32)
    rc = jnp.concatenate([cos, cos, pad_one], axis=-1)
    rs1 = jnp.concatenate([-sin, z, pad_zero], axis=-1)
    rs2 = jnp.concatenate([z, sin, pad_zero], axis=-1)
    tile = lambda t: jnp.tile(t, (1, 1, LANE // NSA_HEAD_DIM)).reshape(-1, LANE)
    return tile(rc), tile(rs1), tile(rs2)


def _s5_params(lam_re, lam_im, b_re, b_im, c_re, c_im, log_dt):
    G, P, Hc = SSM_GROUPS, SSM_STATE, SSM_GROUP
    dt = jnp.exp(log_dt)[:, None]
    mag = jnp.exp(lam_re * dt)
    ang = lam_im * dt
    ab_re = mag * jnp.cos(ang)
    ab_im = mag * jnp.sin(ang)
    den = lam_re * lam_re + lam_im * lam_im
    g_re = ((ab_re - 1.0) * lam_re + ab_im * lam_im) / den
    g_im = (ab_im * lam_re - (ab_re - 1.0) * lam_im) / den
    bb_re = g_re[..., None] * b_re - g_im[..., None] * b_im
    bb_im = g_re[..., None] * b_im + g_im[..., None] * b_re
    eye = jnp.eye(G, dtype=F32)
    blockdiag_in = lambda t: jnp.einsum('gph,gk->ghkp', t, eye).reshape(G * Hc, G * P)
    blockdiag_out = lambda t: jnp.einsum('ghp,gk->gpkh', t, eye).reshape(G * P, G * Hc)
    bb = jnp.concatenate([blockdiag_in(bb_re), blockdiag_in(bb_im)], axis=1).astype(BF16)
    cc = jnp.concatenate([blockdiag_out(c_re), -blockdiag_out(c_im)], axis=0).astype(BF16)
    a = jnp.stack([ab_re.reshape(-1), ab_im.reshape(-1)], axis=0)
    return bb, a, cc


def _group_mean_matrix(width, group):
    idx = np.arange(width) // group
    return jnp.asarray((idx[:, None] == idx[None, :]).astype(np.float32) / group, BF16)


def kernel(x, positions, ln_mix_pre, ln_mix_post, ln_mlp_pre, ln_mlp_post, w_in, w_out, ssm_lambda_re, ssm_lambda_im, ssm_b_re, ssm_b_im, ssm_c_re, ssm_c_im, ssm_d, ssm_log_dt, ssm_w_glu, mlstm_conv, mlstm_b_i, mlstm_b_f, cmp_pe_k, cmp_w1_k, cmp_w2_k, cmp_pe_v, cmp_w1_v, cmp_w2_v, gn_ssm, gn_mlstm, gn_nsa, mlp_w1, mlp_w2):
    B, S, D = x.shape
    depth = w_in.shape[0]
    assert D == D_MODEL and B == SUBLANE and S % 512 == 0 and S >= WINDOW + Q_BLOCK
    G, H = NSA_KV_GROUPS, MLSTM_HEADS
    ts_proj = 512
    ts_scan = 128

    rc, rs1, rs2 = _rope_tables(positions)
    w_in_p = _permute_w_in(w_in)
    w_out_b = w_out.astype(BF16)
    wglu_b = ssm_w_glu.astype(BF16)
    gm_ssm = _group_mean_matrix(SSM_WIDTH, SSM_GROUP)
    hm_mls = _group_mean_matrix(MLSTM_WIDTH, MLSTM_HEAD_DIM)
    consts = _nsa_consts(S)
    half = CMP_STRIDE * NSA_HEAD_DIM
    w1ab = jnp.stack([jnp.concatenate([cmp_w1_k[:, :half], cmp_w1_k[:, half:]], axis=-1),
                      jnp.concatenate([cmp_w1_v[:, :half], cmp_w1_v[:, half:]], axis=-1)], axis=1).astype(BF16)
    w1f = jnp.stack([cmp_w1_k, cmp_w1_v], axis=1)
    pef = jnp.stack([cmp_pe_k.reshape(depth, 1, -1), cmp_pe_v.reshape(depth, 1, -1)], axis=1)
    w2c = jnp.stack([cmp_w2_k, cmp_w2_v], axis=1).astype(BF16)
    w2ct = jnp.swapaxes(w2c, -1, -2)
    bias_row = jnp.concatenate([mlstm_b_i, mlstm_b_f], axis=-1)[:, :, None]

    bb, a, cc = jax.vmap(_s5_params)(ssm_lambda_re, ssm_lambda_im, ssm_b_re, ssm_b_im, ssm_c_re, ssm_c_im,
                                     ssm_log_dt)
    sh3 = lambda t: t.reshape(B, S, t.shape[-1])

    h = x.reshape(B * S, D)
    for l in range(depth):
        (su, mq, mk, mv, mo, aq, ckv, sk, wk, svt, wvt, gates, gates_t) = _inproj(
            h, ln_mix_pre[l][None], w_in_p, l, rc, rs1, rs2, B, S, ts_proj, consts[2])

        y_ssm = _s5(sh3(su), bb, a, cc, ssm_d[l][None], wglu_b, l, gm_ssm, gn_ssm[l][None], B, S, ts_scan)

        y_mls = _mlstm(sh3(mq), sh3(mk), sh3(mv), sh3(mo), gates_t,
                       mlstm_conv[l][:, :MLSTM_WIDTH], mlstm_conv[l][:, MLSTM_WIDTH:],
                       bias_row[l], hm_mls, gn_mlstm[l][None], B, S)

        cmp_k, cmp_t = _compress(ckv, w1ab, w1f, pef, w2c, w2ct, l, B, S)
        y_nsa = _nsa(aq, cmp_k, cmp_t, sk, svt, wk, wvt, gates_t,
                     gn_nsa[l].reshape(NSA_HEADS, NSA_HEAD_DIM, 1), consts, B, S)

        h = _outproj(h, y_ssm.reshape(B * S, SSM_WIDTH), y_mls.reshape(B * S, MLSTM_WIDTH), y_nsa,
                     w_out_b, l, ln_mix_post[l][None], B, S, ts_proj)
        h = _mlp(h, ln_mlp_pre[l][None], mlp_w1, mlp_w2, l, ln_mlp_post[l][None], 1024, 1024)
    return h.reshape(B, S, D)
```

```python
import functools
import math

import numpy as np
import jax
import jax.numpy as jnp
from jax import lax
from jax.experimental import pallas as pl
from jax.experimental.pallas import tpu as pltpu

F32 = jnp.float32
BF16 = jnp.bfloat16
HIGHEST = lax.Precision.HIGHEST

D_MODEL = 1024
DEPTH = 4
SSM_WIDTH = 256
SSM_GROUP = 16
SSM_GROUPS = 16
SSM_STATE = 64
SSM_LANES = SSM_GROUPS * SSM_STATE
MLSTM_WIDTH = 256
MLSTM_HEADS = 4
MLSTM_HEAD_DIM = 64
MLSTM_CHUNK = 128
MLSTM_CONV = 4
NSA_WIDTH = 512
NSA_HEAD_DIM = 64
NSA_HEADS = 8
NSA_KV_GROUPS = 2
NSA_REP = NSA_HEADS // NSA_KV_GROUPS
NSA_KV_WIDTH = 128
CMP_BLOCK = 32
CMP_STRIDE = 16
CMP_HIDDEN = 256
SEL_BLOCK = 64
SEL_TOPN = 8
WINDOW = 256
Q_BLOCK = 128
FORCE_SCORE = 1e4
NEG_INF = -1e30
ROPE_THETA = 500000.0
ROPE_DIMS = 16
ROPE_HALF = 8
D_FF = 4096
EPS = 1e-6
D_IN = 2592

LANE = 128
SUBLANE = 8
VMEM_LIMIT = 56 * 1024 * 1024

C_SU, C_MQ, C_MK, C_MV, C_MO = 0, 256, 512, 768, 1024
C_AQ, C_CK, C_SK, C_WK = 1280, 1792, 1920, 2048
C_CV, C_SV, C_WV = 2176, 2304, 2432
C_G0, C_G1 = 2560, 2688
D_INP = 2816
GATE_COL = 16
VAL_ROWS = NSA_HEAD_DIM + 16
Q_SCALE = NSA_HEAD_DIM ** -0.5 * math.log2(math.e)
MLSTM_ROWS = 4
SEL_UNROLL = 2


def _dot(a, b, precision=None):
    return jnp.dot(a, b, preferred_element_type=F32, precision=precision)


def _dot_nt(a, b):
    return lax.dot_general(a, b, (((1,), (1,)), ((), ())), preferred_element_type=F32)


def _dot_tn(a, b):
    return lax.dot_general(a, b, (((0,), (0,)), ((), ())), preferred_element_type=F32)


def _sigmoid(x):
    return 1.0 / (1.0 + jnp.exp(-x))


def _dot_split(x, w_bf16):
    hi = x.astype(BF16)
    lo = (x - hi.astype(F32)).astype(BF16)
    return _dot(hi, w_bf16) + _dot(lo, w_bf16)


def _gelu_tanh(x):
    return 0.5 * x * (1.0 + jnp.tanh(math.sqrt(2.0 / math.pi) * (x + 0.044715 * (x * x * x))))


def _log_sigmoid(x):
    return jnp.minimum(x, 0.0) - jnp.log(1.0 + jnp.exp(-jnp.abs(x)))


def _inproj_kernel(x_ref, g_ref, w_ref, rc_ref, rs1_ref, rs2_ref,
                   su_ref, mq_ref, mk_ref, mv_ref, mo_ref, aq_ref, ckv_ref, sk_ref, wk_ref,
                   svt_ref, wvt_ref, gt_ref, gtt_ref):
    x = x_ref[...]
    ms = jnp.mean(x * x, axis=-1, keepdims=True)
    u = (x * lax.rsqrt(ms + EPS) * g_ref[...]).astype(BF16)
    rc, rs1, rs2 = rc_ref[...], rs1_ref[...], rs2_ref[...]

    def mm(c0, width):
        return _dot(u, w_ref[:, c0:c0 + width])

    def rope(z):
        return z * rc + pltpu.roll(z, LANE - ROPE_HALF, 1) * rs1 + pltpu.roll(z, ROPE_HALF, 1) * rs2

    su_ref[...] = mm(C_SU, 256)
    mq_ref[...] = mm(C_MQ, 256)
    mk_ref[...] = mm(C_MK, 256)
    mv_ref[...] = mm(C_MV, 256).astype(BF16)
    mo_ref[...] = mm(C_MO, 256)
    def mm_pair(c0):
        z = mm(c0, 2 * LANE)
        return z[:, :LANE], z[:, LANE:]

    def put_heads(ref, first, z):
        ref[first] = z[:, :NSA_HEAD_DIM].astype(BF16)
        ref[first + 1] = z[:, NSA_HEAD_DIM:].astype(BF16)

    for j in range(NSA_HEADS // 4):
        for k, z in enumerate(mm_pair(C_AQ + 2 * LANE * j)):
            put_heads(aq_ref, 4 * j + 2 * k, rope(z) * Q_SCALE)
    z_ck, z_sk = mm_pair(C_CK)
    z_wk, z_cv = mm_pair(C_WK)
    z_sv, z_wv = mm_pair(C_SV)
    ckv_ref[0] = rope(z_ck)
    ckv_ref[1] = z_cv
    put_heads(sk_ref, 0, rope(z_sk))
    put_heads(wk_ref, 0, rope(z_wk))

    def put_chunks_t(ref, z):
        zt = jnp.transpose(z)
        width = ref.shape[-1]
        ones = jnp.ones((VAL_ROWS - NSA_HEAD_DIM, width), BF16)
        for g in range(NSA_KV_GROUPS):
            for j in range(ref.shape[1]):
                ref[g, j, :NSA_HEAD_DIM] = zt[g * NSA_HEAD_DIM:(g + 1) * NSA_HEAD_DIM,
                                              j * width:(j + 1) * width].astype(BF16)
                ref[g, j, NSA_HEAD_DIM:] = ones

    put_chunks_t(svt_ref, z_sv)
    put_chunks_t(wvt_ref, z_wv)
    z_g0, z_g1 = mm_pair(C_G0)
    gt_ref[...] = z_g0
    gtt_ref[0] = jnp.transpose(z_g0)
    gtt_ref[1] = jnp.transpose(z_g1)


def _inproj(h2, gain, w, layer, rc, rs1, rs2, B, S, ts, ck):
    nt = S // ts
    BS = B * S
    row = lambda b, i: (b * nt + i, 0)
    full = lambda b, i: (0, 0)
    headed = lambda b, i: (b, 0, i, 0)
    paired = lambda b, i: (0, b * nt + i, 0)
    in_specs = [
        pl.BlockSpec((ts, D_MODEL), row),
        pl.BlockSpec((1, D_MODEL), full),
        pl.BlockSpec((None, D_MODEL, D_INP), lambda b, i: (layer, 0, 0)),
        pl.BlockSpec((ts, LANE), row),
        pl.BlockSpec((ts, LANE), row),
        pl.BlockSpec((ts, LANE), row),
    ]
    kv_shape = jax.ShapeDtypeStruct((B, NSA_KV_GROUPS, S, NSA_HEAD_DIM), BF16)
    kv_spec = pl.BlockSpec((None, NSA_KV_GROUPS, ts, NSA_HEAD_DIM), headed)
    out_shape = [
        jax.ShapeDtypeStruct((BS, SSM_WIDTH), F32),
        jax.ShapeDtypeStruct((BS, MLSTM_WIDTH), F32),
        jax.ShapeDtypeStruct((BS, MLSTM_WIDTH), F32),
        jax.ShapeDtypeStruct((BS, MLSTM_WIDTH), BF16),
        jax.ShapeDtypeStruct((BS, MLSTM_WIDTH), F32),
        jax.ShapeDtypeStruct((B, NSA_HEADS, S, NSA_HEAD_DIM), BF16),
        jax.ShapeDtypeStruct((2, BS, NSA_KV_WIDTH), F32),
        kv_shape, kv_shape,
        jax.ShapeDtypeStruct((B, NSA_KV_GROUPS, S // ck, VAL_ROWS, ck), BF16),
        jax.ShapeDtypeStruct((B, NSA_KV_GROUPS, S // Q_BLOCK, VAL_ROWS, Q_BLOCK), BF16),
        jax.ShapeDtypeStruct((BS, LANE), F32),
        jax.ShapeDtypeStruct((NSA_KV_GROUPS, B, LANE, S), F32),
    ]
    out_specs = [
        pl.BlockSpec((ts, SSM_WIDTH), row),
        pl.BlockSpec((ts, MLSTM_WIDTH), row),
        pl.BlockSpec((ts, MLSTM_WIDTH), row),
        pl.BlockSpec((ts, MLSTM_WIDTH), row),
        pl.BlockSpec((ts, MLSTM_WIDTH), row),
        pl.BlockSpec((None, NSA_HEADS, ts, NSA_HEAD_DIM), headed),
        pl.BlockSpec((2, ts, NSA_KV_WIDTH), paired),
        kv_spec, kv_spec,
        pl.BlockSpec((None, NSA_KV_GROUPS, ts // ck, VAL_ROWS, ck), lambda b, i: (b, 0, i, 0, 0)),
        pl.BlockSpec((None, NSA_KV_GROUPS, ts // Q_BLOCK, VAL_ROWS, Q_BLOCK), lambda b, i: (b, 0, i, 0, 0)),
        pl.BlockSpec((ts, LANE), row),
        pl.BlockSpec((NSA_KV_GROUPS, None, LANE, ts), lambda b, i: (0, b, 0, i)),
    ]
    return pl.pallas_call(
        _inproj_kernel,
        grid=(B, nt),
        in_specs=in_specs,
        out_specs=out_specs,
        out_shape=out_shape,
        compiler_params=pltpu.CompilerParams(
            dimension_semantics=("parallel", "parallel"), vmem_limit_bytes=VMEM_LIMIT),
        name="inproj",
    )(h2, gain, w, rc, rs1, rs2)


def _s5_kernel(u_ref, bb_ref, a_ref, cc_ref, d_ref, wg_ref, gm_ref, gain_ref, o_ref, x_sc, st_sc, tm_sc, *, B, ts):
    @pl.when(pl.program_id(0) == 0)
    def _():
        st_sc[...] = jnp.zeros_like(st_sc)

    nl = SSM_WIDTH // LANE
    for b in range(B):
        for c in range(nl):
            tm_sc[c, pl.ds(b, ts, stride=B), :] = u_ref[b, :, c * LANE:(c + 1) * LANE]
    u = jnp.concatenate([tm_sc[c] for c in range(nl)], axis=1)
    ub = u.astype(BF16)
    for part in range(2):
        cols = slice(part * SSM_LANES, (part + 1) * SSM_LANES)
        x_sc[:, cols] = _dot(ub, bb_ref[:, cols])
    ar = jnp.broadcast_to(a_ref[0:1, :], (B, SSM_LANES))
    ai = jnp.broadcast_to(a_ref[1:2, :], (B, SSM_LANES))

    def step(t, carry):
        xr, xi = carry
        r = pl.multiple_of(t * B, B)
        br = x_sc[pl.ds(r, B), 0:SSM_LANES]
        bi = x_sc[pl.ds(r, B), SSM_LANES:2 * SSM_LANES]
        nr = ar * xr - ai * xi + br
        ni = ar * xi + ai * xr + bi
        x_sc[pl.ds(r, B), 0:SSM_LANES] = nr
        x_sc[pl.ds(r, B), SSM_LANES:2 * SSM_LANES] = ni
        return nr, ni

    xr, xi = lax.fori_loop(0, ts, step, (st_sc[0], st_sc[1]))
    st_sc[0] = xr
    st_sc[1] = xi

    half = (ts * B) // 2
    y = jnp.concatenate([_dot(x_sc[r * half:(r + 1) * half, :].astype(BF16), cc_ref[...]) for r in range(2)],
                        axis=0) + d_ref[...] * u
    y = _gelu_tanh(y)
    y = y * _sigmoid(_dot(y.astype(BF16), wg_ref[...]))
    ms = _dot_split(y * y, gm_ref[...])
    y = y * lax.rsqrt(ms + EPS) * gain_ref[...]
    for c in range(nl):
        tm_sc[c] = y[:, c * LANE:(c + 1) * LANE]
    for b in range(B):
        o_ref[b] = jnp.concatenate(
            [tm_sc[c, pl.ds(b, ts, stride=B), :] for c in range(nl)], axis=1).astype(BF16)


def _s5(u, bb, a, cc, d, wg, layer, gm, gain, B, S, ts):
    rows = ts * B
    full = lambda i: (0, 0)
    lsel = lambda i: (layer, 0, 0)
    return pl.pallas_call(
        functools.partial(_s5_kernel, B=B, ts=ts),
        grid=(S // ts,),
        in_specs=[
            pl.BlockSpec((B, ts, SSM_WIDTH), lambda i: (0, i, 0)),
            pl.BlockSpec((None, SSM_WIDTH, 2 * SSM_LANES), lsel),
            pl.BlockSpec((None, 2, SSM_LANES), lsel),
            pl.BlockSpec((None, 2 * SSM_LANES, SSM_WIDTH), lsel),
            pl.BlockSpec((1, SSM_WIDTH), full),
            pl.BlockSpec((None, SSM_WIDTH, SSM_WIDTH), lsel),
            pl.BlockSpec((SSM_WIDTH, SSM_WIDTH), full),
            pl.BlockSpec((1, SSM_WIDTH), full),
        ],
        out_specs=pl.BlockSpec((B, ts, SSM_WIDTH), lambda i: (0, i, 0)),
        out_shape=jax.ShapeDtypeStruct((B, S, SSM_WIDTH), BF16),
        scratch_shapes=[pltpu.VMEM((rows, 2 * SSM_LANES), F32), pltpu.VMEM((2, B, SSM_LANES), F32),
                        pltpu.VMEM((SSM_WIDTH // LANE, rows, LANE), F32)],
        compiler_params=pltpu.CompilerParams(
            dimension_semantics=("arbitrary",), vmem_limit_bytes=VMEM_LIMIT),
        name="s5",
    )(u, bb, a, cc, d, wg, gm, gain)


def _mlstm_kernel(q_ref, k_ref, v_ref, o_ref, gr_ref, cwq_ref, cwk_ref, br_ref, hm_ref,
                  gain_ref, y_ref, qt_sc, kt_sc, c_sc, m_sc, *, B):
    L, H, Dh, W = MLSTM_CHUNK, MLSTM_HEADS, MLSTM_HEAD_DIM, MLSTM_WIDTH

    @pl.when(pl.program_id(0) == 0)
    def _():
        qt_sc[...] = jnp.zeros_like(qt_sc)
        kt_sc[...] = jnp.zeros_like(kt_sc)
        c_sc[...] = jnp.zeros_like(c_sc)
        m_sc[...] = jnp.zeros_like(m_sc)

    visible = lax.broadcasted_iota(jnp.int32, (L, L), 0) <= lax.broadcasted_iota(jnp.int32, (L, L), 1)
    triu = visible.astype(F32)
    lane_w = lax.broadcasted_iota(jnp.int32, (1, W), 1) // Dh
    bd_mask = ((lax.broadcasted_iota(jnp.int32, (2 * W, W), 0) % W) // Dh
               == lax.broadcasted_iota(jnp.int32, (2 * W, W), 1) // Dh)
    row8 = lax.broadcasted_iota(jnp.int32, (SUBLANE, W), 0)
    cwq = cwq_ref[...]
    cwk = cwk_ref[...]
    ones_rows = jnp.ones((Dh, L), F32)

    def conv_silu(x, tail, w):
        acc = x * w[MLSTM_CONV - 1:MLSTM_CONV, :]
        for sft in range(1, MLSTM_CONV):
            xs = pltpu.roll(x, sft, 0)
            head = jnp.where(row8 < sft, pltpu.roll(tail, sft, 0), xs[:SUBLANE])
            xs = jnp.concatenate([head, xs[SUBLANE:]], axis=0)
            acc = acc + xs * w[MLSTM_CONV - 1 - sft:MLSTM_CONV - sft, :]
        return acc * _sigmoid(acc)

    def per_group(grp, _):
        bs = [grp * MLSTM_ROWS + n for n in range(MLSTM_ROWS)]
        st = [dict() for _ in bs]

        for b, d in zip(bs, st):
            q_raw = q_ref[b]
            k_raw = k_ref[b]
            d['q'] = conv_silu(q_raw, qt_sc[b], cwq)
            d['k'] = conv_silu(k_raw, kt_sc[b], cwk) * (Dh ** -0.5)
            qt_sc[b] = q_raw[L - SUBLANE:, :]
            kt_sc[b] = k_raw[L - SUBLANE:, :]
            d['gr'] = gr_ref[b] + br_ref[...]
        for d in st:
            d['brow'] = _dot(_log_sigmoid(d['gr']), triu, precision=HIGHEST)

        for b, d in zip(bs, st):
            gr, brow = d['gr'], d['brow']
            ccol = jnp.transpose(brow - pltpu.roll(gr, H, 0))
            m_all = m_sc[b]
            for key in ('w_intra', 'w_inter', 'e_mt', 'w_k', 'dec', 'm_new'):
                d[key] = []
            for hh in range(H):
                b_r = brow[H + hh:H + hh + 1, :]
                i_r = gr[hh:hh + 1, :]
                m_prev = m_all[hh:hh + 1, 0:1]
                dm = jnp.where(visible, b_r - ccol[:, H + hh:H + hh + 1], NEG_INF)
                inter = b_r + m_prev
                mt = jnp.maximum(inter, jnp.max(dm, axis=0, keepdims=True))
                d['w_intra'].append(jnp.exp(dm - mt))
                d['w_inter'].append(jnp.exp(inter - mt))
                d['e_mt'].append(jnp.exp(-mt))
                b_last = b_r[:, L - 1:L]
                logw = b_last - b_r + i_r
                mn = jnp.maximum(b_last + m_prev, jnp.max(logw, axis=1, keepdims=True))
                d['w_k'].append(jnp.exp(logw - mn))
                d['dec'].append(jnp.exp(b_last + m_prev - mn))
                d['m_new'].append(mn)
            d['qb'] = d['q'].astype(BF16)
            d['kb'] = d['k'].astype(BF16)
            d['vt'] = jnp.transpose(v_ref[b].astype(F32))
            d['c_t'] = c_sc[b]

        for d in st:
            d['qc'] = _dot(d['c_t'].astype(BF16), jnp.transpose(d['q']).astype(BF16))
            d['s_t'] = [_dot_nt(d['kb'], jnp.where(lane_w == hh, d['qb'], jnp.zeros_like(d['qb'])))
                        for hh in range(H)]
        for d in st:
            d['r'] = []
            for hh in range(H):
                v_aug = jnp.concatenate([d['vt'][hh * Dh:(hh + 1) * Dh], ones_rows], axis=0).astype(BF16)
                d['r'].append(_dot(v_aug, (d['s_t'][hh] * d['w_intra'][hh]).astype(BF16)))

        for b, d in zip(bs, st):
            h_t = []
            for hh in range(H):
                ch = slice(hh * Dh, (hh + 1) * Dh)
                num = d['w_inter'][hh] * d['qc'][ch] + d['r'][hh][:Dh]
                den = d['w_inter'][hh] * d['qc'][W + hh * Dh:W + (hh + 1) * Dh] + d['r'][hh][Dh:]
                h_t.append(num / jnp.maximum(jnp.abs(den), d['e_mt'][hh]))
            hout = jnp.transpose(jnp.concatenate(h_t, axis=0))
            d['y'] = _sigmoid(o_ref[b]) * hout
            d['vw'] = jnp.concatenate(
                [d['vt'][hh * Dh:(hh + 1) * Dh] * d['w_k'][hh] for hh in range(H)]
                + [jnp.broadcast_to(d['w_k'][hh], (Dh, L)) for hh in range(H)], axis=0).astype(BF16)
        for d in st:
            d['ms'] = _dot_split(d['y'] * d['y'], hm_ref[...])
            d['upd'] = _dot(d['vw'], d['kb'])

        for b, d in zip(bs, st):
            y_ref[b] = (d['y'] * lax.rsqrt(d['ms'] + EPS) * gain_ref[...]).astype(BF16)
            decay = d['dec'][H - 1]
            for hh in range(H - 2, -1, -1):
                decay = jnp.where(lane_w == hh, d['dec'][hh], decay)
            c_sc[b] = decay * d['c_t'] + jnp.where(bd_mask, d['upd'], 0.0)
            for hh in range(H):
                m_sc[b, hh:hh + 1, :] = jnp.broadcast_to(d['m_new'][hh], (1, LANE))
        return 0

    lax.fori_loop(0, B // MLSTM_ROWS, per_group, 0)


def _mlstm(mq, mk, mv, mo, grow, cwq, cwk, brow, hm, gain, B, S):
    L, W = MLSTM_CHUNK, MLSTM_WIDTH
    seq = lambda c: (0, c, 0)
    full = lambda c: (0, 0)
    return pl.pallas_call(
        functools.partial(_mlstm_kernel, B=B),
        grid=(S // L,),
        in_specs=[
            pl.BlockSpec((B, L, W), seq),
            pl.BlockSpec((B, L, W), seq),
            pl.BlockSpec((B, L, W), seq),
            pl.BlockSpec((B, L, W), seq),
            pl.BlockSpec((None, B, SUBLANE, L), lambda c: (0, 0, 0, c)),
            pl.BlockSpec((MLSTM_CONV, W), full),
            pl.BlockSpec((MLSTM_CONV, W), full),
            pl.BlockSpec((SUBLANE, 1), full),
            pl.BlockSpec((W, W), full),
            pl.BlockSpec((1, W), full),
        ],
        out_specs=pl.BlockSpec((B, L, W), seq),
        out_shape=jax.ShapeDtypeStruct((B, S, W), BF16),
        scratch_shapes=[
            pltpu.VMEM((B, SUBLANE, W), F32),
            pltpu.VMEM((B, SUBLANE, W), F32),
            pltpu.VMEM((B, 2 * W, W), F32),
            pltpu.VMEM((B, SUBLANE, LANE), F32),
        ],
        compiler_params=pltpu.CompilerParams(
            dimension_semantics=("arbitrary",), vmem_limit_bytes=VMEM_LIMIT),
        name="mlstm",
    )(mq, mk, mv, mo, grow, cwq, cwk, brow, hm, gain)


def _compress_kernel(c_ref, w1ab_ref, w1_ref, pe_ref, w2_ref, w2t_ref, o_ref, ot_ref, ch_sc):
    G, Dh = NSA_KV_GROUPS, NSA_HEAD_DIM
    rows = ch_sc.shape[0]
    n = rows // G
    for r in range(CMP_STRIDE):
        tok = c_ref[pl.ds(r, n, stride=CMP_STRIDE), :]
        for g in range(G):
            ch_sc[g * n:(g + 1) * n, r * Dh:(r + 1) * Dh] = tok[:, g * Dh:(g + 1) * Dh]
    ab = _dot(ch_sc[...].astype(BF16), w1ab_ref[...])
    const = _dot(pe_ref[...], w1_ref[...], precision=HIGHEST)
    hid = ab[:, :CMP_HIDDEN] + pltpu.roll(ab[:, CMP_HIDDEN:], rows - 1, 0) + const
    act = _gelu_tanh(hid).astype(BF16)
    o_ref[...] = _dot(act, w2_ref[...]).astype(BF16)
    ot_ref[...] = _dot_nt(w2t_ref[...], act).astype(BF16)


def _compress(ckv, w1ab, w1, pe, w2, w2t, layer, B, S):
    G, Dh = NSA_KV_GROUPS, NSA_HEAD_DIM
    n = S // CMP_STRIDE
    width = CMP_STRIDE * Dh
    wsel = lambda i, b: (layer, i, 0, 0)
    return pl.pallas_call(
        _compress_kernel,
        grid=(2, B),
        in_specs=[
            pl.BlockSpec((None, S, G * Dh), lambda i, b: (i, b, 0)),
            pl.BlockSpec((None, None, width, 2 * CMP_HIDDEN), wsel),
            pl.BlockSpec((None, None, 2 * width, CMP_HIDDEN), wsel),
            pl.BlockSpec((None, None, 1, 2 * width), wsel),
            pl.BlockSpec((None, None, CMP_HIDDEN, Dh), wsel),
            pl.BlockSpec((None, None, Dh, CMP_HIDDEN), wsel),
        ],
        out_specs=[pl.BlockSpec((None, None, G * n, Dh), lambda i, b: (i, b, 0, 0)),
                   pl.BlockSpec((None, None, Dh, G * n), lambda i, b: (i, b, 0, 0))],
        out_shape=[jax.ShapeDtypeStruct((2, B, G * n, Dh), BF16),
                   jax.ShapeDtypeStruct((2, B, Dh, G * n), BF16)],
        scratch_shapes=[pltpu.VMEM((G * n, width), F32)],
        compiler_params=pltpu.CompilerParams(
            dimension_semantics=("parallel", "parallel"), vmem_limit_bytes=VMEM_LIMIT),
        name="compress",
    )(ckv, w1ab, w1, pe, w2, w2t)


def _nsa_kernel(q_ref, kc_ref, vct_ref, ks_ref, vst_ref, kw_ref, vwt_ref, gtt_ref, gain_ref,
                ovt_ref, et_ref, o_ref, *, n_sel, n_top, ck, unroll):
    TQ, R, Dh, G = Q_BLOCK, NSA_REP, NSA_HEAD_DIM, NSA_KV_GROUPS
    groups = range(G)
    i = pl.program_id(1)
    t0 = i * TQ
    qs = [q_ref[g * R:(g + 1) * R].reshape(R * TQ, Dh) for g in groups]
    tq1 = t0 + lax.broadcasted_iota(jnp.int32, (1, TQ), 1)
    heads = lambda t: jnp.concatenate([t] * R, axis=1)

    ncmp = kc_ref.shape[0] // G
    nwb = WINDOW // TQ + 1
    wb0 = jnp.maximum(i - WINDOW // TQ, 0)
    ws = pl.multiple_of(wb0 * TQ, TQ)
    sc = [_dot_nt(kc_ref[g * ncmp:(g + 1) * ncmp, :], qs[g]) for g in groups]
    sw = [_dot_nt(kw_ref[g, pl.ds(ws, nwb * TQ), :], qs[g]) for g in groups]

    kpos = ws + lax.broadcasted_iota(jnp.int32, (nwb * TQ, 1), 0)
    wbias = heads(jnp.where((kpos <= tq1) & (tq1 - kpos < WINDOW), 0.0, NEG_INF))
    ow, l_w = [], []
    for g in groups:
        swb = sw[g] + wbias
        pw = jnp.exp2(swb - jnp.max(swb, axis=0, keepdims=True))
        vwt = jnp.concatenate([vwt_ref[g, wb0 + j] for j in range(nwb)], axis=1)
        owl = _dot(vwt, pw.astype(BF16))
        ow.append(owl[:Dh])
        l_w.append(owl[Dh:Dh + 1])

    cend = lax.broadcasted_iota(jnp.int32, (ncmp, 1), 0) * CMP_STRIDE + (CMP_BLOCK - 1)
    cmask = heads(cend <= tq1)
    pc = []
    for g in groups:
        scm = jnp.where(cmask, sc[g], NEG_INF)
        ec = jnp.where(cmask, jnp.exp2(scm - jnp.max(scm, axis=0, keepdims=True)), 0.0)
        pc.append(ec * (1.0 / jnp.maximum(jnp.sum(ec, axis=0, keepdims=True), 1e-30)))
    oc = [_dot(vct_ref[:, g * ncmp:(g + 1) * ncmp], pc[g].astype(BF16)) for g in groups]

    imp = []
    for g in groups:
        psum = pc[g][:, 0:TQ]
        for r in range(1, R):
            psum = psum + pc[g][:, r * TQ:(r + 1) * TQ]
        imp.append(_dot(ovt_ref[...], psum, precision=HIGHEST))

    blk = lax.broadcasted_iota(jnp.int32, (n_sel, 1), 0)
    valid = blk * SEL_BLOCK <= tq1
    forced = (blk == 0) | (blk == tq1 // SEL_BLOCK)
    selb = []
    for g in groups:
        val = jnp.where(forced, FORCE_SCORE, jnp.where(valid, imp[g], -FORCE_SCORE))
        rank = jnp.zeros((n_sel, TQ), F32)
        for jp in range(n_sel):
            other = val[jp:jp + 1, :]
            wins = jnp.where(blk > jp, jnp.where(other >= val, 1.0, 0.0), jnp.where(other > val, 1.0, 0.0))
            rank = rank + wins
        selb.append(jnp.where(rank < n_top, 0.0, NEG_INF).astype(BF16))

    def scores(g, c, causal):
        k0 = c * ck
        bias = _dot(et_ref[c], selb[g])
        if causal:
            kpos = k0 + lax.broadcasted_iota(jnp.int32, (ck, 1), 0)
            bias = jnp.where(kpos <= tq1, bias, NEG_INF)
        return _dot_nt(ks_ref[g, pl.ds(k0, ck), :], qs[g]) + heads(bias)

    def update(g, c, s, carry):
        m, acc = carry
        mn = jnp.maximum(m, jnp.max(s, axis=0, keepdims=True))
        p = jnp.exp2(s - mn)
        acc = jnp.exp2(m - mn) * acc + _dot(vst_ref[g, c], p.astype(BF16))
        return mn, acc

    def chunk_group(cg, carry, causal):
        cs = [cg * unroll + sub for sub in range(unroll)]
        ss = [[scores(g, c, causal) for g in groups] for c in cs]
        carry = list(carry)
        for c, s in zip(cs, ss):
            for g in groups:
                carry[g] = update(g, c, s[g], carry[g])
        return tuple(carry)

    n_chunks = (t0 + TQ + ck - 1) // ck
    init = tuple((jnp.full((1, R * TQ), NEG_INF, F32), jnp.zeros((VAL_ROWS, R * TQ), F32)) for _ in groups)
    def unrolled(trips):
        def run():
            carry = init
            for cg in range(trips):
                carry = chunk_group(cg, carry, causal=cg == trips - 1)
            return carry
        return run

    max_trips = ks_ref.shape[1] // (ck * unroll)
    sel = lax.switch((n_chunks + unroll - 1) // unroll - 1, [unrolled(t) for t in range(1, max_trips + 1)])

    normed = []
    for g in groups:
        gs = _sigmoid(gtt_ref[g])
        acc_s, l_s = sel[g][1][:Dh], sel[g][1][Dh:Dh + 1]
        for r in range(R):
            ln = slice(r * TQ, (r + 1) * TQ)
            o = (gs[3 * r:3 * r + 1, :] * oc[g][:, ln]
                 + (gs[3 * r + 1:3 * r + 2, :] / l_s[:, ln]) * acc_s[:, ln]
                 + (gs[3 * r + 2:3 * r + 3, :] / l_w[g][:, ln]) * ow[g][:, ln])
            ms = jnp.mean(o * o, axis=0, keepdims=True)
            normed.append(o * lax.rsqrt(ms + EPS) * gain_ref[g * R + r])
    for pair in range(G * R // 2):
        both = jnp.concatenate(normed[2 * pair:2 * pair + 2], axis=0)
        o_ref[:, pair * 2 * Dh:(pair + 1) * 2 * Dh] = jnp.transpose(both).astype(BF16)


def _nsa(aq, cmp_k, cmp_vt, ks, vst, kw, vwt, gates_t, gain, consts, B, S):
    G, H, TQ, Dh = NSA_KV_GROUPS, NSA_HEADS, Q_BLOCK, NSA_HEAD_DIM
    nq = S // TQ
    ncmp = S // CMP_STRIDE
    n_sel = S // SEL_BLOCK
    ovt, emat_t, ck = consts
    k_spec = pl.BlockSpec((None, G, S, Dh), lambda b, i: (b, 0, 0, 0))
    return pl.pallas_call(
        functools.partial(_nsa_kernel, n_sel=n_sel, n_top=min(SEL_TOPN, n_sel), ck=ck, unroll=SEL_UNROLL),
        grid=(B, nq),
        in_specs=[
            pl.BlockSpec((None, H, TQ, Dh), lambda b, i: (b, 0, i, 0)),
            pl.BlockSpec((None, None, G * ncmp, Dh), lambda b, i: (0, b, 0, 0)),
            pl.BlockSpec((None, None, Dh, G * ncmp), lambda b, i: (1, b, 0, 0)),
            k_spec,
            pl.BlockSpec((None, G, S // ck, VAL_ROWS, ck), lambda b, i: (b, 0, 0, 0, 0)),
            k_spec,
            pl.BlockSpec((None, G, S // TQ, VAL_ROWS, TQ), lambda b, i: (b, 0, 0, 0, 0)),
            pl.BlockSpec((G, None, 2 * SUBLANE, TQ), lambda b, i: (0, b, GATE_COL // (2 * SUBLANE), i)),
            pl.BlockSpec((H, Dh, 1), lambda b, i: (0, 0, 0)),
            pl.BlockSpec(ovt.shape, lambda b, i: (0, 0)),
            pl.BlockSpec(emat_t.shape, lambda b, i: (0, 0, 0)),
        ],
        out_specs=pl.BlockSpec((TQ, H * Dh), lambda b, i: (b * nq + i, 0)),
        out_shape=jax.ShapeDtypeStruct((B * S, H * Dh), BF16),
        compiler_params=pltpu.CompilerParams(
            dimension_semantics=("parallel", "arbitrary"), vmem_limit_bytes=VMEM_LIMIT),
        name="nsa",
    )(aq, cmp_k, cmp_vt, ks, vst, kw, vwt, gates_t, gain, ovt, emat_t)


def _nsa_consts(S):
    n_cmp = S // CMP_STRIDE
    n_sel = S // SEL_BLOCK
    ck = 256
    i = np.arange(n_cmp)[:, None]
    j = np.arange(n_sel)[None, :]
    lo = np.maximum(i * CMP_STRIDE, j * SEL_BLOCK)
    hi = np.minimum(i * CMP_STRIDE + CMP_BLOCK, (j + 1) * SEL_BLOCK)
    ov = np.maximum(hi - lo, 0) / CMP_STRIDE
    ov[n_cmp - 1] = 0.0
    key = np.arange(S)
    emat_t = (key[:, None] // SEL_BLOCK == np.arange(n_sel)[None, :]).astype(np.float32)
    return (jnp.asarray(ov.T, F32), jnp.asarray(emat_t.reshape(S // ck, ck, n_sel), BF16), ck)


def _outproj_kernel(h_ref, ys_ref, ym_ref, yn_ref, w_ref, g_ref, o_ref):
    acc = _dot(ys_ref[...], w_ref[0:SSM_WIDTH, :])
    acc = acc + _dot(ym_ref[...], w_ref[SSM_WIDTH:SSM_WIDTH + MLSTM_WIDTH, :])
    acc = acc + _dot(yn_ref[...], w_ref[SSM_WIDTH + MLSTM_WIDTH:, :])
    ms = jnp.mean(acc * acc, axis=-1, keepdims=True)
    o_ref[...] = h_ref[...] + acc * lax.rsqrt(ms + EPS) * g_ref[...]


def _outproj(h2, y_ssm, y_mls, y_nsa, w, layer, gain, B, S, ts):
    nt = S // ts
    row = lambda b, i: (b * nt + i, 0)
    full = lambda b, i: (0, 0)
    return pl.pallas_call(
        _outproj_kernel,
        grid=(B, nt),
        in_specs=[
            pl.BlockSpec((ts, D_MODEL), row),
            pl.BlockSpec((ts, SSM_WIDTH), row),
            pl.BlockSpec((ts, MLSTM_WIDTH), row),
            pl.BlockSpec((ts, NSA_WIDTH), row),
            pl.BlockSpec((None, D_MODEL, D_MODEL), lambda b, i: (layer, 0, 0)),
            pl.BlockSpec((1, D_MODEL), full),
        ],
        out_specs=pl.BlockSpec((ts, D_MODEL), row),
        out_shape=jax.ShapeDtypeStruct((B * S, D_MODEL), F32),
        compiler_params=pltpu.CompilerParams(
            dimension_semantics=("parallel", "parallel"), vmem_limit_bytes=VMEM_LIMIT),
        name="outproj",
    )(h2, y_ssm, y_mls, y_nsa, w, gain)


def _mlp_kernel(h_ref, g1_ref, w1_ref, w2_ref, g2_ref, o_ref, u_sc, acc_sc):
    kf = pl.program_id(1)

    @pl.when(kf == 0)
    def _():
        x = h_ref[...]
        ms = jnp.mean(x * x, axis=-1, keepdims=True)
        u_sc[...] = (x * lax.rsqrt(ms + EPS) * g1_ref[...]).astype(BF16)
        acc_sc[...] = jnp.zeros_like(acc_sc)

    a = jnp.maximum(_dot(u_sc[...], w1_ref[...].astype(BF16)), 0.0)
    acc_sc[...] += _dot((a * a).astype(BF16), w2_ref[...].astype(BF16))

    @pl.when(kf == pl.num_programs(1) - 1)
    def _():
        f = acc_sc[...]
        ms = jnp.mean(f * f, axis=-1, keepdims=True)
        o_ref[...] = h_ref[...] + f * lax.rsqrt(ms + EPS) * g2_ref[...]


def _mlp(h2, g1, w1, w2, layer, g2, tm, tf):
    rows = h2.shape[0]
    return pl.pallas_call(
        _mlp_kernel,
        grid=(rows // tm, D_FF // tf),
        in_specs=[
            pl.BlockSpec((tm, D_MODEL), lambda i, k: (i, 0)),
            pl.BlockSpec((1, D_MODEL), lambda i, k: (0, 0)),
            pl.BlockSpec((None, D_MODEL, tf), lambda i, k: (layer, 0, k)),
            pl.BlockSpec((None, tf, D_MODEL), lambda i, k: (layer, k, 0)),
            pl.BlockSpec((1, D_MODEL), lambda i, k: (0, 0)),
        ],
        out_specs=pl.BlockSpec((tm, D_MODEL), lambda i, k: (i, 0)),
        out_shape=jax.ShapeDtypeStruct((rows, D_MODEL), F32),
        scratch_shapes=[pltpu.VMEM((tm, D_MODEL), BF16), pltpu.VMEM((tm, D_MODEL), F32)],
        compiler_params=pltpu.CompilerParams(
            dimension_semantics=("parallel", "arbitrary"), vmem_limit_bytes=VMEM_LIMIT),
        name="mlp",
    )(h2, g1, w1, w2, g2)


def _inproj_pieces():
    return ((0, 1280), (1288, 1800), (1800, 1928), (2056, 2184), (2312, 2440),
            (1928, 2056), (2184, 2312), (2440, 2568),
            (1280, 1288), (None, GATE_COL - 8), (2568, 2580), (None, LANE - GATE_COL - 12),
            (None, GATE_COL), (2580, 2592), (None, LANE - GATE_COL - 12))


def _permute_w_in_kernel(w_ref, o_ref):
    x = w_ref[...]
    col = 0
    for a, b in _inproj_pieces():
        width = b if a is None else b - a
        piece = jnp.zeros((x.shape[0], width), BF16) if a is None else x[:, a:b].astype(BF16)
        o_ref[:, col:col + width] = piece
        col += width
    assert col == D_INP


def _permute_w_in(w_in):
    depth, rows, cols = w_in.shape
    tr = 256
    return pl.pallas_call(
        _permute_w_in_kernel,
        grid=(depth, rows // tr),
        in_specs=[pl.BlockSpec((None, tr, cols), lambda l, i: (l, i, 0))],
        out_specs=pl.BlockSpec((None, tr, D_INP), lambda l, i: (l, i, 0)),
        out_shape=jax.ShapeDtypeStruct((depth, rows, D_INP), BF16),
        compiler_params=pltpu.CompilerParams(
            dimension_semantics=("parallel", "parallel"), vmem_limit_bytes=VMEM_LIMIT),
        name="permute_w_in",
    )(w_in)


def _rope_tables(positions):
    inv = ROPE_THETA ** (-jnp.arange(0, ROPE_DIMS, 2, dtype=F32) / ROPE_DIMS)
    ang = positions.astype(F32)[..., None] * inv
    cos, sin = jnp.cos(ang), jnp.sin(ang)
    z = jnp.zeros_like(cos)
    rest = NSA_HEAD_DIM - ROPE_DIMS
    pad_one = jnp.ones(cos.shape[:-1] + (rest,), F32)
    pad_zero = jnp.zeros(cos.shape[:-1] + (rest,), F32)
    rc = jnp.concatenate([cos, cos, pad_one], axis=-1)
    rs1 = jnp.concatenate([-sin, z, pad_zero], axis=-1)
    rs2 = jnp.concatenate([z, sin, pad_zero], axis=-1)
    tile = lambda t: jnp.tile(t, (1, 1, LANE // NSA_HEAD_DIM)).reshape(-1, LANE)
    return tile(rc), tile(rs1), tile(rs2)


def _s5_params(lam_re, lam_im, b_re, b_im, c_re, c_im, log_dt):
    G, P, Hc = SSM_GROUPS, SSM_STATE, SSM_GROUP
    dt = jnp.exp(log_dt)[:, None]
    mag = jnp.exp(lam_re * dt)
    ang = lam_im * dt
    ab_re = mag * jnp.cos(ang)
    ab_im = mag * jnp.sin(ang)
    den = lam_re * lam_re + lam_im * lam_im
    g_re = ((ab_re - 1.0) * lam_re + ab_im * lam_im) / den
    g_im = (ab_im * lam_re - (ab_re - 1.0) * lam_im) / den
    bb_re = g_re[..., None] * b_re - g_im[..., None] * b_im
    bb_im = g_re[..., None] * b_im + g_im[..., None] * b_re
    eye = jnp.eye(G, dtype=F32)
    blockdiag_in = lambda t: jnp.einsum('gph,gk->ghkp', t, eye).reshape(G * Hc, G * P)
    blockdiag_out = lambda t: jnp.einsum('ghp,gk->gpkh', t, eye).reshape(G * P, G * Hc)
    bb = jnp.concatenate([blockdiag_in(bb_re), blockdiag_in(bb_im)], axis=1).astype(BF16)
    cc = jnp.concatenate([blockdiag_out(c_re), -blockdiag_out(c_im)], axis=0).astype(BF16)
    a = jnp.stack([ab_re.reshape(-1), ab_im.reshape(-1)], axis=0)
    return bb, a, cc


def _group_mean_matrix(width, group):
    idx = np.arange(width) // group
    return jnp.asarray((idx[:, None] == idx[None, :]).astype(np.float32) / group, BF16)


def kernel(x, positions, ln_mix_pre, ln_mix_post, ln_mlp_pre, ln_mlp_post, w_in, w_out, ssm_lambda_re, ssm_lambda_im, ssm_b_re, ssm_b_im, ssm_c_re, ssm_c_im, ssm_d, ssm_log_dt, ssm_w_glu, mlstm_conv, mlstm_b_i, mlstm_b_f, cmp_pe_k, cmp_w1_k, cmp_w2_k, cmp_pe_v, cmp_w1_v, cmp_w2_v, gn_ssm, gn_mlstm, gn_nsa, mlp_w1, mlp_w2):
    B, S, D = x.shape
    depth = w_in.shape[0]
    assert D == D_MODEL and B == SUBLANE and S % 512 == 0 and S >= WINDOW + Q_BLOCK
    G, H = NSA_KV_GROUPS, MLSTM_HEADS
    ts_proj = 512
    ts_scan = 128

    rc, rs1, rs2 = _rope_tables(positions)
    w_in_p = _permute_w_in(w_in)
    w_out_b = w_out.astype(BF16)
    wglu_b = ssm_w_glu.astype(BF16)
    gm_ssm = _group_mean_matrix(SSM_WIDTH, SSM_GROUP)
    hm_mls = _group_mean_matrix(MLSTM_WIDTH, MLSTM_HEAD_DIM)
    consts = _nsa_consts(S)
    half = CMP_STRIDE * NSA_HEAD_DIM
    w1ab = jnp.stack([jnp.concatenate([cmp_w1_k[:, :half], cmp_w1_k[:, half:]], axis=-1),
                      jnp.concatenate([cmp_w1_v[:, :half], cmp_w1_v[:, half:]], axis=-1)], axis=1).astype(BF16)
    w1f = jnp.stack([cmp_w1_k, cmp_w1_v], axis=1)
    pef = jnp.stack([cmp_pe_k.reshape(depth, 1, -1), cmp_pe_v.reshape(depth, 1, -1)], axis=1)
    w2c = jnp.stack([cmp_w2_k, cmp_w2_v], axis=1).astype(BF16)
    w2ct = jnp.swapaxes(w2c, -1, -2)
    bias_row = jnp.concatenate([mlstm_b_i, mlstm_b_f], axis=-1)[:, :, None]

    bb, a, cc = jax.vmap(_s5_params)(ssm_lambda_re, ssm_lambda_im, ssm_b_re, ssm_b_im, ssm_c_re, ssm_c_im,
                                     ssm_log_dt)
    sh3 = lambda t: t.reshape(B, S, t.shape[-1])

    h = x.reshape(B * S, D)
    for l in range(depth):
        (su, mq, mk, mv, mo, aq, ckv, sk, wk, svt, wvt, gates, gates_t) = _inproj(
            h, ln_mix_pre[l][None], w_in_p, l, rc, rs1, rs2, B, S, ts_proj, consts[2])

        y_ssm = _s5(sh3(su), bb, a, cc, ssm_d[l][None], wglu_b, l, gm_ssm, gn_ssm[l][None], B, S, ts_scan)

        y_mls = _mlstm(sh3(mq), sh3(mk), sh3(mv), sh3(mo), gates_t,
                       mlstm_conv[l][:, :MLSTM_WIDTH], mlstm_conv[l][:, MLSTM_WIDTH:],
                       bias_row[l], hm_mls, gn_mlstm[l][None], B, S)

        cmp_k, cmp_t = _compress(ckv, w1ab, w1f, pef, w2c, w2ct, l, B, S)
        y_nsa = _nsa(aq, cmp_k, cmp_t, sk, svt, wk, wvt, gates_t,
                     gn_nsa[l].reshape(NSA_HEADS, NSA_HEAD_DIM, 1), consts, B, S)

        h = _outproj(h, y_ssm.reshape(B * S, SSM_WIDTH), y_mls.reshape(B * S, MLSTM_WIDTH), y_nsa,
                     w_out_b, l, ln_mix_post[l][None], B, S, ts_proj)
        h = _mlp(h, ln_mlp_pre[l][None], mlp_w1, mlp_w2, l, ln_mlp_post[l][None], 1024, 1024)
    return h.reshape(B, S, D)
```

```python
import functools
import math

import numpy as np
import jax
import jax.numpy as jnp
from jax import lax
from jax.experimental import pallas as pl
from jax.experimental.pallas import tpu as pltpu

F32 = jnp.float32
BF16 = jnp.bfloat16
HIGHEST = lax.Precision.HIGHEST

D_MODEL = 1024
DEPTH = 4
SSM_WIDTH = 256
SSM_GROUP = 16
SSM_GROUPS = 16
SSM_STATE = 64
SSM_LANES = SSM_GROUPS * SSM_STATE
MLSTM_WIDTH = 256
MLSTM_HEADS = 4
MLSTM_HEAD_DIM = 64
MLSTM_CHUNK = 128
MLSTM_CONV = 4
NSA_WIDTH = 512
NSA_HEAD_DIM = 64
NSA_HEADS = 8
NSA_KV_GROUPS = 2
NSA_REP = NSA_HEADS // NSA_KV_GROUPS
NSA_KV_WIDTH = 128
CMP_BLOCK = 32
CMP_STRIDE = 16
CMP_HIDDEN = 256
SEL_BLOCK = 64
SEL_TOPN = 8
WINDOW = 256
Q_BLOCK = 128
FORCE_SCORE = 1e4
NEG_INF = -1e30
ROPE_THETA = 500000.0
ROPE_DIMS = 16
ROPE_HALF = 8
D_FF = 4096
EPS = 1e-6
D_IN = 2592

LANE = 128
SUBLANE = 8
VMEM_LIMIT = 56 * 1024 * 1024

C_SU, C_MQ, C_MK, C_MV, C_MO = 0, 256, 512, 768, 1024
C_AQ, C_CK, C_SK, C_WK = 1280, 1792, 1920, 2048
C_CV, C_SV, C_WV = 2176, 2304, 2432
C_G0, C_G1 = 2560, 2688
D_INP = 2816
GATE_COL = 16
VAL_ROWS = NSA_HEAD_DIM + 16
Q_SCALE = NSA_HEAD_DIM ** -0.5 * math.log2(math.e)
MLSTM_ROWS = 4
NSA_TQ = 256
SEL_UNROLL = 2


def _dot(a, b, precision=None):
    return jnp.dot(a, b, preferred_element_type=F32, precision=precision)


def _dot_nt(a, b):
    return lax.dot_general(a, b, (((1,), (1,)), ((), ())), preferred_element_type=F32)


def _dot_tn(a, b):
    return lax.dot_general(a, b, (((0,), (0,)), ((), ())), preferred_element_type=F32)


def _sigmoid(x):
    return 1.0 / (1.0 + jnp.exp(-x))


def _dot_split(x, w_bf16):
    hi = x.astype(BF16)
    lo = (x - hi.astype(F32)).astype(BF16)
    return _dot(hi, w_bf16) + _dot(lo, w_bf16)


def _gelu_tanh(x):
    return 0.5 * x * (1.0 + jnp.tanh(math.sqrt(2.0 / math.pi) * (x + 0.044715 * (x * x * x))))


def _log_sigmoid(x):
    return jnp.minimum(x, 0.0) - jnp.log(1.0 + jnp.exp(-jnp.abs(x)))


def _inproj_kernel(x_ref, g_ref, w_ref, rc_ref, rs1_ref, rs2_ref,
                   su_ref, mq_ref, mk_ref, mv_ref, mo_ref, aq_ref, ckv_ref, sk_ref, wk_ref,
                   svt_ref, wvt_ref, gt_ref, gtt_ref):
    x = x_ref[...]
    ms = jnp.mean(x * x, axis=-1, keepdims=True)
    u = (x * lax.rsqrt(ms + EPS) * g_ref[...]).astype(BF16)
    rc, rs1, rs2 = rc_ref[...], rs1_ref[...], rs2_ref[...]

    def mm(c0, width):
        return _dot(u, w_ref[:, c0:c0 + width])

    def rope(z):
        return z * rc + pltpu.roll(z, LANE - ROPE_HALF, 1) * rs1 + pltpu.roll(z, ROPE_HALF, 1) * rs2

    su_ref[...] = mm(C_SU, 256)
    mq_ref[...] = mm(C_MQ, 256)
    mk_ref[...] = mm(C_MK, 256)
    mv_ref[...] = mm(C_MV, 256).astype(BF16)
    mo_ref[...] = mm(C_MO, 256)
    def mm_pair(c0):
        z = mm(c0, 2 * LANE)
        return z[:, :LANE], z[:, LANE:]

    def put_heads(ref, first, z):
        ref[first] = z[:, :NSA_HEAD_DIM].astype(BF16)
        ref[first + 1] = z[:, NSA_HEAD_DIM:].astype(BF16)

    for j in range(NSA_HEADS // 4):
        for k, z in enumerate(mm_pair(C_AQ + 2 * LANE * j)):
            put_heads(aq_ref, 4 * j + 2 * k, rope(z) * Q_SCALE)
    z_ck, z_sk = mm_pair(C_CK)
    z_wk, z_cv = mm_pair(C_WK)
    z_sv, z_wv = mm_pair(C_SV)
    ckv_ref[0] = rope(z_ck)
    ckv_ref[1] = z_cv
    put_heads(sk_ref, 0, rope(z_sk))
    put_heads(wk_ref, 0, rope(z_wk))

    def put_chunks_t(ref, z):
        zt = jnp.transpose(z)
        width = ref.shape[-1]
        ones = jnp.ones((VAL_ROWS - NSA_HEAD_DIM, width), BF16)
        for g in range(NSA_KV_GROUPS):
            for j in range(ref.shape[1]):
                ref[g, j, :NSA_HEAD_DIM] = zt[g * NSA_HEAD_DIM:(g + 1) * NSA_HEAD_DIM,
                                              j * width:(j + 1) * width].astype(BF16)
                ref[g, j, NSA_HEAD_DIM:] = ones

    put_chunks_t(svt_ref, z_sv)
    put_chunks_t(wvt_ref, z_wv)
    z_g0, z_g1 = mm_pair(C_G0)
    gt_ref[...] = z_g0
    gtt_ref[0] = jnp.transpose(z_g0)
    gtt_ref[1] = jnp.transpose(z_g1)


def _inproj(h2, gain, w, layer, rc, rs1, rs2, B, S, ts, ck):
    nt = S // ts
    BS = B * S
    row = lambda b, i: (b * nt + i, 0)
    full = lambda b, i: (0, 0)
    headed = lambda b, i: (b, 0, i, 0)
    paired = lambda b, i: (0, b * nt + i, 0)
    in_specs = [
        pl.BlockSpec((ts, D_MODEL), row),
        pl.BlockSpec((1, D_MODEL), full),
        pl.BlockSpec((None, D_MODEL, D_INP), lambda b, i: (layer, 0, 0)),
        pl.BlockSpec((ts, LANE), row),
        pl.BlockSpec((ts, LANE), row),
        pl.BlockSpec((ts, LANE), row),
    ]
    kv_shape = jax.ShapeDtypeStruct((B, NSA_KV_GROUPS, S, NSA_HEAD_DIM), BF16)
    kv_spec = pl.BlockSpec((None, NSA_KV_GROUPS, ts, NSA_HEAD_DIM), headed)
    out_shape = [
        jax.ShapeDtypeStruct((BS, SSM_WIDTH), F32),
        jax.ShapeDtypeStruct((BS, MLSTM_WIDTH), F32),
        jax.ShapeDtypeStruct((BS, MLSTM_WIDTH), F32),
        jax.ShapeDtypeStruct((BS, MLSTM_WIDTH), BF16),
        jax.ShapeDtypeStruct((BS, MLSTM_WIDTH), F32),
        jax.ShapeDtypeStruct((B, NSA_HEADS, S, NSA_HEAD_DIM), BF16),
        jax.ShapeDtypeStruct((2, BS, NSA_KV_WIDTH), F32),
        kv_shape, kv_shape,
        jax.ShapeDtypeStruct((B, NSA_KV_GROUPS, S // ck, VAL_ROWS, ck), BF16),
        jax.ShapeDtypeStruct((B, NSA_KV_GROUPS, S // Q_BLOCK, VAL_ROWS, Q_BLOCK), BF16),
        jax.ShapeDtypeStruct((BS, LANE), F32),
        jax.ShapeDtypeStruct((NSA_KV_GROUPS, B, LANE, S), F32),
    ]
    out_specs = [
        pl.BlockSpec((ts, SSM_WIDTH), row),
        pl.BlockSpec((ts, MLSTM_WIDTH), row),
        pl.BlockSpec((ts, MLSTM_WIDTH), row),
        pl.BlockSpec((ts, MLSTM_WIDTH), row),
        pl.BlockSpec((ts, MLSTM_WIDTH), row),
        pl.BlockSpec((None, NSA_HEADS, ts, NSA_HEAD_DIM), headed),
        pl.BlockSpec((2, ts, NSA_KV_WIDTH), paired),
        kv_spec, kv_spec,
        pl.BlockSpec((None, NSA_KV_GROUPS, ts // ck, VAL_ROWS, ck), lambda b, i: (b, 0, i, 0, 0)),
        pl.BlockSpec((None, NSA_KV_GROUPS, ts // Q_BLOCK, VAL_ROWS, Q_BLOCK), lambda b, i: (b, 0, i, 0, 0)),
        pl.BlockSpec((ts, LANE), row),
        pl.BlockSpec((NSA_KV_GROUPS, None, LANE, ts), lambda b, i: (0, b, 0, i)),
    ]
    return pl.pallas_call(
        _inproj_kernel,
        grid=(B, nt),
        in_specs=in_specs,
        out_specs=out_specs,
        out_shape=out_shape,
        compiler_params=pltpu.CompilerParams(
            dimension_semantics=("parallel", "parallel"), vmem_limit_bytes=VMEM_LIMIT),
        name="inproj",
    )(h2, gain, w, rc, rs1, rs2)


def _s5_kernel(u_ref, bb_ref, a_ref, cc_ref, d_ref, wg_ref, gm_ref, gain_ref, o_ref, x_sc, st_sc, tm_sc, *, B, ts):
    @pl.when(pl.program_id(0) == 0)
    def _():
        st_sc[...] = jnp.zeros_like(st_sc)

    nl = SSM_WIDTH // LANE
    for b in range(B):
        for c in range(nl):
            tm_sc[c, pl.ds(b, ts, stride=B), :] = u_ref[b, :, c * LANE:(c + 1) * LANE]
    u = jnp.concatenate([tm_sc[c] for c in range(nl)], axis=1)
    ub = u.astype(BF16)
    for part in range(2):
        cols = slice(part * SSM_LANES, (part + 1) * SSM_LANES)
        x_sc[:, cols] = _dot(ub, bb_ref[:, cols])
    ar = jnp.broadcast_to(a_ref[0:1, :], (B, SSM_LANES))
    ai = jnp.broadcast_to(a_ref[1:2, :], (B, SSM_LANES))

    def step(t, carry):
        xr, xi = carry
        r = pl.multiple_of(t * B, B)
        br = x_sc[pl.ds(r, B), 0:SSM_LANES]
        bi = x_sc[pl.ds(r, B), SSM_LANES:2 * SSM_LANES]
        nr = ar * xr - ai * xi + br
        ni = ar * xi + ai * xr + bi
        x_sc[pl.ds(r, B), 0:SSM_LANES] = nr
        x_sc[pl.ds(r, B), SSM_LANES:2 * SSM_LANES] = ni
        return nr, ni

    xr, xi = lax.fori_loop(0, ts, step, (st_sc[0], st_sc[1]))
    st_sc[0] = xr
    st_sc[1] = xi

    half = (ts * B) // 2
    y = jnp.concatenate([_dot(x_sc[r * half:(r + 1) * half, :].astype(BF16), cc_ref[...]) for r in range(2)],
                        axis=0) + d_ref[...] * u
    y = _gelu_tanh(y)
    y = y * _sigmoid(_dot(y.astype(BF16), wg_ref[...]))
    ms = _dot_split(y * y, gm_ref[...])
    y = y * lax.rsqrt(ms + EPS) * gain_ref[...]
    for c in range(nl):
        tm_sc[c] = y[:, c * LANE:(c + 1) * LANE]
    for b in range(B):
        o_ref[b] = jnp.concatenate(
            [tm_sc[c, pl.ds(b, ts, stride=B), :] for c in range(nl)], axis=1).astype(BF16)


def _s5(u, bb, a, cc, d, wg, layer, gm, gain, B, S, ts):
    rows = ts * B
    full = lambda i: (0, 0)
    lsel = lambda i: (layer, 0, 0)
    return pl.pallas_call(
        functools.partial(_s5_kernel, B=B, ts=ts),
        grid=(S // ts,),
        in_specs=[
            pl.BlockSpec((B, ts, SSM_WIDTH), lambda i: (0, i, 0)),
            pl.BlockSpec((None, SSM_WIDTH, 2 * SSM_LANES), lsel),
            pl.BlockSpec((None, 2, SSM_LANES), lsel),
            pl.BlockSpec((None, 2 * SSM_LANES, SSM_WIDTH), lsel),
            pl.BlockSpec((1, SSM_WIDTH), full),
            pl.BlockSpec((None, SSM_WIDTH, SSM_WIDTH), lsel),
            pl.BlockSpec((SSM_WIDTH, SSM_WIDTH), full),
            pl.BlockSpec((1, SSM_WIDTH), full),
        ],
        out_specs=pl.BlockSpec((B, ts, SSM_WIDTH), lambda i: (0, i, 0)),
        out_shape=jax.ShapeDtypeStruct((B, S, SSM_WIDTH), BF16),
        scratch_shapes=[pltpu.VMEM((rows, 2 * SSM_LANES), F32), pltpu.VMEM((2, B, SSM_LANES), F32),
                        pltpu.VMEM((SSM_WIDTH // LANE, rows, LANE), F32)],
        compiler_params=pltpu.CompilerParams(
            dimension_semantics=("arbitrary",), vmem_limit_bytes=VMEM_LIMIT),
        name="s5",
    )(u, bb, a, cc, d, wg, gm, gain)


def _mlstm_kernel(q_ref, k_ref, v_ref, o_ref, gr_ref, cwq_ref, cwk_ref, br_ref, hm_ref,
                  gain_ref, y_ref, qt_sc, kt_sc, c_sc, m_sc, *, B):
    L, H, Dh, W = MLSTM_CHUNK, MLSTM_HEADS, MLSTM_HEAD_DIM, MLSTM_WIDTH

    @pl.when(pl.program_id(0) == 0)
    def _():
        qt_sc[...] = jnp.zeros_like(qt_sc)
        kt_sc[...] = jnp.zeros_like(kt_sc)
        c_sc[...] = jnp.zeros_like(c_sc)
        m_sc[...] = jnp.zeros_like(m_sc)

    visible = lax.broadcasted_iota(jnp.int32, (L, L), 0) <= lax.broadcasted_iota(jnp.int32, (L, L), 1)
    triu = visible.astype(F32)
    lane_w = lax.broadcasted_iota(jnp.int32, (1, W), 1) // Dh
    bd_mask = ((lax.broadcasted_iota(jnp.int32, (2 * W, W), 0) % W) // Dh
               == lax.broadcasted_iota(jnp.int32, (2 * W, W), 1) // Dh)
    row8 = lax.broadcasted_iota(jnp.int32, (SUBLANE, W), 0)
    cwq = cwq_ref[...]
    cwk = cwk_ref[...]
    ones_rows = jnp.ones((Dh, L), F32)

    def conv_silu(x, tail, w):
        acc = x * w[MLSTM_CONV - 1:MLSTM_CONV, :]
        for sft in range(1, MLSTM_CONV):
            xs = pltpu.roll(x, sft, 0)
            head = jnp.where(row8 < sft, pltpu.roll(tail, sft, 0), xs[:SUBLANE])
            xs = jnp.concatenate([head, xs[SUBLANE:]], axis=0)
            acc = acc + xs * w[MLSTM_CONV - 1 - sft:MLSTM_CONV - sft, :]
        return acc * _sigmoid(acc)

    def per_group(grp, _):
        bs = [grp * MLSTM_ROWS + n for n in range(MLSTM_ROWS)]
        st = [dict() for _ in bs]

        for b, d in zip(bs, st):
            q_raw = q_ref[b]
            k_raw = k_ref[b]
            d['q'] = conv_silu(q_raw, qt_sc[b], cwq)
            d['k'] = conv_silu(k_raw, kt_sc[b], cwk) * (Dh ** -0.5)
            qt_sc[b] = q_raw[L - SUBLANE:, :]
            kt_sc[b] = k_raw[L - SUBLANE:, :]
            d['gr'] = gr_ref[b] + br_ref[...]
        for d in st:
            d['brow'] = _dot(_log_sigmoid(d['gr']), triu, precision=HIGHEST)

        for b, d in zip(bs, st):
            gr, brow = d['gr'], d['brow']
            ccol = jnp.transpose(brow - pltpu.roll(gr, H, 0))
            m_all = m_sc[b]
            for key in ('w_intra', 'w_inter', 'e_mt', 'w_k', 'dec', 'm_new'):
                d[key] = []
            for hh in range(H):
                b_r = brow[H + hh:H + hh + 1, :]
                i_r = gr[hh:hh + 1, :]
                m_prev = m_all[hh:hh + 1, 0:1]
                dm = jnp.where(visible, b_r - ccol[:, H + hh:H + hh + 1], NEG_INF)
                inter = b_r + m_prev
                mt = jnp.maximum(inter, jnp.max(dm, axis=0, keepdims=True))
                d['w_intra'].append(jnp.exp(dm - mt))
                d['w_inter'].append(jnp.exp(inter - mt))
                d['e_mt'].append(jnp.exp(-mt))
                b_last = b_r[:, L - 1:L]
                logw = b_last - b_r + i_r
                mn = jnp.maximum(b_last + m_prev, jnp.max(logw, axis=1, keepdims=True))
                d['w_k'].append(jnp.exp(logw - mn))
                d['dec'].append(jnp.exp(b_last + m_prev - mn))
                d['m_new'].append(mn)
            d['qb'] = d['q'].astype(BF16)
            d['kb'] = d['k'].astype(BF16)
            d['vt'] = jnp.transpose(v_ref[b].astype(F32))
            d['c_t'] = c_sc[b]

        for d in st:
            d['qc'] = _dot(d['c_t'].astype(BF16), jnp.transpose(d['q']).astype(BF16))
            d['s_t'] = [_dot_nt(d['kb'], jnp.where(lane_w == hh, d['qb'], jnp.zeros_like(d['qb'])))
                        for hh in range(H)]
        for d in st:
            d['r'] = []
            for hh in range(H):
                v_aug = jnp.concatenate([d['vt'][hh * Dh:(hh + 1) * Dh], ones_rows], axis=0).astype(BF16)
                d['r'].append(_dot(v_aug, (d['s_t'][hh] * d['w_intra'][hh]).astype(BF16)))

        for b, d in zip(bs, st):
            h_t = []
            for hh in range(H):
                ch = slice(hh * Dh, (hh + 1) * Dh)
                num = d['w_inter'][hh] * d['qc'][ch] + d['r'][hh][:Dh]
                den = d['w_inter'][hh] * d['qc'][W + hh * Dh:W + (hh + 1) * Dh] + d['r'][hh][Dh:]
                h_t.append(num / jnp.maximum(jnp.abs(den), d['e_mt'][hh]))
            hout = jnp.transpose(jnp.concatenate(h_t, axis=0))
            d['y'] = _sigmoid(o_ref[b]) * hout
            d['vw'] = jnp.concatenate(
                [d['vt'][hh * Dh:(hh + 1) * Dh] * d['w_k'][hh] for hh in range(H)]
                + [jnp.broadcast_to(d['w_k'][hh], (Dh, L)) for hh in range(H)], axis=0).astype(BF16)
        for d in st:
            d['ms'] = _dot_split(d['y'] * d['y'], hm_ref[...])
            d['upd'] = _dot(d['vw'], d['kb'])

        for b, d in zip(bs, st):
            y_ref[b] = (d['y'] * lax.rsqrt(d['ms'] + EPS) * gain_ref[...]).astype(BF16)
            decay = d['dec'][H - 1]
            for hh in range(H - 2, -1, -1):
                decay = jnp.where(lane_w == hh, d['dec'][hh], decay)
            c_sc[b] = decay * d['c_t'] + jnp.where(bd_mask, d['upd'], 0.0)
            for hh in range(H):
                m_sc[b, hh:hh + 1, :] = jnp.broadcast_to(d['m_new'][hh], (1, LANE))
        return 0

    lax.fori_loop(0, B // MLSTM_ROWS, per_group, 0)


def _mlstm(mq, mk, mv, mo, grow, cwq, cwk, brow, hm, gain, B, S):
    L, W = MLSTM_CHUNK, MLSTM_WIDTH
    seq = lambda c: (0, c, 0)
    full = lambda c: (0, 0)
    return pl.pallas_call(
        functools.partial(_mlstm_kernel, B=B),
        grid=(S // L,),
        in_specs=[
            pl.BlockSpec((B, L, W), seq),
            pl.BlockSpec((B, L, W), seq),
            pl.BlockSpec((B, L, W), seq),
            pl.BlockSpec((B, L, W), seq),
            pl.BlockSpec((None, B, SUBLANE, L), lambda c: (0, 0, 0, c)),
            pl.BlockSpec((MLSTM_CONV, W), full),
            pl.BlockSpec((MLSTM_CONV, W), full),
            pl.BlockSpec((SUBLANE, 1), full),
            pl.BlockSpec((W, W), full),
            pl.BlockSpec((1, W), full),
        ],
        out_specs=pl.BlockSpec((B, L, W), seq),
        out_shape=jax.ShapeDtypeStruct((B, S, W), BF16),
        scratch_shapes=[
            pltpu.VMEM((B, SUBLANE, W), F32),
            pltpu.VMEM((B, SUBLANE, W), F32),
            pltpu.VMEM((B, 2 * W, W), F32),
            pltpu.VMEM((B, SUBLANE, LANE), F32),
        ],
        compiler_params=pltpu.CompilerParams(
            dimension_semantics=("arbitrary",), vmem_limit_bytes=VMEM_LIMIT),
        name="mlstm",
    )(mq, mk, mv, mo, grow, cwq, cwk, brow, hm, gain)


def _compress_kernel(c_ref, w1ab_ref, w1_ref, pe_ref, w2_ref, w2t_ref, o_ref, ot_ref, ch_sc):
    G, Dh = NSA_KV_GROUPS, NSA_HEAD_DIM
    rows = ch_sc.shape[0]
    n = rows // G
    for r in range(CMP_STRIDE):
        tok = c_ref[pl.ds(r, n, stride=CMP_STRIDE), :]
        for g in range(G):
            ch_sc[g * n:(g + 1) * n, r * Dh:(r + 1) * Dh] = tok[:, g * Dh:(g + 1) * Dh]
    ab = _dot(ch_sc[...].astype(BF16), w1ab_ref[...])
    const = _dot(pe_ref[...], w1_ref[...], precision=HIGHEST)
    hid = ab[:, :CMP_HIDDEN] + pltpu.roll(ab[:, CMP_HIDDEN:], rows - 1, 0) + const
    act = _gelu_tanh(hid).astype(BF16)
    o_ref[...] = _dot(act, w2_ref[...]).astype(BF16)
    ot_ref[...] = _dot_nt(w2t_ref[...], act).astype(BF16)


def _compress(ckv, w1ab, w1, pe, w2, w2t, layer, B, S):
    G, Dh = NSA_KV_GROUPS, NSA_HEAD_DIM
    n = S // CMP_STRIDE
    width = CMP_STRIDE * Dh
    wsel = lambda i, b: (layer, i, 0, 0)
    return pl.pallas_call(
        _compress_kernel,
        grid=(2, B),
        in_specs=[
            pl.BlockSpec((None, S, G * Dh), lambda i, b: (i, b, 0)),
            pl.BlockSpec((None, None, width, 2 * CMP_HIDDEN), wsel),
            pl.BlockSpec((None, None, 2 * width, CMP_HIDDEN), wsel),
            pl.BlockSpec((None, None, 1, 2 * width), wsel),
            pl.BlockSpec((None, None, CMP_HIDDEN, Dh), wsel),
            pl.BlockSpec((None, None, Dh, CMP_HIDDEN), wsel),
        ],
        out_specs=[pl.BlockSpec((None, None, G * n, Dh), lambda i, b: (i, b, 0, 0)),
                   pl.BlockSpec((None, None, Dh, G * n), lambda i, b: (i, b, 0, 0))],
        out_shape=[jax.ShapeDtypeStruct((2, B, G * n, Dh), BF16),
                   jax.ShapeDtypeStruct((2, B, Dh, G * n), BF16)],
        scratch_shapes=[pltpu.VMEM((G * n, width), F32)],
        compiler_params=pltpu.CompilerParams(
            dimension_semantics=("parallel", "parallel"), vmem_limit_bytes=VMEM_LIMIT),
        name="compress",
    )(ckv, w1ab, w1, pe, w2, w2t)


def _nsa_kernel(q_ref, kc_ref, vct_ref, ks_ref, vst_ref, kw_ref, vwt_ref, gtt_ref, gain_ref,
                ovt_ref, et_ref, o_ref, *, n_sel, n_top, ck, unroll):
    TQ, R, Dh, G = NSA_TQ, NSA_REP, NSA_HEAD_DIM, NSA_KV_GROUPS
    groups = range(G)
    i = pl.program_id(1)
    t0 = i * TQ
    qs = [q_ref[g * R:(g + 1) * R].reshape(R * TQ, Dh) for g in groups]
    tq1 = t0 + lax.broadcasted_iota(jnp.int32, (1, TQ), 1)
    heads = lambda t: jnp.concatenate([t] * R, axis=1)

    ncmp = kc_ref.shape[0] // G
    wkeys = WINDOW + TQ
    nwb = wkeys // Q_BLOCK
    ws = pl.multiple_of(jnp.maximum(t0 - WINDOW, 0), Q_BLOCK)
    wb0 = ws // Q_BLOCK
    sc = [_dot_nt(kc_ref[g * ncmp:(g + 1) * ncmp, :], qs[g]) for g in groups]
    sw = [_dot_nt(kw_ref[g, pl.ds(ws, wkeys), :], qs[g]) for g in groups]

    kpos = ws + lax.broadcasted_iota(jnp.int32, (wkeys, 1), 0)
    wbias = heads(jnp.where((kpos <= tq1) & (tq1 - kpos < WINDOW), 0.0, NEG_INF))
    ow, l_w = [], []
    for g in groups:
        swb = sw[g] + wbias
        pw = jnp.exp2(swb - jnp.max(swb, axis=0, keepdims=True))
        vwt = jnp.concatenate([vwt_ref[g, wb0 + j] for j in range(nwb)], axis=1)
        owl = _dot(vwt, pw.astype(BF16))
        ow.append(owl[:Dh])
        l_w.append(owl[Dh:Dh + 1])

    cend = lax.broadcasted_iota(jnp.int32, (ncmp, 1), 0) * CMP_STRIDE + (CMP_BLOCK - 1)
    cmask = heads(cend <= tq1)
    pc = []
    for g in groups:
        scm = jnp.where(cmask, sc[g], NEG_INF)
        ec = jnp.where(cmask, jnp.exp2(scm - jnp.max(scm, axis=0, keepdims=True)), 0.0)
        pc.append(ec * (1.0 / jnp.maximum(jnp.sum(ec, axis=0, keepdims=True), 1e-30)))
    oc = [_dot(vct_ref[:, g * ncmp:(g + 1) * ncmp], pc[g].astype(BF16)) for g in groups]

    imp = []
    for g in groups:
        psum = pc[g][:, 0:TQ]
        for r in range(1, R):
            psum = psum + pc[g][:, r * TQ:(r + 1) * TQ]
        imp.append(_dot(ovt_ref[...], psum, precision=HIGHEST))

    blk = lax.broadcasted_iota(jnp.int32, (n_sel, 1), 0)
    valid = blk * SEL_BLOCK <= tq1
    forced = (blk == 0) | (blk == tq1 // SEL_BLOCK)
    selb = []
    for g in groups:
        val = jnp.where(forced, FORCE_SCORE, jnp.where(valid, imp[g], -FORCE_SCORE))
        rank = jnp.zeros((n_sel, TQ), F32)
        for jp in range(n_sel):
            other = val[jp:jp + 1, :]
            wins = jnp.where(blk > jp, jnp.where(other >= val, 1.0, 0.0), jnp.where(other > val, 1.0, 0.0))
            rank = rank + wins
        selb.append(jnp.where(rank < n_top, 0.0, NEG_INF).astype(BF16))

    def scores(g, c, causal):
        k0 = c * ck
        bias = _dot(et_ref[c], selb[g])
        if causal:
            kpos = k0 + lax.broadcasted_iota(jnp.int32, (ck, 1), 0)
            bias = jnp.where(kpos <= tq1, bias, NEG_INF)
        return _dot_nt(ks_ref[g, pl.ds(k0, ck), :], qs[g]) + heads(bias)

    def update(g, c, s, carry):
        m, acc = carry
        mn = jnp.maximum(m, jnp.max(s, axis=0, keepdims=True))
        p = jnp.exp2(s - mn)
        acc = jnp.exp2(m - mn) * acc + _dot(vst_ref[g, c], p.astype(BF16))
        return mn, acc

    def chunk_group(cg, carry, causal):
        cs = [cg * unroll + sub for sub in range(unroll)]
        ss = [[scores(g, c, causal) for g in groups] for c in cs]
        carry = list(carry)
        for c, s in zip(cs, ss):
            for g in groups:
                carry[g] = update(g, c, s[g], carry[g])
        return tuple(carry)

    n_chunks = (t0 + TQ + ck - 1) // ck
    init = tuple((jnp.full((1, R * TQ), NEG_INF, F32), jnp.zeros((VAL_ROWS, R * TQ), F32)) for _ in groups)
    def unrolled(trips):
        def run():
            carry = init
            for cg in range(trips):
                carry = chunk_group(cg, carry, causal=cg == trips - 1)
            return carry
        return run

    max_trips = ks_ref.shape[1] // (ck * unroll)
    sel = lax.switch((n_chunks + unroll - 1) // unroll - 1, [unrolled(t) for t in range(1, max_trips + 1)])

    normed = []
    for g in groups:
        gs = _sigmoid(gtt_ref[g])
        acc_s, l_s = sel[g][1][:Dh], sel[g][1][Dh:Dh + 1]
        for r in range(R):
            ln = slice(r * TQ, (r + 1) * TQ)
            o = (gs[3 * r:3 * r + 1, :] * oc[g][:, ln]
                 + (gs[3 * r + 1:3 * r + 2, :] / l_s[:, ln]) * acc_s[:, ln]
                 + (gs[3 * r + 2:3 * r + 3, :] / l_w[g][:, ln]) * ow[g][:, ln])
            ms = jnp.mean(o * o, axis=0, keepdims=True)
            normed.append(o * lax.rsqrt(ms + EPS) * gain_ref[g * R + r])
    for pair in range(G * R // 2):
        both = jnp.concatenate(normed[2 * pair:2 * pair + 2], axis=0)
        o_ref[:, pair * 2 * Dh:(pair + 1) * 2 * Dh] = jnp.transpose(both).astype(BF16)


def _nsa(aq, cmp_k, cmp_vt, ks, vst, kw, vwt, gates_t, gain, consts, B, S):
    G, H, TQ, Dh = NSA_KV_GROUPS, NSA_HEADS, NSA_TQ, NSA_HEAD_DIM
    nq = S // TQ
    ncmp = S // CMP_STRIDE
    n_sel = S // SEL_BLOCK
    ovt, emat_t, ck = consts
    k_spec = pl.BlockSpec((None, G, S, Dh), lambda b, i: (b, 0, 0, 0))
    return pl.pallas_call(
        functools.partial(_nsa_kernel, n_sel=n_sel, n_top=min(SEL_TOPN, n_sel), ck=ck, unroll=SEL_UNROLL),
        grid=(B, nq),
        in_specs=[
            pl.BlockSpec((None, H, TQ, Dh), lambda b, i: (b, 0, i, 0)),
            pl.BlockSpec((None, None, G * ncmp, Dh), lambda b, i: (0, b, 0, 0)),
            pl.BlockSpec((None, None, Dh, G * ncmp), lambda b, i: (1, b, 0, 0)),
            k_spec,
            pl.BlockSpec((None, G, S // ck, VAL_ROWS, ck), lambda b, i: (b, 0, 0, 0, 0)),
            k_spec,
            pl.BlockSpec((None, G, S // Q_BLOCK, VAL_ROWS, Q_BLOCK), lambda b, i: (b, 0, 0, 0, 0)),
            pl.BlockSpec((G, None, 2 * SUBLANE, TQ), lambda b, i: (0, b, GATE_COL // (2 * SUBLANE), i)),
            pl.BlockSpec((H, Dh, 1), lambda b, i: (0, 0, 0)),
            pl.BlockSpec(ovt.shape, lambda b, i: (0, 0)),
            pl.BlockSpec(emat_t.shape, lambda b, i: (0, 0, 0)),
        ],
        out_specs=pl.BlockSpec((TQ, H * Dh), lambda b, i: (b * nq + i, 0)),
        out_shape=jax.ShapeDtypeStruct((B * S, H * Dh), BF16),
        compiler_params=pltpu.CompilerParams(
            dimension_semantics=("parallel", "arbitrary"), vmem_limit_bytes=VMEM_LIMIT),
        name="nsa",
    )(aq, cmp_k, cmp_vt, ks, vst, kw, vwt, gates_t, gain, ovt, emat_t)


def _nsa_consts(S):
    n_cmp = S // CMP_STRIDE
    n_sel = S // SEL_BLOCK
    ck = 256
    i = np.arange(n_cmp)[:, None]
    j = np.arange(n_sel)[None, :]
    lo = np.maximum(i * CMP_STRIDE, j * SEL_BLOCK)
    hi = np.minimum(i * CMP_STRIDE + CMP_BLOCK, (j + 1) * SEL_BLOCK)
    ov = np.maximum(hi - lo, 0) / CMP_STRIDE
    ov[n_cmp - 1] = 0.0
    key = np.arange(S)
    emat_t = (key[:, None] // SEL_BLOCK == np.arange(n_sel)[None, :]).astype(np.float32)
    return (jnp.asarray(ov.T, F32), jnp.asarray(emat_t.reshape(S // ck, ck, n_sel), BF16), ck)


def _outproj_kernel(h_ref, ys_ref, ym_ref, yn_ref, w_ref, g_ref, o_ref):
    acc = _dot(ys_ref[...], w_ref[0:SSM_WIDTH, :])
    acc = acc + _dot(ym_ref[...], w_ref[SSM_WIDTH:SSM_WIDTH + MLSTM_WIDTH, :])
    acc = acc + _dot(yn_ref[...], w_ref[SSM_WIDTH + MLSTM_WIDTH:, :])
    ms = jnp.mean(acc * acc, axis=-1, keepdims=True)
    o_ref[...] = h_ref[...] + acc * lax.rsqrt(ms + EPS) * g_ref[...]


def _outproj(h2, y_ssm, y_mls, y_nsa, w, layer, gain, B, S, ts):
    nt = S // ts
    row = lambda b, i: (b * nt + i, 0)
    full = lambda b, i: (0, 0)
    return pl.pallas_call(
        _outproj_kernel,
        grid=(B, nt),
        in_specs=[
            pl.BlockSpec((ts, D_MODEL), row),
            pl.BlockSpec((ts, SSM_WIDTH), row),
            pl.BlockSpec((ts, MLSTM_WIDTH), row),
            pl.BlockSpec((ts, NSA_WIDTH), row),
            pl.BlockSpec((None, D_MODEL, D_MODEL), lambda b, i: (layer, 0, 0)),
            pl.BlockSpec((1, D_MODEL), full),
        ],
        out_specs=pl.BlockSpec((ts, D_MODEL), row),
        out_shape=jax.ShapeDtypeStruct((B * S, D_MODEL), F32),
        compiler_params=pltpu.CompilerParams(
            dimension_semantics=("parallel", "parallel"), vmem_limit_bytes=VMEM_LIMIT),
        name="outproj",
    )(h2, y_ssm, y_mls, y_nsa, w, gain)


def _mlp_kernel(h_ref, g1_ref, w1_ref, w2_ref, g2_ref, o_ref, u_sc, acc_sc):
    kf = pl.program_id(1)

    @pl.when(kf == 0)
    def _():
        x = h_ref[...]
        ms = jnp.mean(x * x, axis=-1, keepdims=True)
        u_sc[...] = (x * lax.rsqrt(ms + EPS) * g1_ref[...]).astype(BF16)
        acc_sc[...] = jnp.zeros_like(acc_sc)

    a = jnp.maximum(_dot(u_sc[...], w1_ref[...].astype(BF16)), 0.0)
    acc_sc[...] += _dot((a * a).astype(BF16), w2_ref[...].astype(BF16))

    @pl.when(kf == pl.num_programs(1) - 1)
    def _():
        f = acc_sc[...]
        ms = jnp.mean(f * f, axis=-1, keepdims=True)
        o_ref[...] = h_ref[...] + f * lax.rsqrt(ms + EPS) * g2_ref[...]


def _mlp(h2, g1, w1, w2, layer, g2, tm, tf):
    rows = h2.shape[0]
    return pl.pallas_call(
        _mlp_kernel,
        grid=(rows // tm, D_FF // tf),
        in_specs=[
            pl.BlockSpec((tm, D_MODEL), lambda i, k: (i, 0)),
            pl.BlockSpec((1, D_MODEL), lambda i, k: (0, 0)),
            pl.BlockSpec((None, D_MODEL, tf), lambda i, k: (layer, 0, k)),
            pl.BlockSpec((None, tf, D_MODEL), lambda i, k: (layer, k, 0)),
            pl.BlockSpec((1, D_MODEL), lambda i, k: (0, 0)),
        ],
        out_specs=pl.BlockSpec((tm, D_MODEL), lambda i, k: (i, 0)),
        out_shape=jax.ShapeDtypeStruct((rows, D_MODEL), F32),
        scratch_shapes=[pltpu.VMEM((tm, D_MODEL), BF16), pltpu.VMEM((tm, D_MODEL), F32)],
        compiler_params=pltpu.CompilerParams(
            dimension_semantics=("parallel", "arbitrary"), vmem_limit_bytes=VMEM_LIMIT),
        name="mlp",
    )(h2, g1, w1, w2, g2)


def _inproj_pieces():
    return ((0, 1280), (1288, 1800), (1800, 1928), (2056, 2184), (2312, 2440),
            (1928, 2056), (2184, 2312), (2440, 2568),
            (1280, 1288), (None, GATE_COL - 8), (2568, 2580), (None, LANE - GATE_COL - 12),
            (None, GATE_COL), (2580, 2592), (None, LANE - GATE_COL - 12))


def _permute_w_in_kernel(w_ref, o_ref):
    x = w_ref[...]
    col = 0
    for a, b in _inproj_pieces():
        width = b if a is None else b - a
        piece = jnp.zeros((x.shape[0], width), BF16) if a is None else x[:, a:b].astype(BF16)
        o_ref[:, col:col + width] = piece
        col += width
    assert col == D_INP


def _permute_w_in(w_in):
    depth, rows, cols = w_in.shape
    tr = 256
    return pl.pallas_call(
        _permute_w_in_kernel,
        grid=(depth, rows // tr),
        in_specs=[pl.BlockSpec((None, tr, cols), lambda l, i: (l, i, 0))],
        out_specs=pl.BlockSpec((None, tr, D_INP), lambda l, i: (l, i, 0)),
        out_shape=jax.ShapeDtypeStruct((depth, rows, D_INP), BF16),
        compiler_params=pltpu.CompilerParams(
            dimension_semantics=("parallel", "parallel"), vmem_limit_bytes=VMEM_LIMIT),
        name="permute_w_in",
    )(w_in)


def _rope_tables(positions):
    inv = ROPE_THETA ** (-jnp.arange(0, ROPE_DIMS, 2, dtype=F32) / ROPE_DIMS)
    ang = positions.astype(F32)[..., None] * inv
    cos, sin = jnp.cos(ang), jnp.sin(ang)
    z = jnp.zeros_like(cos)
    rest = NSA_HEAD_DIM - ROPE_DIMS
    pad_one = jnp.ones(cos.shape[:-1] + (rest,), F32)
    pad_zero = jnp.zeros(cos.shape[:-1] + (rest,), F32)
    rc = jnp.concatenate([cos, cos, pad_one], axis=-1)
    rs1 = jnp.concatenate([-sin, z, pad_zero], axis=-1)
    rs2 = jnp.concatenate([z, sin, pad_zero], axis=-1)
    tile = lambda t: jnp.tile(t, (1, 1, LANE // NSA_HEAD_DIM)).reshape(-1, LANE)
    return tile(rc), tile(rs1), tile(rs2)


def _s5_params(lam_re, lam_im, b_re, b_im, c_re, c_im, log_dt):
    G, P, Hc = SSM_GROUPS, SSM_STATE, SSM_GROUP
    dt = jnp.exp(log_dt)[:, None]
    mag = jnp.exp(lam_re * dt)
    ang = lam_im * dt
    ab_re = mag * jnp.cos(ang)
    ab_im = mag * jnp.sin(ang)
    den = lam_re * lam_re + lam_im * lam_im
    g_re = ((ab_re - 1.0) * lam_re + ab_im * lam_im) / den
    g_im = (ab_im * lam_re - (ab_re - 1.0) * lam_im) / den
    bb_re = g_re[..., None] * b_re - g_im[..., None] * b_im
    bb_im = g_re[..., None] * b_im + g_im[..., None] * b_re
    eye = jnp.eye(G, dtype=F32)
    blockdiag_in = lambda t: jnp.einsum('gph,gk->ghkp', t, eye).reshape(G * Hc, G * P)
    blockdiag_out = lambda t: jnp.einsum('ghp,gk->gpkh', t, eye).reshape(G * P, G * Hc)
    bb = jnp.concatenate([blockdiag_in(bb_re), blockdiag_in(bb_im)], axis=1).astype(BF16)
    cc = jnp.concatenate([blockdiag_out(c_re), -blockdiag_out(c_im)], axis=0).astype(BF16)
    a = jnp.stack([ab_re.reshape(-1), ab_im.reshape(-1)], axis=0)
    return bb, a, cc


def _group_mean_matrix(width, group):
    idx = np.arange(width) // group
    return jnp.asarray((idx[:, None] == idx[None, :]).astype(np.float32) / group, BF16)


def kernel(x, positions, ln_mix_pre, ln_mix_post, ln_mlp_pre, ln_mlp_post, w_in, w_out, ssm_lambda_re, ssm_lambda_im, ssm_b_re, ssm_b_im, ssm_c_re, ssm_c_im, ssm_d, ssm_log_dt, ssm_w_glu, mlstm_conv, mlstm_b_i, mlstm_b_f, cmp_pe_k, cmp_w1_k, cmp_w2_k, cmp_pe_v, cmp_w1_v, cmp_w2_v, gn_ssm, gn_mlstm, gn_nsa, mlp_w1, mlp_w2):
    B, S, D = x.shape
    depth = w_in.shape[0]
    assert D == D_MODEL and B == SUBLANE and S % 512 == 0 and S >= WINDOW + NSA_TQ
    G, H = NSA_KV_GROUPS, MLSTM_HEADS
    ts_proj = 512
    ts_scan = 128

    rc, rs1, rs2 = _rope_tables(positions)
    w_in_p = _permute_w_in(w_in)
    w_out_b = w_out.astype(BF16)
    wglu_b = ssm_w_glu.astype(BF16)
    gm_ssm = _group_mean_matrix(SSM_WIDTH, SSM_GROUP)
    hm_mls = _group_mean_matrix(MLSTM_WIDTH, MLSTM_HEAD_DIM)
    consts = _nsa_consts(S)
    half = CMP_STRIDE * NSA_HEAD_DIM
    w1ab = jnp.stack([jnp.concatenate([cmp_w1_k[:, :half], cmp_w1_k[:, half:]], axis=-1),
                      jnp.concatenate([cmp_w1_v[:, :half], cmp_w1_v[:, half:]], axis=-1)], axis=1).astype(BF16)
    w1f = jnp.stack([cmp_w1_k, cmp_w1_v], axis=1)
    pef = jnp.stack([cmp_pe_k.reshape(depth, 1, -1), cmp_pe_v.reshape(depth, 1, -1)], axis=1)
    w2c = jnp.stack([cmp_w2_k, cmp_w2_v], axis=1).astype(BF16)
    w2ct = jnp.swapaxes(w2c, -1, -2)
    bias_row = jnp.concatenate([mlstm_b_i, mlstm_b_f], axis=-1)[:, :, None]

    bb, a, cc = jax.vmap(_s5_params)(ssm_lambda_re, ssm_lambda_im, ssm_b_re, ssm_b_im, ssm_c_re, ssm_c_im,
                                     ssm_log_dt)
    sh3 = lambda t: t.reshape(B, S, t.shape[-1])

    h = x.reshape(B * S, D)
    for l in range(depth):
        (su, mq, mk, mv, mo, aq, ckv, sk, wk, svt, wvt, gates, gates_t) = _inproj(
            h, ln_mix_pre[l][None], w_in_p, l, rc, rs1, rs2, B, S, ts_proj, consts[2])

        y_ssm = _s5(sh3(su), bb, a, cc, ssm_d[l][None], wglu_b, l, gm_ssm, gn_ssm[l][None], B, S, ts_scan)

        y_mls = _mlstm(sh3(mq), sh3(mk), sh3(mv), sh3(mo), gates_t,
                       mlstm_conv[l][:, :MLSTM_WIDTH], mlstm_conv[l][:, MLSTM_WIDTH:],
                       bias_row[l], hm_mls, gn_mlstm[l][None], B, S)

        cmp_k, cmp_t = _compress(ckv, w1ab, w1f, pef, w2c, w2ct, l, B, S)
        y_nsa = _nsa(aq, cmp_k, cmp_t, sk, svt, wk, wvt, gates_t,
                     gn_nsa[l].reshape(NSA_HEADS, NSA_HEAD_DIM, 1), consts, B, S)

        h = _outproj(h, y_ssm.reshape(B * S, SSM_WIDTH), y_mls.reshape(B * S, MLSTM_WIDTH), y_nsa,
                     w_out_b, l, ln_mix_post[l][None], B, S, ts_proj)
        h = _mlp(h, ln_mlp_pre[l][None], mlp_w1, mlp_w2, l, ln_mlp_post[l][None], 1024, 1024)
    return h.reshape(B, S, D)
```

```python
import functools
import math

import numpy as np
import jax
import jax.numpy as jnp
from jax import lax
from jax.experimental import pallas as pl
from jax.experimental.pallas import tpu as pltpu

F32 = jnp.float32
BF16 = jnp.bfloat16
HIGHEST = lax.Precision.HIGHEST

D_MODEL = 1024
DEPTH = 4
SSM_WIDTH = 256
SSM_GROUP = 16
SSM_GROUPS = 16
SSM_STATE = 64
SSM_LANES = SSM_GROUPS * SSM_STATE
MLSTM_WIDTH = 256
MLSTM_HEADS = 4
MLSTM_HEAD_DIM = 64
MLSTM_CHUNK = 128
MLSTM_CONV = 4
NSA_WIDTH = 512
NSA_HEAD_DIM = 64
NSA_HEADS = 8
NSA_KV_GROUPS = 2
NSA_REP = NSA_HEADS // NSA_KV_GROUPS
NSA_KV_WIDTH = 128
CMP_BLOCK = 32
CMP_STRIDE = 16
CMP_HIDDEN = 256
SEL_BLOCK = 64
SEL_TOPN = 8
WINDOW = 256
Q_BLOCK = 128
FORCE_SCORE = 1e4
NEG_INF = -1e30
ROPE_THETA = 500000.0
ROPE_DIMS = 16
ROPE_HALF = 8
D_FF = 4096
EPS = 1e-6
D_IN = 2592

LANE = 128
SUBLANE = 8
VMEM_LIMIT = 56 * 1024 * 1024

C_SU, C_MQ, C_MK, C_MV, C_MO = 0, 256, 512, 768, 1024
C_AQ, C_CK, C_SK, C_WK = 1280, 1792, 1920, 2048
C_CV, C_SV, C_WV = 2176, 2304, 2432
C_G0, C_G1 = 2560, 2688
D_INP = 2816
GATE_COL = 16
VAL_ROWS = NSA_HEAD_DIM + 16
Q_SCALE = NSA_HEAD_DIM ** -0.5 * math.log2(math.e)
MLSTM_ROWS = 8
NSA_TQ = 256
SEL_UNROLL = 2


def _dot(a, b, precision=None):
    return jnp.dot(a, b, preferred_element_type=F32, precision=precision)


def _dot_nt(a, b):
    return lax.dot_general(a, b, (((1,), (1,)), ((), ())), preferred_element_type=F32)


def _dot_tn(a, b):
    return lax.dot_general(a, b, (((0,), (0,)), ((), ())), preferred_element_type=F32)


def _sigmoid(x):
    return 1.0 / (1.0 + jnp.exp(-x))


def _dot_split(x, w_bf16):
    hi = x.astype(BF16)
    lo = (x - hi.astype(F32)).astype(BF16)
    return _dot(hi, w_bf16) + _dot(lo, w_bf16)


def _gelu_tanh(x):
    return 0.5 * x * (1.0 + jnp.tanh(math.sqrt(2.0 / math.pi) * (x + 0.044715 * (x * x * x))))


def _log_sigmoid(x):
    return jnp.minimum(x, 0.0) - jnp.log(1.0 + jnp.exp(-jnp.abs(x)))


def _inproj_kernel(x_ref, g_ref, w_ref, rt_ref,
                   su_ref, mq_ref, mk_ref, mv_ref, mo_ref, aq_ref, ckv_ref, sk_ref, wk_ref,
                   svt_ref, wvt_ref, gt_ref, gtt_ref):
    x = x_ref[...]
    ms = jnp.mean(x * x, axis=-1, keepdims=True)
    u = (x * lax.rsqrt(ms + EPS) * g_ref[...]).astype(BF16)
    rc, rs1, rs2 = (rt_ref[:, n * LANE:(n + 1) * LANE] for n in range(3))

    def mm(c0, width):
        return _dot(u, w_ref[:, c0:c0 + width])

    def rope(z):
        return z * rc + pltpu.roll(z, LANE - ROPE_HALF, 1) * rs1 + pltpu.roll(z, ROPE_HALF, 1) * rs2

    su_ref[...] = mm(C_SU, 256)
    mq_ref[...] = mm(C_MQ, 256)
    mk_ref[...] = mm(C_MK, 256)
    mv_ref[...] = mm(C_MV, 256).astype(BF16)
    mo_ref[...] = mm(C_MO, 256)
    def mm_pair(c0):
        z = mm(c0, 2 * LANE)
        return z[:, :LANE], z[:, LANE:]

    def put_heads(ref, first, z):
        ref[first] = z[:, :NSA_HEAD_DIM].astype(BF16)
        ref[first + 1] = z[:, NSA_HEAD_DIM:].astype(BF16)

    for j in range(NSA_HEADS // 4):
        for k, z in enumerate(mm_pair(C_AQ + 2 * LANE * j)):
            put_heads(aq_ref, 4 * j + 2 * k, rope(z) * Q_SCALE)
    z_ck, z_sk = mm_pair(C_CK)
    z_wk, z_cv = mm_pair(C_WK)
    z_sv, z_wv = mm_pair(C_SV)
    ckv_ref[0] = rope(z_ck)
    ckv_ref[1] = z_cv
    put_heads(sk_ref, 0, rope(z_sk))
    put_heads(wk_ref, 0, rope(z_wk))

    def put_chunks_t(ref, z):
        zt = jnp.transpose(z)
        width = ref.shape[-1]
        ones = jnp.ones((VAL_ROWS - NSA_HEAD_DIM, width), BF16)
        for g in range(NSA_KV_GROUPS):
            for j in range(ref.shape[1]):
                ref[g, j, :NSA_HEAD_DIM] = zt[g * NSA_HEAD_DIM:(g + 1) * NSA_HEAD_DIM,
                                              j * width:(j + 1) * width].astype(BF16)
                ref[g, j, NSA_HEAD_DIM:] = ones

    put_chunks_t(svt_ref, z_sv)
    put_chunks_t(wvt_ref, z_wv)
    z_g0, z_g1 = mm_pair(C_G0)
    gt_ref[...] = z_g0
    gtt_ref[0] = jnp.transpose(z_g0)
    gtt_ref[1] = jnp.transpose(z_g1)


def _inproj(h2, gain, w, layer, rope, B, S, ts, ck):
    nt = S // ts
    BS = B * S
    row = lambda b, i: (b * nt + i, 0)
    full = lambda b, i: (0, 0)
    headed = lambda b, i: (b, 0, i, 0)
    paired = lambda b, i: (0, b * nt + i, 0)
    in_specs = [
        pl.BlockSpec((ts, D_MODEL), row),
        pl.BlockSpec((1, D_MODEL), full),
        pl.BlockSpec((None, D_MODEL, D_INP), lambda b, i: (layer, 0, 0)),
        pl.BlockSpec((ts, 3 * LANE), row),
    ]
    kv_shape = jax.ShapeDtypeStruct((B, NSA_KV_GROUPS, S, NSA_HEAD_DIM), BF16)
    kv_spec = pl.BlockSpec((None, NSA_KV_GROUPS, ts, NSA_HEAD_DIM), headed)
    out_shape = [
        jax.ShapeDtypeStruct((BS, SSM_WIDTH), F32),
        jax.ShapeDtypeStruct((BS, MLSTM_WIDTH), F32),
        jax.ShapeDtypeStruct((BS, MLSTM_WIDTH), F32),
        jax.ShapeDtypeStruct((BS, MLSTM_WIDTH), BF16),
        jax.ShapeDtypeStruct((BS, MLSTM_WIDTH), F32),
        jax.ShapeDtypeStruct((B, NSA_HEADS, S, NSA_HEAD_DIM), BF16),
        jax.ShapeDtypeStruct((2, BS, NSA_KV_WIDTH), F32),
        kv_shape, kv_shape,
        jax.ShapeDtypeStruct((B, NSA_KV_GROUPS, S // ck, VAL_ROWS, ck), BF16),
        jax.ShapeDtypeStruct((B, NSA_KV_GROUPS, S // Q_BLOCK, VAL_ROWS, Q_BLOCK), BF16),
        jax.ShapeDtypeStruct((BS, LANE), F32),
        jax.ShapeDtypeStruct((NSA_KV_GROUPS, B, LANE, S), F32),
    ]
    out_specs = [
        pl.BlockSpec((ts, SSM_WIDTH), row),
        pl.BlockSpec((ts, MLSTM_WIDTH), row),
        pl.BlockSpec((ts, MLSTM_WIDTH), row),
        pl.BlockSpec((ts, MLSTM_WIDTH), row),
        pl.BlockSpec((ts, MLSTM_WIDTH), row),
        pl.BlockSpec((None, NSA_HEADS, ts, NSA_HEAD_DIM), headed),
        pl.BlockSpec((2, ts, NSA_KV_WIDTH), paired),
        kv_spec, kv_spec,
        pl.BlockSpec((None, NSA_KV_GROUPS, ts // ck, VAL_ROWS, ck), lambda b, i: (b, 0, i, 0, 0)),
        pl.BlockSpec((None, NSA_KV_GROUPS, ts // Q_BLOCK, VAL_ROWS, Q_BLOCK), lambda b, i: (b, 0, i, 0, 0)),
        pl.BlockSpec((ts, LANE), row),
        pl.BlockSpec((NSA_KV_GROUPS, None, LANE, ts), lambda b, i: (0, b, 0, i)),
    ]
    return pl.pallas_call(
        _inproj_kernel,
        grid=(B, nt),
        in_specs=in_specs,
        out_specs=out_specs,
        out_shape=out_shape,
        compiler_params=pltpu.CompilerParams(
            dimension_semantics=("parallel", "parallel"), vmem_limit_bytes=VMEM_LIMIT),
        name="inproj",
    )(h2, gain, w, rope)


def _s5_kernel(u_ref, bb_ref, a_ref, cc_ref, d_ref, wg_ref, gm_ref, gain_ref, o_ref, x_sc, st_sc, tm_sc, *, B, ts):
    @pl.when(pl.program_id(0) == 0)
    def _():
        st_sc[...] = jnp.zeros_like(st_sc)

    nl = SSM_WIDTH // LANE
    for b in range(B):
        for c in range(nl):
            tm_sc[c, pl.ds(b, ts, stride=B), :] = u_ref[b, :, c * LANE:(c + 1) * LANE]
    u = jnp.concatenate([tm_sc[c] for c in range(nl)], axis=1)
    ub = u.astype(BF16)
    for part in range(2):
        cols = slice(part * SSM_LANES, (part + 1) * SSM_LANES)
        x_sc[:, cols] = _dot(ub, bb_ref[:, cols])
    ar = jnp.broadcast_to(a_ref[0:1, :], (B, SSM_LANES))
    ai = jnp.broadcast_to(a_ref[1:2, :], (B, SSM_LANES))

    def step(t, carry):
        xr, xi = carry
        r = pl.multiple_of(t * B, B)
        br = x_sc[pl.ds(r, B), 0:SSM_LANES]
        bi = x_sc[pl.ds(r, B), SSM_LANES:2 * SSM_LANES]
        nr = ar * xr - ai * xi + br
        ni = ar * xi + ai * xr + bi
        x_sc[pl.ds(r, B), 0:SSM_LANES] = nr
        x_sc[pl.ds(r, B), SSM_LANES:2 * SSM_LANES] = ni
        return nr, ni

    xr, xi = lax.fori_loop(0, ts, step, (st_sc[0], st_sc[1]))
    st_sc[0] = xr
    st_sc[1] = xi

    half = (ts * B) // 2
    y = jnp.concatenate([_dot(x_sc[r * half:(r + 1) * half, :].astype(BF16), cc_ref[...]) for r in range(2)],
                        axis=0) + d_ref[...] * u
    y = _gelu_tanh(y)
    y = y * _sigmoid(_dot(y.astype(BF16), wg_ref[...]))
    ms = _dot_split(y * y, gm_ref[...])
    y = y * lax.rsqrt(ms + EPS) * gain_ref[...]
    for c in range(nl):
        tm_sc[c] = y[:, c * LANE:(c + 1) * LANE]
    for b in range(B):
        o_ref[b] = jnp.concatenate(
            [tm_sc[c, pl.ds(b, ts, stride=B), :] for c in range(nl)], axis=1).astype(BF16)


def _s5(u, bb, a, cc, d, wg, layer, gm, gain, B, S, ts):
    rows = ts * B
    full = lambda i: (0, 0)
    lsel = lambda i: (layer, 0, 0)
    return pl.pallas_call(
        functools.partial(_s5_kernel, B=B, ts=ts),
        grid=(S // ts,),
        in_specs=[
            pl.BlockSpec((B, ts, SSM_WIDTH), lambda i: (0, i, 0)),
            pl.BlockSpec((None, SSM_WIDTH, 2 * SSM_LANES), lsel),
            pl.BlockSpec((None, 2, SSM_LANES), lsel),
            pl.BlockSpec((None, 2 * SSM_LANES, SSM_WIDTH), lsel),
            pl.BlockSpec((1, SSM_WIDTH), full),
            pl.BlockSpec((None, SSM_WIDTH, SSM_WIDTH), lsel),
            pl.BlockSpec((SSM_WIDTH, SSM_WIDTH), full),
            pl.BlockSpec((1, SSM_WIDTH), full),
        ],
        out_specs=pl.BlockSpec((B, ts, SSM_WIDTH), lambda i: (0, i, 0)),
        out_shape=jax.ShapeDtypeStruct((B, S, SSM_WIDTH), BF16),
        scratch_shapes=[pltpu.VMEM((rows, 2 * SSM_LANES), F32), pltpu.VMEM((2, B, SSM_LANES), F32),
                        pltpu.VMEM((SSM_WIDTH // LANE, rows, LANE), F32)],
        compiler_params=pltpu.CompilerParams(
            dimension_semantics=("arbitrary",), vmem_limit_bytes=VMEM_LIMIT),
        name="s5",
    )(u, bb, a, cc, d, wg, gm, gain)


def _mlstm_kernel(q_ref, k_ref, v_ref, o_ref, gr_ref, cwq_ref, cwk_ref, br_ref, hm_ref,
                  gain_ref, y_ref, qt_sc, kt_sc, c_sc, m_sc, *, B):
    L, H, Dh, W = MLSTM_CHUNK, MLSTM_HEADS, MLSTM_HEAD_DIM, MLSTM_WIDTH

    @pl.when(pl.program_id(0) == 0)
    def _():
        qt_sc[...] = jnp.zeros_like(qt_sc)
        kt_sc[...] = jnp.zeros_like(kt_sc)
        c_sc[...] = jnp.zeros_like(c_sc)
        m_sc[...] = jnp.zeros_like(m_sc)

    visible = lax.broadcasted_iota(jnp.int32, (L, L), 0) <= lax.broadcasted_iota(jnp.int32, (L, L), 1)
    triu = visible.astype(F32)
    lane_w = lax.broadcasted_iota(jnp.int32, (1, W), 1) // Dh
    bd_mask = ((lax.broadcasted_iota(jnp.int32, (2 * W, W), 0) % W) // Dh
               == lax.broadcasted_iota(jnp.int32, (2 * W, W), 1) // Dh)
    row8 = lax.broadcasted_iota(jnp.int32, (SUBLANE, W), 0)
    cwq = cwq_ref[...]
    cwk = cwk_ref[...]
    ones_rows = jnp.ones((Dh, L), F32)

    def conv_silu(x, tail, w):
        acc = x * w[MLSTM_CONV - 1:MLSTM_CONV, :]
        for sft in range(1, MLSTM_CONV):
            xs = pltpu.roll(x, sft, 0)
            head = jnp.where(row8 < sft, pltpu.roll(tail, sft, 0), xs[:SUBLANE])
            xs = jnp.concatenate([head, xs[SUBLANE:]], axis=0)
            acc = acc + xs * w[MLSTM_CONV - 1 - sft:MLSTM_CONV - sft, :]
        return acc * _sigmoid(acc)

    def per_group(grp, _):
        bs = [grp * MLSTM_ROWS + n for n in range(MLSTM_ROWS)]
        st = [dict() for _ in bs]

        for b, d in zip(bs, st):
            q_raw = q_ref[b]
            k_raw = k_ref[b]
            d['q'] = conv_silu(q_raw, qt_sc[b], cwq)
            d['k'] = conv_silu(k_raw, kt_sc[b], cwk) * (Dh ** -0.5)
            qt_sc[b] = q_raw[L - SUBLANE:, :]
            kt_sc[b] = k_raw[L - SUBLANE:, :]
            d['gr'] = gr_ref[b] + br_ref[...]
        for d in st:
            d['brow'] = _dot(_log_sigmoid(d['gr']), triu, precision=HIGHEST)

        for b, d in zip(bs, st):
            gr, brow = d['gr'], d['brow']
            ccol = jnp.transpose(brow - pltpu.roll(gr, H, 0))
            m_all = m_sc[b]
            for key in ('w_intra', 'w_inter', 'e_mt', 'w_k', 'dec', 'm_new'):
                d[key] = []
            for hh in range(H):
                b_r = brow[H + hh:H + hh + 1, :]
                i_r = gr[hh:hh + 1, :]
                m_prev = m_all[hh:hh + 1, 0:1]
                dm = jnp.where(visible, b_r - ccol[:, H + hh:H + hh + 1], NEG_INF)
                inter = b_r + m_prev
                mt = jnp.maximum(inter, jnp.max(dm, axis=0, keepdims=True))
                d['w_intra'].append(jnp.exp(dm - mt))
                d['w_inter'].append(jnp.exp(inter - mt))
                d['e_mt'].append(jnp.exp(-mt))
                b_last = b_r[:, L - 1:L]
                logw = b_last - b_r + i_r
                mn = jnp.maximum(b_last + m_prev, jnp.max(logw, axis=1, keepdims=True))
                d['w_k'].append(jnp.exp(logw - mn))
                d['dec'].append(jnp.exp(b_last + m_prev - mn))
                d['m_new'].append(mn)
            d['qb'] = d['q'].astype(BF16)
            d['kb'] = d['k'].astype(BF16)
            d['vt'] = jnp.transpose(v_ref[b].astype(F32))
            d['c_t'] = c_sc[b]

        for d in st:
            d['qc'] = _dot(d['c_t'].astype(BF16), jnp.transpose(d['q']).astype(BF16))
            d['s_t'] = [_dot_nt(d['kb'], jnp.where(lane_w == hh, d['qb'], jnp.zeros_like(d['qb'])))
                        for hh in range(H)]
        for d in st:
            d['r'] = []
            for hh in range(H):
                v_aug = jnp.concatenate([d['vt'][hh * Dh:(hh + 1) * Dh], ones_rows], axis=0).astype(BF16)
                d['r'].append(_dot(v_aug, (d['s_t'][hh] * d['w_intra'][hh]).astype(BF16)))

        for b, d in zip(bs, st):
            h_t = []
            for hh in range(H):
                ch = slice(hh * Dh, (hh + 1) * Dh)
                num = d['w_inter'][hh] * d['qc'][ch] + d['r'][hh][:Dh]
                den = d['w_inter'][hh] * d['qc'][W + hh * Dh:W + (hh + 1) * Dh] + d['r'][hh][Dh:]
                h_t.append(num / jnp.maximum(jnp.abs(den), d['e_mt'][hh]))
            hout = jnp.transpose(jnp.concatenate(h_t, axis=0))
            d['y'] = _sigmoid(o_ref[b]) * hout
            d['vw'] = jnp.concatenate(
                [d['vt'][hh * Dh:(hh + 1) * Dh] * d['w_k'][hh] for hh in range(H)]
                + [jnp.broadcast_to(d['w_k'][hh], (Dh, L)) for hh in range(H)], axis=0).astype(BF16)
        for d in st:
            d['ms'] = _dot_split(d['y'] * d['y'], hm_ref[...])
            d['upd'] = _dot(d['vw'], d['kb'])

        for b, d in zip(bs, st):
            y_ref[b] = (d['y'] * lax.rsqrt(d['ms'] + EPS) * gain_ref[...]).astype(BF16)
            decay = d['dec'][H - 1]
            for hh in range(H - 2, -1, -1):
                decay = jnp.where(lane_w == hh, d['dec'][hh], decay)
            c_sc[b] = decay * d['c_t'] + jnp.where(bd_mask, d['upd'], 0.0)
            for hh in range(H):
                m_sc[b, hh:hh + 1, :] = jnp.broadcast_to(d['m_new'][hh], (1, LANE))
        return 0

    lax.fori_loop(0, B // MLSTM_ROWS, per_group, 0)


def _mlstm(mq, mk, mv, mo, grow, cwq, cwk, brow, hm, gain, B, S):
    L, W = MLSTM_CHUNK, MLSTM_WIDTH
    seq = lambda c: (0, c, 0)
    full = lambda c: (0, 0)
    return pl.pallas_call(
        functools.partial(_mlstm_kernel, B=B),
        grid=(S // L,),
        in_specs=[
            pl.BlockSpec((B, L, W), seq),
            pl.BlockSpec((B, L, W), seq),
            pl.BlockSpec((B, L, W), seq),
            pl.BlockSpec((B, L, W), seq),
            pl.BlockSpec((None, B, SUBLANE, L), lambda c: (0, 0, 0, c)),
            pl.BlockSpec((MLSTM_CONV, W), full),
            pl.BlockSpec((MLSTM_CONV, W), full),
            pl.BlockSpec((SUBLANE, 1), full),
            pl.BlockSpec((W, W), full),
            pl.BlockSpec((1, W), full),
        ],
        out_specs=pl.BlockSpec((B, L, W), seq),
        out_shape=jax.ShapeDtypeStruct((B, S, W), BF16),
        scratch_shapes=[
            pltpu.VMEM((B, SUBLANE, W), F32),
            pltpu.VMEM((B, SUBLANE, W), F32),
            pltpu.VMEM((B, 2 * W, W), F32),
            pltpu.VMEM((B, SUBLANE, LANE), F32),
        ],
        compiler_params=pltpu.CompilerParams(
            dimension_semantics=("arbitrary",), vmem_limit_bytes=VMEM_LIMIT),
        name="mlstm",
    )(mq, mk, mv, mo, grow, cwq, cwk, brow, hm, gain)


def _compress_kernel(c_ref, w1ab_ref, w1_ref, pe_ref, w2_ref, w2t_ref, o_ref, ot_ref, ch_sc):
    G, Dh = NSA_KV_GROUPS, NSA_HEAD_DIM
    rows = ch_sc.shape[0]
    n = rows // G
    for r in range(CMP_STRIDE):
        tok = c_ref[pl.ds(r, n, stride=CMP_STRIDE), :]
        for g in range(G):
            ch_sc[g * n:(g + 1) * n, r * Dh:(r + 1) * Dh] = tok[:, g * Dh:(g + 1) * Dh]
    ab = _dot(ch_sc[...].astype(BF16), w1ab_ref[...])
    const = _dot(pe_ref[...], w1_ref[...], precision=HIGHEST)
    hid = ab[:, :CMP_HIDDEN] + pltpu.roll(ab[:, CMP_HIDDEN:], rows - 1, 0) + const
    act = _gelu_tanh(hid).astype(BF16)
    o_ref[...] = _dot(act, w2_ref[...]).astype(BF16)
    ot_ref[...] = _dot_nt(w2t_ref[...], act).astype(BF16)


def _compress(ckv, w1ab, w1, pe, w2, w2t, layer, B, S):
    G, Dh = NSA_KV_GROUPS, NSA_HEAD_DIM
    n = S // CMP_STRIDE
    width = CMP_STRIDE * Dh
    wsel = lambda i, b: (layer, i, 0, 0)
    return pl.pallas_call(
        _compress_kernel,
        grid=(2, B),
        in_specs=[
            pl.BlockSpec((None, S, G * Dh), lambda i, b: (i, b, 0)),
            pl.BlockSpec((None, None, width, 2 * CMP_HIDDEN), wsel),
            pl.BlockSpec((None, None, 2 * width, CMP_HIDDEN), wsel),
            pl.BlockSpec((None, None, 1, 2 * width), wsel),
            pl.BlockSpec((None, None, CMP_HIDDEN, Dh), wsel),
            pl.BlockSpec((None, None, Dh, CMP_HIDDEN), wsel),
        ],
        out_specs=[pl.BlockSpec((None, None, G * n, Dh), lambda i, b: (i, b, 0, 0)),
                   pl.BlockSpec((None, None, Dh, G * n), lambda i, b: (i, b, 0, 0))],
        out_shape=[jax.ShapeDtypeStruct((2, B, G * n, Dh), BF16),
                   jax.ShapeDtypeStruct((2, B, Dh, G * n), BF16)],
        scratch_shapes=[pltpu.VMEM((G * n, width), F32)],
        compiler_params=pltpu.CompilerParams(
            dimension_semantics=("parallel", "parallel"), vmem_limit_bytes=VMEM_LIMIT),
        name="compress",
    )(ckv, w1ab, w1, pe, w2, w2t)


def _nsa_kernel(q_ref, kc_ref, vct_ref, ks_ref, vst_ref, kw_ref, vwt_ref, gtt_ref, gain_ref,
                ovt_ref, et_ref, o_ref, *, n_sel, n_top, ck, unroll):
    TQ, R, Dh, G = NSA_TQ, NSA_REP, NSA_HEAD_DIM, NSA_KV_GROUPS
    groups = range(G)
    i = pl.program_id(1)
    t0 = i * TQ
    qs = [q_ref[g * R:(g + 1) * R].reshape(R * TQ, Dh) for g in groups]
    tq1 = t0 + lax.broadcasted_iota(jnp.int32, (1, TQ), 1)
    heads = lambda t: jnp.concatenate([t] * R, axis=1)

    ncmp = kc_ref.shape[0] // G
    wkeys = WINDOW + TQ
    nwb = wkeys // Q_BLOCK
    ws = pl.multiple_of(jnp.maximum(t0 - WINDOW, 0), Q_BLOCK)
    wb0 = ws // Q_BLOCK
    sc = [_dot_nt(kc_ref[g * ncmp:(g + 1) * ncmp, :], qs[g]) for g in groups]
    sw = [_dot_nt(kw_ref[g, pl.ds(ws, wkeys), :], qs[g]) for g in groups]

    kpos = ws + lax.broadcasted_iota(jnp.int32, (wkeys, 1), 0)
    wbias = heads(jnp.where((kpos <= tq1) & (tq1 - kpos < WINDOW), 0.0, NEG_INF))
    ow, l_w = [], []
    for g in groups:
        swb = sw[g] + wbias
        pw = jnp.exp2(swb - jnp.max(swb, axis=0, keepdims=True))
        vwt = jnp.concatenate([vwt_ref[g, wb0 + j] for j in range(nwb)], axis=1)
        owl = _dot(vwt, pw.astype(BF16))
        ow.append(owl[:Dh])
        l_w.append(owl[Dh:Dh + 1])

    cend = lax.broadcasted_iota(jnp.int32, (ncmp, 1), 0) * CMP_STRIDE + (CMP_BLOCK - 1)
    cmask = heads(cend <= tq1)
    pc = []
    for g in groups:
        scm = jnp.where(cmask, sc[g], NEG_INF)
        ec = jnp.where(cmask, jnp.exp2(scm - jnp.max(scm, axis=0, keepdims=True)), 0.0)
        pc.append(ec * (1.0 / jnp.maximum(jnp.sum(ec, axis=0, keepdims=True), 1e-30)))
    oc = [_dot(vct_ref[:, g * ncmp:(g + 1) * ncmp], pc[g].astype(BF16)) for g in groups]

    imp = []
    for g in groups:
        psum = pc[g][:, 0:TQ]
        for r in range(1, R):
            psum = psum + pc[g][:, r * TQ:(r + 1) * TQ]
        imp.append(_dot(ovt_ref[...], psum, precision=HIGHEST))

    blk = lax.broadcasted_iota(jnp.int32, (n_sel, 1), 0)
    valid = blk * SEL_BLOCK <= tq1
    forced = (blk == 0) | (blk == tq1 // SEL_BLOCK)
    selb = []
    for g in groups:
        val = jnp.where(forced, FORCE_SCORE, jnp.where(valid, imp[g], -FORCE_SCORE))
        rank = jnp.zeros((n_sel, TQ), F32)
        for jp in range(n_sel):
            other = val[jp:jp + 1, :]
            wins = jnp.where(blk > jp, jnp.where(other >= val, 1.0, 0.0), jnp.where(other > val, 1.0, 0.0))
            rank = rank + wins
        selb.append(jnp.where(rank < n_top, 0.0, NEG_INF).astype(BF16))

    def scores(g, c, causal):
        k0 = c * ck
        bias = _dot(et_ref[c], selb[g])
        if causal:
            kpos = k0 + lax.broadcasted_iota(jnp.int32, (ck, 1), 0)
            bias = jnp.where(kpos <= tq1, bias, NEG_INF)
        return _dot_nt(ks_ref[g, pl.ds(k0, ck), :], qs[g]) + heads(bias)

    def update(g, c, s, carry):
        m, acc = carry
        mn = jnp.maximum(m, jnp.max(s, axis=0, keepdims=True))
        p = jnp.exp2(s - mn)
        acc = jnp.exp2(m - mn) * acc + _dot(vst_ref[g, c], p.astype(BF16))
        return mn, acc

    def chunk_group(cg, carry, causal):
        cs = [cg * unroll + sub for sub in range(unroll)]
        ss = [[scores(g, c, causal) for g in groups] for c in cs]
        carry = list(carry)
        for c, s in zip(cs, ss):
            for g in groups:
                carry[g] = update(g, c, s[g], carry[g])
        return tuple(carry)

    n_chunks = (t0 + TQ + ck - 1) // ck
    init = tuple((jnp.full((1, R * TQ), NEG_INF, F32), jnp.zeros((VAL_ROWS, R * TQ), F32)) for _ in groups)
    def unrolled(trips):
        def run():
            carry = init
            for cg in range(trips):
                carry = chunk_group(cg, carry, causal=cg == trips - 1)
            return carry
        return run

    max_trips = ks_ref.shape[1] // (ck * unroll)
    sel = lax.switch((n_chunks + unroll - 1) // unroll - 1, [unrolled(t) for t in range(1, max_trips + 1)])

    normed = []
    for g in groups:
        gs = _sigmoid(gtt_ref[g])
        acc_s, l_s = sel[g][1][:Dh], sel[g][1][Dh:Dh + 1]
        for r in range(R):
            ln = slice(r * TQ, (r + 1) * TQ)
            o = (gs[3 * r:3 * r + 1, :] * oc[g][:, ln]
                 + (gs[3 * r + 1:3 * r + 2, :] / l_s[:, ln]) * acc_s[:, ln]
                 + (gs[3 * r + 2:3 * r + 3, :] / l_w[g][:, ln]) * ow[g][:, ln])
            ms = jnp.mean(o * o, axis=0, keepdims=True)
            normed.append(o * lax.rsqrt(ms + EPS) * gain_ref[g * R + r])
    for pair in range(G * R // 2):
        both = jnp.concatenate(normed[2 * pair:2 * pair + 2], axis=0)
        o_ref[:, pair * 2 * Dh:(pair + 1) * 2 * Dh] = jnp.transpose(both).astype(BF16)


def _nsa(aq, cmp_k, cmp_vt, ks, vst, kw, vwt, gates_t, gain, consts, B, S):
    G, H, TQ, Dh = NSA_KV_GROUPS, NSA_HEADS, NSA_TQ, NSA_HEAD_DIM
    nq = S // TQ
    ncmp = S // CMP_STRIDE
    n_sel = S // SEL_BLOCK
    ovt, emat_t, ck = consts
    k_spec = pl.BlockSpec((None, G, S, Dh), lambda b, i: (b, 0, 0, 0))
    return pl.pallas_call(
        functools.partial(_nsa_kernel, n_sel=n_sel, n_top=min(SEL_TOPN, n_sel), ck=ck, unroll=SEL_UNROLL),
        grid=(B, nq),
        in_specs=[
            pl.BlockSpec((None, H, TQ, Dh), lambda b, i: (b, 0, i, 0)),
            pl.BlockSpec((None, None, G * ncmp, Dh), lambda b, i: (0, b, 0, 0)),
            pl.BlockSpec((None, None, Dh, G * ncmp), lambda b, i: (1, b, 0, 0)),
            k_spec,
            pl.BlockSpec((None, G, S // ck, VAL_ROWS, ck), lambda b, i: (b, 0, 0, 0, 0)),
            k_spec,
            pl.BlockSpec((None, G, S // Q_BLOCK, VAL_ROWS, Q_BLOCK), lambda b, i: (b, 0, 0, 0, 0)),
            pl.BlockSpec((G, None, 2 * SUBLANE, TQ), lambda b, i: (0, b, GATE_COL // (2 * SUBLANE), i)),
            pl.BlockSpec((H, Dh, 1), lambda b, i: (0, 0, 0)),
            pl.BlockSpec(ovt.shape, lambda b, i: (0, 0)),
            pl.BlockSpec(emat_t.shape, lambda b, i: (0, 0, 0)),
        ],
        out_specs=pl.BlockSpec((TQ, H * Dh), lambda b, i: (b * nq + i, 0)),
        out_shape=jax.ShapeDtypeStruct((B * S, H * Dh), BF16),
        compiler_params=pltpu.CompilerParams(
            dimension_semantics=("parallel", "arbitrary"), vmem_limit_bytes=VMEM_LIMIT),
        name="nsa",
    )(aq, cmp_k, cmp_vt, ks, vst, kw, vwt, gates_t, gain, ovt, emat_t)


def _nsa_consts(S):
    n_cmp = S // CMP_STRIDE
    n_sel = S // SEL_BLOCK
    ck = 256
    i = np.arange(n_cmp)[:, None]
    j = np.arange(n_sel)[None, :]
    lo = np.maximum(i * CMP_STRIDE, j * SEL_BLOCK)
    hi = np.minimum(i * CMP_STRIDE + CMP_BLOCK, (j + 1) * SEL_BLOCK)
    ov = np.maximum(hi - lo, 0) / CMP_STRIDE
    ov[n_cmp - 1] = 0.0
    key = np.arange(S)
    emat_t = (key[:, None] // SEL_BLOCK == np.arange(n_sel)[None, :]).astype(np.float32)
    return (jnp.asarray(ov.T, F32), jnp.asarray(emat_t.reshape(S // ck, ck, n_sel), BF16), ck)


def _outproj_kernel(h_ref, ys_ref, ym_ref, yn_ref, w_ref, g_ref, o_ref):
    acc = _dot(ys_ref[...], w_ref[0:SSM_WIDTH, :])
    acc = acc + _dot(ym_ref[...], w_ref[SSM_WIDTH:SSM_WIDTH + MLSTM_WIDTH, :])
    acc = acc + _dot(yn_ref[...], w_ref[SSM_WIDTH + MLSTM_WIDTH:, :])
    ms = jnp.mean(acc * acc, axis=-1, keepdims=True)
    o_ref[...] = h_ref[...] + acc * lax.rsqrt(ms + EPS) * g_ref[...]


def _outproj(h2, y_ssm, y_mls, y_nsa, w, layer, gain, B, S, ts):
    nt = S // ts
    row = lambda b, i: (b * nt + i, 0)
    full = lambda b, i: (0, 0)
    return pl.pallas_call(
        _outproj_kernel,
        grid=(B, nt),
        in_specs=[
            pl.BlockSpec((ts, D_MODEL), row),
            pl.BlockSpec((ts, SSM_WIDTH), row),
            pl.BlockSpec((ts, MLSTM_WIDTH), row),
            pl.BlockSpec((ts, NSA_WIDTH), row),
            pl.BlockSpec((None, D_MODEL, D_MODEL), lambda b, i: (layer, 0, 0)),
            pl.BlockSpec((1, D_MODEL), full),
        ],
        out_specs=pl.BlockSpec((ts, D_MODEL), row),
        out_shape=jax.ShapeDtypeStruct((B * S, D_MODEL), F32),
        compiler_params=pltpu.CompilerParams(
            dimension_semantics=("parallel", "parallel"), vmem_limit_bytes=VMEM_LIMIT),
        name="outproj",
    )(h2, y_ssm, y_mls, y_nsa, w, gain)


def _mlp_kernel(h_ref, g1_ref, w1_ref, w2_ref, g2_ref, o_ref, u_sc, acc_sc):
    kf = pl.program_id(1)

    @pl.when(kf == 0)
    def _():
        x = h_ref[...]
        ms = jnp.mean(x * x, axis=-1, keepdims=True)
        u_sc[...] = (x * lax.rsqrt(ms + EPS) * g1_ref[...]).astype(BF16)
        acc_sc[...] = jnp.zeros_like(acc_sc)

    a = jnp.maximum(_dot(u_sc[...], w1_ref[...].astype(BF16)), 0.0)
    acc_sc[...] += _dot((a * a).astype(BF16), w2_ref[...].astype(BF16))

    @pl.when(kf == pl.num_programs(1) - 1)
    def _():
        f = acc_sc[...]
        ms = jnp.mean(f * f, axis=-1, keepdims=True)
        o_ref[...] = h_ref[...] + f * lax.rsqrt(ms + EPS) * g2_ref[...]


def _mlp(h2, g1, w1, w2, layer, g2, tm, tf):
    rows = h2.shape[0]
    return pl.pallas_call(
        _mlp_kernel,
        grid=(rows // tm, D_FF // tf),
        in_specs=[
            pl.BlockSpec((tm, D_MODEL), lambda i, k: (i, 0)),
            pl.BlockSpec((1, D_MODEL), lambda i, k: (0, 0)),
            pl.BlockSpec((None, D_MODEL, tf), lambda i, k: (layer, 0, k)),
            pl.BlockSpec((None, tf, D_MODEL), lambda i, k: (layer, k, 0)),
            pl.BlockSpec((1, D_MODEL), lambda i, k: (0, 0)),
        ],
        out_specs=pl.BlockSpec((tm, D_MODEL), lambda i, k: (i, 0)),
        out_shape=jax.ShapeDtypeStruct((rows, D_MODEL), F32),
        scratch_shapes=[pltpu.VMEM((tm, D_MODEL), BF16), pltpu.VMEM((tm, D_MODEL), F32)],
        compiler_params=pltpu.CompilerParams(
            dimension_semantics=("parallel", "arbitrary"), vmem_limit_bytes=VMEM_LIMIT),
        name="mlp",
    )(h2, g1, w1, w2, g2)


def _inproj_pieces():
    return ((0, 1280), (1288, 1800), (1800, 1928), (2056, 2184), (2312, 2440),
            (1928, 2056), (2184, 2312), (2440, 2568),
            (1280, 1288), (None, GATE_COL - 8), (2568, 2580), (None, LANE - GATE_COL - 12),
            (None, GATE_COL), (2580, 2592), (None, LANE - GATE_COL - 12))


def _permute_w_in_kernel(w_ref, o_ref):
    x = w_ref[...]
    col = 0
    for a, b in _inproj_pieces():
        width = b if a is None else b - a
        piece = jnp.zeros((x.shape[0], width), BF16) if a is None else x[:, a:b].astype(BF16)
        o_ref[:, col:col + width] = piece
        col += width
    assert col == D_INP


def _permute_w_in(w_in):
    depth, rows, cols = w_in.shape
    tr = 256
    return pl.pallas_call(
        _permute_w_in_kernel,
        grid=(depth, rows // tr),
        in_specs=[pl.BlockSpec((None, tr, cols), lambda l, i: (l, i, 0))],
        out_specs=pl.BlockSpec((None, tr, D_INP), lambda l, i: (l, i, 0)),
        out_shape=jax.ShapeDtypeStruct((depth, rows, D_INP), BF16),
        compiler_params=pltpu.CompilerParams(
            dimension_semantics=("parallel", "parallel"), vmem_limit_bytes=VMEM_LIMIT),
        name="permute_w_in",
    )(w_in)


def _rope_tables(positions):
    inv = ROPE_THETA ** (-jnp.arange(0, ROPE_DIMS, 2, dtype=F32) / ROPE_DIMS)
    ang = positions.astype(F32)[..., None] * inv
    ones = jnp.ones(ang.shape[:-1] + (1,), F32)
    feats = jnp.concatenate([jnp.cos(ang), jnp.sin(ang), ones], axis=-1).reshape(-1, 2 * ROPE_HALF + 1)
    place = np.zeros((2 * ROPE_HALF + 1, 3 * LANE), np.float32)
    for lane in range(LANE):
        d = lane % NSA_HEAD_DIM
        if d < ROPE_HALF:
            place[d, lane] = 1.0
            place[ROPE_HALF + d, LANE + lane] = -1.0
        elif d < ROPE_DIMS:
            place[d - ROPE_HALF, lane] = 1.0
            place[d, 2 * LANE + lane] = 1.0
        else:
            place[2 * ROPE_HALF, lane] = 1.0
    return jnp.dot(feats, jnp.asarray(place), precision=HIGHEST)


def _s5_params(lam_re, lam_im, b_re, b_im, c_re, c_im, log_dt):
    G, P, Hc = SSM_GROUPS, SSM_STATE, SSM_GROUP
    dt = jnp.exp(log_dt)[:, None]
    mag = jnp.exp(lam_re * dt)
    ang = lam_im * dt
    ab_re = mag * jnp.cos(ang)
    ab_im = mag * jnp.sin(ang)
    den = lam_re * lam_re + lam_im * lam_im
    g_re = ((ab_re - 1.0) * lam_re + ab_im * lam_im) / den
    g_im = (ab_im * lam_re - (ab_re - 1.0) * lam_im) / den
    bb_re = g_re[..., None] * b_re - g_im[..., None] * b_im
    bb_im = g_re[..., None] * b_im + g_im[..., None] * b_re
    eye = jnp.eye(G, dtype=F32)
    blockdiag_in = lambda t: jnp.einsum('gph,gk->ghkp', t, eye).reshape(G * Hc, G * P)
    blockdiag_out = lambda t: jnp.einsum('ghp,gk->gpkh', t, eye).reshape(G * P, G * Hc)
    bb = jnp.concatenate([blockdiag_in(bb_re), blockdiag_in(bb_im)], axis=1).astype(BF16)
    cc = jnp.concatenate([blockdiag_out(c_re), -blockdiag_out(c_im)], axis=0).astype(BF16)
    a = jnp.stack([ab_re.reshape(-1), ab_im.reshape(-1)], axis=0)
    return bb, a, cc


def _group_mean_matrix(width, group):
    idx = np.arange(width) // group
    return jnp.asarray((idx[:, None] == idx[None, :]).astype(np.float32) / group, BF16)


def kernel(x, positions, ln_mix_pre, ln_mix_post, ln_mlp_pre, ln_mlp_post, w_in, w_out, ssm_lambda_re, ssm_lambda_im, ssm_b_re, ssm_b_im, ssm_c_re, ssm_c_im, ssm_d, ssm_log_dt, ssm_w_glu, mlstm_conv, mlstm_b_i, mlstm_b_f, cmp_pe_k, cmp_w1_k, cmp_w2_k, cmp_pe_v, cmp_w1_v, cmp_w2_v, gn_ssm, gn_mlstm, gn_nsa, mlp_w1, mlp_w2):
    B, S, D = x.shape
    depth = w_in.shape[0]
    assert D == D_MODEL and B == SUBLANE and S % 512 == 0 and S >= WINDOW + NSA_TQ
    G, H = NSA_KV_GROUPS, MLSTM_HEADS
    ts_proj = 1024
    ts_scan = 128

    rope = _rope_tables(positions)
    w_in_p = _permute_w_in(w_in)
    w_out_b = w_out.astype(BF16)
    wglu_b = ssm_w_glu.astype(BF16)
    gm_ssm = _group_mean_matrix(SSM_WIDTH, SSM_GROUP)
    hm_mls = _group_mean_matrix(MLSTM_WIDTH, MLSTM_HEAD_DIM)
    consts = _nsa_consts(S)
    half = CMP_STRIDE * NSA_HEAD_DIM
    w1ab = jnp.stack([jnp.concatenate([cmp_w1_k[:, :half], cmp_w1_k[:, half:]], axis=-1),
                      jnp.concatenate([cmp_w1_v[:, :half], cmp_w1_v[:, half:]], axis=-1)], axis=1).astype(BF16)
    w1f = jnp.stack([cmp_w1_k, cmp_w1_v], axis=1)
    pef = jnp.stack([cmp_pe_k.reshape(depth, 1, -1), cmp_pe_v.reshape(depth, 1, -1)], axis=1)
    w2c = jnp.stack([cmp_w2_k, cmp_w2_v], axis=1).astype(BF16)
    w2ct = jnp.swapaxes(w2c, -1, -2)
    bias_row = jnp.concatenate([mlstm_b_i, mlstm_b_f], axis=-1)[:, :, None]

    bb, a, cc = jax.vmap(_s5_params)(ssm_lambda_re, ssm_lambda_im, ssm_b_re, ssm_b_im, ssm_c_re, ssm_c_im,
                                     ssm_log_dt)
    sh3 = lambda t: t.reshape(B, S, t.shape[-1])

    h = x.reshape(B * S, D)
    for l in range(depth):
        (su, mq, mk, mv, mo, aq, ckv, sk, wk, svt, wvt, gates, gates_t) = _inproj(
            h, ln_mix_pre[l][None], w_in_p, l, rope, B, S, ts_proj, consts[2])

        y_ssm = _s5(sh3(su), bb, a, cc, ssm_d[l][None], wglu_b, l, gm_ssm, gn_ssm[l][None], B, S, ts_scan)

        y_mls = _mlstm(sh3(mq), sh3(mk), sh3(mv), sh3(mo), gates_t,
                       mlstm_conv[l][:, :MLSTM_WIDTH], mlstm_conv[l][:, MLSTM_WIDTH:],
                       bias_row[l], hm_mls, gn_mlstm[l][None], B, S)

        cmp_k, cmp_t = _compress(ckv, w1ab, w1f, pef, w2c, w2ct, l, B, S)
        y_nsa = _nsa(aq, cmp_k, cmp_t, sk, svt, wk, wvt, gates_t,
                     gn_nsa[l].reshape(NSA_HEADS, NSA_HEAD_DIM, 1), consts, B, S)

        h = _outproj(h, y_ssm.reshape(B * S, SSM_WIDTH), y_mls.reshape(B * S, MLSTM_WIDTH), y_nsa,
                     w_out_b, l, ln_mix_post[l][None], B, S, ts_proj)
        h = _mlp(h, ln_mlp_pre[l][None], mlp_w1, mlp_w2, l, ln_mlp_post[l][None], 1024, 1024)
    return h.reshape(B, S, D)
```

```python
import functools
import math

import numpy as np
import jax
import jax.numpy as jnp
from jax import lax
from jax.experimental import pallas as pl
from jax.experimental.pallas import tpu as pltpu

F32 = jnp.float32
BF16 = jnp.bfloat16
HIGHEST = lax.Precision.HIGHEST

D_MODEL = 1024
DEPTH = 4
SSM_WIDTH = 256
SSM_GROUP = 16
SSM_GROUPS = 16
SSM_STATE = 64
SSM_LANES = SSM_GROUPS * SSM_STATE
MLSTM_WIDTH = 256
MLSTM_HEADS = 4
MLSTM_HEAD_DIM = 64
MLSTM_CHUNK = 128
MLSTM_CONV = 4
NSA_WIDTH = 512
NSA_HEAD_DIM = 64
NSA_HEADS = 8
NSA_KV_GROUPS = 2
NSA_REP = NSA_HEADS // NSA_KV_GROUPS
NSA_KV_WIDTH = 128
CMP_BLOCK = 32
CMP_STRIDE = 16
CMP_HIDDEN = 256
SEL_BLOCK = 64
SEL_TOPN = 8
WINDOW = 256
Q_BLOCK = 128
FORCE_SCORE = 1e4
NEG_INF = -1e30
ROPE_THETA = 500000.0
ROPE_DIMS = 16
ROPE_HALF = 8
D_FF = 4096
EPS = 1e-6
D_IN = 2592

LANE = 128
SUBLANE = 8
VMEM_LIMIT = 56 * 1024 * 1024

C_SU, C_MQ, C_MK, C_MV, C_MO = 0, 256, 512, 768, 1024
C_AQ, C_CK, C_SK, C_WK = 1280, 1792, 1920, 2048
C_CV, C_SV, C_WV = 2176, 2304, 2432
C_G0, C_G1 = 2560, 2688
D_INP = 2816
GATE_COL = 16
VAL_ROWS = NSA_HEAD_DIM + 16
Q_SCALE = NSA_HEAD_DIM ** -0.5 * math.log2(math.e)
MLSTM_ROWS = 8
NSA_TQ = 256
SEL_UNROLL = 2


def _dot(a, b, precision=None):
    return jnp.dot(a, b, preferred_element_type=F32, precision=precision)


def _dot_nt(a, b):
    return lax.dot_general(a, b, (((1,), (1,)), ((), ())), preferred_element_type=F32)


def _dot_tn(a, b):
    return lax.dot_general(a, b, (((0,), (0,)), ((), ())), preferred_element_type=F32)


def _sigmoid(x):
    return 1.0 / (1.0 + jnp.exp(-x))


def _dot_split(x, w_bf16):
    hi = x.astype(BF16)
    lo = (x - hi.astype(F32)).astype(BF16)
    return _dot(hi, w_bf16) + _dot(lo, w_bf16)


def _gelu_tanh(x):
    return 0.5 * x * (1.0 + jnp.tanh(math.sqrt(2.0 / math.pi) * (x + 0.044715 * (x * x * x))))


def _log_sigmoid(x):
    return jnp.minimum(x, 0.0) - jnp.log(1.0 + jnp.exp(-jnp.abs(x)))


def _inproj_kernel(x_ref, g_ref, w_ref, rt_ref,
                   su_ref, mq_ref, mk_ref, mv_ref, mo_ref, aq_ref, ckv_ref, sk_ref, wk_ref,
                   svt_ref, wvt_ref, gt_ref, gtt_ref):
    x = x_ref[...]
    ms = jnp.mean(x * x, axis=-1, keepdims=True)
    u = (x * lax.rsqrt(ms + EPS) * g_ref[...]).astype(BF16)
    rc, rs1, rs2 = (rt_ref[:, n * LANE:(n + 1) * LANE] for n in range(3))

    def mm(c0, width):
        return _dot(u, w_ref[:, c0:c0 + width])

    def rope(z):
        return z * rc + pltpu.roll(z, LANE - ROPE_HALF, 1) * rs1 + pltpu.roll(z, ROPE_HALF, 1) * rs2

    su_ref[...] = mm(C_SU, 256)
    mq_ref[...] = mm(C_MQ, 256)
    mk_ref[...] = mm(C_MK, 256)
    mv_ref[...] = mm(C_MV, 256).astype(BF16)
    mo_ref[...] = mm(C_MO, 256)
    def mm_pair(c0):
        z = mm(c0, 2 * LANE)
        return z[:, :LANE], z[:, LANE:]

    def put_heads(ref, first, z):
        ref[first] = z[:, :NSA_HEAD_DIM].astype(BF16)
        ref[first + 1] = z[:, NSA_HEAD_DIM:].astype(BF16)

    for j in range(NSA_HEADS // 4):
        for k, z in enumerate(mm_pair(C_AQ + 2 * LANE * j)):
            put_heads(aq_ref, 4 * j + 2 * k, rope(z) * Q_SCALE)
    z_ck, z_sk = mm_pair(C_CK)
    z_wk, z_cv = mm_pair(C_WK)
    z_sv, z_wv = mm_pair(C_SV)
    ckv_ref[0] = rope(z_ck)
    ckv_ref[1] = z_cv
    put_heads(sk_ref, 0, rope(z_sk))
    put_heads(wk_ref, 0, rope(z_wk))

    def put_chunks_t(ref, z):
        zt = jnp.transpose(z)
        width = ref.shape[-1]
        ones = jnp.ones((VAL_ROWS - NSA_HEAD_DIM, width), BF16)
        for g in range(NSA_KV_GROUPS):
            for j in range(ref.shape[1]):
                ref[g, j, :NSA_HEAD_DIM] = zt[g * NSA_HEAD_DIM:(g + 1) * NSA_HEAD_DIM,
                                              j * width:(j + 1) * width].astype(BF16)
                ref[g, j, NSA_HEAD_DIM:] = ones

    put_chunks_t(svt_ref, z_sv)
    put_chunks_t(wvt_ref, z_wv)
    z_g0, z_g1 = mm_pair(C_G0)
    gt_ref[...] = z_g0
    gtt_ref[0] = jnp.transpose(z_g0)
    gtt_ref[1] = jnp.transpose(z_g1)


def _inproj(h2, gain, w, layer, rope, B, S, ts, ck):
    nt = S // ts
    BS = B * S
    row = lambda b, i: (b * nt + i, 0)
    full = lambda b, i: (0, 0)
    headed = lambda b, i: (b, 0, i, 0)
    paired = lambda b, i: (0, b * nt + i, 0)
    in_specs = [
        pl.BlockSpec((ts, D_MODEL), row),
        pl.BlockSpec((1, D_MODEL), full),
        pl.BlockSpec((None, D_MODEL, D_INP), lambda b, i: (layer, 0, 0)),
        pl.BlockSpec((ts, 3 * LANE), row),
    ]
    kv_shape = jax.ShapeDtypeStruct((B, NSA_KV_GROUPS, S, NSA_HEAD_DIM), BF16)
    kv_spec = pl.BlockSpec((None, NSA_KV_GROUPS, ts, NSA_HEAD_DIM), headed)
    out_shape = [
        jax.ShapeDtypeStruct((BS, SSM_WIDTH), F32),
        jax.ShapeDtypeStruct((BS, MLSTM_WIDTH), F32),
        jax.ShapeDtypeStruct((BS, MLSTM_WIDTH), F32),
        jax.ShapeDtypeStruct((BS, MLSTM_WIDTH), BF16),
        jax.ShapeDtypeStruct((BS, MLSTM_WIDTH), F32),
        jax.ShapeDtypeStruct((B, NSA_HEADS, S, NSA_HEAD_DIM), BF16),
        jax.ShapeDtypeStruct((2, BS, NSA_KV_WIDTH), F32),
        kv_shape, kv_shape,
        jax.ShapeDtypeStruct((B, NSA_KV_GROUPS, S // ck, VAL_ROWS, ck), BF16),
        jax.ShapeDtypeStruct((B, NSA_KV_GROUPS, S // Q_BLOCK, VAL_ROWS, Q_BLOCK), BF16),
        jax.ShapeDtypeStruct((BS, LANE), F32),
        jax.ShapeDtypeStruct((NSA_KV_GROUPS, B, LANE, S), F32),
    ]
    out_specs = [
        pl.BlockSpec((ts, SSM_WIDTH), row),
        pl.BlockSpec((ts, MLSTM_WIDTH), row),
        pl.BlockSpec((ts, MLSTM_WIDTH), row),
        pl.BlockSpec((ts, MLSTM_WIDTH), row),
        pl.BlockSpec((ts, MLSTM_WIDTH), row),
        pl.BlockSpec((None, NSA_HEADS, ts, NSA_HEAD_DIM), headed),
        pl.BlockSpec((2, ts, NSA_KV_WIDTH), paired),
        kv_spec, kv_spec,
        pl.BlockSpec((None, NSA_KV_GROUPS, ts // ck, VAL_ROWS, ck), lambda b, i: (b, 0, i, 0, 0)),
        pl.BlockSpec((None, NSA_KV_GROUPS, ts // Q_BLOCK, VAL_ROWS, Q_BLOCK), lambda b, i: (b, 0, i, 0, 0)),
        pl.BlockSpec((ts, LANE), row),
        pl.BlockSpec((NSA_KV_GROUPS, None, LANE, ts), lambda b, i: (0, b, 0, i)),
    ]
    return pl.pallas_call(
        _inproj_kernel,
        grid=(B, nt),
        in_specs=in_specs,
        out_specs=out_specs,
        out_shape=out_shape,
        compiler_params=pltpu.CompilerParams(
            dimension_semantics=("parallel", "parallel"), vmem_limit_bytes=VMEM_LIMIT),
        name="inproj",
    )(h2, gain, w, rope)


def _s5_kernel(u_ref, bb_ref, a_ref, cc_ref, d_ref, wg_ref, gm_ref, gain_ref, o_ref, x_sc, st_sc, tm_sc, *, B, ts):
    @pl.when(pl.program_id(0) == 0)
    def _():
        st_sc[...] = jnp.zeros_like(st_sc)

    nl = SSM_WIDTH // LANE
    for b in range(B):
        for c in range(nl):
            tm_sc[c, pl.ds(b, ts, stride=B), :] = u_ref[b, :, c * LANE:(c + 1) * LANE]
    u = jnp.concatenate([tm_sc[c] for c in range(nl)], axis=1)
    ub = u.astype(BF16)
    for part in range(2):
        cols = slice(part * SSM_LANES, (part + 1) * SSM_LANES)
        x_sc[:, cols] = _dot(ub, bb_ref[:, cols])
    ar = jnp.broadcast_to(a_ref[0:1, :], (B, SSM_LANES))
    ai = jnp.broadcast_to(a_ref[1:2, :], (B, SSM_LANES))

    def step(t, carry):
        xr, xi = carry
        r = pl.multiple_of(t * B, B)
        br = x_sc[pl.ds(r, B), 0:SSM_LANES]
        bi = x_sc[pl.ds(r, B), SSM_LANES:2 * SSM_LANES]
        nr = ar * xr - ai * xi + br
        ni = ar * xi + ai * xr + bi
        x_sc[pl.ds(r, B), 0:SSM_LANES] = nr
        x_sc[pl.ds(r, B), SSM_LANES:2 * SSM_LANES] = ni
        return nr, ni

    xr, xi = lax.fori_loop(0, ts, step, (st_sc[0], st_sc[1]))
    st_sc[0] = xr
    st_sc[1] = xi

    half = (ts * B) // 2
    y = jnp.concatenate([_dot(x_sc[r * half:(r + 1) * half, :].astype(BF16), cc_ref[...]) for r in range(2)],
                        axis=0) + d_ref[...] * u
    y = _gelu_tanh(y)
    y = y * _sigmoid(_dot(y.astype(BF16), wg_ref[...]))
    ms = _dot_split(y * y, gm_ref[...])
    y = y * lax.rsqrt(ms + EPS) * gain_ref[...]
    for c in range(nl):
        tm_sc[c] = y[:, c * LANE:(c + 1) * LANE]
    for b in range(B):
        o_ref[b] = jnp.concatenate(
            [tm_sc[c, pl.ds(b, ts, stride=B), :] for c in range(nl)], axis=1).astype(BF16)


def _s5(u, bb, a, cc, d, wg, layer, gm, gain, B, S, ts):
    rows = ts * B
    full = lambda i: (0, 0)
    lsel = lambda i: (layer, 0, 0)
    return pl.pallas_call(
        functools.partial(_s5_kernel, B=B, ts=ts),
        grid=(S // ts,),
        in_specs=[
            pl.BlockSpec((B, ts, SSM_WIDTH), lambda i: (0, i, 0)),
            pl.BlockSpec((None, SSM_WIDTH, 2 * SSM_LANES), lsel),
            pl.BlockSpec((None, 2, SSM_LANES), lsel),
            pl.BlockSpec((None, 2 * SSM_LANES, SSM_WIDTH), lsel),
            pl.BlockSpec((1, SSM_WIDTH), full),
            pl.BlockSpec((None, SSM_WIDTH, SSM_WIDTH), lsel),
            pl.BlockSpec((SSM_WIDTH, SSM_WIDTH), full),
            pl.BlockSpec((1, SSM_WIDTH), full),
        ],
        out_specs=pl.BlockSpec((B, ts, SSM_WIDTH), lambda i: (0, i, 0)),
        out_shape=jax.ShapeDtypeStruct((B, S, SSM_WIDTH), BF16),
        scratch_shapes=[pltpu.VMEM((rows, 2 * SSM_LANES), F32), pltpu.VMEM((2, B, SSM_LANES), F32),
                        pltpu.VMEM((SSM_WIDTH // LANE, rows, LANE), F32)],
        compiler_params=pltpu.CompilerParams(
            dimension_semantics=("arbitrary",), vmem_limit_bytes=VMEM_LIMIT),
        name="s5",
    )(u, bb, a, cc, d, wg, gm, gain)


def _mlstm_kernel(q_ref, k_ref, v_ref, o_ref, gr_ref, cwq_ref, cwk_ref, br_ref, hm_ref,
                  gain_ref, y_ref, qt_sc, kt_sc, c_sc, m_sc, *, B):
    L, H, Dh, W = MLSTM_CHUNK, MLSTM_HEADS, MLSTM_HEAD_DIM, MLSTM_WIDTH

    @pl.when(pl.program_id(0) == 0)
    def _():
        qt_sc[...] = jnp.zeros_like(qt_sc)
        kt_sc[...] = jnp.zeros_like(kt_sc)
        c_sc[...] = jnp.zeros_like(c_sc)
        m_sc[...] = jnp.zeros_like(m_sc)

    visible = lax.broadcasted_iota(jnp.int32, (L, L), 0) <= lax.broadcasted_iota(jnp.int32, (L, L), 1)
    triu = visible.astype(F32)
    lane_w = lax.broadcasted_iota(jnp.int32, (1, W), 1) // Dh
    bd_mask = ((lax.broadcasted_iota(jnp.int32, (2 * W, W), 0) % W) // Dh
               == lax.broadcasted_iota(jnp.int32, (2 * W, W), 1) // Dh)
    row8 = lax.broadcasted_iota(jnp.int32, (SUBLANE, W), 0)
    cwq = cwq_ref[...]
    cwk = cwk_ref[...]
    ones_rows = jnp.ones((Dh, L), F32)

    def conv_silu(x, tail, w):
        acc = x * w[MLSTM_CONV - 1:MLSTM_CONV, :]
        for sft in range(1, MLSTM_CONV):
            xs = pltpu.roll(x, sft, 0)
            head = jnp.where(row8 < sft, pltpu.roll(tail, sft, 0), xs[:SUBLANE])
            xs = jnp.concatenate([head, xs[SUBLANE:]], axis=0)
            acc = acc + xs * w[MLSTM_CONV - 1 - sft:MLSTM_CONV - sft, :]
        return acc * _sigmoid(acc)

    def per_group(grp, _):
        bs = [grp * MLSTM_ROWS + n for n in range(MLSTM_ROWS)]
        st = [dict() for _ in bs]

        for b, d in zip(bs, st):
            q_raw = q_ref[b]
            k_raw = k_ref[b]
            d['q'] = conv_silu(q_raw, qt_sc[b], cwq)
            d['k'] = conv_silu(k_raw, kt_sc[b], cwk) * (Dh ** -0.5)
            qt_sc[b] = q_raw[L - SUBLANE:, :]
            kt_sc[b] = k_raw[L - SUBLANE:, :]
            d['gr'] = gr_ref[b] + br_ref[...]
        for d in st:
            d['brow'] = _dot(_log_sigmoid(d['gr']), triu, precision=HIGHEST)

        for b, d in zip(bs, st):
            gr, brow = d['gr'], d['brow']
            ccol = jnp.transpose(brow - pltpu.roll(gr, H, 0))
            m_all = m_sc[b]
            for key in ('w_intra', 'w_inter', 'e_mt', 'w_k', 'dec', 'm_new'):
                d[key] = []
            for hh in range(H):
                b_r = brow[H + hh:H + hh + 1, :]
                i_r = gr[hh:hh + 1, :]
                m_prev = m_all[hh:hh + 1, 0:1]
                dm = jnp.where(visible, b_r - ccol[:, H + hh:H + hh + 1], NEG_INF)
                inter = b_r + m_prev
                mt = jnp.maximum(inter, jnp.max(dm, axis=0, keepdims=True))
                d['w_intra'].append(jnp.exp(dm - mt))
                d['w_inter'].append(jnp.exp(inter - mt))
                d['e_mt'].append(jnp.exp(-mt))
                b_last = b_r[:, L - 1:L]
                logw = b_last - b_r + i_r
                mn = jnp.maximum(b_last + m_prev, jnp.max(logw, axis=1, keepdims=True))
                d['w_k'].append(jnp.exp(logw - mn))
                d['dec'].append(jnp.exp(b_last + m_prev - mn))
                d['m_new'].append(mn)
            d['qb'] = d['q'].astype(BF16)
            d['kb'] = d['k'].astype(BF16)
            d['vt'] = jnp.transpose(v_ref[b].astype(F32))
            d['c_t'] = c_sc[b]

        for d in st:
            d['qc'] = _dot(d['c_t'].astype(BF16), jnp.transpose(d['q']).astype(BF16))
            d['s_t'] = [_dot_nt(d['kb'], jnp.where(lane_w == hh, d['qb'], jnp.zeros_like(d['qb'])))
                        for hh in range(H)]
        for d in st:
            d['r'] = []
            for hh in range(H):
                v_aug = jnp.concatenate([d['vt'][hh * Dh:(hh + 1) * Dh], ones_rows], axis=0).astype(BF16)
                d['r'].append(_dot(v_aug, (d['s_t'][hh] * d['w_intra'][hh]).astype(BF16)))

        for b, d in zip(bs, st):
            h_t = []
            for hh in range(H):
                ch = slice(hh * Dh, (hh + 1) * Dh)
                num = d['w_inter'][hh] * d['qc'][ch] + d['r'][hh][:Dh]
                den = d['w_inter'][hh] * d['qc'][W + hh * Dh:W + (hh + 1) * Dh] + d['r'][hh][Dh:]
                h_t.append(num / jnp.maximum(jnp.abs(den), d['e_mt'][hh]))
            hout = jnp.transpose(jnp.concatenate(h_t, axis=0))
            d['y'] = _sigmoid(o_ref[b]) * hout
            d['vw'] = jnp.concatenate(
                [d['vt'][hh * Dh:(hh + 1) * Dh] * d['w_k'][hh] for hh in range(H)]
                + [jnp.broadcast_to(d['w_k'][hh], (Dh, L)) for hh in range(H)], axis=0).astype(BF16)
        for d in st:
            d['ms'] = _dot_split(d['y'] * d['y'], hm_ref[...])
            d['upd'] = _dot(d['vw'], d['kb'])

        for b, d in zip(bs, st):
            y_ref[b] = (d['y'] * lax.rsqrt(d['ms'] + EPS) * gain_ref[...]).astype(BF16)
            decay = d['dec'][H - 1]
            for hh in range(H - 2, -1, -1):
                decay = jnp.where(lane_w == hh, d['dec'][hh], decay)
            c_sc[b] = decay * d['c_t'] + jnp.where(bd_mask, d['upd'], 0.0)
            for hh in range(H):
                m_sc[b, hh:hh + 1, :] = jnp.broadcast_to(d['m_new'][hh], (1, LANE))
        return 0

    lax.fori_loop(0, B // MLSTM_ROWS, per_group, 0)


def _mlstm(mq, mk, mv, mo, grow, cwq, cwk, brow, hm, gain, B, S):
    L, W = MLSTM_CHUNK, MLSTM_WIDTH
    seq = lambda c: (0, c, 0)
    full = lambda c: (0, 0)
    return pl.pallas_call(
        functools.partial(_mlstm_kernel, B=B),
        grid=(S // L,),
        in_specs=[
            pl.BlockSpec((B, L, W), seq),
            pl.BlockSpec((B, L, W), seq),
            pl.BlockSpec((B, L, W), seq),
            pl.BlockSpec((B, L, W), seq),
            pl.BlockSpec((None, B, SUBLANE, L), lambda c: (0, 0, 0, c)),
            pl.BlockSpec((MLSTM_CONV, W), full),
            pl.BlockSpec((MLSTM_CONV, W), full),
            pl.BlockSpec((SUBLANE, 1), full),
            pl.BlockSpec((W, W), full),
            pl.BlockSpec((1, W), full),
        ],
        out_specs=pl.BlockSpec((B, L, W), seq),
        out_shape=jax.ShapeDtypeStruct((B, S, W), BF16),
        scratch_shapes=[
            pltpu.VMEM((B, SUBLANE, W), F32),
            pltpu.VMEM((B, SUBLANE, W), F32),
            pltpu.VMEM((B, 2 * W, W), F32),
            pltpu.VMEM((B, SUBLANE, LANE), F32),
        ],
        compiler_params=pltpu.CompilerParams(
            dimension_semantics=("arbitrary",), vmem_limit_bytes=VMEM_LIMIT),
        name="mlstm",
    )(mq, mk, mv, mo, grow, cwq, cwk, brow, hm, gain)


def _compress_kernel(c_ref, w1ab_ref, w1_ref, pe_ref, w2_ref, w2t_ref, o_ref, ot_ref, ch_sc):
    G, Dh = NSA_KV_GROUPS, NSA_HEAD_DIM
    rows = ch_sc.shape[1]
    n = rows // G
    both = range(2)
    for i in both:
        for r in range(CMP_STRIDE):
            tok = c_ref[i, pl.ds(r, n, stride=CMP_STRIDE), :]
            for g in range(G):
                ch_sc[i, g * n:(g + 1) * n, r * Dh:(r + 1) * Dh] = tok[:, g * Dh:(g + 1) * Dh]
    ab = [_dot(ch_sc[i].astype(BF16), w1ab_ref[i]) for i in both]
    const = [_dot(pe_ref[i], w1_ref[i], precision=HIGHEST) for i in both]
    act = []
    for i in both:
        hid = ab[i][:, :CMP_HIDDEN] + pltpu.roll(ab[i][:, CMP_HIDDEN:], rows - 1, 0) + const[i]
        act.append(_gelu_tanh(hid).astype(BF16))
    for i in both:
        o_ref[i] = _dot(act[i], w2_ref[i]).astype(BF16)
        ot_ref[i] = _dot_nt(w2t_ref[i], act[i]).astype(BF16)


def _compress(ckv, w1ab, w1, pe, w2, w2t, layer, B, S):
    G, Dh = NSA_KV_GROUPS, NSA_HEAD_DIM
    n = S // CMP_STRIDE
    width = CMP_STRIDE * Dh
    wsel = lambda b: (layer, 0, 0, 0)
    return pl.pallas_call(
        _compress_kernel,
        grid=(B,),
        in_specs=[
            pl.BlockSpec((2, S, G * Dh), lambda b: (0, b, 0)),
            pl.BlockSpec((None, 2, width, 2 * CMP_HIDDEN), wsel),
            pl.BlockSpec((None, 2, 2 * width, CMP_HIDDEN), wsel),
            pl.BlockSpec((None, 2, 1, 2 * width), wsel),
            pl.BlockSpec((None, 2, CMP_HIDDEN, Dh), wsel),
            pl.BlockSpec((None, 2, Dh, CMP_HIDDEN), wsel),
        ],
        out_specs=[pl.BlockSpec((2, None, G * n, Dh), lambda b: (0, b, 0, 0)),
                   pl.BlockSpec((2, None, Dh, G * n), lambda b: (0, b, 0, 0))],
        out_shape=[jax.ShapeDtypeStruct((2, B, G * n, Dh), BF16),
                   jax.ShapeDtypeStruct((2, B, Dh, G * n), BF16)],
        scratch_shapes=[pltpu.VMEM((2, G * n, width), F32)],
        compiler_params=pltpu.CompilerParams(
            dimension_semantics=("parallel",), vmem_limit_bytes=VMEM_LIMIT),
        name="compress",
    )(ckv, w1ab, w1, pe, w2, w2t)


def _nsa_kernel(q_ref, kc_ref, vct_ref, ks_ref, vst_ref, kw_ref, vwt_ref, gtt_ref, gain_ref,
                ovt_ref, et_ref, o_ref, *, n_sel, n_top, ck, unroll):
    TQ, R, Dh, G = NSA_TQ, NSA_REP, NSA_HEAD_DIM, NSA_KV_GROUPS
    groups = range(G)
    i = pl.program_id(1)
    t0 = i * TQ
    qs = [q_ref[g * R:(g + 1) * R].reshape(R * TQ, Dh) for g in groups]
    tq1 = t0 + lax.broadcasted_iota(jnp.int32, (1, TQ), 1)
    heads = lambda t: jnp.concatenate([t] * R, axis=1)

    ncmp = kc_ref.shape[0] // G
    wkeys = WINDOW + TQ
    nwb = wkeys // Q_BLOCK
    ws = pl.multiple_of(jnp.maximum(t0 - WINDOW, 0), Q_BLOCK)
    wb0 = ws // Q_BLOCK
    sc = [_dot_nt(kc_ref[g * ncmp:(g + 1) * ncmp, :], qs[g]) for g in groups]
    sw = [_dot_nt(kw_ref[g, pl.ds(ws, wkeys), :], qs[g]) for g in groups]

    kpos = ws + lax.broadcasted_iota(jnp.int32, (wkeys, 1), 0)
    wbias = heads(jnp.where((kpos <= tq1) & (tq1 - kpos < WINDOW), 0.0, NEG_INF))
    ow, l_w = [], []
    for g in groups:
        swb = sw[g] + wbias
        pw = jnp.exp2(swb - jnp.max(swb, axis=0, keepdims=True))
        vwt = jnp.concatenate([vwt_ref[g, wb0 + j] for j in range(nwb)], axis=1)
        owl = _dot(vwt, pw.astype(BF16))
        ow.append(owl[:Dh])
        l_w.append(owl[Dh:Dh + 1])

    cend = lax.broadcasted_iota(jnp.int32, (ncmp, 1), 0) * CMP_STRIDE + (CMP_BLOCK - 1)
    cmask = heads(cend <= tq1)
    pc = []
    for g in groups:
        scm = jnp.where(cmask, sc[g], NEG_INF)
        ec = jnp.where(cmask, jnp.exp2(scm - jnp.max(scm, axis=0, keepdims=True)), 0.0)
        pc.append(ec * (1.0 / jnp.maximum(jnp.sum(ec, axis=0, keepdims=True), 1e-30)))
    oc = [_dot(vct_ref[:, g * ncmp:(g + 1) * ncmp], pc[g].astype(BF16)) for g in groups]

    imp = []
    for g in groups:
        psum = pc[g][:, 0:TQ]
        for r in range(1, R):
            psum = psum + pc[g][:, r * TQ:(r + 1) * TQ]
        imp.append(_dot(ovt_ref[...], psum, precision=HIGHEST))

    blk = lax.broadcasted_iota(jnp.int32, (n_sel, 1), 0)
    valid = blk * SEL_BLOCK <= tq1
    forced = (blk == 0) | (blk == tq1 // SEL_BLOCK)
    selb = []
    for g in groups:
        val = jnp.where(forced, FORCE_SCORE, jnp.where(valid, imp[g], -FORCE_SCORE))
        rank = jnp.zeros((n_sel, TQ), F32)
        for jp in range(n_sel):
            other = val[jp:jp + 1, :]
            wins = jnp.where(blk > jp, jnp.where(other >= val, 1.0, 0.0), jnp.where(other > val, 1.0, 0.0))
            rank = rank + wins
        selb.append(jnp.where(rank < n_top, 0.0, NEG_INF).astype(BF16))

    def scores(g, c, causal):
        k0 = c * ck
        bias = _dot(et_ref[c], selb[g])
        if causal:
            kpos = k0 + lax.broadcasted_iota(jnp.int32, (ck, 1), 0)
            bias = jnp.where(kpos <= tq1, bias, NEG_INF)
        return _dot_nt(ks_ref[g, pl.ds(k0, ck), :], qs[g]) + heads(bias)

    def update(g, c, s, carry):
        m, acc = carry
        mn = jnp.maximum(m, jnp.max(s, axis=0, keepdims=True))
        p = jnp.exp2(s - mn)
        acc = jnp.exp2(m - mn) * acc + _dot(vst_ref[g, c], p.astype(BF16))
        return mn, acc

    def chunk_group(cg, carry, causal):
        cs = [cg * unroll + sub for sub in range(unroll)]
        ss = [[scores(g, c, causal) for g in groups] for c in cs]
        carry = list(carry)
        for c, s in zip(cs, ss):
            for g in groups:
                carry[g] = update(g, c, s[g], carry[g])
        return tuple(carry)

    n_chunks = (t0 + TQ + ck - 1) // ck
    init = tuple((jnp.full((1, R * TQ), NEG_INF, F32), jnp.zeros((VAL_ROWS, R * TQ), F32)) for _ in groups)
    def unrolled(trips):
        def run():
            carry = init
            for cg in range(trips):
                carry = chunk_group(cg, carry, causal=cg == trips - 1)
            return carry
        return run

    max_trips = ks_ref.shape[1] // (ck * unroll)
    sel = lax.switch((n_chunks + unroll - 1) // unroll - 1, [unrolled(t) for t in range(1, max_trips + 1)])

    normed = []
    for g in groups:
        gs = _sigmoid(gtt_ref[g])
        acc_s, l_s = sel[g][1][:Dh], sel[g][1][Dh:Dh + 1]
        for r in range(R):
            ln = slice(r * TQ, (r + 1) * TQ)
            o = (gs[3 * r:3 * r + 1, :] * oc[g][:, ln]
                 + (gs[3 * r + 1:3 * r + 2, :] / l_s[:, ln]) * acc_s[:, ln]
                 + (gs[3 * r + 2:3 * r + 3, :] / l_w[g][:, ln]) * ow[g][:, ln])
            ms = jnp.mean(o * o, axis=0, keepdims=True)
            normed.append(o * lax.rsqrt(ms + EPS) * gain_ref[g * R + r])
    for pair in range(G * R // 2):
        both = jnp.concatenate(normed[2 * pair:2 * pair + 2], axis=0)
        o_ref[:, pair * 2 * Dh:(pair + 1) * 2 * Dh] = jnp.transpose(both).astype(BF16)


def _nsa(aq, cmp_k, cmp_vt, ks, vst, kw, vwt, gates_t, gain, consts, B, S):
    G, H, TQ, Dh = NSA_KV_GROUPS, NSA_HEADS, NSA_TQ, NSA_HEAD_DIM
    nq = S // TQ
    ncmp = S // CMP_STRIDE
    n_sel = S // SEL_BLOCK
    ovt, emat_t, ck = consts
    k_spec = pl.BlockSpec((None, G, S, Dh), lambda b, i: (b, 0, 0, 0))
    return pl.pallas_call(
        functools.partial(_nsa_kernel, n_sel=n_sel, n_top=min(SEL_TOPN, n_sel), ck=ck, unroll=SEL_UNROLL),
        grid=(B, nq),
        in_specs=[
            pl.BlockSpec((None, H, TQ, Dh), lambda b, i: (b, 0, i, 0)),
            pl.BlockSpec((None, None, G * ncmp, Dh), lambda b, i: (0, b, 0, 0)),
            pl.BlockSpec((None, None, Dh, G * ncmp), lambda b, i: (1, b, 0, 0)),
            k_spec,
            pl.BlockSpec((None, G, S // ck, VAL_ROWS, ck), lambda b, i: (b, 0, 0, 0, 0)),
            k_spec,
            pl.BlockSpec((None, G, S // Q_BLOCK, VAL_ROWS, Q_BLOCK), lambda b, i: (b, 0, 0, 0, 0)),
            pl.BlockSpec((G, None, 2 * SUBLANE, TQ), lambda b, i: (0, b, GATE_COL // (2 * SUBLANE), i)),
            pl.BlockSpec((H, Dh, 1), lambda b, i: (0, 0, 0)),
            pl.BlockSpec(ovt.shape, lambda b, i: (0, 0)),
            pl.BlockSpec(emat_t.shape, lambda b, i: (0, 0, 0)),
        ],
        out_specs=pl.BlockSpec((TQ, H * Dh), lambda b, i: (b * nq + i, 0)),
        out_shape=jax.ShapeDtypeStruct((B * S, H * Dh), BF16),
        compiler_params=pltpu.CompilerParams(
            dimension_semantics=("parallel", "arbitrary"), vmem_limit_bytes=VMEM_LIMIT),
        name="nsa",
    )(aq, cmp_k, cmp_vt, ks, vst, kw, vwt, gates_t, gain, ovt, emat_t)


def _nsa_consts(S):
    n_cmp = S // CMP_STRIDE
    n_sel = S // SEL_BLOCK
    ck = 256
    i = np.arange(n_cmp)[:, None]
    j = np.arange(n_sel)[None, :]
    lo = np.maximum(i * CMP_STRIDE, j * SEL_BLOCK)
    hi = np.minimum(i * CMP_STRIDE + CMP_BLOCK, (j + 1) * SEL_BLOCK)
    ov = np.maximum(hi - lo, 0) / CMP_STRIDE
    ov[n_cmp - 1] = 0.0
    key = np.arange(S)
    emat_t = (key[:, None] // SEL_BLOCK == np.arange(n_sel)[None, :]).astype(np.float32)
    return (jnp.asarray(ov.T, F32), jnp.asarray(emat_t.reshape(S // ck, ck, n_sel), BF16), ck)


def _outproj_kernel(h_ref, ys_ref, ym_ref, yn_ref, w_ref, g_ref, o_ref):
    acc = _dot(ys_ref[...], w_ref[0:SSM_WIDTH, :])
    acc = acc + _dot(ym_ref[...], w_ref[SSM_WIDTH:SSM_WIDTH + MLSTM_WIDTH, :])
    acc = acc + _dot(yn_ref[...], w_ref[SSM_WIDTH + MLSTM_WIDTH:, :])
    ms = jnp.mean(acc * acc, axis=-1, keepdims=True)
    o_ref[...] = h_ref[...] + acc * lax.rsqrt(ms + EPS) * g_ref[...]


def _outproj(h2, y_ssm, y_mls, y_nsa, w, layer, gain, B, S, ts):
    nt = S // ts
    row = lambda b, i: (b * nt + i, 0)
    full = lambda b, i: (0, 0)
    return pl.pallas_call(
        _outproj_kernel,
        grid=(B, nt),
        in_specs=[
            pl.BlockSpec((ts, D_MODEL), row),
            pl.BlockSpec((ts, SSM_WIDTH), row),
            pl.BlockSpec((ts, MLSTM_WIDTH), row),
            pl.BlockSpec((ts, NSA_WIDTH), row),
            pl.BlockSpec((None, D_MODEL, D_MODEL), lambda b, i: (layer, 0, 0)),
            pl.BlockSpec((1, D_MODEL), full),
        ],
        out_specs=pl.BlockSpec((ts, D_MODEL), row),
        out_shape=jax.ShapeDtypeStruct((B * S, D_MODEL), F32),
        compiler_params=pltpu.CompilerParams(
            dimension_semantics=("parallel", "parallel"), vmem_limit_bytes=VMEM_LIMIT),
        name="outproj",
    )(h2, y_ssm, y_mls, y_nsa, w, gain)


def _mlp_kernel(h_ref, g1_ref, w1_ref, w2_ref, g2_ref, o_ref, u_sc, acc_sc):
    kf = pl.program_id(1)
    last = pl.num_programs(1) - 1

    def partial_ff(u):
        a = jnp.maximum(_dot(u, w1_ref[...].astype(BF16)), 0.0)
        return _dot((a * a).astype(BF16), w2_ref[...].astype(BF16))

    @pl.when(kf == 0)
    def _():
        x = h_ref[...]
        ms = jnp.mean(x * x, axis=-1, keepdims=True)
        u = (x * lax.rsqrt(ms + EPS) * g1_ref[...]).astype(BF16)
        u_sc[...] = u
        acc_sc[...] = partial_ff(u)

    @pl.when((kf > 0) & (kf < last))
    def _():
        acc_sc[...] += partial_ff(u_sc[...])

    @pl.when(kf == last)
    def _():
        f = acc_sc[...] + partial_ff(u_sc[...])
        ms = jnp.mean(f * f, axis=-1, keepdims=True)
        o_ref[...] = h_ref[...] + f * lax.rsqrt(ms + EPS) * g2_ref[...]


def _mlp(h2, g1, w1, w2, layer, g2, tm, tf):
    rows = h2.shape[0]
    return pl.pallas_call(
        _mlp_kernel,
        grid=(rows // tm, D_FF // tf),
        in_specs=[
            pl.BlockSpec((tm, D_MODEL), lambda i, k: (i, 0)),
            pl.BlockSpec((1, D_MODEL), lambda i, k: (0, 0)),
            pl.BlockSpec((None, D_MODEL, tf), lambda i, k: (layer, 0, k)),
            pl.BlockSpec((None, tf, D_MODEL), lambda i, k: (layer, k, 0)),
            pl.BlockSpec((1, D_MODEL), lambda i, k: (0, 0)),
        ],
        out_specs=pl.BlockSpec((tm, D_MODEL), lambda i, k: (i, 0)),
        out_shape=jax.ShapeDtypeStruct((rows, D_MODEL), F32),
        scratch_shapes=[pltpu.VMEM((tm, D_MODEL), BF16), pltpu.VMEM((tm, D_MODEL), F32)],
        compiler_params=pltpu.CompilerParams(
            dimension_semantics=("parallel", "arbitrary"), vmem_limit_bytes=VMEM_LIMIT),
        name="mlp",
    )(h2, g1, w1, w2, g2)


def _inproj_pieces():
    return ((0, 1280), (1288, 1800), (1800, 1928), (2056, 2184), (2312, 2440),
            (1928, 2056), (2184, 2312), (2440, 2568),
            (1280, 1288), (None, GATE_COL - 8), (2568, 2580), (None, LANE - GATE_COL - 12),
            (None, GATE_COL), (2580, 2592), (None, LANE - GATE_COL - 12))


def _permute_w_in_kernel(w_ref, o_ref):
    x = w_ref[...]
    col = 0
    for a, b in _inproj_pieces():
        width = b if a is None else b - a
        piece = jnp.zeros((x.shape[0], width), BF16) if a is None else x[:, a:b].astype(BF16)
        o_ref[:, col:col + width] = piece
        col += width
    assert col == D_INP


def _permute_w_in(w_in):
    depth, rows, cols = w_in.shape
    tr = 256
    return pl.pallas_call(
        _permute_w_in_kernel,
        grid=(depth, rows // tr),
        in_specs=[pl.BlockSpec((None, tr, cols), lambda l, i: (l, i, 0))],
        out_specs=pl.BlockSpec((None, tr, D_INP), lambda l, i: (l, i, 0)),
        out_shape=jax.ShapeDtypeStruct((depth, rows, D_INP), BF16),
        compiler_params=pltpu.CompilerParams(
            dimension_semantics=("parallel", "parallel"), vmem_limit_bytes=VMEM_LIMIT),
        name="permute_w_in",
    )(w_in)


def _rope_tables(positions):
    inv = ROPE_THETA ** (-jnp.arange(0, ROPE_DIMS, 2, dtype=F32) / ROPE_DIMS)
    ang = positions.astype(F32)[..., None] * inv
    ones = jnp.ones(ang.shape[:-1] + (1,), F32)
    feats = jnp.concatenate([jnp.cos(ang), jnp.sin(ang), ones], axis=-1).reshape(-1, 2 * ROPE_HALF + 1)
    place = np.zeros((2 * ROPE_HALF + 1, 3 * LANE), np.float32)
    for lane in range(LANE):
        d = lane % NSA_HEAD_DIM
        if d < ROPE_HALF:
            place[d, lane] = 1.0
            place[ROPE_HALF + d, LANE + lane] = -1.0
        elif d < ROPE_DIMS:
            place[d - ROPE_HALF, lane] = 1.0
            place[d, 2 * LANE + lane] = 1.0
        else:
            place[2 * ROPE_HALF, lane] = 1.0
    return jnp.dot(feats, jnp.asarray(place), precision=HIGHEST)


def _s5_params(lam_re, lam_im, b_re, b_im, c_re, c_im, log_dt):
    G, P, Hc = SSM_GROUPS, SSM_STATE, SSM_GROUP
    dt = jnp.exp(log_dt)[:, None]
    mag = jnp.exp(lam_re * dt)
    ang = lam_im * dt
    ab_re = mag * jnp.cos(ang)
    ab_im = mag * jnp.sin(ang)
    den = lam_re * lam_re + lam_im * lam_im
    g_re = ((ab_re - 1.0) * lam_re + ab_im * lam_im) / den
    g_im = (ab_im * lam_re - (ab_re - 1.0) * lam_im) / den
    bb_re = g_re[..., None] * b_re - g_im[..., None] * b_im
    bb_im = g_re[..., None] * b_im + g_im[..., None] * b_re
    eye = jnp.eye(G, dtype=F32)
    blockdiag_in = lambda t: jnp.einsum('gph,gk->ghkp', t, eye).reshape(G * Hc, G * P)
    blockdiag_out = lambda t: jnp.einsum('ghp,gk->gpkh', t, eye).reshape(G * P, G * Hc)
    bb = jnp.concatenate([blockdiag_in(bb_re), blockdiag_in(bb_im)], axis=1).astype(BF16)
    cc = jnp.concatenate([blockdiag_out(c_re), -blockdiag_out(c_im)], axis=0).astype(BF16)
    a = jnp.stack([ab_re.reshape(-1), ab_im.reshape(-1)], axis=0)
    return bb, a, cc


def _group_mean_matrix(width, group):
    idx = np.arange(width) // group
    return jnp.asarray((idx[:, None] == idx[None, :]).astype(np.float32) / group, BF16)


def kernel(x, positions, ln_mix_pre, ln_mix_post, ln_mlp_pre, ln_mlp_post, w_in, w_out, ssm_lambda_re, ssm_lambda_im, ssm_b_re, ssm_b_im, ssm_c_re, ssm_c_im, ssm_d, ssm_log_dt, ssm_w_glu, mlstm_conv, mlstm_b_i, mlstm_b_f, cmp_pe_k, cmp_w1_k, cmp_w2_k, cmp_pe_v, cmp_w1_v, cmp_w2_v, gn_ssm, gn_mlstm, gn_nsa, mlp_w1, mlp_w2):
    B, S, D = x.shape
    depth = w_in.shape[0]
    assert D == D_MODEL and B == SUBLANE and S % 512 == 0 and S >= WINDOW + NSA_TQ
    G, H = NSA_KV_GROUPS, MLSTM_HEADS
    ts_proj = 1024
    ts_scan = 128

    rope = _rope_tables(positions)
    w_in_p = _permute_w_in(w_in)
    w_out_b = w_out.astype(BF16)
    wglu_b = ssm_w_glu.astype(BF16)
    gm_ssm = _group_mean_matrix(SSM_WIDTH, SSM_GROUP)
    hm_mls = _group_mean_matrix(MLSTM_WIDTH, MLSTM_HEAD_DIM)
    consts = _nsa_consts(S)
    half = CMP_STRIDE * NSA_HEAD_DIM
    w1ab = jnp.stack([jnp.concatenate([cmp_w1_k[:, :half], cmp_w1_k[:, half:]], axis=-1),
                      jnp.concatenate([cmp_w1_v[:, :half], cmp_w1_v[:, half:]], axis=-1)], axis=1).astype(BF16)
    w1f = jnp.stack([cmp_w1_k, cmp_w1_v], axis=1)
    pef = jnp.stack([cmp_pe_k.reshape(depth, 1, -1), cmp_pe_v.reshape(depth, 1, -1)], axis=1)
    w2c = jnp.stack([cmp_w2_k, cmp_w2_v], axis=1).astype(BF16)
    w2ct = jnp.swapaxes(w2c, -1, -2)
    bias_row = jnp.concatenate([mlstm_b_i, mlstm_b_f], axis=-1)[:, :, None]

    bb, a, cc = jax.vmap(_s5_params)(ssm_lambda_re, ssm_lambda_im, ssm_b_re, ssm_b_im, ssm_c_re, ssm_c_im,
                                     ssm_log_dt)
    sh3 = lambda t: t.reshape(B, S, t.shape[-1])

    h = x.reshape(B * S, D)
    for l in range(depth):
        (su, mq, mk, mv, mo, aq, ckv, sk, wk, svt, wvt, gates, gates_t) = _inproj(
            h, ln_mix_pre[l][None], w_in_p, l, rope, B, S, ts_proj, consts[2])

        y_ssm = _s5(sh3(su), bb, a, cc, ssm_d[l][None], wglu_b, l, gm_ssm, gn_ssm[l][None], B, S, ts_scan)

        y_mls = _mlstm(sh3(mq), sh3(mk), sh3(mv), sh3(mo), gates_t,
                       mlstm_conv[l][:, :MLSTM_WIDTH], mlstm_conv[l][:, MLSTM_WIDTH:],
                       bias_row[l], hm_mls, gn_mlstm[l][None], B, S)

        cmp_k, cmp_t = _compress(ckv, w1ab, w1f, pef, w2c, w2ct, l, B, S)
        y_nsa = _nsa(aq, cmp_k, cmp_t, sk, svt, wk, wvt, gates_t,
                     gn_nsa[l].reshape(NSA_HEADS, NSA_HEAD_DIM, 1), consts, B, S)

        h = _outproj(h, y_ssm.reshape(B * S, SSM_WIDTH), y_mls.reshape(B * S, MLSTM_WIDTH), y_nsa,
                     w_out_b, l, ln_mix_post[l][None], B, S, ts_proj)
        h = _mlp(h, ln_mlp_pre[l][None], mlp_w1, mlp_w2, l, ln_mlp_post[l][None], 1024, 1024)
    return h.reshape(B, S, D)
```

```python
import functools
import math

import numpy as np
import jax
import jax.numpy as jnp
from jax import lax
from jax.experimental import pallas as pl
from jax.experimental.pallas import tpu as pltpu

F32 = jnp.float32
BF16 = jnp.bfloat16
HIGHEST = lax.Precision.HIGHEST

D_MODEL = 1024
DEPTH = 4
SSM_WIDTH = 256
SSM_GROUP = 16
SSM_GROUPS = 16
SSM_STATE = 64
SSM_LANES = SSM_GROUPS * SSM_STATE
MLSTM_WIDTH = 256
MLSTM_HEADS = 4
MLSTM_HEAD_DIM = 64
MLSTM_CHUNK = 128
MLSTM_CONV = 4
NSA_WIDTH = 512
NSA_HEAD_DIM = 64
NSA_HEADS = 8
NSA_KV_GROUPS = 2
NSA_REP = NSA_HEADS // NSA_KV_GROUPS
NSA_KV_WIDTH = 128
CMP_BLOCK = 32
CMP_STRIDE = 16
CMP_HIDDEN = 256
SEL_BLOCK = 64
SEL_TOPN = 8
WINDOW = 256
Q_BLOCK = 128
FORCE_SCORE = 1e4
NEG_INF = -1e30
ROPE_THETA = 500000.0
ROPE_DIMS = 16
ROPE_HALF = 8
D_FF = 4096
EPS = 1e-6
D_IN = 2592

LANE = 128
SUBLANE = 8
VMEM_LIMIT = 56 * 1024 * 1024

C_SU, C_MQ, C_MK, C_MV, C_MO = 0, 256, 512, 768, 1024
C_AQ, C_CK, C_SK, C_WK = 1280, 1792, 1920, 2048
C_CV, C_SV, C_WV = 2176, 2304, 2432
C_G0, C_G1 = 2560, 2688
D_INP = 2816
GATE_COL = 16
VAL_ROWS = NSA_HEAD_DIM + 16
Q_SCALE = NSA_HEAD_DIM ** -0.5 * math.log2(math.e)
MLSTM_ROWS = 8
NSA_TQ = 256
SEL_UNROLL = 2


def _dot(a, b, precision=None):
    return jnp.dot(a, b, preferred_element_type=F32, precision=precision)


def _dot_nt(a, b):
    return lax.dot_general(a, b, (((1,), (1,)), ((), ())), preferred_element_type=F32)


def _dot_tn(a, b):
    return lax.dot_general(a, b, (((0,), (0,)), ((), ())), preferred_element_type=F32)


def _sigmoid(x):
    return 1.0 / (1.0 + jnp.exp(-x))


def _dot_split(x, w_bf16):
    hi = x.astype(BF16)
    lo = (x - hi.astype(F32)).astype(BF16)
    return _dot(hi, w_bf16) + _dot(lo, w_bf16)


def _gelu_tanh(x):
    return 0.5 * x * (1.0 + jnp.tanh(math.sqrt(2.0 / math.pi) * (x + 0.044715 * (x * x * x))))


def _log_sigmoid(x):
    return jnp.minimum(x, 0.0) - jnp.log(1.0 + jnp.exp(-jnp.abs(x)))


def _inproj_kernel(x_ref, g_ref, w_ref, rt_ref,
                   su_ref, mq_ref, mk_ref, mv_ref, mo_ref, aq_ref, ckv_ref, sk_ref, wk_ref,
                   svt_ref, wvt_ref, gt_ref, gtt_ref):
    x = x_ref[...]
    ms = jnp.mean(x * x, axis=-1, keepdims=True)
    u = (x * lax.rsqrt(ms + EPS) * g_ref[...]).astype(BF16)
    rc, rs1, rs2 = (rt_ref[:, n * LANE:(n + 1) * LANE] for n in range(3))

    def mm(c0, width):
        return _dot(u, w_ref[:, c0:c0 + width])

    def rope(z):
        return z * rc + pltpu.roll(z, LANE - ROPE_HALF, 1) * rs1 + pltpu.roll(z, ROPE_HALF, 1) * rs2

    su_ref[...] = mm(C_SU, 256)
    mq_ref[...] = mm(C_MQ, 256)
    mk_ref[...] = mm(C_MK, 256)
    mv_ref[...] = mm(C_MV, 256).astype(BF16)
    mo_ref[...] = mm(C_MO, 256)
    def mm_pair(c0):
        z = mm(c0, 2 * LANE)
        return z[:, :LANE], z[:, LANE:]

    def put_heads(ref, first, z):
        ref[first] = z[:, :NSA_HEAD_DIM].astype(BF16)
        ref[first + 1] = z[:, NSA_HEAD_DIM:].astype(BF16)

    for j in range(NSA_HEADS // 4):
        for k, z in enumerate(mm_pair(C_AQ + 2 * LANE * j)):
            put_heads(aq_ref, 4 * j + 2 * k, rope(z) * Q_SCALE)
    z_ck, z_sk = mm_pair(C_CK)
    z_wk, z_cv = mm_pair(C_WK)
    z_sv, z_wv = mm_pair(C_SV)
    ckv_ref[0] = rope(z_ck)
    ckv_ref[1] = z_cv
    put_heads(sk_ref, 0, rope(z_sk))
    put_heads(wk_ref, 0, rope(z_wk))

    def put_chunks_t(ref, z):
        zt = jnp.transpose(z)
        width = ref.shape[-1]
        ones = jnp.ones((VAL_ROWS - NSA_HEAD_DIM, width), BF16)
        for g in range(NSA_KV_GROUPS):
            for j in range(ref.shape[1]):
                ref[g, j, :NSA_HEAD_DIM] = zt[g * NSA_HEAD_DIM:(g + 1) * NSA_HEAD_DIM,
                                              j * width:(j + 1) * width].astype(BF16)
                ref[g, j, NSA_HEAD_DIM:] = ones

    put_chunks_t(svt_ref, z_sv)
    put_chunks_t(wvt_ref, z_wv)
    z_g0, z_g1 = mm_pair(C_G0)
    gt_ref[...] = z_g0
    gtt_ref[0] = jnp.transpose(z_g0)
    gtt_ref[1] = jnp.transpose(z_g1)


def _inproj(h2, gain, w, layer, rope, B, S, ts, ck):
    nt = S // ts
    BS = B * S
    row = lambda b, i: (b * nt + i, 0)
    full = lambda b, i: (0, 0)
    headed = lambda b, i: (b, 0, i, 0)
    paired = lambda b, i: (0, b * nt + i, 0)
    in_specs = [
        pl.BlockSpec((ts, D_MODEL), row),
        pl.BlockSpec((1, D_MODEL), full),
        pl.BlockSpec((None, D_MODEL, D_INP), lambda b, i: (layer, 0, 0)),
        pl.BlockSpec((ts, 3 * LANE), row),
    ]
    kv_shape = jax.ShapeDtypeStruct((B, NSA_KV_GROUPS, S, NSA_HEAD_DIM), BF16)
    kv_spec = pl.BlockSpec((None, NSA_KV_GROUPS, ts, NSA_HEAD_DIM), headed)
    out_shape = [
        jax.ShapeDtypeStruct((BS, SSM_WIDTH), F32),
        jax.ShapeDtypeStruct((BS, MLSTM_WIDTH), F32),
        jax.ShapeDtypeStruct((BS, MLSTM_WIDTH), F32),
        jax.ShapeDtypeStruct((BS, MLSTM_WIDTH), BF16),
        jax.ShapeDtypeStruct((BS, MLSTM_WIDTH), F32),
        jax.ShapeDtypeStruct((B, NSA_HEADS, S, NSA_HEAD_DIM), BF16),
        jax.ShapeDtypeStruct((2, BS, NSA_KV_WIDTH), F32),
        kv_shape, kv_shape,
        jax.ShapeDtypeStruct((B, NSA_KV_GROUPS, S // ck, VAL_ROWS, ck), BF16),
        jax.ShapeDtypeStruct((B, NSA_KV_GROUPS, S // Q_BLOCK, VAL_ROWS, Q_BLOCK), BF16),
        jax.ShapeDtypeStruct((BS, LANE), F32),
        jax.ShapeDtypeStruct((NSA_KV_GROUPS, B, LANE, S), F32),
    ]
    out_specs = [
        pl.BlockSpec((ts, SSM_WIDTH), row),
        pl.BlockSpec((ts, MLSTM_WIDTH), row),
        pl.BlockSpec((ts, MLSTM_WIDTH), row),
        pl.BlockSpec((ts, MLSTM_WIDTH), row),
        pl.BlockSpec((ts, MLSTM_WIDTH), row),
        pl.BlockSpec((None, NSA_HEADS, ts, NSA_HEAD_DIM), headed),
        pl.BlockSpec((2, ts, NSA_KV_WIDTH), paired),
        kv_spec, kv_spec,
        pl.BlockSpec((None, NSA_KV_GROUPS, ts // ck, VAL_ROWS, ck), lambda b, i: (b, 0, i, 0, 0)),
        pl.BlockSpec((None, NSA_KV_GROUPS, ts // Q_BLOCK, VAL_ROWS, Q_BLOCK), lambda b, i: (b, 0, i, 0, 0)),
        pl.BlockSpec((ts, LANE), row),
        pl.BlockSpec((NSA_KV_GROUPS, None, LANE, ts), lambda b, i: (0, b, 0, i)),
    ]
    return pl.pallas_call(
        _inproj_kernel,
        grid=(B, nt),
        in_specs=in_specs,
        out_specs=out_specs,
        out_shape=out_shape,
        compiler_params=pltpu.CompilerParams(
            dimension_semantics=("parallel", "parallel"), vmem_limit_bytes=VMEM_LIMIT),
        name="inproj",
    )(h2, gain, w, rope)


def _s5_kernel(u_ref, bb_ref, a_ref, cc_ref, d_ref, wg_ref, gm_ref, gain_ref, o_ref, x_sc, st_sc, tm_sc, *, B, ts):
    @pl.when(pl.program_id(0) == 0)
    def _():
        st_sc[...] = jnp.zeros_like(st_sc)

    nl = SSM_WIDTH // LANE
    for b in range(B):
        for c in range(nl):
            tm_sc[c, pl.ds(b, ts, stride=B), :] = u_ref[b, :, c * LANE:(c + 1) * LANE]
    u = jnp.concatenate([tm_sc[c] for c in range(nl)], axis=1)
    ub = u.astype(BF16)
    for part in range(2):
        cols = slice(part * SSM_LANES, (part + 1) * SSM_LANES)
        x_sc[:, cols] = _dot(ub, bb_ref[:, cols])
    ar = jnp.broadcast_to(a_ref[0:1, :], (B, SSM_LANES))
    ai = jnp.broadcast_to(a_ref[1:2, :], (B, SSM_LANES))

    def step(t, carry):
        xr, xi = carry
        r = pl.multiple_of(t * B, B)
        br = x_sc[pl.ds(r, B), 0:SSM_LANES]
        bi = x_sc[pl.ds(r, B), SSM_LANES:2 * SSM_LANES]
        nr = ar * xr - ai * xi + br
        ni = ar * xi + ai * xr + bi
        x_sc[pl.ds(r, B), 0:SSM_LANES] = nr
        x_sc[pl.ds(r, B), SSM_LANES:2 * SSM_LANES] = ni
        return nr, ni

    xr, xi = lax.fori_loop(0, ts, step, (st_sc[0], st_sc[1]), unroll=4)
    st_sc[0] = xr
    st_sc[1] = xi

    half = (ts * B) // 2
    y = jnp.concatenate([_dot(x_sc[r * half:(r + 1) * half, :].astype(BF16), cc_ref[...]) for r in range(2)],
                        axis=0) + d_ref[...] * u
    y = _gelu_tanh(y)
    y = y * _sigmoid(_dot(y.astype(BF16), wg_ref[...]))
    ms = _dot_split(y * y, gm_ref[...])
    y = y * lax.rsqrt(ms + EPS) * gain_ref[...]
    for c in range(nl):
        tm_sc[c] = y[:, c * LANE:(c + 1) * LANE]
    for b in range(B):
        o_ref[b] = jnp.concatenate(
            [tm_sc[c, pl.ds(b, ts, stride=B), :] for c in range(nl)], axis=1).astype(BF16)


def _s5(u, bb, a, cc, d, wg, layer, gm, gain, B, S, ts):
    rows = ts * B
    full = lambda i: (0, 0)
    lsel = lambda i: (layer, 0, 0)
    return pl.pallas_call(
        functools.partial(_s5_kernel, B=B, ts=ts),
        grid=(S // ts,),
        in_specs=[
            pl.BlockSpec((B, ts, SSM_WIDTH), lambda i: (0, i, 0)),
            pl.BlockSpec((None, SSM_WIDTH, 2 * SSM_LANES), lsel),
            pl.BlockSpec((None, 2, SSM_LANES), lsel),
            pl.BlockSpec((None, 2 * SSM_LANES, SSM_WIDTH), lsel),
            pl.BlockSpec((1, SSM_WIDTH), full),
            pl.BlockSpec((None, SSM_WIDTH, SSM_WIDTH), lsel),
            pl.BlockSpec((SSM_WIDTH, SSM_WIDTH), full),
            pl.BlockSpec((1, SSM_WIDTH), full),
        ],
        out_specs=pl.BlockSpec((B, ts, SSM_WIDTH), lambda i: (0, i, 0)),
        out_shape=jax.ShapeDtypeStruct((B, S, SSM_WIDTH), BF16),
        scratch_shapes=[pltpu.VMEM((rows, 2 * SSM_LANES), F32), pltpu.VMEM((2, B, SSM_LANES), F32),
                        pltpu.VMEM((SSM_WIDTH // LANE, rows, LANE), F32)],
        compiler_params=pltpu.CompilerParams(
            dimension_semantics=("arbitrary",), vmem_limit_bytes=VMEM_LIMIT),
        name="s5",
    )(u, bb, a, cc, d, wg, gm, gain)


def _mlstm_kernel(q_ref, k_ref, v_ref, o_ref, gr_ref, cwq_ref, cwk_ref, br_ref, hm_ref,
                  gain_ref, y_ref, qt_sc, kt_sc, c_sc, m_sc, *, B):
    L, H, Dh, W = MLSTM_CHUNK, MLSTM_HEADS, MLSTM_HEAD_DIM, MLSTM_WIDTH

    @pl.when(pl.program_id(0) == 0)
    def _():
        qt_sc[...] = jnp.zeros_like(qt_sc)
        kt_sc[...] = jnp.zeros_like(kt_sc)
        c_sc[...] = jnp.zeros_like(c_sc)
        m_sc[...] = jnp.zeros_like(m_sc)

    visible = lax.broadcasted_iota(jnp.int32, (L, L), 0) <= lax.broadcasted_iota(jnp.int32, (L, L), 1)
    triu = visible.astype(F32)
    lane_w = lax.broadcasted_iota(jnp.int32, (1, W), 1) // Dh
    bd_mask = ((lax.broadcasted_iota(jnp.int32, (2 * W, W), 0) % W) // Dh
               == lax.broadcasted_iota(jnp.int32, (2 * W, W), 1) // Dh)
    row8 = lax.broadcasted_iota(jnp.int32, (SUBLANE, W), 0)
    cwq = cwq_ref[...]
    cwk = cwk_ref[...]
    ones_rows = jnp.ones((Dh, L), F32)

    def conv_silu(x, tail, w):
        acc = x * w[MLSTM_CONV - 1:MLSTM_CONV, :]
        for sft in range(1, MLSTM_CONV):
            xs = pltpu.roll(x, sft, 0)
            head = jnp.where(row8 < sft, pltpu.roll(tail, sft, 0), xs[:SUBLANE])
            xs = jnp.concatenate([head, xs[SUBLANE:]], axis=0)
            acc = acc + xs * w[MLSTM_CONV - 1 - sft:MLSTM_CONV - sft, :]
        return acc * _sigmoid(acc)

    def per_group(grp, _):
        bs = [grp * MLSTM_ROWS + n for n in range(MLSTM_ROWS)]
        st = [dict() for _ in bs]

        for b, d in zip(bs, st):
            q_raw = q_ref[b]
            k_raw = k_ref[b]
            d['q'] = conv_silu(q_raw, qt_sc[b], cwq)
            d['k'] = conv_silu(k_raw, kt_sc[b], cwk) * (Dh ** -0.5)
            qt_sc[b] = q_raw[L - SUBLANE:, :]
            kt_sc[b] = k_raw[L - SUBLANE:, :]
            d['gr'] = gr_ref[b] + br_ref[...]
        for d in st:
            d['brow'] = _dot(_log_sigmoid(d['gr']), triu, precision=HIGHEST)

        for b, d in zip(bs, st):
            gr, brow = d['gr'], d['brow']
            ccol = jnp.transpose(brow - pltpu.roll(gr, H, 0))
            m_all = m_sc[b]
            for key in ('w_intra', 'w_inter', 'e_mt', 'w_k', 'dec', 'm_new'):
                d[key] = []
            for hh in range(H):
                b_r = brow[H + hh:H + hh + 1, :]
                i_r = gr[hh:hh + 1, :]
                m_prev = m_all[hh:hh + 1, 0:1]
                dm = jnp.where(visible, b_r - ccol[:, H + hh:H + hh + 1], NEG_INF)
                inter = b_r + m_prev
                mt = jnp.maximum(inter, jnp.max(dm, axis=0, keepdims=True))
                d['w_intra'].append(jnp.exp(dm - mt))
                d['w_inter'].append(jnp.exp(inter - mt))
                d['e_mt'].append(jnp.exp(-mt))
                b_last = b_r[:, L - 1:L]
                logw = b_last - b_r + i_r
                mn = jnp.maximum(b_last + m_prev, jnp.max(logw, axis=1, keepdims=True))
                d['w_k'].append(jnp.exp(logw - mn))
                d['dec'].append(jnp.exp(b_last + m_prev - mn))
                d['m_new'].append(mn)
            d['qb'] = d['q'].astype(BF16)
            d['kb'] = d['k'].astype(BF16)
            d['vt'] = jnp.transpose(v_ref[b].astype(F32))
            d['c_t'] = c_sc[b]

        for d in st:
            d['qc'] = _dot(d['c_t'].astype(BF16), jnp.transpose(d['q']).astype(BF16))
            d['s_t'] = [_dot_nt(d['kb'], jnp.where(lane_w == hh, d['qb'], jnp.zeros_like(d['qb'])))
                        for hh in range(H)]
        for d in st:
            d['r'] = []
            for hh in range(H):
                v_aug = jnp.concatenate([d['vt'][hh * Dh:(hh + 1) * Dh], ones_rows], axis=0).astype(BF16)
                d['r'].append(_dot(v_aug, (d['s_t'][hh] * d['w_intra'][hh]).astype(BF16)))

        for b, d in zip(bs, st):
            h_t = []
            for hh in range(H):
                ch = slice(hh * Dh, (hh + 1) * Dh)
                num = d['w_inter'][hh] * d['qc'][ch] + d['r'][hh][:Dh]
                den = d['w_inter'][hh] * d['qc'][W + hh * Dh:W + (hh + 1) * Dh] + d['r'][hh][Dh:]
                h_t.append(num / jnp.maximum(jnp.abs(den), d['e_mt'][hh]))
            hout = jnp.transpose(jnp.concatenate(h_t, axis=0))
            d['y'] = _sigmoid(o_ref[b]) * hout
            d['vw'] = jnp.concatenate(
                [d['vt'][hh * Dh:(hh + 1) * Dh] * d['w_k'][hh] for hh in range(H)]
                + [jnp.broadcast_to(d['w_k'][hh], (Dh, L)) for hh in range(H)], axis=0).astype(BF16)
        for d in st:
            d['ms'] = _dot_split(d['y'] * d['y'], hm_ref[...])
            d['upd'] = _dot(d['vw'], d['kb'])

        for b, d in zip(bs, st):
            y_ref[b] = (d['y'] * lax.rsqrt(d['ms'] + EPS) * gain_ref[...]).astype(BF16)
            decay = d['dec'][H - 1]
            for hh in range(H - 2, -1, -1):
                decay = jnp.where(lane_w == hh, d['dec'][hh], decay)
            c_sc[b] = decay * d['c_t'] + jnp.where(bd_mask, d['upd'], 0.0)
            for hh in range(H):
                m_sc[b, hh:hh + 1, :] = jnp.broadcast_to(d['m_new'][hh], (1, LANE))
        return 0

    lax.fori_loop(0, B // MLSTM_ROWS, per_group, 0)


def _mlstm(mq, mk, mv, mo, grow, cwq, cwk, brow, hm, gain, B, S):
    L, W = MLSTM_CHUNK, MLSTM_WIDTH
    seq = lambda c: (0, c, 0)
    full = lambda c: (0, 0)
    return pl.pallas_call(
        functools.partial(_mlstm_kernel, B=B),
        grid=(S // L,),
        in_specs=[
            pl.BlockSpec((B, L, W), seq),
            pl.BlockSpec((B, L, W), seq),
            pl.BlockSpec((B, L, W), seq),
            pl.BlockSpec((B, L, W), seq),
            pl.BlockSpec((None, B, SUBLANE, L), lambda c: (0, 0, 0, c)),
            pl.BlockSpec((MLSTM_CONV, W), full),
            pl.BlockSpec((MLSTM_CONV, W), full),
            pl.BlockSpec((SUBLANE, 1), full),
            pl.BlockSpec((W, W), full),
            pl.BlockSpec((1, W), full),
        ],
        out_specs=pl.BlockSpec((B, L, W), seq),
        out_shape=jax.ShapeDtypeStruct((B, S, W), BF16),
        scratch_shapes=[
            pltpu.VMEM((B, SUBLANE, W), F32),
            pltpu.VMEM((B, SUBLANE, W), F32),
            pltpu.VMEM((B, 2 * W, W), F32),
            pltpu.VMEM((B, SUBLANE, LANE), F32),
        ],
        compiler_params=pltpu.CompilerParams(
            dimension_semantics=("arbitrary",), vmem_limit_bytes=VMEM_LIMIT),
        name="mlstm",
    )(mq, mk, mv, mo, grow, cwq, cwk, brow, hm, gain)


def _compress_kernel(c_ref, w1ab_ref, w1_ref, pe_ref, w2_ref, w2t_ref, o_ref, ot_ref, ch_sc):
    G, Dh = NSA_KV_GROUPS, NSA_HEAD_DIM
    rows = ch_sc.shape[1]
    n = rows // G
    both = range(2)
    for i in both:
        for r in range(CMP_STRIDE):
            tok = c_ref[i, pl.ds(r, n, stride=CMP_STRIDE), :]
            for g in range(G):
                ch_sc[i, g * n:(g + 1) * n, r * Dh:(r + 1) * Dh] = tok[:, g * Dh:(g + 1) * Dh]
    ab = [_dot(ch_sc[i].astype(BF16), w1ab_ref[i]) for i in both]
    const = [_dot(pe_ref[i], w1_ref[i], precision=HIGHEST) for i in both]
    act = []
    for i in both:
        hid = ab[i][:, :CMP_HIDDEN] + pltpu.roll(ab[i][:, CMP_HIDDEN:], rows - 1, 0) + const[i]
        act.append(_gelu_tanh(hid).astype(BF16))
    for i in both:
        o_ref[i] = _dot(act[i], w2_ref[i]).astype(BF16)
        ot_ref[i] = _dot_nt(w2t_ref[i], act[i]).astype(BF16)


def _compress(ckv, w1ab, w1, pe, w2, w2t, layer, B, S):
    G, Dh = NSA_KV_GROUPS, NSA_HEAD_DIM
    n = S // CMP_STRIDE
    width = CMP_STRIDE * Dh
    wsel = lambda b: (layer, 0, 0, 0)
    return pl.pallas_call(
        _compress_kernel,
        grid=(B,),
        in_specs=[
            pl.BlockSpec((2, S, G * Dh), lambda b: (0, b, 0)),
            pl.BlockSpec((None, 2, width, 2 * CMP_HIDDEN), wsel),
            pl.BlockSpec((None, 2, 2 * width, CMP_HIDDEN), wsel),
            pl.BlockSpec((None, 2, 1, 2 * width), wsel),
            pl.BlockSpec((None, 2, CMP_HIDDEN, Dh), wsel),
            pl.BlockSpec((None, 2, Dh, CMP_HIDDEN), wsel),
        ],
        out_specs=[pl.BlockSpec((2, None, G * n, Dh), lambda b: (0, b, 0, 0)),
                   pl.BlockSpec((2, None, Dh, G * n), lambda b: (0, b, 0, 0))],
        out_shape=[jax.ShapeDtypeStruct((2, B, G * n, Dh), BF16),
                   jax.ShapeDtypeStruct((2, B, Dh, G * n), BF16)],
        scratch_shapes=[pltpu.VMEM((2, G * n, width), F32)],
        compiler_params=pltpu.CompilerParams(
            dimension_semantics=("parallel",), vmem_limit_bytes=VMEM_LIMIT),
        name="compress",
    )(ckv, w1ab, w1, pe, w2, w2t)


def _nsa_kernel(q_ref, kc_ref, vct_ref, ks_ref, vst_ref, kw_ref, vwt_ref, gtt_ref, gain_ref,
                ovt_ref, et_ref, o_ref, *, n_sel, n_top, ck, unroll):
    TQ, R, Dh, G = NSA_TQ, NSA_REP, NSA_HEAD_DIM, NSA_KV_GROUPS
    groups = range(G)
    i = pl.program_id(1)
    t0 = i * TQ
    qs = [q_ref[g * R:(g + 1) * R].reshape(R * TQ, Dh) for g in groups]
    tq1 = t0 + lax.broadcasted_iota(jnp.int32, (1, TQ), 1)
    heads = lambda t: jnp.concatenate([t] * R, axis=1)

    ncmp = kc_ref.shape[0] // G
    wkeys = WINDOW + TQ
    nwb = wkeys // Q_BLOCK
    ws = pl.multiple_of(jnp.maximum(t0 - WINDOW, 0), Q_BLOCK)
    wb0 = ws // Q_BLOCK
    sc = [_dot_nt(kc_ref[g * ncmp:(g + 1) * ncmp, :], qs[g]) for g in groups]
    sw = [_dot_nt(kw_ref[g, pl.ds(ws, wkeys), :], qs[g]) for g in groups]

    kpos = ws + lax.broadcasted_iota(jnp.int32, (wkeys, 1), 0)
    wbias = heads(jnp.where((kpos <= tq1) & (tq1 - kpos < WINDOW), 0.0, NEG_INF))
    ow, l_w = [], []
    for g in groups:
        swb = sw[g] + wbias
        pw = jnp.exp2(swb - jnp.max(swb, axis=0, keepdims=True))
        vwt = jnp.concatenate([vwt_ref[g, wb0 + j] for j in range(nwb)], axis=1)
        owl = _dot(vwt, pw.astype(BF16))
        ow.append(owl[:Dh])
        l_w.append(owl[Dh:Dh + 1])

    cend = lax.broadcasted_iota(jnp.int32, (ncmp, 1), 0) * CMP_STRIDE + (CMP_BLOCK - 1)
    cmask = heads(cend <= tq1)
    pc = []
    for g in groups:
        scm = jnp.where(cmask, sc[g], NEG_INF)
        ec = jnp.where(cmask, jnp.exp2(scm - jnp.max(scm, axis=0, keepdims=True)), 0.0)
        pc.append(ec * (1.0 / jnp.maximum(jnp.sum(ec, axis=0, keepdims=True), 1e-30)))
    oc = [_dot(vct_ref[:, g * ncmp:(g + 1) * ncmp], pc[g].astype(BF16)) for g in groups]

    imp = []
    for g in groups:
        psum = pc[g][:, 0:TQ]
        for r in range(1, R):
            psum = psum + pc[g][:, r * TQ:(r + 1) * TQ]
        imp.append(_dot(ovt_ref[...], psum, precision=HIGHEST))

    blk = lax.broadcasted_iota(jnp.int32, (n_sel, 1), 0)
    valid = blk * SEL_BLOCK <= tq1
    forced = (blk == 0) | (blk == tq1 // SEL_BLOCK)
    selb = []
    for g in groups:
        val = jnp.where(forced, FORCE_SCORE, jnp.where(valid, imp[g], -FORCE_SCORE))
        rank = jnp.zeros((n_sel, TQ), F32)
        for jp in range(n_sel):
            other = val[jp:jp + 1, :]
            wins = jnp.where(blk > jp, jnp.where(other >= val, 1.0, 0.0), jnp.where(other > val, 1.0, 0.0))
            rank = rank + wins
        selb.append(jnp.where(rank < n_top, 0.0, NEG_INF).astype(BF16))

    def scores(g, c, causal):
        k0 = c * ck
        bias = _dot(et_ref[c], selb[g])
        if causal:
            kpos = k0 + lax.broadcasted_iota(jnp.int32, (ck, 1), 0)
            bias = jnp.where(kpos <= tq1, bias, NEG_INF)
        return _dot_nt(ks_ref[g, pl.ds(k0, ck), :], qs[g]) + heads(bias)

    def update(g, c, s, carry):
        m, acc = carry
        mn = jnp.maximum(m, jnp.max(s, axis=0, keepdims=True))
        p = jnp.exp2(s - mn)
        acc = jnp.exp2(m - mn) * acc + _dot(vst_ref[g, c], p.astype(BF16))
        return mn, acc

    def chunk_group(cg, carry, causal):
        cs = [cg * unroll + sub for sub in range(unroll)]
        ss = [[scores(g, c, causal) for g in groups] for c in cs]
        carry = list(carry)
        for c, s in zip(cs, ss):
            for g in groups:
                carry[g] = update(g, c, s[g], carry[g])
        return tuple(carry)

    n_chunks = (t0 + TQ + ck - 1) // ck
    init = tuple((jnp.full((1, R * TQ), NEG_INF, F32), jnp.zeros((VAL_ROWS, R * TQ), F32)) for _ in groups)
    def unrolled(trips):
        def run():
            carry = init
            for cg in range(trips):
                carry = chunk_group(cg, carry, causal=cg == trips - 1)
            return carry
        return run

    max_trips = ks_ref.shape[1] // (ck * unroll)
    sel = lax.switch((n_chunks + unroll - 1) // unroll - 1, [unrolled(t) for t in range(1, max_trips + 1)])

    normed = []
    for g in groups:
        gs = _sigmoid(gtt_ref[g])
        acc_s, l_s = sel[g][1][:Dh], sel[g][1][Dh:Dh + 1]
        for r in range(R):
            ln = slice(r * TQ, (r + 1) * TQ)
            o = (gs[3 * r:3 * r + 1, :] * oc[g][:, ln]
                 + (gs[3 * r + 1:3 * r + 2, :] / l_s[:, ln]) * acc_s[:, ln]
                 + (gs[3 * r + 2:3 * r + 3, :] / l_w[g][:, ln]) * ow[g][:, ln])
            ms = jnp.mean(o * o, axis=0, keepdims=True)
            normed.append(o * lax.rsqrt(ms + EPS) * gain_ref[g * R + r])
    for pair in range(G * R // 2):
        both = jnp.concatenate(normed[2 * pair:2 * pair + 2], axis=0)
        o_ref[:, pair * 2 * Dh:(pair + 1) * 2 * Dh] = jnp.transpose(both).astype(BF16)


def _nsa(aq, cmp_k, cmp_vt, ks, vst, kw, vwt, gates_t, gain, consts, B, S):
    G, H, TQ, Dh = NSA_KV_GROUPS, NSA_HEADS, NSA_TQ, NSA_HEAD_DIM
    nq = S // TQ
    ncmp = S // CMP_STRIDE
    n_sel = S // SEL_BLOCK
    ovt, emat_t, ck = consts
    k_spec = pl.BlockSpec((None, G, S, Dh), lambda b, i: (b, 0, 0, 0))
    return pl.pallas_call(
        functools.partial(_nsa_kernel, n_sel=n_sel, n_top=min(SEL_TOPN, n_sel), ck=ck, unroll=SEL_UNROLL),
        grid=(B, nq),
        in_specs=[
            pl.BlockSpec((None, H, TQ, Dh), lambda b, i: (b, 0, i, 0)),
            pl.BlockSpec((None, None, G * ncmp, Dh), lambda b, i: (0, b, 0, 0)),
            pl.BlockSpec((None, None, Dh, G * ncmp), lambda b, i: (1, b, 0, 0)),
            k_spec,
            pl.BlockSpec((None, G, S // ck, VAL_ROWS, ck), lambda b, i: (b, 0, 0, 0, 0)),
            k_spec,
            pl.BlockSpec((None, G, S // Q_BLOCK, VAL_ROWS, Q_BLOCK), lambda b, i: (b, 0, 0, 0, 0)),
            pl.BlockSpec((G, None, 2 * SUBLANE, TQ), lambda b, i: (0, b, GATE_COL // (2 * SUBLANE), i)),
            pl.BlockSpec((H, Dh, 1), lambda b, i: (0, 0, 0)),
            pl.BlockSpec(ovt.shape, lambda b, i: (0, 0)),
            pl.BlockSpec(emat_t.shape, lambda b, i: (0, 0, 0)),
        ],
        out_specs=pl.BlockSpec((TQ, H * Dh), lambda b, i: (b * nq + i, 0)),
        out_shape=jax.ShapeDtypeStruct((B * S, H * Dh), BF16),
        compiler_params=pltpu.CompilerParams(
            dimension_semantics=("parallel", "arbitrary"), vmem_limit_bytes=VMEM_LIMIT),
        name="nsa",
    )(aq, cmp_k, cmp_vt, ks, vst, kw, vwt, gates_t, gain, ovt, emat_t)


def _nsa_consts(S):
    n_cmp = S // CMP_STRIDE
    n_sel = S // SEL_BLOCK
    ck = 256
    i = np.arange(n_cmp)[:, None]
    j = np.arange(n_sel)[None, :]
    lo = np.maximum(i * CMP_STRIDE, j * SEL_BLOCK)
    hi = np.minimum(i * CMP_STRIDE + CMP_BLOCK, (j + 1) * SEL_BLOCK)
    ov = np.maximum(hi - lo, 0) / CMP_STRIDE
    ov[n_cmp - 1] = 0.0
    key = np.arange(S)
    emat_t = (key[:, None] // SEL_BLOCK == np.arange(n_sel)[None, :]).astype(np.float32)
    return (jnp.asarray(ov.T, F32), jnp.asarray(emat_t.reshape(S // ck, ck, n_sel), BF16), ck)


def _outproj_kernel(h_ref, ys_ref, ym_ref, yn_ref, w_ref, g_ref, o_ref):
    acc = _dot(ys_ref[...], w_ref[0:SSM_WIDTH, :])
    acc = acc + _dot(ym_ref[...], w_ref[SSM_WIDTH:SSM_WIDTH + MLSTM_WIDTH, :])
    acc = acc + _dot(yn_ref[...], w_ref[SSM_WIDTH + MLSTM_WIDTH:, :])
    ms = jnp.mean(acc * acc, axis=-1, keepdims=True)
    o_ref[...] = h_ref[...] + acc * lax.rsqrt(ms + EPS) * g_ref[...]


def _outproj(h2, y_ssm, y_mls, y_nsa, w, layer, gain, B, S, ts):
    nt = S // ts
    row = lambda b, i: (b * nt + i, 0)
    full = lambda b, i: (0, 0)
    return pl.pallas_call(
        _outproj_kernel,
        grid=(B, nt),
        in_specs=[
            pl.BlockSpec((ts, D_MODEL), row),
            pl.BlockSpec((ts, SSM_WIDTH), row),
            pl.BlockSpec((ts, MLSTM_WIDTH), row),
            pl.BlockSpec((ts, NSA_WIDTH), row),
            pl.BlockSpec((None, D_MODEL, D_MODEL), lambda b, i: (layer, 0, 0)),
            pl.BlockSpec((1, D_MODEL), full),
        ],
        out_specs=pl.BlockSpec((ts, D_MODEL), row),
        out_shape=jax.ShapeDtypeStruct((B * S, D_MODEL), F32),
        compiler_params=pltpu.CompilerParams(
            dimension_semantics=("parallel", "parallel"), vmem_limit_bytes=VMEM_LIMIT),
        name="outproj",
    )(h2, y_ssm, y_mls, y_nsa, w, gain)


def _mlp_kernel(h_ref, g1_ref, w1_ref, w2_ref, g2_ref, o_ref, u_sc, acc_sc):
    kf = pl.program_id(1)
    last = pl.num_programs(1) - 1

    def partial_ff(u):
        a = jnp.maximum(_dot(u, w1_ref[...].astype(BF16)), 0.0)
        return _dot((a * a).astype(BF16), w2_ref[...].astype(BF16))

    @pl.when(kf == 0)
    def _():
        x = h_ref[...]
        ms = jnp.mean(x * x, axis=-1, keepdims=True)
        u = (x * lax.rsqrt(ms + EPS) * g1_ref[...]).astype(BF16)
        u_sc[...] = u
        acc_sc[...] = partial_ff(u)

    @pl.when((kf > 0) & (kf < last))
    def _():
        acc_sc[...] += partial_ff(u_sc[...])

    @pl.when(kf == last)
    def _():
        f = acc_sc[...] + partial_ff(u_sc[...])
        ms = jnp.mean(f * f, axis=-1, keepdims=True)
        o_ref[...] = h_ref[...] + f * lax.rsqrt(ms + EPS) * g2_ref[...]


def _mlp(h2, g1, w1, w2, layer, g2, tm, tf):
    rows = h2.shape[0]
    return pl.pallas_call(
        _mlp_kernel,
        grid=(rows // tm, D_FF // tf),
        in_specs=[
            pl.BlockSpec((tm, D_MODEL), lambda i, k: (i, 0)),
            pl.BlockSpec((1, D_MODEL), lambda i, k: (0, 0)),
            pl.BlockSpec((None, D_MODEL, tf), lambda i, k: (layer, 0, k)),
            pl.BlockSpec((None, tf, D_MODEL), lambda i, k: (layer, k, 0)),
            pl.BlockSpec((1, D_MODEL), lambda i, k: (0, 0)),
        ],
        out_specs=pl.BlockSpec((tm, D_MODEL), lambda i, k: (i, 0)),
        out_shape=jax.ShapeDtypeStruct((rows, D_MODEL), F32),
        scratch_shapes=[pltpu.VMEM((tm, D_MODEL), BF16), pltpu.VMEM((tm, D_MODEL), F32)],
        compiler_params=pltpu.CompilerParams(
            dimension_semantics=("parallel", "arbitrary"), vmem_limit_bytes=VMEM_LIMIT),
        name="mlp",
    )(h2, g1, w1, w2, g2)


def _inproj_pieces():
    return ((0, 1280), (1288, 1800), (1800, 1928), (2056, 2184), (2312, 2440),
            (1928, 2056), (2184, 2312), (2440, 2568),
            (1280, 1288), (None, GATE_COL - 8), (2568, 2580), (None, LANE - GATE_COL - 12),
            (None, GATE_COL), (2580, 2592), (None, LANE - GATE_COL - 12))


def _permute_w_in_kernel(w_ref, o_ref):
    x = w_ref[...]
    col = 0
    for a, b in _inproj_pieces():
        width = b if a is None else b - a
        piece = jnp.zeros((x.shape[0], width), BF16) if a is None else x[:, a:b].astype(BF16)
        o_ref[:, col:col + width] = piece
        col += width
    assert col == D_INP


def _permute_w_in(w_in):
    depth, rows, cols = w_in.shape
    tr = 256
    return pl.pallas_call(
        _permute_w_in_kernel,
        grid=(depth, rows // tr),
        in_specs=[pl.BlockSpec((None, tr, cols), lambda l, i: (l, i, 0))],
        out_specs=pl.BlockSpec((None, tr, D_INP), lambda l, i: (l, i, 0)),
        out_shape=jax.ShapeDtypeStruct((depth, rows, D_INP), BF16),
        compiler_params=pltpu.CompilerParams(
            dimension_semantics=("parallel", "parallel"), vmem_limit_bytes=VMEM_LIMIT),
        name="permute_w_in",
    )(w_in)


def _rope_tables(positions):
    inv = ROPE_THETA ** (-jnp.arange(0, ROPE_DIMS, 2, dtype=F32) / ROPE_DIMS)
    ang = positions.astype(F32)[..., None] * inv
    ones = jnp.ones(ang.shape[:-1] + (1,), F32)
    feats = jnp.concatenate([jnp.cos(ang), jnp.sin(ang), ones], axis=-1).reshape(-1, 2 * ROPE_HALF + 1)
    place = np.zeros((2 * ROPE_HALF + 1, 3 * LANE), np.float32)
    for lane in range(LANE):
        d = lane % NSA_HEAD_DIM
        if d < ROPE_HALF:
            place[d, lane] = 1.0
            place[ROPE_HALF + d, LANE + lane] = -1.0
        elif d < ROPE_DIMS:
            place[d - ROPE_HALF, lane] = 1.0
            place[d, 2 * LANE + lane] = 1.0
        else:
            place[2 * ROPE_HALF, lane] = 1.0
    return jnp.dot(feats, jnp.asarray(place), precision=HIGHEST)


def _s5_params(lam_re, lam_im, b_re, b_im, c_re, c_im, log_dt):
    G, P, Hc = SSM_GROUPS, SSM_STATE, SSM_GROUP
    dt = jnp.exp(log_dt)[:, None]
    mag = jnp.exp(lam_re * dt)
    ang = lam_im * dt
    ab_re = mag * jnp.cos(ang)
    ab_im = mag * jnp.sin(ang)
    den = lam_re * lam_re + lam_im * lam_im
    g_re = ((ab_re - 1.0) * lam_re + ab_im * lam_im) / den
    g_im = (ab_im * lam_re - (ab_re - 1.0) * lam_im) / den
    bb_re = g_re[..., None] * b_re - g_im[..., None] * b_im
    bb_im = g_re[..., None] * b_im + g_im[..., None] * b_re
    same_in = jnp.asarray(np.arange(G * Hc)[:, None] // Hc == np.arange(G * P)[None, :] // P)
    blockdiag_in = lambda t: jnp.where(
        same_in, jnp.tile(jnp.swapaxes(t, 1, 2).reshape(G * Hc, P), (1, G)), 0.0)
    blockdiag_out = lambda t: jnp.where(
        same_in.T, jnp.tile(jnp.swapaxes(t, 1, 2).reshape(G * P, Hc), (1, G)), 0.0)
    bb = jnp.concatenate([blockdiag_in(bb_re), blockdiag_in(bb_im)], axis=1).astype(BF16)
    cc = jnp.concatenate([blockdiag_out(c_re), -blockdiag_out(c_im)], axis=0).astype(BF16)
    a = jnp.stack([ab_re.reshape(-1), ab_im.reshape(-1)], axis=0)
    return bb, a, cc


def _group_mean_matrix(width, group):
    idx = np.arange(width) // group
    return jnp.asarray((idx[:, None] == idx[None, :]).astype(np.float32) / group, BF16)


def kernel(x, positions, ln_mix_pre, ln_mix_post, ln_mlp_pre, ln_mlp_post, w_in, w_out, ssm_lambda_re, ssm_lambda_im, ssm_b_re, ssm_b_im, ssm_c_re, ssm_c_im, ssm_d, ssm_log_dt, ssm_w_glu, mlstm_conv, mlstm_b_i, mlstm_b_f, cmp_pe_k, cmp_w1_k, cmp_w2_k, cmp_pe_v, cmp_w1_v, cmp_w2_v, gn_ssm, gn_mlstm, gn_nsa, mlp_w1, mlp_w2):
    B, S, D = x.shape
    depth = w_in.shape[0]
    assert D == D_MODEL and B == SUBLANE and S % 512 == 0 and S >= WINDOW + NSA_TQ
    G, H = NSA_KV_GROUPS, MLSTM_HEADS
    ts_proj = 1024
    ts_scan = 128

    rope = _rope_tables(positions)
    w_in_p = _permute_w_in(w_in)
    w_out_b = w_out.astype(BF16)
    wglu_b = ssm_w_glu.astype(BF16)
    gm_ssm = _group_mean_matrix(SSM_WIDTH, SSM_GROUP)
    hm_mls = _group_mean_matrix(MLSTM_WIDTH, MLSTM_HEAD_DIM)
    consts = _nsa_consts(S)
    half = CMP_STRIDE * NSA_HEAD_DIM
    w1ab = jnp.stack([jnp.concatenate([cmp_w1_k[:, :half], cmp_w1_k[:, half:]], axis=-1),
                      jnp.concatenate([cmp_w1_v[:, :half], cmp_w1_v[:, half:]], axis=-1)], axis=1).astype(BF16)
    w1f = jnp.stack([cmp_w1_k, cmp_w1_v], axis=1)
    pef = jnp.stack([cmp_pe_k.reshape(depth, 1, -1), cmp_pe_v.reshape(depth, 1, -1)], axis=1)
    w2c = jnp.stack([cmp_w2_k, cmp_w2_v], axis=1).astype(BF16)
    w2ct = jnp.swapaxes(w2c, -1, -2)
    bias_row = jnp.concatenate([mlstm_b_i, mlstm_b_f], axis=-1)[:, :, None]

    bb, a, cc = jax.vmap(_s5_params)(ssm_lambda_re, ssm_lambda_im, ssm_b_re, ssm_b_im, ssm_c_re, ssm_c_im,
                                     ssm_log_dt)
    sh3 = lambda t: t.reshape(B, S, t.shape[-1])

    h = x.reshape(B * S, D)
    for l in range(depth):
        (su, mq, mk, mv, mo, aq, ckv, sk, wk, svt, wvt, gates, gates_t) = _inproj(
            h, ln_mix_pre[l][None], w_in_p, l, rope, B, S, ts_proj, consts[2])

        y_ssm = _s5(sh3(su), bb, a, cc, ssm_d[l][None], wglu_b, l, gm_ssm, gn_ssm[l][None], B, S, ts_scan)

        y_mls = _mlstm(sh3(mq), sh3(mk), sh3(mv), sh3(mo), gates_t,
                       mlstm_conv[l][:, :MLSTM_WIDTH], mlstm_conv[l][:, MLSTM_WIDTH:],
                       bias_row[l], hm_mls, gn_mlstm[l][None], B, S)

        cmp_k, cmp_t = _compress(ckv, w1ab, w1f, pef, w2c, w2ct, l, B, S)
        y_nsa = _nsa(aq, cmp_k, cmp_t, sk, svt, wk, wvt, gates_t,
                     gn_nsa[l].reshape(NSA_HEADS, NSA_HEAD_DIM, 1), consts, B, S)

        h = _outproj(h, y_ssm.reshape(B * S, SSM_WIDTH), y_mls.reshape(B * S, MLSTM_WIDTH), y_nsa,
                     w_out_b, l, ln_mix_post[l][None], B, S, ts_proj)
        h = _mlp(h, ln_mlp_pre[l][None], mlp_w1, mlp_w2, l, ln_mlp_post[l][None], 1024, 1024)
    return h.reshape(B, S, D)
```

```python
import functools
import math

import numpy as np
import jax
import jax.numpy as jnp
from jax import lax
from jax.experimental import pallas as pl
from jax.experimental.pallas import tpu as pltpu

F32 = jnp.float32
BF16 = jnp.bfloat16
HIGHEST = lax.Precision.HIGHEST

D_MODEL = 1024
DEPTH = 4
SSM_WIDTH = 256
SSM_GROUP = 16
SSM_GROUPS = 16
SSM_STATE = 64
SSM_LANES = SSM_GROUPS * SSM_STATE
MLSTM_WIDTH = 256
MLSTM_HEADS = 4
MLSTM_HEAD_DIM = 64
MLSTM_CHUNK = 128
MLSTM_CONV = 4
NSA_WIDTH = 512
NSA_HEAD_DIM = 64
NSA_HEADS = 8
NSA_KV_GROUPS = 2
NSA_REP = NSA_HEADS // NSA_KV_GROUPS
NSA_KV_WIDTH = 128
CMP_BLOCK = 32
CMP_STRIDE = 16
CMP_HIDDEN = 256
SEL_BLOCK = 64
SEL_TOPN = 8
WINDOW = 256
Q_BLOCK = 128
FORCE_SCORE = 1e4
NEG_INF = -1e30
ROPE_THETA = 500000.0
ROPE_DIMS = 16
ROPE_HALF = 8
D_FF = 4096
EPS = 1e-6
D_IN = 2592

LANE = 128
SUBLANE = 8
VMEM_LIMIT = 56 * 1024 * 1024

C_SU, C_MQ, C_MK, C_MV, C_MO = 0, 256, 512, 768, 1024
C_AQ, C_CK, C_SK, C_WK = 1280, 1792, 1920, 2048
C_CV, C_SV, C_WV = 2176, 2304, 2432
C_G0, C_G1 = 2560, 2688
D_INP = 2816
GATE_COL = 16
VAL_ROWS = NSA_HEAD_DIM + 16
Q_SCALE = NSA_HEAD_DIM ** -0.5 * math.log2(math.e)
MLSTM_ROWS = 8
NSA_TQ = 256
SEL_UNROLL = 2


def _dot(a, b, precision=None):
    return jnp.dot(a, b, preferred_element_type=F32, precision=precision)


def _dot_nt(a, b):
    return lax.dot_general(a, b, (((1,), (1,)), ((), ())), preferred_element_type=F32)


def _dot_tn(a, b):
    return lax.dot_general(a, b, (((0,), (0,)), ((), ())), preferred_element_type=F32)


def _sigmoid(x):
    return 1.0 / (1.0 + jnp.exp(-x))


def _dot_split(x, w_bf16):
    hi = x.astype(BF16)
    lo = (x - hi.astype(F32)).astype(BF16)
    return _dot(hi, w_bf16) + _dot(lo, w_bf16)


def _gelu_tanh(x):
    return 0.5 * x * (1.0 + jnp.tanh(math.sqrt(2.0 / math.pi) * (x + 0.044715 * (x * x * x))))


def _log_sigmoid(x):
    return jnp.minimum(x, 0.0) - jnp.log(1.0 + jnp.exp(-jnp.abs(x)))


def _inproj_kernel(x_ref, g_ref, w_ref, rt_ref,
                   su_ref, mq_ref, mk_ref, mv_ref, mo_ref, aq_ref, ckv_ref, sk_ref, wk_ref,
                   svt_ref, wvt_ref, gt_ref, gtt_ref):
    x = x_ref[...]
    ms = jnp.mean(x * x, axis=-1, keepdims=True)
    u = (x * lax.rsqrt(ms + EPS) * g_ref[...]).astype(BF16)
    rc, rs1, rs2 = (rt_ref[:, n * LANE:(n + 1) * LANE] for n in range(3))

    def mm(c0, width):
        return _dot(u, w_ref[:, c0:c0 + width])

    def rope(z):
        return z * rc + pltpu.roll(z, LANE - ROPE_HALF, 1) * rs1 + pltpu.roll(z, ROPE_HALF, 1) * rs2

    su_ref[...] = mm(C_SU, 256)
    mq_ref[...] = mm(C_MQ, 256)
    mk_ref[...] = mm(C_MK, 256)
    mv_ref[...] = mm(C_MV, 256).astype(BF16)
    mo_ref[...] = mm(C_MO, 256)
    def mm_pair(c0):
        z = mm(c0, 2 * LANE)
        return z[:, :LANE], z[:, LANE:]

    def put_heads(ref, first, z):
        ref[first] = z[:, :NSA_HEAD_DIM].astype(BF16)
        ref[first + 1] = z[:, NSA_HEAD_DIM:].astype(BF16)

    for j in range(NSA_HEADS // 4):
        for k, z in enumerate(mm_pair(C_AQ + 2 * LANE * j)):
            put_heads(aq_ref, 4 * j + 2 * k, rope(z) * Q_SCALE)
    z_ck, z_sk = mm_pair(C_CK)
    z_wk, z_cv = mm_pair(C_WK)
    z_sv, z_wv = mm_pair(C_SV)
    ckv_ref[0] = rope(z_ck)
    ckv_ref[1] = z_cv
    put_heads(sk_ref, 0, rope(z_sk))
    put_heads(wk_ref, 0, rope(z_wk))

    def put_chunks_t(ref, z):
        zt = jnp.transpose(z)
        width = ref.shape[-1]
        ones = jnp.ones((VAL_ROWS - NSA_HEAD_DIM, width), BF16)
        for g in range(NSA_KV_GROUPS):
            for j in range(ref.shape[1]):
                ref[g, j, :NSA_HEAD_DIM] = zt[g * NSA_HEAD_DIM:(g + 1) * NSA_HEAD_DIM,
                                              j * width:(j + 1) * width].astype(BF16)
                ref[g, j, NSA_HEAD_DIM:] = ones

    put_chunks_t(svt_ref, z_sv)
    put_chunks_t(wvt_ref, z_wv)
    z_g0, z_g1 = mm_pair(C_G0)
    gt_ref[...] = z_g0
    gtt_ref[0] = jnp.transpose(z_g0)
    gtt_ref[1] = jnp.transpose(z_g1)


def _inproj(h2, gain, w, layer, rope, B, S, ts, ck):
    nt = S // ts
    BS = B * S
    row = lambda b, i: (b * nt + i, 0)
    full = lambda b, i: (0, 0)
    headed = lambda b, i: (b, 0, i, 0)
    paired = lambda b, i: (0, b * nt + i, 0)
    in_specs = [
        pl.BlockSpec((ts, D_MODEL), row),
        pl.BlockSpec((1, D_MODEL), full),
        pl.BlockSpec((None, D_MODEL, D_INP), lambda b, i: (layer, 0, 0)),
        pl.BlockSpec((ts, 3 * LANE), row),
    ]
    kv_shape = jax.ShapeDtypeStruct((B, NSA_KV_GROUPS, S, NSA_HEAD_DIM), BF16)
    kv_spec = pl.BlockSpec((None, NSA_KV_GROUPS, ts, NSA_HEAD_DIM), headed)
    out_shape = [
        jax.ShapeDtypeStruct((BS, SSM_WIDTH), F32),
        jax.ShapeDtypeStruct((BS, MLSTM_WIDTH), F32),
        jax.ShapeDtypeStruct((BS, MLSTM_WIDTH), F32),
        jax.ShapeDtypeStruct((BS, MLSTM_WIDTH), BF16),
        jax.ShapeDtypeStruct((BS, MLSTM_WIDTH), F32),
        jax.ShapeDtypeStruct((B, NSA_HEADS, S, NSA_HEAD_DIM), BF16),
        jax.ShapeDtypeStruct((2, BS, NSA_KV_WIDTH), F32),
        kv_shape, kv_shape,
        jax.ShapeDtypeStruct((B, NSA_KV_GROUPS, S // ck, VAL_ROWS, ck), BF16),
        jax.ShapeDtypeStruct((B, NSA_KV_GROUPS, S // Q_BLOCK, VAL_ROWS, Q_BLOCK), BF16),
        jax.ShapeDtypeStruct((BS, LANE), F32),
        jax.ShapeDtypeStruct((NSA_KV_GROUPS, B, LANE, S), F32),
    ]
    out_specs = [
        pl.BlockSpec((ts, SSM_WIDTH), row),
        pl.BlockSpec((ts, MLSTM_WIDTH), row),
        pl.BlockSpec((ts, MLSTM_WIDTH), row),
        pl.BlockSpec((ts, MLSTM_WIDTH), row),
        pl.BlockSpec((ts, MLSTM_WIDTH), row),
        pl.BlockSpec((None, NSA_HEADS, ts, NSA_HEAD_DIM), headed),
        pl.BlockSpec((2, ts, NSA_KV_WIDTH), paired),
        kv_spec, kv_spec,
        pl.BlockSpec((None, NSA_KV_GROUPS, ts // ck, VAL_ROWS, ck), lambda b, i: (b, 0, i, 0, 0)),
        pl.BlockSpec((None, NSA_KV_GROUPS, ts // Q_BLOCK, VAL_ROWS, Q_BLOCK), lambda b, i: (b, 0, i, 0, 0)),
        pl.BlockSpec((ts, LANE), row),
        pl.BlockSpec((NSA_KV_GROUPS, None, LANE, ts), lambda b, i: (0, b, 0, i)),
    ]
    return pl.pallas_call(
        _inproj_kernel,
        grid=(B, nt),
        in_specs=in_specs,
        out_specs=out_specs,
        out_shape=out_shape,
        compiler_params=pltpu.CompilerParams(
            dimension_semantics=("parallel", "parallel"), vmem_limit_bytes=VMEM_LIMIT),
        name="inproj",
    )(h2, gain, w, rope)


def _s5_kernel(u_ref, bb_ref, a_ref, cc_ref, d_ref, wg_ref, gm_ref, gain_ref, o_ref, x_sc, st_sc, tm_sc, *, B, ts):
    @pl.when(pl.program_id(0) == 0)
    def _():
        st_sc[...] = jnp.zeros_like(st_sc)

    nl = SSM_WIDTH // LANE
    for b in range(B):
        for c in range(nl):
            tm_sc[c, pl.ds(b, ts, stride=B), :] = u_ref[b, :, c * LANE:(c + 1) * LANE]
    u = jnp.concatenate([tm_sc[c] for c in range(nl)], axis=1)
    ub = u.astype(BF16)
    for part in range(2):
        cols = slice(part * SSM_LANES, (part + 1) * SSM_LANES)
        x_sc[:, cols] = _dot(ub, bb_ref[:, cols])
    ar = jnp.broadcast_to(a_ref[0:1, :], (B, SSM_LANES))
    ai = jnp.broadcast_to(a_ref[1:2, :], (B, SSM_LANES))

    def step(t, carry):
        xr, xi = carry
        r = pl.multiple_of(t * B, B)
        br = x_sc[pl.ds(r, B), 0:SSM_LANES]
        bi = x_sc[pl.ds(r, B), SSM_LANES:2 * SSM_LANES]
        nr = ar * xr - ai * xi + br
        ni = ar * xi + ai * xr + bi
        x_sc[pl.ds(r, B), 0:SSM_LANES] = nr
        x_sc[pl.ds(r, B), SSM_LANES:2 * SSM_LANES] = ni
        return nr, ni

    xr, xi = lax.fori_loop(0, ts, step, (st_sc[0], st_sc[1]), unroll=4)
    st_sc[0] = xr
    st_sc[1] = xi

    half = (ts * B) // 2
    y = jnp.concatenate([_dot(x_sc[r * half:(r + 1) * half, :].astype(BF16), cc_ref[...]) for r in range(2)],
                        axis=0) + d_ref[...] * u
    y = _gelu_tanh(y)
    y = y * _sigmoid(_dot(y.astype(BF16), wg_ref[...]))
    ms = _dot_split(y * y, gm_ref[...])
    y = y * lax.rsqrt(ms + EPS) * gain_ref[...]
    for c in range(nl):
        tm_sc[c] = y[:, c * LANE:(c + 1) * LANE]
    for b in range(B):
        o_ref[b] = jnp.concatenate(
            [tm_sc[c, pl.ds(b, ts, stride=B), :] for c in range(nl)], axis=1).astype(BF16)


def _s5(u, bb, a, cc, d, wg, layer, gm, gain, B, S, ts):
    rows = ts * B
    full = lambda i: (0, 0)
    lsel = lambda i: (layer, 0, 0)
    return pl.pallas_call(
        functools.partial(_s5_kernel, B=B, ts=ts),
        grid=(S // ts,),
        in_specs=[
            pl.BlockSpec((B, ts, SSM_WIDTH), lambda i: (0, i, 0)),
            pl.BlockSpec((None, SSM_WIDTH, 2 * SSM_LANES), lsel),
            pl.BlockSpec((None, 2, SSM_LANES), lsel),
            pl.BlockSpec((None, 2 * SSM_LANES, SSM_WIDTH), lsel),
            pl.BlockSpec((1, SSM_WIDTH), full),
            pl.BlockSpec((None, SSM_WIDTH, SSM_WIDTH), lsel),
            pl.BlockSpec((SSM_WIDTH, SSM_WIDTH), full),
            pl.BlockSpec((1, SSM_WIDTH), full),
        ],
        out_specs=pl.BlockSpec((B, ts, SSM_WIDTH), lambda i: (0, i, 0)),
        out_shape=jax.ShapeDtypeStruct((B, S, SSM_WIDTH), BF16),
        scratch_shapes=[pltpu.VMEM((rows, 2 * SSM_LANES), F32), pltpu.VMEM((2, B, SSM_LANES), F32),
                        pltpu.VMEM((SSM_WIDTH // LANE, rows, LANE), F32)],
        compiler_params=pltpu.CompilerParams(
            dimension_semantics=("arbitrary",), vmem_limit_bytes=VMEM_LIMIT),
        name="s5",
    )(u, bb, a, cc, d, wg, gm, gain)


def _mlstm_kernel(q_ref, k_ref, v_ref, o_ref, gr_ref, cwq_ref, cwk_ref, br_ref, hm_ref,
                  gain_ref, y_ref, qt_sc, kt_sc, c_sc, m_sc, *, B):
    L, H, Dh, W = MLSTM_CHUNK, MLSTM_HEADS, MLSTM_HEAD_DIM, MLSTM_WIDTH

    @pl.when(pl.program_id(0) == 0)
    def _():
        qt_sc[...] = jnp.zeros_like(qt_sc)
        kt_sc[...] = jnp.zeros_like(kt_sc)
        c_sc[...] = jnp.zeros_like(c_sc)
        m_sc[...] = jnp.zeros_like(m_sc)

    visible = lax.broadcasted_iota(jnp.int32, (L, L), 0) <= lax.broadcasted_iota(jnp.int32, (L, L), 1)
    triu = visible.astype(F32)
    lane_w = lax.broadcasted_iota(jnp.int32, (1, W), 1) // Dh
    bd_mask = ((lax.broadcasted_iota(jnp.int32, (2 * W, W), 0) % W) // Dh
               == lax.broadcasted_iota(jnp.int32, (2 * W, W), 1) // Dh)
    row8 = lax.broadcasted_iota(jnp.int32, (SUBLANE, W), 0)
    cwq = cwq_ref[...]
    cwk = cwk_ref[...]
    ones_rows = jnp.ones((Dh, L), F32)

    def conv_silu(x, tail, w):
        acc = x * w[MLSTM_CONV - 1:MLSTM_CONV, :]
        for sft in range(1, MLSTM_CONV):
            xs = pltpu.roll(x, sft, 0)
            head = jnp.where(row8 < sft, pltpu.roll(tail, sft, 0), xs[:SUBLANE])
            xs = jnp.concatenate([head, xs[SUBLANE:]], axis=0)
            acc = acc + xs * w[MLSTM_CONV - 1 - sft:MLSTM_CONV - sft, :]
        return acc * _sigmoid(acc)

    def per_group(grp, _):
        bs = [grp * MLSTM_ROWS + n for n in range(MLSTM_ROWS)]
        st = [dict() for _ in bs]

        for b, d in zip(bs, st):
            q_raw = q_ref[b]
            k_raw = k_ref[b]
            d['q'] = conv_silu(q_raw, qt_sc[b], cwq)
            d['k'] = conv_silu(k_raw, kt_sc[b], cwk) * (Dh ** -0.5)
            qt_sc[b] = q_raw[L - SUBLANE:, :]
            kt_sc[b] = k_raw[L - SUBLANE:, :]
            d['gr'] = gr_ref[b] + br_ref[...]
        for d in st:
            d['brow'] = _dot(_log_sigmoid(d['gr']), triu, precision=HIGHEST)

        for b, d in zip(bs, st):
            gr, brow = d['gr'], d['brow']
            ccol = jnp.transpose(brow - pltpu.roll(gr, H, 0))
            m_all = m_sc[b]
            for key in ('w_intra', 'w_inter', 'e_mt', 'w_k', 'dec', 'm_new'):
                d[key] = []
            for hh in range(H):
                b_r = brow[H + hh:H + hh + 1, :]
                i_r = gr[hh:hh + 1, :]
                m_prev = m_all[hh:hh + 1, 0:1]
                dm = jnp.where(visible, b_r - ccol[:, H + hh:H + hh + 1], NEG_INF)
                inter = b_r + m_prev
                mt = jnp.maximum(inter, jnp.max(dm, axis=0, keepdims=True))
                d['w_intra'].append(jnp.exp(dm - mt))
                d['w_inter'].append(jnp.exp(inter - mt))
                d['e_mt'].append(jnp.exp(-mt))
                b_last = b_r[:, L - 1:L]
                logw = b_last - b_r + i_r
                mn = jnp.maximum(b_last + m_prev, jnp.max(logw, axis=1, keepdims=True))
                d['w_k'].append(jnp.exp(logw - mn))
                d['dec'].append(jnp.exp(b_last + m_prev - mn))
                d['m_new'].append(mn)
            d['qb'] = d['q'].astype(BF16)
            d['kb'] = d['k'].astype(BF16)
            d['vt'] = jnp.transpose(v_ref[b].astype(F32))
            d['c_t'] = c_sc[b]

        for d in st:
            d['qc'] = _dot(d['c_t'].astype(BF16), jnp.transpose(d['q']).astype(BF16))
            d['s_t'] = [_dot_nt(d['kb'], jnp.where(lane_w == hh, d['qb'], jnp.zeros_like(d['qb'])))
                        for hh in range(H)]
        for d in st:
            d['r'] = []
            for hh in range(H):
                v_aug = jnp.concatenate([d['vt'][hh * Dh:(hh + 1) * Dh], ones_rows], axis=0).astype(BF16)
                d['r'].append(_dot(v_aug, (d['s_t'][hh] * d['w_intra'][hh]).astype(BF16)))

        for b, d in zip(bs, st):
            h_t = []
            for hh in range(H):
                ch = slice(hh * Dh, (hh + 1) * Dh)
                num = d['w_inter'][hh] * d['qc'][ch] + d['r'][hh][:Dh]
                den = d['w_inter'][hh] * d['qc'][W + hh * Dh:W + (hh + 1) * Dh] + d['r'][hh][Dh:]
                h_t.append(num / jnp.maximum(jnp.abs(den), d['e_mt'][hh]))
            hout = jnp.transpose(jnp.concatenate(h_t, axis=0))
            d['y'] = _sigmoid(o_ref[b]) * hout
            d['vw'] = jnp.concatenate(
                [d['vt'][hh * Dh:(hh + 1) * Dh] * d['w_k'][hh] for hh in range(H)]
                + [jnp.broadcast_to(d['w_k'][hh], (Dh, L)) for hh in range(H)], axis=0).astype(BF16)
        for d in st:
            d['ms'] = _dot_split(d['y'] * d['y'], hm_ref[...])
            d['upd'] = _dot(d['vw'], d['kb'])

        for b, d in zip(bs, st):
            y_ref[b] = (d['y'] * lax.rsqrt(d['ms'] + EPS) * gain_ref[...]).astype(BF16)
            decay = d['dec'][H - 1]
            for hh in range(H - 2, -1, -1):
                decay = jnp.where(lane_w == hh, d['dec'][hh], decay)
            c_sc[b] = decay * d['c_t'] + jnp.where(bd_mask, d['upd'], 0.0)
            for hh in range(H):
                m_sc[b, hh:hh + 1, :] = jnp.broadcast_to(d['m_new'][hh], (1, LANE))
        return 0

    lax.fori_loop(0, B // MLSTM_ROWS, per_group, 0)


def _mlstm(mq, mk, mv, mo, grow, cwq, cwk, brow, hm, gain, B, S):
    L, W = MLSTM_CHUNK, MLSTM_WIDTH
    seq = lambda c: (0, c, 0)
    full = lambda c: (0, 0)
    return pl.pallas_call(
        functools.partial(_mlstm_kernel, B=B),
        grid=(S // L,),
        in_specs=[
            pl.BlockSpec((B, L, W), seq),
            pl.BlockSpec((B, L, W), seq),
            pl.BlockSpec((B, L, W), seq),
            pl.BlockSpec((B, L, W), seq),
            pl.BlockSpec((None, B, SUBLANE, L), lambda c: (0, 0, 0, c)),
            pl.BlockSpec((MLSTM_CONV, W), full),
            pl.BlockSpec((MLSTM_CONV, W), full),
            pl.BlockSpec((SUBLANE, 1), full),
            pl.BlockSpec((W, W), full),
            pl.BlockSpec((1, W), full),
        ],
        out_specs=pl.BlockSpec((B, L, W), seq),
        out_shape=jax.ShapeDtypeStruct((B, S, W), BF16),
        scratch_shapes=[
            pltpu.VMEM((B, SUBLANE, W), F32),
            pltpu.VMEM((B, SUBLANE, W), F32),
            pltpu.VMEM((B, 2 * W, W), F32),
            pltpu.VMEM((B, SUBLANE, LANE), F32),
        ],
        compiler_params=pltpu.CompilerParams(
            dimension_semantics=("arbitrary",), vmem_limit_bytes=VMEM_LIMIT),
        name="mlstm",
    )(mq, mk, mv, mo, grow, cwq, cwk, brow, hm, gain)


def _compress_kernel(c_ref, w1ab_ref, w1k_ref, w1v_ref, pe_ref, w2_ref, w2t_ref, o_ref, ot_ref, ch_sc):
    G, Dh = NSA_KV_GROUPS, NSA_HEAD_DIM
    rows = ch_sc.shape[1]
    n = rows // G
    both = range(2)
    for i in both:
        for r in range(CMP_STRIDE):
            tok = c_ref[i, pl.ds(r, n, stride=CMP_STRIDE), :]
            for g in range(G):
                ch_sc[i, g * n:(g + 1) * n, r * Dh:(r + 1) * Dh] = tok[:, g * Dh:(g + 1) * Dh]
    ab = [_dot(ch_sc[i].astype(BF16), w1ab_ref[i]) for i in both]
    const = [_dot(pe_ref[i], w1[...], precision=HIGHEST) for i, w1 in zip(both, (w1k_ref, w1v_ref))]
    act = []
    for i in both:
        hid = ab[i][:, :CMP_HIDDEN] + pltpu.roll(ab[i][:, CMP_HIDDEN:], rows - 1, 0) + const[i]
        act.append(_gelu_tanh(hid).astype(BF16))
    for i in both:
        o_ref[i] = _dot(act[i], w2_ref[i]).astype(BF16)
        ot_ref[i] = _dot_nt(w2t_ref[i], act[i]).astype(BF16)


def _compress(ckv, w1ab, w1k, w1v, pe, w2, w2t, layer, B, S):
    G, Dh = NSA_KV_GROUPS, NSA_HEAD_DIM
    n = S // CMP_STRIDE
    width = CMP_STRIDE * Dh
    wsel = lambda b: (layer, 0, 0, 0)
    return pl.pallas_call(
        _compress_kernel,
        grid=(B,),
        in_specs=[
            pl.BlockSpec((2, S, G * Dh), lambda b: (0, b, 0)),
            pl.BlockSpec((None, 2, width, 2 * CMP_HIDDEN), wsel),
            pl.BlockSpec((None, 2 * width, CMP_HIDDEN), lambda b: (layer, 0, 0)),
            pl.BlockSpec((None, 2 * width, CMP_HIDDEN), lambda b: (layer, 0, 0)),
            pl.BlockSpec((None, 2, 1, 2 * width), wsel),
            pl.BlockSpec((None, 2, CMP_HIDDEN, Dh), wsel),
            pl.BlockSpec((None, 2, Dh, CMP_HIDDEN), wsel),
        ],
        out_specs=[pl.BlockSpec((2, None, G * n, Dh), lambda b: (0, b, 0, 0)),
                   pl.BlockSpec((2, None, Dh, G * n), lambda b: (0, b, 0, 0))],
        out_shape=[jax.ShapeDtypeStruct((2, B, G * n, Dh), BF16),
                   jax.ShapeDtypeStruct((2, B, Dh, G * n), BF16)],
        scratch_shapes=[pltpu.VMEM((2, G * n, width), F32)],
        compiler_params=pltpu.CompilerParams(
            dimension_semantics=("parallel",), vmem_limit_bytes=VMEM_LIMIT),
        name="compress",
    )(ckv, w1ab, w1k, w1v, pe, w2, w2t)


def _nsa_kernel(q_ref, kc_ref, vct_ref, ks_ref, vst_ref, kw_ref, vwt_ref, gtt_ref, gain_ref,
                ovt_ref, et_ref, o_ref, *, n_sel, n_top, ck, unroll):
    TQ, R, Dh, G = NSA_TQ, NSA_REP, NSA_HEAD_DIM, NSA_KV_GROUPS
    groups = range(G)
    i = pl.program_id(1)
    t0 = i * TQ
    qs = [q_ref[g * R:(g + 1) * R].reshape(R * TQ, Dh) for g in groups]
    tq1 = t0 + lax.broadcasted_iota(jnp.int32, (1, TQ), 1)
    heads = lambda t: jnp.concatenate([t] * R, axis=1)

    ncmp = kc_ref.shape[0] // G
    wkeys = WINDOW + TQ
    nwb = wkeys // Q_BLOCK
    ws = pl.multiple_of(jnp.maximum(t0 - WINDOW, 0), Q_BLOCK)
    wb0 = ws // Q_BLOCK
    sc = [_dot_nt(kc_ref[g * ncmp:(g + 1) * ncmp, :], qs[g]) for g in groups]
    sw = [_dot_nt(kw_ref[g, pl.ds(ws, wkeys), :], qs[g]) for g in groups]

    kpos = ws + lax.broadcasted_iota(jnp.int32, (wkeys, 1), 0)
    wbias = heads(jnp.where((kpos <= tq1) & (tq1 - kpos < WINDOW), 0.0, NEG_INF))
    ow, l_w = [], []
    for g in groups:
        swb = sw[g] + wbias
        pw = jnp.exp2(swb - jnp.max(swb, axis=0, keepdims=True))
        vwt = jnp.concatenate([vwt_ref[g, wb0 + j] for j in range(nwb)], axis=1)
        owl = _dot(vwt, pw.astype(BF16))
        ow.append(owl[:Dh])
        l_w.append(owl[Dh:Dh + 1])

    cend = lax.broadcasted_iota(jnp.int32, (ncmp, 1), 0) * CMP_STRIDE + (CMP_BLOCK - 1)
    cmask = heads(cend <= tq1)
    pc = []
    for g in groups:
        scm = jnp.where(cmask, sc[g], NEG_INF)
        ec = jnp.where(cmask, jnp.exp2(scm - jnp.max(scm, axis=0, keepdims=True)), 0.0)
        pc.append(ec * (1.0 / jnp.maximum(jnp.sum(ec, axis=0, keepdims=True), 1e-30)))
    oc = [_dot(vct_ref[:, g * ncmp:(g + 1) * ncmp], pc[g].astype(BF16)) for g in groups]

    imp = []
    for g in groups:
        psum = pc[g][:, 0:TQ]
        for r in range(1, R):
            psum = psum + pc[g][:, r * TQ:(r + 1) * TQ]
        imp.append(_dot(ovt_ref[...], psum, precision=HIGHEST))

    blk = lax.broadcasted_iota(jnp.int32, (n_sel, 1), 0)
    valid = blk * SEL_BLOCK <= tq1
    forced = (blk == 0) | (blk == tq1 // SEL_BLOCK)
    selb = []
    for g in groups:
        val = jnp.where(forced, FORCE_SCORE, jnp.where(valid, imp[g], -FORCE_SCORE))
        rank = jnp.zeros((n_sel, TQ), F32)
        for jp in range(n_sel):
            other = val[jp:jp + 1, :]
            wins = jnp.where(blk > jp, jnp.where(other >= val, 1.0, 0.0), jnp.where(other > val, 1.0, 0.0))
            rank = rank + wins
        selb.append(jnp.where(rank < n_top, 0.0, NEG_INF).astype(BF16))

    def scores(g, c, causal):
        k0 = c * ck
        bias = _dot(et_ref[c], selb[g])
        if causal:
            kpos = k0 + lax.broadcasted_iota(jnp.int32, (ck, 1), 0)
            bias = jnp.where(kpos <= tq1, bias, NEG_INF)
        return _dot_nt(ks_ref[g, pl.ds(k0, ck), :], qs[g]) + heads(bias)

    def update(g, c, s, carry):
        m, acc = carry
        mn = jnp.maximum(m, jnp.max(s, axis=0, keepdims=True))
        p = jnp.exp2(s - mn)
        acc = jnp.exp2(m - mn) * acc + _dot(vst_ref[g, c], p.astype(BF16))
        return mn, acc

    def chunk_group(cg, carry, causal):
        cs = [cg * unroll + sub for sub in range(unroll)]
        ss = [[scores(g, c, causal) for g in groups] for c in cs]
        carry = list(carry)
        for c, s in zip(cs, ss):
            for g in groups:
                carry[g] = update(g, c, s[g], carry[g])
        return tuple(carry)

    n_chunks = (t0 + TQ + ck - 1) // ck
    init = tuple((jnp.full((1, R * TQ), NEG_INF, F32), jnp.zeros((VAL_ROWS, R * TQ), F32)) for _ in groups)
    def unrolled(trips):
        def run():
            carry = init
            for cg in range(trips):
                carry = chunk_group(cg, carry, causal=cg == trips - 1)
            return carry
        return run

    max_trips = ks_ref.shape[1] // (ck * unroll)
    sel = lax.switch((n_chunks + unroll - 1) // unroll - 1, [unrolled(t) for t in range(1, max_trips + 1)])

    normed = []
    for g in groups:
        gs = _sigmoid(gtt_ref[g])
        acc_s, l_s = sel[g][1][:Dh], sel[g][1][Dh:Dh + 1]
        for r in range(R):
            ln = slice(r * TQ, (r + 1) * TQ)
            o = (gs[3 * r:3 * r + 1, :] * oc[g][:, ln]
                 + (gs[3 * r + 1:3 * r + 2, :] / l_s[:, ln]) * acc_s[:, ln]
                 + (gs[3 * r + 2:3 * r + 3, :] / l_w[g][:, ln]) * ow[g][:, ln])
            ms = jnp.mean(o * o, axis=0, keepdims=True)
            normed.append(o * lax.rsqrt(ms + EPS) * gain_ref[g * R + r])
    for pair in range(G * R // 2):
        both = jnp.concatenate(normed[2 * pair:2 * pair + 2], axis=0)
        o_ref[:, pair * 2 * Dh:(pair + 1) * 2 * Dh] = jnp.transpose(both).astype(BF16)


def _nsa(aq, cmp_k, cmp_vt, ks, vst, kw, vwt, gates_t, gain, consts, B, S):
    G, H, TQ, Dh = NSA_KV_GROUPS, NSA_HEADS, NSA_TQ, NSA_HEAD_DIM
    nq = S // TQ
    ncmp = S // CMP_STRIDE
    n_sel = S // SEL_BLOCK
    ovt, emat_t, ck = consts
    k_spec = pl.BlockSpec((None, G, S, Dh), lambda b, i: (b, 0, 0, 0))
    return pl.pallas_call(
        functools.partial(_nsa_kernel, n_sel=n_sel, n_top=min(SEL_TOPN, n_sel), ck=ck, unroll=SEL_UNROLL),
        grid=(B, nq),
        in_specs=[
            pl.BlockSpec((None, H, TQ, Dh), lambda b, i: (b, 0, i, 0)),
            pl.BlockSpec((None, None, G * ncmp, Dh), lambda b, i: (0, b, 0, 0)),
            pl.BlockSpec((None, None, Dh, G * ncmp), lambda b, i: (1, b, 0, 0)),
            k_spec,
            pl.BlockSpec((None, G, S // ck, VAL_ROWS, ck), lambda b, i: (b, 0, 0, 0, 0)),
            k_spec,
            pl.BlockSpec((None, G, S // Q_BLOCK, VAL_ROWS, Q_BLOCK), lambda b, i: (b, 0, 0, 0, 0)),
            pl.BlockSpec((G, None, 2 * SUBLANE, TQ), lambda b, i: (0, b, GATE_COL // (2 * SUBLANE), i)),
            pl.BlockSpec((H, Dh, 1), lambda b, i: (0, 0, 0)),
            pl.BlockSpec(ovt.shape, lambda b, i: (0, 0)),
            pl.BlockSpec(emat_t.shape, lambda b, i: (0, 0, 0)),
        ],
        out_specs=pl.BlockSpec((TQ, H * Dh), lambda b, i: (b * nq + i, 0)),
        out_shape=jax.ShapeDtypeStruct((B * S, H * Dh), BF16),
        compiler_params=pltpu.CompilerParams(
            dimension_semantics=("parallel", "arbitrary"), vmem_limit_bytes=VMEM_LIMIT),
        name="nsa",
    )(aq, cmp_k, cmp_vt, ks, vst, kw, vwt, gates_t, gain, ovt, emat_t)


def _nsa_consts(S):
    n_cmp = S // CMP_STRIDE
    n_sel = S // SEL_BLOCK
    ck = 256
    i = np.arange(n_cmp)[:, None]
    j = np.arange(n_sel)[None, :]
    lo = np.maximum(i * CMP_STRIDE, j * SEL_BLOCK)
    hi = np.minimum(i * CMP_STRIDE + CMP_BLOCK, (j + 1) * SEL_BLOCK)
    ov = np.maximum(hi - lo, 0) / CMP_STRIDE
    ov[n_cmp - 1] = 0.0
    key = np.arange(S)
    emat_t = (key[:, None] // SEL_BLOCK == np.arange(n_sel)[None, :]).astype(np.float32)
    return (jnp.asarray(ov.T, F32), jnp.asarray(emat_t.reshape(S // ck, ck, n_sel), BF16), ck)


def _outproj_kernel(h_ref, ys_ref, ym_ref, yn_ref, w_ref, g_ref, o_ref):
    wbf = lambda r0, r1: w_ref[r0:r1, :].astype(BF16)
    acc = _dot(ys_ref[...], wbf(0, SSM_WIDTH))
    acc = acc + _dot(ym_ref[...], wbf(SSM_WIDTH, SSM_WIDTH + MLSTM_WIDTH))
    acc = acc + _dot(yn_ref[...], wbf(SSM_WIDTH + MLSTM_WIDTH, D_MODEL))
    ms = jnp.mean(acc * acc, axis=-1, keepdims=True)
    o_ref[...] = h_ref[...] + acc * lax.rsqrt(ms + EPS) * g_ref[...]


def _outproj(h2, y_ssm, y_mls, y_nsa, w, layer, gain, B, S, ts):
    nt = S // ts
    row = lambda b, i: (b * nt + i, 0)
    full = lambda b, i: (0, 0)
    return pl.pallas_call(
        _outproj_kernel,
        grid=(B, nt),
        in_specs=[
            pl.BlockSpec((ts, D_MODEL), row),
            pl.BlockSpec((ts, SSM_WIDTH), row),
            pl.BlockSpec((ts, MLSTM_WIDTH), row),
            pl.BlockSpec((ts, NSA_WIDTH), row),
            pl.BlockSpec((None, D_MODEL, D_MODEL), lambda b, i: (layer, 0, 0)),
            pl.BlockSpec((1, D_MODEL), full),
        ],
        out_specs=pl.BlockSpec((ts, D_MODEL), row),
        out_shape=jax.ShapeDtypeStruct((B * S, D_MODEL), F32),
        compiler_params=pltpu.CompilerParams(
            dimension_semantics=("parallel", "parallel"), vmem_limit_bytes=VMEM_LIMIT),
        name="outproj",
    )(h2, y_ssm, y_mls, y_nsa, w, gain)


def _mlp_kernel(h_ref, g1_ref, w1_ref, w2_ref, g2_ref, o_ref, u_sc, acc_sc):
    kf = pl.program_id(1)
    last = pl.num_programs(1) - 1

    def partial_ff(u):
        a = jnp.maximum(_dot(u, w1_ref[...].astype(BF16)), 0.0)
        return _dot((a * a).astype(BF16), w2_ref[...].astype(BF16))

    @pl.when(kf == 0)
    def _():
        x = h_ref[...]
        ms = jnp.mean(x * x, axis=-1, keepdims=True)
        u = (x * lax.rsqrt(ms + EPS) * g1_ref[...]).astype(BF16)
        u_sc[...] = u
        acc_sc[...] = partial_ff(u)

    @pl.when((kf > 0) & (kf < last))
    def _():
        acc_sc[...] += partial_ff(u_sc[...])

    @pl.when(kf == last)
    def _():
        f = acc_sc[...] + partial_ff(u_sc[...])
        ms = jnp.mean(f * f, axis=-1, keepdims=True)
        o_ref[...] = h_ref[...] + f * lax.rsqrt(ms + EPS) * g2_ref[...]


def _mlp(h2, g1, w1, w2, layer, g2, tm, tf):
    rows = h2.shape[0]
    return pl.pallas_call(
        _mlp_kernel,
        grid=(rows // tm, D_FF // tf),
        in_specs=[
            pl.BlockSpec((tm, D_MODEL), lambda i, k: (i, 0)),
            pl.BlockSpec((1, D_MODEL), lambda i, k: (0, 0)),
            pl.BlockSpec((None, D_MODEL, tf), lambda i, k: (layer, 0, k)),
            pl.BlockSpec((None, tf, D_MODEL), lambda i, k: (layer, k, 0)),
            pl.BlockSpec((1, D_MODEL), lambda i, k: (0, 0)),
        ],
        out_specs=pl.BlockSpec((tm, D_MODEL), lambda i, k: (i, 0)),
        out_shape=jax.ShapeDtypeStruct((rows, D_MODEL), F32),
        scratch_shapes=[pltpu.VMEM((tm, D_MODEL), BF16), pltpu.VMEM((tm, D_MODEL), F32)],
        compiler_params=pltpu.CompilerParams(
            dimension_semantics=("parallel", "arbitrary"), vmem_limit_bytes=VMEM_LIMIT),
        name="mlp",
    )(h2, g1, w1, w2, g2)


def _inproj_pieces():
    return ((0, 1280), (1288, 1800), (1800, 1928), (2056, 2184), (2312, 2440),
            (1928, 2056), (2184, 2312), (2440, 2568),
            (1280, 1288), (None, GATE_COL - 8), (2568, 2580), (None, LANE - GATE_COL - 12),
            (None, GATE_COL), (2580, 2592), (None, LANE - GATE_COL - 12))


def _permute_w_in_kernel(w_ref, o_ref):
    x = w_ref[...]
    col = 0
    for a, b in _inproj_pieces():
        width = b if a is None else b - a
        piece = jnp.zeros((x.shape[0], width), BF16) if a is None else x[:, a:b].astype(BF16)
        o_ref[:, col:col + width] = piece
        col += width
    assert col == D_INP


def _permute_w_in(w_in):
    depth, rows, cols = w_in.shape
    tr = 256
    return pl.pallas_call(
        _permute_w_in_kernel,
        grid=(depth, rows // tr),
        in_specs=[pl.BlockSpec((None, tr, cols), lambda l, i: (l, i, 0))],
        out_specs=pl.BlockSpec((None, tr, D_INP), lambda l, i: (l, i, 0)),
        out_shape=jax.ShapeDtypeStruct((depth, rows, D_INP), BF16),
        compiler_params=pltpu.CompilerParams(
            dimension_semantics=("parallel", "parallel"), vmem_limit_bytes=VMEM_LIMIT),
        name="permute_w_in",
    )(w_in)


def _rope_tables(positions):
    inv = ROPE_THETA ** (-jnp.arange(0, ROPE_DIMS, 2, dtype=F32) / ROPE_DIMS)
    ang = positions.astype(F32)[..., None] * inv
    ones = jnp.ones(ang.shape[:-1] + (1,), F32)
    feats = jnp.concatenate([jnp.cos(ang), jnp.sin(ang), ones], axis=-1).reshape(-1, 2 * ROPE_HALF + 1)
    place = np.zeros((2 * ROPE_HALF + 1, 3 * LANE), np.float32)
    for lane in range(LANE):
        d = lane % NSA_HEAD_DIM
        if d < ROPE_HALF:
            place[d, lane] = 1.0
            place[ROPE_HALF + d, LANE + lane] = -1.0
        elif d < ROPE_DIMS:
            place[d - ROPE_HALF, lane] = 1.0
            place[d, 2 * LANE + lane] = 1.0
        else:
            place[2 * ROPE_HALF, lane] = 1.0
    return jnp.dot(feats, jnp.asarray(place), precision=HIGHEST)


def _s5_params(lam_re, lam_im, b_re, b_im, c_re, c_im, log_dt):
    G, P, Hc = SSM_GROUPS, SSM_STATE, SSM_GROUP
    dt = jnp.exp(log_dt)[:, None]
    mag = jnp.exp(lam_re * dt)
    ang = lam_im * dt
    ab_re = mag * jnp.cos(ang)
    ab_im = mag * jnp.sin(ang)
    den = lam_re * lam_re + lam_im * lam_im
    g_re = ((ab_re - 1.0) * lam_re + ab_im * lam_im) / den
    g_im = (ab_im * lam_re - (ab_re - 1.0) * lam_im) / den
    bb_re = g_re[..., None] * b_re - g_im[..., None] * b_im
    bb_im = g_re[..., None] * b_im + g_im[..., None] * b_re
    same_in = jnp.asarray(np.arange(G * Hc)[:, None] // Hc == np.arange(G * P)[None, :] // P)
    blockdiag_in = lambda t: jnp.where(
        same_in, jnp.tile(jnp.swapaxes(t, 1, 2).reshape(G * Hc, P), (1, G)), 0.0)
    blockdiag_out = lambda t: jnp.where(
        same_in.T, jnp.tile(jnp.swapaxes(t, 1, 2).reshape(G * P, Hc), (1, G)), 0.0)
    bb = jnp.concatenate([blockdiag_in(bb_re), blockdiag_in(bb_im)], axis=1).astype(BF16)
    cc = jnp.concatenate([blockdiag_out(c_re), -blockdiag_out(c_im)], axis=0).astype(BF16)
    a = jnp.stack([ab_re.reshape(-1), ab_im.reshape(-1)], axis=0)
    return bb, a, cc


def _group_mean_matrix(width, group):
    idx = np.arange(width) // group
    return jnp.asarray((idx[:, None] == idx[None, :]).astype(np.float32) / group, BF16)


def kernel(x, positions, ln_mix_pre, ln_mix_post, ln_mlp_pre, ln_mlp_post, w_in, w_out, ssm_lambda_re, ssm_lambda_im, ssm_b_re, ssm_b_im, ssm_c_re, ssm_c_im, ssm_d, ssm_log_dt, ssm_w_glu, mlstm_conv, mlstm_b_i, mlstm_b_f, cmp_pe_k, cmp_w1_k, cmp_w2_k, cmp_pe_v, cmp_w1_v, cmp_w2_v, gn_ssm, gn_mlstm, gn_nsa, mlp_w1, mlp_w2):
    B, S, D = x.shape
    depth = w_in.shape[0]
    assert D == D_MODEL and B == SUBLANE and S % 512 == 0 and S >= WINDOW + NSA_TQ
    G, H = NSA_KV_GROUPS, MLSTM_HEADS
    ts_proj = 1024
    ts_scan = 128

    rope = _rope_tables(positions)
    w_in_p = _permute_w_in(w_in)
    wglu_b = ssm_w_glu.astype(BF16)
    gm_ssm = _group_mean_matrix(SSM_WIDTH, SSM_GROUP)
    hm_mls = _group_mean_matrix(MLSTM_WIDTH, MLSTM_HEAD_DIM)
    consts = _nsa_consts(S)
    half = CMP_STRIDE * NSA_HEAD_DIM
    w1ab = jnp.stack([jnp.concatenate([cmp_w1_k[:, :half], cmp_w1_k[:, half:]], axis=-1),
                      jnp.concatenate([cmp_w1_v[:, :half], cmp_w1_v[:, half:]], axis=-1)], axis=1).astype(BF16)
    pef = jnp.stack([cmp_pe_k.reshape(depth, 1, -1), cmp_pe_v.reshape(depth, 1, -1)], axis=1)
    w2c = jnp.stack([cmp_w2_k, cmp_w2_v], axis=1).astype(BF16)
    w2ct = jnp.swapaxes(w2c, -1, -2)
    bias_row = jnp.concatenate([mlstm_b_i, mlstm_b_f], axis=-1)[:, :, None]

    bb, a, cc = jax.vmap(_s5_params)(ssm_lambda_re, ssm_lambda_im, ssm_b_re, ssm_b_im, ssm_c_re, ssm_c_im,
                                     ssm_log_dt)
    sh3 = lambda t: t.reshape(B, S, t.shape[-1])

    h = x.reshape(B * S, D)
    for l in range(depth):
        (su, mq, mk, mv, mo, aq, ckv, sk, wk, svt, wvt, gates, gates_t) = _inproj(
            h, ln_mix_pre[l][None], w_in_p, l, rope, B, S, ts_proj, consts[2])

        y_ssm = _s5(sh3(su), bb, a, cc, ssm_d[l][None], wglu_b, l, gm_ssm, gn_ssm[l][None], B, S, ts_scan)

        y_mls = _mlstm(sh3(mq), sh3(mk), sh3(mv), sh3(mo), gates_t,
                       mlstm_conv[l][:, :MLSTM_WIDTH], mlstm_conv[l][:, MLSTM_WIDTH:],
                       bias_row[l], hm_mls, gn_mlstm[l][None], B, S)

        cmp_k, cmp_t = _compress(ckv, w1ab, cmp_w1_k, cmp_w1_v, pef, w2c, w2ct, l, B, S)
        y_nsa = _nsa(aq, cmp_k, cmp_t, sk, svt, wk, wvt, gates_t,
                     gn_nsa[l].reshape(NSA_HEADS, NSA_HEAD_DIM, 1), consts, B, S)

        h = _outproj(h, y_ssm.reshape(B * S, SSM_WIDTH), y_mls.reshape(B * S, MLSTM_WIDTH), y_nsa,
                     w_out, l, ln_mix_post[l][None], B, S, ts_proj)
        h = _mlp(h, ln_mlp_pre[l][None], mlp_w1, mlp_w2, l, ln_mlp_post[l][None], 1024, 1024)
    return h.reshape(B, S, D)
```

```python
import functools
import math

import numpy as np
import jax
import jax.numpy as jnp
from jax import lax
from jax.experimental import pallas as pl
from jax.experimental.pallas import tpu as pltpu

F32 = jnp.float32
BF16 = jnp.bfloat16
HIGHEST = lax.Precision.HIGHEST

D_MODEL = 1024
DEPTH = 4
SSM_WIDTH = 256
SSM_GROUP = 16
SSM_GROUPS = 16
SSM_STATE = 64
SSM_LANES = SSM_GROUPS * SSM_STATE
MLSTM_WIDTH = 256
MLSTM_HEADS = 4
MLSTM_HEAD_DIM = 64
MLSTM_CHUNK = 128
MLSTM_CONV = 4
NSA_WIDTH = 512
NSA_HEAD_DIM = 64
NSA_HEADS = 8
NSA_KV_GROUPS = 2
NSA_REP = NSA_HEADS // NSA_KV_GROUPS
NSA_KV_WIDTH = 128
CMP_BLOCK = 32
CMP_STRIDE = 16
CMP_HIDDEN = 256
SEL_BLOCK = 64
SEL_TOPN = 8
WINDOW = 256
Q_BLOCK = 128
FORCE_SCORE = 1e4
NEG_INF = -1e30
ROPE_THETA = 500000.0
ROPE_DIMS = 16
ROPE_HALF = 8
D_FF = 4096
EPS = 1e-6
D_IN = 2592

LANE = 128
SUBLANE = 8
VMEM_LIMIT = 56 * 1024 * 1024

C_SU, C_MQ, C_MK, C_MV, C_MO = 0, 256, 512, 768, 1024
C_AQ, C_CK, C_SK, C_WK = 1280, 1792, 1920, 2048
C_CV, C_SV, C_WV = 2176, 2304, 2432
C_G0, C_G1 = 2560, 2688
D_INP = 2816
GATE_COL = 16
VAL_ROWS = NSA_HEAD_DIM + 16
Q_SCALE = NSA_HEAD_DIM ** -0.5 * math.log2(math.e)
MLSTM_ROWS = 8
NSA_TQ = 256
SEL_UNROLL = 2


def _dot(a, b, precision=None):
    return jnp.dot(a, b, preferred_element_type=F32, precision=precision)


def _dot_nt(a, b):
    return lax.dot_general(a, b, (((1,), (1,)), ((), ())), preferred_element_type=F32)


def _dot_tn(a, b):
    return lax.dot_general(a, b, (((0,), (0,)), ((), ())), preferred_element_type=F32)


def _sigmoid(x):
    return 1.0 / (1.0 + jnp.exp(-x))


def _dot_split(x, w_bf16):
    hi = x.astype(BF16)
    lo = (x - hi.astype(F32)).astype(BF16)
    return _dot(hi, w_bf16) + _dot(lo, w_bf16)


def _gelu_tanh(x):
    return 0.5 * x * (1.0 + jnp.tanh(math.sqrt(2.0 / math.pi) * (x + 0.044715 * (x * x * x))))


def _log_sigmoid(x):
    return jnp.minimum(x, 0.0) - jnp.log(1.0 + jnp.exp(-jnp.abs(x)))


def _inproj_kernel(x_ref, g_ref, w_ref, rt_ref,
                   su_ref, mq_ref, mk_ref, mv_ref, mo_ref, aq_ref, ckv_ref, sk_ref, wk_ref,
                   svt_ref, wvt_ref, gt_ref, gtt_ref):
    x = x_ref[...]
    ms = jnp.mean(x * x, axis=-1, keepdims=True)
    u = (x * lax.rsqrt(ms + EPS) * g_ref[...]).astype(BF16)
    rc, rs1, rs2 = (rt_ref[:, n * LANE:(n + 1) * LANE] for n in range(3))

    def mm(c0, width):
        return _dot(u, w_ref[:, c0:c0 + width])

    def rope(z):
        return z * rc + pltpu.roll(z, LANE - ROPE_HALF, 1) * rs1 + pltpu.roll(z, ROPE_HALF, 1) * rs2

    su_ref[...] = mm(C_SU, 256)
    mq_ref[...] = mm(C_MQ, 256)
    mk_ref[...] = mm(C_MK, 256)
    mv_ref[...] = mm(C_MV, 256).astype(BF16)
    mo_ref[...] = mm(C_MO, 256)
    def mm_pair(c0):
        z = mm(c0, 2 * LANE)
        return z[:, :LANE], z[:, LANE:]

    def put_heads(ref, first, z):
        ref[first] = z[:, :NSA_HEAD_DIM].astype(BF16)
        ref[first + 1] = z[:, NSA_HEAD_DIM:].astype(BF16)

    for j in range(NSA_HEADS // 4):
        for k, z in enumerate(mm_pair(C_AQ + 2 * LANE * j)):
            put_heads(aq_ref, 4 * j + 2 * k, rope(z) * Q_SCALE)
    z_ck, z_sk = mm_pair(C_CK)
    z_wk, z_cv = mm_pair(C_WK)
    z_sv, z_wv = mm_pair(C_SV)
    ckv_ref[0] = rope(z_ck)
    ckv_ref[1] = z_cv
    put_heads(sk_ref, 0, rope(z_sk))
    put_heads(wk_ref, 0, rope(z_wk))

    def put_chunks_t(ref, z):
        zt = jnp.transpose(z)
        width = ref.shape[-1]
        ones = jnp.ones((VAL_ROWS - NSA_HEAD_DIM, width), BF16)
        for g in range(NSA_KV_GROUPS):
            for j in range(ref.shape[1]):
                ref[g, j, :NSA_HEAD_DIM] = zt[g * NSA_HEAD_DIM:(g + 1) * NSA_HEAD_DIM,
                                              j * width:(j + 1) * width].astype(BF16)
                ref[g, j, NSA_HEAD_DIM:] = ones

    put_chunks_t(svt_ref, z_sv)
    put_chunks_t(wvt_ref, z_wv)
    z_g0, z_g1 = mm_pair(C_G0)
    gt_ref[...] = z_g0
    gtt_ref[0] = jnp.transpose(z_g0)
    gtt_ref[1] = jnp.transpose(z_g1)


def _inproj(h2, gain, w, layer, rope, B, S, ts, ck):
    nt = S // ts
    BS = B * S
    row = lambda b, i: (b * nt + i, 0)
    full = lambda b, i: (0, 0)
    headed = lambda b, i: (b, 0, i, 0)
    paired = lambda b, i: (0, b * nt + i, 0)
    in_specs = [
        pl.BlockSpec((ts, D_MODEL), row),
        pl.BlockSpec((1, D_MODEL), full),
        pl.BlockSpec((None, D_MODEL, D_INP), lambda b, i: (layer, 0, 0)),
        pl.BlockSpec((ts, 3 * LANE), row),
    ]
    kv_shape = jax.ShapeDtypeStruct((B, NSA_KV_GROUPS, S, NSA_HEAD_DIM), BF16)
    kv_spec = pl.BlockSpec((None, NSA_KV_GROUPS, ts, NSA_HEAD_DIM), headed)
    out_shape = [
        jax.ShapeDtypeStruct((BS, SSM_WIDTH), F32),
        jax.ShapeDtypeStruct((BS, MLSTM_WIDTH), F32),
        jax.ShapeDtypeStruct((BS, MLSTM_WIDTH), F32),
        jax.ShapeDtypeStruct((BS, MLSTM_WIDTH), BF16),
        jax.ShapeDtypeStruct((BS, MLSTM_WIDTH), F32),
        jax.ShapeDtypeStruct((B, NSA_HEADS, S, NSA_HEAD_DIM), BF16),
        jax.ShapeDtypeStruct((2, BS, NSA_KV_WIDTH), F32),
        kv_shape, kv_shape,
        jax.ShapeDtypeStruct((B, NSA_KV_GROUPS, S // ck, VAL_ROWS, ck), BF16),
        jax.ShapeDtypeStruct((B, NSA_KV_GROUPS, S // Q_BLOCK, VAL_ROWS, Q_BLOCK), BF16),
        jax.ShapeDtypeStruct((BS, LANE), F32),
        jax.ShapeDtypeStruct((NSA_KV_GROUPS, B, LANE, S), F32),
    ]
    out_specs = [
        pl.BlockSpec((ts, SSM_WIDTH), row),
        pl.BlockSpec((ts, MLSTM_WIDTH), row),
        pl.BlockSpec((ts, MLSTM_WIDTH), row),
        pl.BlockSpec((ts, MLSTM_WIDTH), row),
        pl.BlockSpec((ts, MLSTM_WIDTH), row),
        pl.BlockSpec((None, NSA_HEADS, ts, NSA_HEAD_DIM), headed),
        pl.BlockSpec((2, ts, NSA_KV_WIDTH), paired),
        kv_spec, kv_spec,
        pl.BlockSpec((None, NSA_KV_GROUPS, ts // ck, VAL_ROWS, ck), lambda b, i: (b, 0, i, 0, 0)),
        pl.BlockSpec((None, NSA_KV_GROUPS, ts // Q_BLOCK, VAL_ROWS, Q_BLOCK), lambda b, i: (b, 0, i, 0, 0)),
        pl.BlockSpec((ts, LANE), row),
        pl.BlockSpec((NSA_KV_GROUPS, None, LANE, ts), lambda b, i: (0, b, 0, i)),
    ]
    return pl.pallas_call(
        _inproj_kernel,
        grid=(B, nt),
        in_specs=in_specs,
        out_specs=out_specs,
        out_shape=out_shape,
        compiler_params=pltpu.CompilerParams(
            dimension_semantics=("parallel", "parallel"), vmem_limit_bytes=VMEM_LIMIT),
        name="inproj",
    )(h2, gain, w, rope)


def _s5_kernel(u_ref, bb_ref, a_ref, cc_ref, d_ref, wg_ref, gm_ref, gain_ref, o_ref, x_sc, st_sc, tm_sc, *, B, ts):
    @pl.when(pl.program_id(0) == 0)
    def _():
        st_sc[...] = jnp.zeros_like(st_sc)

    nl = SSM_WIDTH // LANE
    for b in range(B):
        for c in range(nl):
            tm_sc[c, pl.ds(b, ts, stride=B), :] = u_ref[b, :, c * LANE:(c + 1) * LANE]
    u = jnp.concatenate([tm_sc[c] for c in range(nl)], axis=1)
    ub = u.astype(BF16)
    for part in range(2):
        cols = slice(part * SSM_LANES, (part + 1) * SSM_LANES)
        x_sc[:, cols] = _dot(ub, bb_ref[:, cols])
    ar = jnp.broadcast_to(a_ref[0:1, :], (B, SSM_LANES))
    ai = jnp.broadcast_to(a_ref[1:2, :], (B, SSM_LANES))

    def step(t, carry):
        xr, xi = carry
        r = pl.multiple_of(t * B, B)
        br = x_sc[pl.ds(r, B), 0:SSM_LANES]
        bi = x_sc[pl.ds(r, B), SSM_LANES:2 * SSM_LANES]
        nr = ar * xr - ai * xi + br
        ni = ar * xi + ai * xr + bi
        x_sc[pl.ds(r, B), 0:SSM_LANES] = nr
        x_sc[pl.ds(r, B), SSM_LANES:2 * SSM_LANES] = ni
        return nr, ni

    xr, xi = lax.fori_loop(0, ts, step, (st_sc[0], st_sc[1]), unroll=4)
    st_sc[0] = xr
    st_sc[1] = xi

    half = (ts * B) // 2
    y = jnp.concatenate([_dot(x_sc[r * half:(r + 1) * half, :].astype(BF16), cc_ref[...]) for r in range(2)],
                        axis=0) + d_ref[...] * u
    y = _gelu_tanh(y)
    y = y * _sigmoid(_dot(y.astype(BF16), wg_ref[...]))
    ms = _dot_split(y * y, gm_ref[...])
    y = y * lax.rsqrt(ms + EPS) * gain_ref[...]
    for c in range(nl):
        tm_sc[c] = y[:, c * LANE:(c + 1) * LANE]
    for b in range(B):
        o_ref[b] = jnp.concatenate(
            [tm_sc[c, pl.ds(b, ts, stride=B), :] for c in range(nl)], axis=1).astype(BF16)


def _s5(u, bb, a, cc, d, wg, layer, gm, gain, B, S, ts):
    rows = ts * B
    full = lambda i: (0, 0)
    lsel = lambda i: (layer, 0, 0)
    return pl.pallas_call(
        functools.partial(_s5_kernel, B=B, ts=ts),
        grid=(S // ts,),
        in_specs=[
            pl.BlockSpec((B, ts, SSM_WIDTH), lambda i: (0, i, 0)),
            pl.BlockSpec((None, SSM_WIDTH, 2 * SSM_LANES), lsel),
            pl.BlockSpec((None, 2, SSM_LANES), lsel),
            pl.BlockSpec((None, 2 * SSM_LANES, SSM_WIDTH), lsel),
            pl.BlockSpec((1, SSM_WIDTH), full),
            pl.BlockSpec((None, SSM_WIDTH, SSM_WIDTH), lsel),
            pl.BlockSpec((SSM_WIDTH, SSM_WIDTH), full),
            pl.BlockSpec((1, SSM_WIDTH), full),
        ],
        out_specs=pl.BlockSpec((B, ts, SSM_WIDTH), lambda i: (0, i, 0)),
        out_shape=jax.ShapeDtypeStruct((B, S, SSM_WIDTH), BF16),
        scratch_shapes=[pltpu.VMEM((rows, 2 * SSM_LANES), F32), pltpu.VMEM((2, B, SSM_LANES), F32),
                        pltpu.VMEM((SSM_WIDTH // LANE, rows, LANE), F32)],
        compiler_params=pltpu.CompilerParams(
            dimension_semantics=("arbitrary",), vmem_limit_bytes=VMEM_LIMIT),
        name="s5",
    )(u, bb, a, cc, d, wg, gm, gain)


def _mlstm_kernel(q_ref, k_ref, v_ref, o_ref, gr_ref, cwq_ref, cwk_ref, br_ref, hm_ref,
                  gain_ref, y_ref, qt_sc, kt_sc, c_sc, m_sc, *, B):
    L, H, Dh, W = MLSTM_CHUNK, MLSTM_HEADS, MLSTM_HEAD_DIM, MLSTM_WIDTH

    @pl.when(pl.program_id(0) == 0)
    def _():
        qt_sc[...] = jnp.zeros_like(qt_sc)
        kt_sc[...] = jnp.zeros_like(kt_sc)
        c_sc[...] = jnp.zeros_like(c_sc)
        m_sc[...] = jnp.zeros_like(m_sc)

    visible = lax.broadcasted_iota(jnp.int32, (L, L), 0) <= lax.broadcasted_iota(jnp.int32, (L, L), 1)
    triu = visible.astype(F32)
    lane_w = lax.broadcasted_iota(jnp.int32, (1, W), 1) // Dh
    bd_mask = ((lax.broadcasted_iota(jnp.int32, (2 * W, W), 0) % W) // Dh
               == lax.broadcasted_iota(jnp.int32, (2 * W, W), 1) // Dh)
    row8 = lax.broadcasted_iota(jnp.int32, (SUBLANE, W), 0)
    cwq = cwq_ref[...]
    cwk = cwk_ref[...]
    ones_rows = jnp.ones((Dh, L), F32)

    def conv_silu(x, tail, w):
        acc = x * w[MLSTM_CONV - 1:MLSTM_CONV, :]
        for sft in range(1, MLSTM_CONV):
            xs = pltpu.roll(x, sft, 0)
            head = jnp.where(row8 < sft, pltpu.roll(tail, sft, 0), xs[:SUBLANE])
            xs = jnp.concatenate([head, xs[SUBLANE:]], axis=0)
            acc = acc + xs * w[MLSTM_CONV - 1 - sft:MLSTM_CONV - sft, :]
        return acc * _sigmoid(acc)

    def per_group(grp, _):
        bs = [grp * MLSTM_ROWS + n for n in range(MLSTM_ROWS)]
        st = [dict() for _ in bs]

        for b, d in zip(bs, st):
            q_raw = q_ref[b]
            k_raw = k_ref[b]
            d['q'] = conv_silu(q_raw, qt_sc[b], cwq)
            d['k'] = conv_silu(k_raw, kt_sc[b], cwk) * (Dh ** -0.5)
            qt_sc[b] = q_raw[L - SUBLANE:, :]
            kt_sc[b] = k_raw[L - SUBLANE:, :]
            d['gr'] = gr_ref[b] + br_ref[...]
        for d in st:
            d['brow'] = _dot(_log_sigmoid(d['gr']), triu, precision=HIGHEST)

        for b, d in zip(bs, st):
            gr, brow = d['gr'], d['brow']
            ccol = jnp.transpose(brow - pltpu.roll(gr, H, 0))
            m_all = m_sc[b]
            for key in ('w_intra', 'w_inter', 'e_mt', 'w_k', 'dec', 'm_new'):
                d[key] = []
            for hh in range(H):
                b_r = brow[H + hh:H + hh + 1, :]
                i_r = gr[hh:hh + 1, :]
                m_prev = m_all[hh:hh + 1, 0:1]
                dm = jnp.where(visible, b_r - ccol[:, H + hh:H + hh + 1], NEG_INF)
                inter = b_r + m_prev
                mt = jnp.maximum(inter, jnp.max(dm, axis=0, keepdims=True))
                d['w_intra'].append(jnp.exp(dm - mt))
                d['w_inter'].append(jnp.exp(inter - mt))
                d['e_mt'].append(jnp.exp(-mt))
                b_last = b_r[:, L - 1:L]
                logw = b_last - b_r + i_r
                mn = jnp.maximum(b_last + m_prev, jnp.max(logw, axis=1, keepdims=True))
                d['w_k'].append(jnp.exp(logw - mn))
                d['dec'].append(jnp.exp(b_last + m_prev - mn))
                d['m_new'].append(mn)
            d['qb'] = d['q'].astype(BF16)
            d['kb'] = d['k'].astype(BF16)
            d['vt'] = jnp.transpose(v_ref[b].astype(F32))
            d['c_t'] = c_sc[b]

        for d in st:
            d['qc'] = _dot(d['c_t'].astype(BF16), jnp.transpose(d['q']).astype(BF16))
            d['s_t'] = [_dot_nt(d['kb'], jnp.where(lane_w == hh, d['qb'], jnp.zeros_like(d['qb'])))
                        for hh in range(H)]
        for d in st:
            d['r'] = []
            for hh in range(H):
                v_aug = jnp.concatenate([d['vt'][hh * Dh:(hh + 1) * Dh], ones_rows], axis=0).astype(BF16)
                d['r'].append(_dot(v_aug, (d['s_t'][hh] * d['w_intra'][hh]).astype(BF16)))

        for b, d in zip(bs, st):
            h_t = []
            for hh in range(H):
                ch = slice(hh * Dh, (hh + 1) * Dh)
                num = d['w_inter'][hh] * d['qc'][ch] + d['r'][hh][:Dh]
                den = d['w_inter'][hh] * d['qc'][W + hh * Dh:W + (hh + 1) * Dh] + d['r'][hh][Dh:]
                h_t.append(num / jnp.maximum(jnp.abs(den), d['e_mt'][hh]))
            hout = jnp.transpose(jnp.concatenate(h_t, axis=0))
            d['y'] = _sigmoid(o_ref[b]) * hout
            d['vw'] = jnp.concatenate(
                [d['vt'][hh * Dh:(hh + 1) * Dh] * d['w_k'][hh] for hh in range(H)]
                + [jnp.broadcast_to(d['w_k'][hh], (Dh, L)) for hh in range(H)], axis=0).astype(BF16)
        for d in st:
            d['ms'] = _dot_split(d['y'] * d['y'], hm_ref[...])
            d['upd'] = _dot(d['vw'], d['kb'])

        for b, d in zip(bs, st):
            y_ref[b] = (d['y'] * lax.rsqrt(d['ms'] + EPS) * gain_ref[...]).astype(BF16)
            decay = d['dec'][H - 1]
            for hh in range(H - 2, -1, -1):
                decay = jnp.where(lane_w == hh, d['dec'][hh], decay)
            c_sc[b] = decay * d['c_t'] + jnp.where(bd_mask, d['upd'], 0.0)
            for hh in range(H):
                m_sc[b, hh:hh + 1, :] = jnp.broadcast_to(d['m_new'][hh], (1, LANE))
        return 0

    lax.fori_loop(0, B // MLSTM_ROWS, per_group, 0)


def _mlstm(mq, mk, mv, mo, grow, cwq, cwk, brow, hm, gain, B, S):
    L, W = MLSTM_CHUNK, MLSTM_WIDTH
    seq = lambda c: (0, c, 0)
    full = lambda c: (0, 0)
    return pl.pallas_call(
        functools.partial(_mlstm_kernel, B=B),
        grid=(S // L,),
        in_specs=[
            pl.BlockSpec((B, L, W), seq),
            pl.BlockSpec((B, L, W), seq),
            pl.BlockSpec((B, L, W), seq),
            pl.BlockSpec((B, L, W), seq),
            pl.BlockSpec((None, B, SUBLANE, L), lambda c: (0, 0, 0, c)),
            pl.BlockSpec((MLSTM_CONV, W), full),
            pl.BlockSpec((MLSTM_CONV, W), full),
            pl.BlockSpec((SUBLANE, 1), full),
            pl.BlockSpec((W, W), full),
            pl.BlockSpec((1, W), full),
        ],
        out_specs=pl.BlockSpec((B, L, W), seq),
        out_shape=jax.ShapeDtypeStruct((B, S, W), BF16),
        scratch_shapes=[
            pltpu.VMEM((B, SUBLANE, W), F32),
            pltpu.VMEM((B, SUBLANE, W), F32),
            pltpu.VMEM((B, 2 * W, W), F32),
            pltpu.VMEM((B, SUBLANE, LANE), F32),
        ],
        compiler_params=pltpu.CompilerParams(
            dimension_semantics=("arbitrary",), vmem_limit_bytes=VMEM_LIMIT),
        name="mlstm",
    )(mq, mk, mv, mo, grow, cwq, cwk, brow, hm, gain)


def _compress_kernel(c_ref, w1ab_ref, w1_ref, pe_ref, w2_ref, w2t_ref, o_ref, ot_ref, ch_sc):
    G, Dh = NSA_KV_GROUPS, NSA_HEAD_DIM
    rows = ch_sc.shape[1]
    n = rows // G
    both = range(2)
    for i in both:
        for r in range(CMP_STRIDE):
            tok = c_ref[i, pl.ds(r, n, stride=CMP_STRIDE), :]
            for g in range(G):
                ch_sc[i, g * n:(g + 1) * n, r * Dh:(r + 1) * Dh] = tok[:, g * Dh:(g + 1) * Dh]
    ab = [_dot(ch_sc[i].astype(BF16), w1ab_ref[i]) for i in both]
    const = [_dot(pe_ref[i], w1_ref[i], precision=HIGHEST) for i in both]
    act = []
    for i in both:
        hid = ab[i][:, :CMP_HIDDEN] + pltpu.roll(ab[i][:, CMP_HIDDEN:], rows - 1, 0) + const[i]
        act.append(_gelu_tanh(hid).astype(BF16))
    for i in both:
        o_ref[i] = _dot(act[i], w2_ref[i]).astype(BF16)
        ot_ref[i] = _dot_nt(w2t_ref[i], act[i]).astype(BF16)


def _compress(ckv, w1ab, w1, pe, w2, w2t, layer, B, S):
    G, Dh = NSA_KV_GROUPS, NSA_HEAD_DIM
    n = S // CMP_STRIDE
    width = CMP_STRIDE * Dh
    wsel = lambda b: (layer, 0, 0, 0)
    return pl.pallas_call(
        _compress_kernel,
        grid=(B,),
        in_specs=[
            pl.BlockSpec((2, S, G * Dh), lambda b: (0, b, 0)),
            pl.BlockSpec((None, 2, width, 2 * CMP_HIDDEN), wsel),
            pl.BlockSpec((None, 2, 2 * width, CMP_HIDDEN), wsel),
            pl.BlockSpec((None, 2, 1, 2 * width), wsel),
            pl.BlockSpec((None, 2, CMP_HIDDEN, Dh), wsel),
            pl.BlockSpec((None, 2, Dh, CMP_HIDDEN), wsel),
        ],
        out_specs=[pl.BlockSpec((2, None, G * n, Dh), lambda b: (0, b, 0, 0)),
                   pl.BlockSpec((2, None, Dh, G * n), lambda b: (0, b, 0, 0))],
        out_shape=[jax.ShapeDtypeStruct((2, B, G * n, Dh), BF16),
                   jax.ShapeDtypeStruct((2, B, Dh, G * n), BF16)],
        scratch_shapes=[pltpu.VMEM((2, G * n, width), F32)],
        compiler_params=pltpu.CompilerParams(
            dimension_semantics=("parallel",), vmem_limit_bytes=VMEM_LIMIT),
        name="compress",
    )(ckv, w1ab, w1, pe, w2, w2t)


def _nsa_kernel(q_ref, kc_ref, vct_ref, ks_ref, vst_ref, kw_ref, vwt_ref, gtt_ref, gain_ref,
                ovt_ref, et_ref, o_ref, *, n_sel, n_top, ck, unroll):
    TQ, R, Dh, G = NSA_TQ, NSA_REP, NSA_HEAD_DIM, NSA_KV_GROUPS
    groups = range(G)
    i = pl.program_id(1)
    t0 = i * TQ

    def _nsa_tile(trips):
        qs = [q_ref[g * R:(g + 1) * R].reshape(R * TQ, Dh) for g in groups]
        tq1 = t0 + lax.broadcasted_iota(jnp.int32, (1, TQ), 1)
        heads = lambda t: jnp.concatenate([t] * R, axis=1)

        ncmp = kc_ref.shape[0] // G
        wkeys = WINDOW + TQ
        nwb = wkeys // Q_BLOCK
        ws = pl.multiple_of(jnp.maximum(t0 - WINDOW, 0), Q_BLOCK)
        wb0 = ws // Q_BLOCK
        sc = [_dot_nt(kc_ref[g * ncmp:(g + 1) * ncmp, :], qs[g]) for g in groups]
        sw = [_dot_nt(kw_ref[g, pl.ds(ws, wkeys), :], qs[g]) for g in groups]

        kpos = ws + lax.broadcasted_iota(jnp.int32, (wkeys, 1), 0)
        wbias = heads(jnp.where((kpos <= tq1) & (tq1 - kpos < WINDOW), 0.0, NEG_INF))
        ow, l_w = [], []
        for g in groups:
            swb = sw[g] + wbias
            pw = jnp.exp2(swb - jnp.max(swb, axis=0, keepdims=True))
            vwt = jnp.concatenate([vwt_ref[g, wb0 + j] for j in range(nwb)], axis=1)
            owl = _dot(vwt, pw.astype(BF16))
            ow.append(owl[:Dh])
            l_w.append(owl[Dh:Dh + 1])

        cend = lax.broadcasted_iota(jnp.int32, (ncmp, 1), 0) * CMP_STRIDE + (CMP_BLOCK - 1)
        cmask = heads(cend <= tq1)
        pc = []
        for g in groups:
            scm = jnp.where(cmask, sc[g], NEG_INF)
            ec = jnp.where(cmask, jnp.exp2(scm - jnp.max(scm, axis=0, keepdims=True)), 0.0)
            pc.append(ec * (1.0 / jnp.maximum(jnp.sum(ec, axis=0, keepdims=True), 1e-30)))
        oc = [_dot(vct_ref[:, g * ncmp:(g + 1) * ncmp], pc[g].astype(BF16)) for g in groups]

        imp = []
        for g in groups:
            psum = pc[g][:, 0:TQ]
            for r in range(1, R):
                psum = psum + pc[g][:, r * TQ:(r + 1) * TQ]
            imp.append(_dot(ovt_ref[...], psum, precision=HIGHEST))

        blk = lax.broadcasted_iota(jnp.int32, (n_sel, 1), 0)
        valid = blk * SEL_BLOCK <= tq1
        forced = (blk == 0) | (blk == tq1 // SEL_BLOCK)
        selb = []
        for g in groups:
            val = jnp.where(forced, FORCE_SCORE, jnp.where(valid, imp[g], -FORCE_SCORE))
            rank = jnp.zeros((n_sel, TQ), F32)
            for jp in range(n_sel):
                other = val[jp:jp + 1, :]
                wins = jnp.where(blk > jp, jnp.where(other >= val, 1.0, 0.0), jnp.where(other > val, 1.0, 0.0))
                rank = rank + wins
            selb.append(jnp.where(rank < n_top, 0.0, NEG_INF).astype(BF16))

        def scores(g, c, causal):
            k0 = c * ck
            bias = _dot(et_ref[c], selb[g])
            if causal:
                kpos = k0 + lax.broadcasted_iota(jnp.int32, (ck, 1), 0)
                bias = jnp.where(kpos <= tq1, bias, NEG_INF)
            return _dot_nt(ks_ref[g, pl.ds(k0, ck), :], qs[g]) + heads(bias)

        def update(g, c, s, carry):
            m, acc = carry
            mn = jnp.maximum(m, jnp.max(s, axis=0, keepdims=True))
            p = jnp.exp2(s - mn)
            acc = jnp.exp2(m - mn) * acc + _dot(vst_ref[g, c], p.astype(BF16))
            return mn, acc

        def chunk_group(cg, carry, causal):
            cs = [cg * unroll + sub for sub in range(unroll)]
            ss = [[scores(g, c, causal) for g in groups] for c in cs]
            carry = list(carry)
            for c, s in zip(cs, ss):
                for g in groups:
                    carry[g] = update(g, c, s[g], carry[g])
            return tuple(carry)

        sel = tuple((jnp.full((1, R * TQ), NEG_INF, F32), jnp.zeros((VAL_ROWS, R * TQ), F32)) for _ in groups)
        for cg in range(trips):
            sel = chunk_group(cg, sel, causal=cg == trips - 1)

        normed = []
        for g in groups:
            gs = _sigmoid(gtt_ref[g])
            acc_s, l_s = sel[g][1][:Dh], sel[g][1][Dh:Dh + 1]
            for r in range(R):
                ln = slice(r * TQ, (r + 1) * TQ)
                o = (gs[3 * r:3 * r + 1, :] * oc[g][:, ln]
                     + (gs[3 * r + 1:3 * r + 2, :] / l_s[:, ln]) * acc_s[:, ln]
                     + (gs[3 * r + 2:3 * r + 3, :] / l_w[g][:, ln]) * ow[g][:, ln])
                ms = jnp.mean(o * o, axis=0, keepdims=True)
                normed.append(o * lax.rsqrt(ms + EPS) * gain_ref[g * R + r])
        for pair in range(G * R // 2):
            both = jnp.concatenate(normed[2 * pair:2 * pair + 2], axis=0)
            o_ref[:, pair * 2 * Dh:(pair + 1) * 2 * Dh] = jnp.transpose(both).astype(BF16)
        return 0

    n_chunks = (t0 + TQ + ck - 1) // ck
    max_trips = ks_ref.shape[1] // (ck * unroll)
    lax.switch((n_chunks + unroll - 1) // unroll - 1,
               [functools.partial(_nsa_tile, t) for t in range(1, max_trips + 1)])


def _nsa(aq, cmp_k, cmp_vt, ks, vst, kw, vwt, gates_t, gain, consts, B, S):
    G, H, TQ, Dh = NSA_KV_GROUPS, NSA_HEADS, NSA_TQ, NSA_HEAD_DIM
    nq = S // TQ
    ncmp = S // CMP_STRIDE
    n_sel = S // SEL_BLOCK
    ovt, emat_t, ck = consts
    k_spec = pl.BlockSpec((None, G, S, Dh), lambda b, i: (b, 0, 0, 0))
    return pl.pallas_call(
        functools.partial(_nsa_kernel, n_sel=n_sel, n_top=min(SEL_TOPN, n_sel), ck=ck, unroll=SEL_UNROLL),
        grid=(B, nq),
        in_specs=[
            pl.BlockSpec((None, H, TQ, Dh), lambda b, i: (b, 0, i, 0)),
            pl.BlockSpec((None, None, G * ncmp, Dh), lambda b, i: (0, b, 0, 0)),
            pl.BlockSpec((None, None, Dh, G * ncmp), lambda b, i: (1, b, 0, 0)),
            k_spec,
            pl.BlockSpec((None, G, S // ck, VAL_ROWS, ck), lambda b, i: (b, 0, 0, 0, 0)),
            k_spec,
            pl.BlockSpec((None, G, S // Q_BLOCK, VAL_ROWS, Q_BLOCK), lambda b, i: (b, 0, 0, 0, 0)),
            pl.BlockSpec((G, None, 2 * SUBLANE, TQ), lambda b, i: (0, b, GATE_COL // (2 * SUBLANE), i)),
            pl.BlockSpec((H, Dh, 1), lambda b, i: (0, 0, 0)),
            pl.BlockSpec(ovt.shape, lambda b, i: (0, 0)),
            pl.BlockSpec(emat_t.shape, lambda b, i: (0, 0, 0)),
        ],
        out_specs=pl.BlockSpec((TQ, H * Dh), lambda b, i: (b * nq + i, 0)),
        out_shape=jax.ShapeDtypeStruct((B * S, H * Dh), BF16),
        compiler_params=pltpu.CompilerParams(
            dimension_semantics=("parallel", "arbitrary"), vmem_limit_bytes=VMEM_LIMIT),
        name="nsa",
    )(aq, cmp_k, cmp_vt, ks, vst, kw, vwt, gates_t, gain, ovt, emat_t)


def _nsa_consts(S):
    n_cmp = S // CMP_STRIDE
    n_sel = S // SEL_BLOCK
    ck = 256
    i = np.arange(n_cmp)[:, None]
    j = np.arange(n_sel)[None, :]
    lo = np.maximum(i * CMP_STRIDE, j * SEL_BLOCK)
    hi = np.minimum(i * CMP_STRIDE + CMP_BLOCK, (j + 1) * SEL_BLOCK)
    ov = np.maximum(hi - lo, 0) / CMP_STRIDE
    ov[n_cmp - 1] = 0.0
    key = np.arange(S)
    emat_t = (key[:, None] // SEL_BLOCK == np.arange(n_sel)[None, :]).astype(np.float32)
    return (jnp.asarray(ov.T, F32), jnp.asarray(emat_t.reshape(S // ck, ck, n_sel), BF16), ck)


def _outproj_kernel(h_ref, ys_ref, ym_ref, yn_ref, w_ref, g_ref, o_ref):
    acc = _dot(ys_ref[...], w_ref[0:SSM_WIDTH, :])
    acc = acc + _dot(ym_ref[...], w_ref[SSM_WIDTH:SSM_WIDTH + MLSTM_WIDTH, :])
    acc = acc + _dot(yn_ref[...], w_ref[SSM_WIDTH + MLSTM_WIDTH:, :])
    ms = jnp.mean(acc * acc, axis=-1, keepdims=True)
    o_ref[...] = h_ref[...] + acc * lax.rsqrt(ms + EPS) * g_ref[...]


def _outproj(h2, y_ssm, y_mls, y_nsa, w, layer, gain, B, S, ts):
    nt = S // ts
    row = lambda b, i: (b * nt + i, 0)
    full = lambda b, i: (0, 0)
    return pl.pallas_call(
        _outproj_kernel,
        grid=(B, nt),
        in_specs=[
            pl.BlockSpec((ts, D_MODEL), row),
            pl.BlockSpec((ts, SSM_WIDTH), row),
            pl.BlockSpec((ts, MLSTM_WIDTH), row),
            pl.BlockSpec((ts, NSA_WIDTH), row),
            pl.BlockSpec((None, D_MODEL, D_MODEL), lambda b, i: (layer, 0, 0)),
            pl.BlockSpec((1, D_MODEL), full),
        ],
        out_specs=pl.BlockSpec((ts, D_MODEL), row),
        out_shape=jax.ShapeDtypeStruct((B * S, D_MODEL), F32),
        compiler_params=pltpu.CompilerParams(
            dimension_semantics=("parallel", "parallel"), vmem_limit_bytes=VMEM_LIMIT),
        name="outproj",
    )(h2, y_ssm, y_mls, y_nsa, w, gain)


def _mlp_kernel(h_ref, g1_ref, w1_ref, w2_ref, g2_ref, o_ref, u_sc, acc_sc):
    kf = pl.program_id(1)
    last = pl.num_programs(1) - 1

    def partial_ff(u):
        a = jnp.maximum(_dot(u, w1_ref[...].astype(BF16)), 0.0)
        return _dot((a * a).astype(BF16), w2_ref[...].astype(BF16))

    @pl.when(kf == 0)
    def _():
        x = h_ref[...]
        ms = jnp.mean(x * x, axis=-1, keepdims=True)
        u = (x * lax.rsqrt(ms + EPS) * g1_ref[...]).astype(BF16)
        u_sc[...] = u
        acc_sc[...] = partial_ff(u)

    @pl.when((kf > 0) & (kf < last))
    def _():
        acc_sc[...] += partial_ff(u_sc[...])

    @pl.when(kf == last)
    def _():
        f = acc_sc[...] + partial_ff(u_sc[...])
        ms = jnp.mean(f * f, axis=-1, keepdims=True)
        o_ref[...] = h_ref[...] + f * lax.rsqrt(ms + EPS) * g2_ref[...]


def _mlp(h2, g1, w1, w2, layer, g2, tm, tf):
    rows = h2.shape[0]
    return pl.pallas_call(
        _mlp_kernel,
        grid=(rows // tm, D_FF // tf),
        in_specs=[
            pl.BlockSpec((tm, D_MODEL), lambda i, k: (i, 0)),
            pl.BlockSpec((1, D_MODEL), lambda i, k: (0, 0)),
            pl.BlockSpec((None, D_MODEL, tf), lambda i, k: (layer, 0, k)),
            pl.BlockSpec((None, tf, D_MODEL), lambda i, k: (layer, k, 0)),
            pl.BlockSpec((1, D_MODEL), lambda i, k: (0, 0)),
        ],
        out_specs=pl.BlockSpec((tm, D_MODEL), lambda i, k: (i, 0)),
        out_shape=jax.ShapeDtypeStruct((rows, D_MODEL), F32),
        scratch_shapes=[pltpu.VMEM((tm, D_MODEL), BF16), pltpu.VMEM((tm, D_MODEL), F32)],
        compiler_params=pltpu.CompilerParams(
            dimension_semantics=("parallel", "arbitrary"), vmem_limit_bytes=VMEM_LIMIT),
        name="mlp",
    )(h2, g1, w1, w2, g2)


def _inproj_pieces():
    return ((0, 1280), (1288, 1800), (1800, 1928), (2056, 2184), (2312, 2440),
            (1928, 2056), (2184, 2312), (2440, 2568),
            (1280, 1288), (None, GATE_COL - 8), (2568, 2580), (None, LANE - GATE_COL - 12),
            (None, GATE_COL), (2580, 2592), (None, LANE - GATE_COL - 12))


def _permute_w_in_kernel(w_ref, o_ref):
    x = w_ref[...]
    col = 0
    for a, b in _inproj_pieces():
        width = b if a is None else b - a
        piece = jnp.zeros((x.shape[0], width), BF16) if a is None else x[:, a:b].astype(BF16)
        o_ref[:, col:col + width] = piece
        col += width
    assert col == D_INP


def _permute_w_in(w_in):
    depth, rows, cols = w_in.shape
    tr = 256
    return pl.pallas_call(
        _permute_w_in_kernel,
        grid=(depth, rows // tr),
        in_specs=[pl.BlockSpec((None, tr, cols), lambda l, i: (l, i, 0))],
        out_specs=pl.BlockSpec((None, tr, D_INP), lambda l, i: (l, i, 0)),
        out_shape=jax.ShapeDtypeStruct((depth, rows, D_INP), BF16),
        compiler_params=pltpu.CompilerParams(
            dimension_semantics=("parallel", "parallel"), vmem_limit_bytes=VMEM_LIMIT),
        name="permute_w_in",
    )(w_in)


def _rope_tables(positions):
    inv = ROPE_THETA ** (-jnp.arange(0, ROPE_DIMS, 2, dtype=F32) / ROPE_DIMS)
    ang = positions.astype(F32)[..., None] * inv
    ones = jnp.ones(ang.shape[:-1] + (1,), F32)
    feats = jnp.concatenate([jnp.cos(ang), jnp.sin(ang), ones], axis=-1).reshape(-1, 2 * ROPE_HALF + 1)
    place = np.zeros((2 * ROPE_HALF + 1, 3 * LANE), np.float32)
    for lane in range(LANE):
        d = lane % NSA_HEAD_DIM
        if d < ROPE_HALF:
            place[d, lane] = 1.0
            place[ROPE_HALF + d, LANE + lane] = -1.0
        elif d < ROPE_DIMS:
            place[d - ROPE_HALF, lane] = 1.0
            place[d, 2 * LANE + lane] = 1.0
        else:
            place[2 * ROPE_HALF, lane] = 1.0
    return jnp.dot(feats, jnp.asarray(place), precision=HIGHEST)


def _s5_params(lam_re, lam_im, b_re, b_im, c_re, c_im, log_dt):
    G, P, Hc = SSM_GROUPS, SSM_STATE, SSM_GROUP
    dt = jnp.exp(log_dt)[:, None]
    mag = jnp.exp(lam_re * dt)
    ang = lam_im * dt
    ab_re = mag * jnp.cos(ang)
    ab_im = mag * jnp.sin(ang)
    den = lam_re * lam_re + lam_im * lam_im
    g_re = ((ab_re - 1.0) * lam_re + ab_im * lam_im) / den
    g_im = (ab_im * lam_re - (ab_re - 1.0) * lam_im) / den
    bb_re = g_re[..., None] * b_re - g_im[..., None] * b_im
    bb_im = g_re[..., None] * b_im + g_im[..., None] * b_re
    same_in = jnp.asarray(np.arange(G * Hc)[:, None] // Hc == np.arange(G * P)[None, :] // P)
    blockdiag_in = lambda t: jnp.where(
        same_in, jnp.tile(jnp.swapaxes(t, 1, 2).reshape(G * Hc, P), (1, G)), 0.0)
    blockdiag_out = lambda t: jnp.where(
        same_in.T, jnp.tile(jnp.swapaxes(t, 1, 2).reshape(G * P, Hc), (1, G)), 0.0)
    bb = jnp.concatenate([blockdiag_in(bb_re), blockdiag_in(bb_im)], axis=1).astype(BF16)
    cc = jnp.concatenate([blockdiag_out(c_re), -blockdiag_out(c_im)], axis=0).astype(BF16)
    a = jnp.stack([ab_re.reshape(-1), ab_im.reshape(-1)], axis=0)
    return bb, a, cc


def _group_mean_matrix(width, group):
    idx = np.arange(width) // group
    return jnp.asarray((idx[:, None] == idx[None, :]).astype(np.float32) / group, BF16)


def kernel(x, positions, ln_mix_pre, ln_mix_post, ln_mlp_pre, ln_mlp_post, w_in, w_out, ssm_lambda_re, ssm_lambda_im, ssm_b_re, ssm_b_im, ssm_c_re, ssm_c_im, ssm_d, ssm_log_dt, ssm_w_glu, mlstm_conv, mlstm_b_i, mlstm_b_f, cmp_pe_k, cmp_w1_k, cmp_w2_k, cmp_pe_v, cmp_w1_v, cmp_w2_v, gn_ssm, gn_mlstm, gn_nsa, mlp_w1, mlp_w2):
    B, S, D = x.shape
    depth = w_in.shape[0]
    assert D == D_MODEL and B == SUBLANE and S % 512 == 0 and S >= WINDOW + NSA_TQ
    G, H = NSA_KV_GROUPS, MLSTM_HEADS
    ts_proj = 1024
    ts_scan = 128

    rope = _rope_tables(positions)
    w_in_p = _permute_w_in(w_in)
    w_out_b = w_out.astype(BF16)
    wglu_b = ssm_w_glu.astype(BF16)
    gm_ssm = _group_mean_matrix(SSM_WIDTH, SSM_GROUP)
    hm_mls = _group_mean_matrix(MLSTM_WIDTH, MLSTM_HEAD_DIM)
    consts = _nsa_consts(S)
    half = CMP_STRIDE * NSA_HEAD_DIM
    w1ab = jnp.stack([jnp.concatenate([cmp_w1_k[:, :half], cmp_w1_k[:, half:]], axis=-1),
                      jnp.concatenate([cmp_w1_v[:, :half], cmp_w1_v[:, half:]], axis=-1)], axis=1).astype(BF16)
    w1f = jnp.stack([cmp_w1_k, cmp_w1_v], axis=1)
    pef = jnp.stack([cmp_pe_k.reshape(depth, 1, -1), cmp_pe_v.reshape(depth, 1, -1)], axis=1)
    w2c = jnp.stack([cmp_w2_k, cmp_w2_v], axis=1).astype(BF16)
    w2ct = jnp.swapaxes(w2c, -1, -2)
    bias_row = jnp.concatenate([mlstm_b_i, mlstm_b_f], axis=-1)[:, :, None]

    bb, a, cc = jax.vmap(_s5_params)(ssm_lambda_re, ssm_lambda_im, ssm_b_re, ssm_b_im, ssm_c_re, ssm_c_im,
                                     ssm_log_dt)
    sh3 = lambda t: t.reshape(B, S, t.shape[-1])

    h = x.reshape(B * S, D)
    for l in range(depth):
        (su, mq, mk, mv, mo, aq, ckv, sk, wk, svt, wvt, gates, gates_t) = _inproj(
            h, ln_mix_pre[l][None], w_in_p, l, rope, B, S, ts_proj, consts[2])

        y_ssm = _s5(sh3(su), bb, a, cc, ssm_d[l][None], wglu_b, l, gm_ssm, gn_ssm[l][None], B, S, ts_scan)

        y_mls = _mlstm(sh3(mq), sh3(mk), sh3(mv), sh3(mo), gates_t,
                       mlstm_conv[l][:, :MLSTM_WIDTH], mlstm_conv[l][:, MLSTM_WIDTH:],
                       bias_row[l], hm_mls, gn_mlstm[l][None], B, S)

        cmp_k, cmp_t = _compress(ckv, w1ab, w1f, pef, w2c, w2ct, l, B, S)
        y_nsa = _nsa(aq, cmp_k, cmp_t, sk, svt, wk, wvt, gates_t,
                     gn_nsa[l].reshape(NSA_HEADS, NSA_HEAD_DIM, 1), consts, B, S)

        h = _outproj(h, y_ssm.reshape(B * S, SSM_WIDTH), y_mls.reshape(B * S, MLSTM_WIDTH), y_nsa,
                     w_out_b, l, ln_mix_post[l][None], B, S, ts_proj)
        h = _mlp(h, ln_mlp_pre[l][None], mlp_w1, mlp_w2, l, ln_mlp_post[l][None], 1024, 1024)
    return h.reshape(B, S, D)
```

```python
import functools
import math

import numpy as np
import jax
import jax.numpy as jnp
from jax import lax
from jax.experimental import pallas as pl
from jax.experimental.pallas import tpu as pltpu

F32 = jnp.float32
BF16 = jnp.bfloat16
HIGHEST = lax.Precision.HIGHEST

D_MODEL = 1024
DEPTH = 4
SSM_WIDTH = 256
SSM_GROUP = 16
SSM_GROUPS = 16
SSM_STATE = 64
SSM_LANES = SSM_GROUPS * SSM_STATE
MLSTM_WIDTH = 256
MLSTM_HEADS = 4
MLSTM_HEAD_DIM = 64
MLSTM_CHUNK = 128
MLSTM_CONV = 4
NSA_WIDTH = 512
NSA_HEAD_DIM = 64
NSA_HEADS = 8
NSA_KV_GROUPS = 2
NSA_REP = NSA_HEADS // NSA_KV_GROUPS
NSA_KV_WIDTH = 128
CMP_BLOCK = 32
CMP_STRIDE = 16
CMP_HIDDEN = 256
SEL_BLOCK = 64
SEL_TOPN = 8
WINDOW = 256
Q_BLOCK = 128
FORCE_SCORE = 1e4
NEG_INF = -1e30
ROPE_THETA = 500000.0
ROPE_DIMS = 16
ROPE_HALF = 8
D_FF = 4096
EPS = 1e-6
D_IN = 2592

LANE = 128
SUBLANE = 8
VMEM_LIMIT = 56 * 1024 * 1024

C_SU, C_MQ, C_MK, C_MV, C_MO = 0, 256, 512, 768, 1024
C_AQ, C_CK, C_SK, C_WK = 1280, 1792, 1920, 2048
C_CV, C_SV, C_WV = 2176, 2304, 2432
C_G0, C_G1 = 2560, 2688
D_INP = 2816
GATE_COL = 16
VAL_ROWS = NSA_HEAD_DIM + 16
Q_SCALE = NSA_HEAD_DIM ** -0.5 * math.log2(math.e)
MLSTM_ROWS = 8
NSA_TQ = 256
SEL_UNROLL = 2


def _dot(a, b, precision=None):
    return jnp.dot(a, b, preferred_element_type=F32, precision=precision)


def _dot_nt(a, b):
    return lax.dot_general(a, b, (((1,), (1,)), ((), ())), preferred_element_type=F32)


def _dot_tn(a, b):
    return lax.dot_general(a, b, (((0,), (0,)), ((), ())), preferred_element_type=F32)


def _sigmoid(x):
    return 1.0 / (1.0 + jnp.exp(-x))


def _dot_split(x, w_bf16):
    hi = x.astype(BF16)
    lo = (x - hi.astype(F32)).astype(BF16)
    return _dot(hi, w_bf16) + _dot(lo, w_bf16)


def _gelu_tanh(x):
    return 0.5 * x * (1.0 + jnp.tanh(math.sqrt(2.0 / math.pi) * (x + 0.044715 * (x * x * x))))


def _log_sigmoid(x):
    return jnp.minimum(x, 0.0) - jnp.log(1.0 + jnp.exp(-jnp.abs(x)))


def _inproj_kernel(x_ref, g_ref, w_ref, rt_ref,
                   su_ref, mq_ref, mk_ref, mv_ref, mo_ref, aq_ref, ckv_ref, sk_ref, wk_ref,
                   svt_ref, wvt_ref, gt_ref, gtt_ref):
    x = x_ref[...]
    ms = jnp.mean(x * x, axis=-1, keepdims=True)
    u = (x * lax.rsqrt(ms + EPS) * g_ref[...]).astype(BF16)
    rc, rs1, rs2 = (rt_ref[:, n * LANE:(n + 1) * LANE] for n in range(3))

    def mm(c0, width):
        return _dot(u, w_ref[:, c0:c0 + width])

    def rope(z):
        return z * rc + pltpu.roll(z, LANE - ROPE_HALF, 1) * rs1 + pltpu.roll(z, ROPE_HALF, 1) * rs2

    su_ref[...] = mm(C_SU, 256)
    mq_ref[...] = mm(C_MQ, 256)
    mk_ref[...] = mm(C_MK, 256)
    mv_ref[...] = mm(C_MV, 256).astype(BF16)
    mo_ref[...] = mm(C_MO, 256)
    def mm_pair(c0):
        z = mm(c0, 2 * LANE)
        return z[:, :LANE], z[:, LANE:]

    def put_heads(ref, first, z):
        ref[first] = z[:, :NSA_HEAD_DIM].astype(BF16)
        ref[first + 1] = z[:, NSA_HEAD_DIM:].astype(BF16)

    for j in range(NSA_HEADS // 4):
        for k, z in enumerate(mm_pair(C_AQ + 2 * LANE * j)):
            put_heads(aq_ref, 4 * j + 2 * k, rope(z) * Q_SCALE)
    z_ck, z_sk = mm_pair(C_CK)
    z_wk, z_cv = mm_pair(C_WK)
    z_sv, z_wv = mm_pair(C_SV)
    ckv_ref[0] = rope(z_ck)
    ckv_ref[1] = z_cv
    put_heads(sk_ref, 0, rope(z_sk))
    put_heads(wk_ref, 0, rope(z_wk))

    def put_chunks_t(ref, z):
        zt = jnp.transpose(z)
        width = ref.shape[-1]
        ones = jnp.ones((VAL_ROWS - NSA_HEAD_DIM, width), BF16)
        for g in range(NSA_KV_GROUPS):
            for j in range(ref.shape[1]):
                ref[g, j, :NSA_HEAD_DIM] = zt[g * NSA_HEAD_DIM:(g + 1) * NSA_HEAD_DIM,
                                              j * width:(j + 1) * width].astype(BF16)
                ref[g, j, NSA_HEAD_DIM:] = ones

    put_chunks_t(svt_ref, z_sv)
    put_chunks_t(wvt_ref, z_wv)
    z_g0, z_g1 = mm_pair(C_G0)
    gt_ref[...] = z_g0
    gtt_ref[0] = jnp.transpose(z_g0)
    gtt_ref[1] = jnp.transpose(z_g1)


def _inproj(h2, gain, w, layer, rope, B, S, ts, ck):
    nt = S // ts
    BS = B * S
    row = lambda b, i: (b * nt + i, 0)
    full = lambda b, i: (0, 0)
    headed = lambda b, i: (b, 0, i, 0)
    paired = lambda b, i: (0, b * nt + i, 0)
    in_specs = [
        pl.BlockSpec((ts, D_MODEL), row),
        pl.BlockSpec((1, D_MODEL), full),
        pl.BlockSpec((None, D_MODEL, D_INP), lambda b, i: (layer, 0, 0)),
        pl.BlockSpec((ts, 3 * LANE), row),
    ]
    kv_shape = jax.ShapeDtypeStruct((B, NSA_KV_GROUPS, S, NSA_HEAD_DIM), BF16)
    kv_spec = pl.BlockSpec((None, NSA_KV_GROUPS, ts, NSA_HEAD_DIM), headed)
    out_shape = [
        jax.ShapeDtypeStruct((BS, SSM_WIDTH), F32),
        jax.ShapeDtypeStruct((BS, MLSTM_WIDTH), F32),
        jax.ShapeDtypeStruct((BS, MLSTM_WIDTH), F32),
        jax.ShapeDtypeStruct((BS, MLSTM_WIDTH), BF16),
        jax.ShapeDtypeStruct((BS, MLSTM_WIDTH), F32),
        jax.ShapeDtypeStruct((B, NSA_HEADS, S, NSA_HEAD_DIM), BF16),
        jax.ShapeDtypeStruct((2, BS, NSA_KV_WIDTH), F32),
        kv_shape, kv_shape,
        jax.ShapeDtypeStruct((B, NSA_KV_GROUPS, S // ck, VAL_ROWS, ck), BF16),
        jax.ShapeDtypeStruct((B, NSA_KV_GROUPS, S // Q_BLOCK, VAL_ROWS, Q_BLOCK), BF16),
        jax.ShapeDtypeStruct((BS, LANE), F32),
        jax.ShapeDtypeStruct((NSA_KV_GROUPS, B, LANE, S), F32),
    ]
    out_specs = [
        pl.BlockSpec((ts, SSM_WIDTH), row),
        pl.BlockSpec((ts, MLSTM_WIDTH), row),
        pl.BlockSpec((ts, MLSTM_WIDTH), row),
        pl.BlockSpec((ts, MLSTM_WIDTH), row),
        pl.BlockSpec((ts, MLSTM_WIDTH), row),
        pl.BlockSpec((None, NSA_HEADS, ts, NSA_HEAD_DIM), headed),
        pl.BlockSpec((2, ts, NSA_KV_WIDTH), paired),
        kv_spec, kv_spec,
        pl.BlockSpec((None, NSA_KV_GROUPS, ts // ck, VAL_ROWS, ck), lambda b, i: (b, 0, i, 0, 0)),
        pl.BlockSpec((None, NSA_KV_GROUPS, ts // Q_BLOCK, VAL_ROWS, Q_BLOCK), lambda b, i: (b, 0, i, 0, 0)),
        pl.BlockSpec((ts, LANE), row),
        pl.BlockSpec((NSA_KV_GROUPS, None, LANE, ts), lambda b, i: (0, b, 0, i)),
    ]
    return pl.pallas_call(
        _inproj_kernel,
        grid=(B, nt),
        in_specs=in_specs,
        out_specs=out_specs,
        out_shape=out_shape,
        compiler_params=pltpu.CompilerParams(
            dimension_semantics=("parallel", "parallel"), vmem_limit_bytes=VMEM_LIMIT),
        name="inproj",
    )(h2, gain, w, rope)


def _s5_kernel(u_ref, bb_ref, a_ref, cc_ref, d_ref, wg_ref, gm_ref, gain_ref, o_ref, x_sc, st_sc, tm_sc, *, B, ts):
    @pl.when(pl.program_id(0) == 0)
    def _():
        st_sc[...] = jnp.zeros_like(st_sc)

    nl = SSM_WIDTH // LANE
    for b in range(B):
        for c in range(nl):
            tm_sc[c, pl.ds(b, ts, stride=B), :] = u_ref[b, :, c * LANE:(c + 1) * LANE]
    u = jnp.concatenate([tm_sc[c] for c in range(nl)], axis=1)
    ub = u.astype(BF16)
    for part in range(2):
        cols = slice(part * SSM_LANES, (part + 1) * SSM_LANES)
        x_sc[:, cols] = _dot(ub, bb_ref[:, cols])
    ar = jnp.broadcast_to(a_ref[0:1, :], (B, SSM_LANES))
    ai = jnp.broadcast_to(a_ref[1:2, :], (B, SSM_LANES))

    def step(t, carry):
        xr, xi = carry
        r = pl.multiple_of(t * B, B)
        br = x_sc[pl.ds(r, B), 0:SSM_LANES]
        bi = x_sc[pl.ds(r, B), SSM_LANES:2 * SSM_LANES]
        nr = ar * xr - ai * xi + br
        ni = ar * xi + ai * xr + bi
        x_sc[pl.ds(r, B), 0:SSM_LANES] = nr
        x_sc[pl.ds(r, B), SSM_LANES:2 * SSM_LANES] = ni
        return nr, ni

    xr, xi = lax.fori_loop(0, ts, step, (st_sc[0], st_sc[1]), unroll=True)
    st_sc[0] = xr
    st_sc[1] = xi

    half = (ts * B) // 2
    y = jnp.concatenate([_dot(x_sc[r * half:(r + 1) * half, :].astype(BF16), cc_ref[...]) for r in range(2)],
                        axis=0) + d_ref[...] * u
    y = _gelu_tanh(y)
    y = y * _sigmoid(_dot(y.astype(BF16), wg_ref[...]))
    ms = _dot_split(y * y, gm_ref[...])
    y = y * lax.rsqrt(ms + EPS) * gain_ref[...]
    for c in range(nl):
        tm_sc[c] = y[:, c * LANE:(c + 1) * LANE]
    for b in range(B):
        o_ref[b] = jnp.concatenate(
            [tm_sc[c, pl.ds(b, ts, stride=B), :] for c in range(nl)], axis=1).astype(BF16)


def _s5(u, bb, a, cc, d, wg, layer, gm, gain, B, S, ts):
    rows = ts * B
    full = lambda i: (0, 0)
    lsel = lambda i: (layer, 0, 0)
    return pl.pallas_call(
        functools.partial(_s5_kernel, B=B, ts=ts),
        grid=(S // ts,),
        in_specs=[
            pl.BlockSpec((B, ts, SSM_WIDTH), lambda i: (0, i, 0)),
            pl.BlockSpec((None, SSM_WIDTH, 2 * SSM_LANES), lsel),
            pl.BlockSpec((None, 2, SSM_LANES), lsel),
            pl.BlockSpec((None, 2 * SSM_LANES, SSM_WIDTH), lsel),
            pl.BlockSpec((1, SSM_WIDTH), full),
            pl.BlockSpec((None, SSM_WIDTH, SSM_WIDTH), lsel),
            pl.BlockSpec((SSM_WIDTH, SSM_WIDTH), full),
            pl.BlockSpec((1, SSM_WIDTH), full),
        ],
        out_specs=pl.BlockSpec((B, ts, SSM_WIDTH), lambda i: (0, i, 0)),
        out_shape=jax.ShapeDtypeStruct((B, S, SSM_WIDTH), BF16),
        scratch_shapes=[pltpu.VMEM((rows, 2 * SSM_LANES), F32), pltpu.VMEM((2, B, SSM_LANES), F32),
                        pltpu.VMEM((SSM_WIDTH // LANE, rows, LANE), F32)],
        compiler_params=pltpu.CompilerParams(
            dimension_semantics=("arbitrary",), vmem_limit_bytes=VMEM_LIMIT),
        name="s5",
    )(u, bb, a, cc, d, wg, gm, gain)


def _mlstm_kernel(q_ref, k_ref, v_ref, o_ref, gr_ref, cwq_ref, cwk_ref, br_ref, hm_ref,
                  gain_ref, y_ref, qt_sc, kt_sc, c_sc, m_sc, *, B):
    L, H, Dh, W = MLSTM_CHUNK, MLSTM_HEADS, MLSTM_HEAD_DIM, MLSTM_WIDTH

    @pl.when(pl.program_id(0) == 0)
    def _():
        qt_sc[...] = jnp.zeros_like(qt_sc)
        kt_sc[...] = jnp.zeros_like(kt_sc)
        c_sc[...] = jnp.zeros_like(c_sc)
        m_sc[...] = jnp.zeros_like(m_sc)

    visible = lax.broadcasted_iota(jnp.int32, (L, L), 0) <= lax.broadcasted_iota(jnp.int32, (L, L), 1)
    triu = visible.astype(F32)
    lane_w = lax.broadcasted_iota(jnp.int32, (1, W), 1) // Dh
    bd_mask = ((lax.broadcasted_iota(jnp.int32, (2 * W, W), 0) % W) // Dh
               == lax.broadcasted_iota(jnp.int32, (2 * W, W), 1) // Dh)
    row8 = lax.broadcasted_iota(jnp.int32, (SUBLANE, W), 0)
    cwq = cwq_ref[...]
    cwk = cwk_ref[...]
    ones_rows = jnp.ones((Dh, L), F32)

    def conv_silu(x, tail, w):
        acc = x * w[MLSTM_CONV - 1:MLSTM_CONV, :]
        for sft in range(1, MLSTM_CONV):
            xs = pltpu.roll(x, sft, 0)
            head = jnp.where(row8 < sft, pltpu.roll(tail, sft, 0), xs[:SUBLANE])
            xs = jnp.concatenate([head, xs[SUBLANE:]], axis=0)
            acc = acc + xs * w[MLSTM_CONV - 1 - sft:MLSTM_CONV - sft, :]
        return acc * _sigmoid(acc)

    def per_group(grp, _):
        bs = [grp * MLSTM_ROWS + n for n in range(MLSTM_ROWS)]
        st = [dict() for _ in bs]

        for b, d in zip(bs, st):
            q_raw = q_ref[b]
            k_raw = k_ref[b]
            d['q'] = conv_silu(q_raw, qt_sc[b], cwq)
            d['k'] = conv_silu(k_raw, kt_sc[b], cwk) * (Dh ** -0.5)
            qt_sc[b] = q_raw[L - SUBLANE:, :]
            kt_sc[b] = k_raw[L - SUBLANE:, :]
            d['gr'] = gr_ref[b] + br_ref[...]
        for d in st:
            d['brow'] = _dot(_log_sigmoid(d['gr']), triu, precision=HIGHEST)

        for b, d in zip(bs, st):
            gr, brow = d['gr'], d['brow']
            ccol = jnp.transpose(brow - pltpu.roll(gr, H, 0))
            m_all = m_sc[b]
            for key in ('w_intra', 'w_inter', 'e_mt', 'w_k', 'dec', 'm_new'):
                d[key] = []
            for hh in range(H):
                b_r = brow[H + hh:H + hh + 1, :]
                i_r = gr[hh:hh + 1, :]
                m_prev = m_all[hh:hh + 1, 0:1]
                dm = jnp.where(visible, b_r - ccol[:, H + hh:H + hh + 1], NEG_INF)
                inter = b_r + m_prev
                mt = jnp.maximum(inter, jnp.max(dm, axis=0, keepdims=True))
                d['w_intra'].append(jnp.exp(dm - mt))
                d['w_inter'].append(jnp.exp(inter - mt))
                d['e_mt'].append(jnp.exp(-mt))
                b_last = b_r[:, L - 1:L]
                logw = b_last - b_r + i_r
                mn = jnp.maximum(b_last + m_prev, jnp.max(logw, axis=1, keepdims=True))
                d['w_k'].append(jnp.exp(logw - mn))
                d['dec'].append(jnp.exp(b_last + m_prev - mn))
                d['m_new'].append(mn)
            d['qb'] = d['q'].astype(BF16)
            d['kb'] = d['k'].astype(BF16)
            d['vt'] = jnp.transpose(v_ref[b].astype(F32))
            d['c_t'] = c_sc[b]

        for d in st:
            d['qc'] = _dot(d['c_t'].astype(BF16), jnp.transpose(d['q']).astype(BF16))
            d['s_t'] = [_dot_nt(d['kb'], jnp.where(lane_w == hh, d['qb'], jnp.zeros_like(d['qb'])))
                        for hh in range(H)]
        for d in st:
            d['r'] = []
            for hh in range(H):
                v_aug = jnp.concatenate([d['vt'][hh * Dh:(hh + 1) * Dh], ones_rows], axis=0).astype(BF16)
                d['r'].append(_dot(v_aug, (d['s_t'][hh] * d['w_intra'][hh]).astype(BF16)))

        for b, d in zip(bs, st):
            h_t = []
            for hh in range(H):
                ch = slice(hh * Dh, (hh + 1) * Dh)
                num = d['w_inter'][hh] * d['qc'][ch] + d['r'][hh][:Dh]
                den = d['w_inter'][hh] * d['qc'][W + hh * Dh:W + (hh + 1) * Dh] + d['r'][hh][Dh:]
                h_t.append(num / jnp.maximum(jnp.abs(den), d['e_mt'][hh]))
            hout = jnp.transpose(jnp.concatenate(h_t, axis=0))
            d['y'] = _sigmoid(o_ref[b]) * hout
            d['vw'] = jnp.concatenate(
                [d['vt'][hh * Dh:(hh + 1) * Dh] * d['w_k'][hh] for hh in range(H)]
                + [jnp.broadcast_to(d['w_k'][hh], (Dh, L)) for hh in range(H)], axis=0).astype(BF16)
        for d in st:
            d['ms'] = _dot_split(d['y'] * d['y'], hm_ref[...])
            d['upd'] = _dot(d['vw'], d['kb'])

        for b, d in zip(bs, st):
            y_ref[b] = (d['y'] * lax.rsqrt(d['ms'] + EPS) * gain_ref[...]).astype(BF16)
            decay = d['dec'][H - 1]
            for hh in range(H - 2, -1, -1):
                decay = jnp.where(lane_w == hh, d['dec'][hh], decay)
            c_sc[b] = decay * d['c_t'] + jnp.where(bd_mask, d['upd'], 0.0)
            for hh in range(H):
                m_sc[b, hh:hh + 1, :] = jnp.broadcast_to(d['m_new'][hh], (1, LANE))
        return 0

    lax.fori_loop(0, B // MLSTM_ROWS, per_group, 0)


def _mlstm(mq, mk, mv, mo, grow, cwq, cwk, brow, hm, gain, B, S):
    L, W = MLSTM_CHUNK, MLSTM_WIDTH
    seq = lambda c: (0, c, 0)
    full = lambda c: (0, 0)
    return pl.pallas_call(
        functools.partial(_mlstm_kernel, B=B),
        grid=(S // L,),
        in_specs=[
            pl.BlockSpec((B, L, W), seq),
            pl.BlockSpec((B, L, W), seq),
            pl.BlockSpec((B, L, W), seq),
            pl.BlockSpec((B, L, W), seq),
            pl.BlockSpec((None, B, SUBLANE, L), lambda c: (0, 0, 0, c)),
            pl.BlockSpec((MLSTM_CONV, W), full),
            pl.BlockSpec((MLSTM_CONV, W), full),
            pl.BlockSpec((SUBLANE, 1), full),
            pl.BlockSpec((W, W), full),
            pl.BlockSpec((1, W), full),
        ],
        out_specs=pl.BlockSpec((B, L, W), seq),
        out_shape=jax.ShapeDtypeStruct((B, S, W), BF16),
        scratch_shapes=[
            pltpu.VMEM((B, SUBLANE, W), F32),
            pltpu.VMEM((B, SUBLANE, W), F32),
            pltpu.VMEM((B, 2 * W, W), F32),
            pltpu.VMEM((B, SUBLANE, LANE), F32),
        ],
        compiler_params=pltpu.CompilerParams(
            dimension_semantics=("arbitrary",), vmem_limit_bytes=VMEM_LIMIT),
        name="mlstm",
    )(mq, mk, mv, mo, grow, cwq, cwk, brow, hm, gain)


def _compress_kernel(c_ref, w1ab_ref, w1_ref, pe_ref, w2_ref, w2t_ref, o_ref, ot_ref, ch_sc):
    G, Dh = NSA_KV_GROUPS, NSA_HEAD_DIM
    rows = ch_sc.shape[1]
    n = rows // G
    both = range(2)
    for i in both:
        for r in range(CMP_STRIDE):
            tok = c_ref[i, pl.ds(r, n, stride=CMP_STRIDE), :]
            for g in range(G):
                ch_sc[i, g * n:(g + 1) * n, r * Dh:(r + 1) * Dh] = tok[:, g * Dh:(g + 1) * Dh]
    ab = [_dot(ch_sc[i].astype(BF16), w1ab_ref[i]) for i in both]
    const = [_dot(pe_ref[i], w1_ref[i], precision=HIGHEST) for i in both]
    act = []
    for i in both:
        hid = ab[i][:, :CMP_HIDDEN] + pltpu.roll(ab[i][:, CMP_HIDDEN:], rows - 1, 0) + const[i]
        act.append(_gelu_tanh(hid).astype(BF16))
    for i in both:
        o_ref[i] = _dot(act[i], w2_ref[i]).astype(BF16)
        ot_ref[i] = _dot_nt(w2t_ref[i], act[i]).astype(BF16)


def _compress(ckv, w1ab, w1, pe, w2, w2t, layer, B, S):
    G, Dh = NSA_KV_GROUPS, NSA_HEAD_DIM
    n = S // CMP_STRIDE
    width = CMP_STRIDE * Dh
    wsel = lambda b: (layer, 0, 0, 0)
    return pl.pallas_call(
        _compress_kernel,
        grid=(B,),
        in_specs=[
            pl.BlockSpec((2, S, G * Dh), lambda b: (0, b, 0)),
            pl.BlockSpec((None, 2, width, 2 * CMP_HIDDEN), wsel),
            pl.BlockSpec((None, 2, 2 * width, CMP_HIDDEN), wsel),
            pl.BlockSpec((None, 2, 1, 2 * width), wsel),
            pl.BlockSpec((None, 2, CMP_HIDDEN, Dh), wsel),
            pl.BlockSpec((None, 2, Dh, CMP_HIDDEN), wsel),
        ],
        out_specs=[pl.BlockSpec((2, None, G * n, Dh), lambda b: (0, b, 0, 0)),
                   pl.BlockSpec((2, None, Dh, G * n), lambda b: (0, b, 0, 0))],
        out_shape=[jax.ShapeDtypeStruct((2, B, G * n, Dh), BF16),
                   jax.ShapeDtypeStruct((2, B, Dh, G * n), BF16)],
        scratch_shapes=[pltpu.VMEM((2, G * n, width), F32)],
        compiler_params=pltpu.CompilerParams(
            dimension_semantics=("parallel",), vmem_limit_bytes=VMEM_LIMIT),
        name="compress",
    )(ckv, w1ab, w1, pe, w2, w2t)


def _nsa_kernel(q_ref, kc_ref, vct_ref, ks_ref, vst_ref, kw_ref, vwt_ref, gtt_ref, gain_ref,
                ovt_ref, et_ref, o_ref, *, n_sel, n_top, ck, unroll):
    TQ, R, Dh, G = NSA_TQ, NSA_REP, NSA_HEAD_DIM, NSA_KV_GROUPS
    groups = range(G)
    i = pl.program_id(1)
    t0 = i * TQ

    def _nsa_tile(trips):
        qs = [q_ref[g * R:(g + 1) * R].reshape(R * TQ, Dh) for g in groups]
        tq1 = t0 + lax.broadcasted_iota(jnp.int32, (1, TQ), 1)
        heads = lambda t: jnp.concatenate([t] * R, axis=1)

        ncmp = kc_ref.shape[0] // G
        wkeys = WINDOW + TQ
        nwb = wkeys // Q_BLOCK
        ws = pl.multiple_of(jnp.maximum(t0 - WINDOW, 0), Q_BLOCK)
        wb0 = ws // Q_BLOCK
        sc = [_dot_nt(kc_ref[g * ncmp:(g + 1) * ncmp, :], qs[g]) for g in groups]
        sw = [_dot_nt(kw_ref[g, pl.ds(ws, wkeys), :], qs[g]) for g in groups]

        kpos = ws + lax.broadcasted_iota(jnp.int32, (wkeys, 1), 0)
        wbias = heads(jnp.where((kpos <= tq1) & (tq1 - kpos < WINDOW), 0.0, NEG_INF))
        ow, l_w = [], []
        for g in groups:
            swb = sw[g] + wbias
            pw = jnp.exp2(swb - jnp.max(swb, axis=0, keepdims=True))
            vwt = jnp.concatenate([vwt_ref[g, wb0 + j] for j in range(nwb)], axis=1)
            owl = _dot(vwt, pw.astype(BF16))
            ow.append(owl[:Dh])
            l_w.append(owl[Dh:Dh + 1])

        cend = lax.broadcasted_iota(jnp.int32, (ncmp, 1), 0) * CMP_STRIDE + (CMP_BLOCK - 1)
        cmask = heads(cend <= tq1)
        pc = []
        for g in groups:
            scm = jnp.where(cmask, sc[g], NEG_INF)
            ec = jnp.where(cmask, jnp.exp2(scm - jnp.max(scm, axis=0, keepdims=True)), 0.0)
            pc.append(ec * (1.0 / jnp.maximum(jnp.sum(ec, axis=0, keepdims=True), 1e-30)))
        oc = [_dot(vct_ref[:, g * ncmp:(g + 1) * ncmp], pc[g].astype(BF16)) for g in groups]

        imp = []
        for g in groups:
            psum = pc[g][:, 0:TQ]
            for r in range(1, R):
                psum = psum + pc[g][:, r * TQ:(r + 1) * TQ]
            imp.append(_dot(ovt_ref[...], psum, precision=HIGHEST))

        blk = lax.broadcasted_iota(jnp.int32, (n_sel, 1), 0)
        valid = blk * SEL_BLOCK <= tq1
        forced = (blk == 0) | (blk == tq1 // SEL_BLOCK)
        selb = []
        for g in groups:
            val = jnp.where(forced, FORCE_SCORE, jnp.where(valid, imp[g], -FORCE_SCORE))
            rank = jnp.zeros((n_sel, TQ), F32)
            for jp in range(n_sel):
                other = val[jp:jp + 1, :]
                wins = jnp.where(blk > jp, jnp.where(other >= val, 1.0, 0.0), jnp.where(other > val, 1.0, 0.0))
                rank = rank + wins
            selb.append(jnp.where(rank < n_top, 0.0, NEG_INF).astype(BF16))

        def scores(g, c, causal):
            k0 = c * ck
            bias = _dot(et_ref[c], selb[g])
            if causal:
                kpos = k0 + lax.broadcasted_iota(jnp.int32, (ck, 1), 0)
                bias = jnp.where(kpos <= tq1, bias, NEG_INF)
            return _dot_nt(ks_ref[g, pl.ds(k0, ck), :], qs[g]) + heads(bias)

        def update(g, c, s, carry):
            m, acc = carry
            mn = jnp.maximum(m, jnp.max(s, axis=0, keepdims=True))
            p = jnp.exp2(s - mn)
            acc = jnp.exp2(m - mn) * acc + _dot(vst_ref[g, c], p.astype(BF16))
            return mn, acc

        def chunk_group(cg, carry, causal):
            cs = [cg * unroll + sub for sub in range(unroll)]
            ss = [[scores(g, c, causal) for g in groups] for c in cs]
            carry = list(carry)
            for c, s in zip(cs, ss):
                for g in groups:
                    carry[g] = update(g, c, s[g], carry[g])
            return tuple(carry)

        sel = tuple((jnp.full((1, R * TQ), NEG_INF, F32), jnp.zeros((VAL_ROWS, R * TQ), F32)) for _ in groups)
        for cg in range(trips):
            sel = chunk_group(cg, sel, causal=cg == trips - 1)

        normed = []
        for g in groups:
            gs = _sigmoid(gtt_ref[g])
            acc_s, l_s = sel[g][1][:Dh], sel[g][1][Dh:Dh + 1]
            for r in range(R):
                ln = slice(r * TQ, (r + 1) * TQ)
                o = (gs[3 * r:3 * r + 1, :] * oc[g][:, ln]
                     + (gs[3 * r + 1:3 * r + 2, :] / l_s[:, ln]) * acc_s[:, ln]
                     + (gs[3 * r + 2:3 * r + 3, :] / l_w[g][:, ln]) * ow[g][:, ln])
                ms = jnp.mean(o * o, axis=0, keepdims=True)
                normed.append(o * lax.rsqrt(ms + EPS) * gain_ref[g * R + r])
        for pair in range(G * R // 2):
            both = jnp.concatenate(normed[2 * pair:2 * pair + 2], axis=0)
            o_ref[:, pair * 2 * Dh:(pair + 1) * 2 * Dh] = jnp.transpose(both).astype(BF16)
        return 0

    n_chunks = (t0 + TQ + ck - 1) // ck
    max_trips = ks_ref.shape[1] // (ck * unroll)
    lax.switch((n_chunks + unroll - 1) // unroll - 1,
               [functools.partial(_nsa_tile, t) for t in range(1, max_trips + 1)])


def _nsa(aq, cmp_k, cmp_vt, ks, vst, kw, vwt, gates_t, gain, consts, B, S):
    G, H, TQ, Dh = NSA_KV_GROUPS, NSA_HEADS, NSA_TQ, NSA_HEAD_DIM
    nq = S // TQ
    ncmp = S // CMP_STRIDE
    n_sel = S // SEL_BLOCK
    ovt, emat_t, ck = consts
    k_spec = pl.BlockSpec((None, G, S, Dh), lambda b, i: (b, 0, 0, 0))
    return pl.pallas_call(
        functools.partial(_nsa_kernel, n_sel=n_sel, n_top=min(SEL_TOPN, n_sel), ck=ck, unroll=SEL_UNROLL),
        grid=(B, nq),
        in_specs=[
            pl.BlockSpec((None, H, TQ, Dh), lambda b, i: (b, 0, i, 0)),
            pl.BlockSpec((None, None, G * ncmp, Dh), lambda b, i: (0, b, 0, 0)),
            pl.BlockSpec((None, None, Dh, G * ncmp), lambda b, i: (1, b, 0, 0)),
            k_spec,
            pl.BlockSpec((None, G, S // ck, VAL_ROWS, ck), lambda b, i: (b, 0, 0, 0, 0)),
            k_spec,
            pl.BlockSpec((None, G, S // Q_BLOCK, VAL_ROWS, Q_BLOCK), lambda b, i: (b, 0, 0, 0, 0)),
            pl.BlockSpec((G, None, 2 * SUBLANE, TQ), lambda b, i: (0, b, GATE_COL // (2 * SUBLANE), i)),
            pl.BlockSpec((H, Dh, 1), lambda b, i: (0, 0, 0)),
            pl.BlockSpec(ovt.shape, lambda b, i: (0, 0)),
            pl.BlockSpec(emat_t.shape, lambda b, i: (0, 0, 0)),
        ],
        out_specs=pl.BlockSpec((TQ, H * Dh), lambda b, i: (b * nq + i, 0)),
        out_shape=jax.ShapeDtypeStruct((B * S, H * Dh), BF16),
        compiler_params=pltpu.CompilerParams(
            dimension_semantics=("parallel", "arbitrary"), vmem_limit_bytes=VMEM_LIMIT),
        name="nsa",
    )(aq, cmp_k, cmp_vt, ks, vst, kw, vwt, gates_t, gain, ovt, emat_t)


def _nsa_consts(S):
    n_cmp = S // CMP_STRIDE
    n_sel = S // SEL_BLOCK
    ck = 256
    i = np.arange(n_cmp)[:, None]
    j = np.arange(n_sel)[None, :]
    lo = np.maximum(i * CMP_STRIDE, j * SEL_BLOCK)
    hi = np.minimum(i * CMP_STRIDE + CMP_BLOCK, (j + 1) * SEL_BLOCK)
    ov = np.maximum(hi - lo, 0) / CMP_STRIDE
    ov[n_cmp - 1] = 0.0
    key = np.arange(S)
    emat_t = (key[:, None] // SEL_BLOCK == np.arange(n_sel)[None, :]).astype(np.float32)
    return (jnp.asarray(ov.T, F32), jnp.asarray(emat_t.reshape(S // ck, ck, n_sel), BF16), ck)


def _outproj_kernel(h_ref, ys_ref, ym_ref, yn_ref, w_ref, g_ref, o_ref):
    acc = _dot(ys_ref[...], w_ref[0:SSM_WIDTH, :])
    acc = acc + _dot(ym_ref[...], w_ref[SSM_WIDTH:SSM_WIDTH + MLSTM_WIDTH, :])
    acc = acc + _dot(yn_ref[...], w_ref[SSM_WIDTH + MLSTM_WIDTH:, :])
    ms = jnp.mean(acc * acc, axis=-1, keepdims=True)
    o_ref[...] = h_ref[...] + acc * lax.rsqrt(ms + EPS) * g_ref[...]


def _outproj(h2, y_ssm, y_mls, y_nsa, w, layer, gain, B, S, ts):
    nt = S // ts
    row = lambda b, i: (b * nt + i, 0)
    full = lambda b, i: (0, 0)
    return pl.pallas_call(
        _outproj_kernel,
        grid=(B, nt),
        in_specs=[
            pl.BlockSpec((ts, D_MODEL), row),
            pl.BlockSpec((ts, SSM_WIDTH), row),
            pl.BlockSpec((ts, MLSTM_WIDTH), row),
            pl.BlockSpec((ts, NSA_WIDTH), row),
            pl.BlockSpec((None, D_MODEL, D_MODEL), lambda b, i: (layer, 0, 0)),
            pl.BlockSpec((1, D_MODEL), full),
        ],
        out_specs=pl.BlockSpec((ts, D_MODEL), row),
        out_shape=jax.ShapeDtypeStruct((B * S, D_MODEL), F32),
        compiler_params=pltpu.CompilerParams(
            dimension_semantics=("parallel", "parallel"), vmem_limit_bytes=VMEM_LIMIT),
        name="outproj",
    )(h2, y_ssm, y_mls, y_nsa, w, gain)


def _mlp_kernel(h_ref, g1_ref, w1_ref, w2_ref, g2_ref, o_ref, u_sc, acc_sc):
    kf = pl.program_id(1)
    last = pl.num_programs(1) - 1

    def partial_ff(u):
        a = jnp.maximum(_dot(u, w1_ref[...].astype(BF16)), 0.0)
        return _dot((a * a).astype(BF16), w2_ref[...].astype(BF16))

    @pl.when(kf == 0)
    def _():
        x = h_ref[...]
        ms = jnp.mean(x * x, axis=-1, keepdims=True)
        u = (x * lax.rsqrt(ms + EPS) * g1_ref[...]).astype(BF16)
        u_sc[...] = u
        acc_sc[...] = partial_ff(u)

    @pl.when((kf > 0) & (kf < last))
    def _():
        acc_sc[...] += partial_ff(u_sc[...])

    @pl.when(kf == last)
    def _():
        f = acc_sc[...] + partial_ff(u_sc[...])
        ms = jnp.mean(f * f, axis=-1, keepdims=True)
        o_ref[...] = h_ref[...] + f * lax.rsqrt(ms + EPS) * g2_ref[...]


def _mlp(h2, g1, w1, w2, layer, g2, tm, tf):
    rows = h2.shape[0]
    return pl.pallas_call(
        _mlp_kernel,
        grid=(rows // tm, D_FF // tf),
        in_specs=[
            pl.BlockSpec((tm, D_MODEL), lambda i, k: (i, 0)),
            pl.BlockSpec((1, D_MODEL), lambda i, k: (0, 0)),
            pl.BlockSpec((None, D_MODEL, tf), lambda i, k: (layer, 0, k)),
            pl.BlockSpec((None, tf, D_MODEL), lambda i, k: (layer, k, 0)),
            pl.BlockSpec((1, D_MODEL), lambda i, k: (0, 0)),
        ],
        out_specs=pl.BlockSpec((tm, D_MODEL), lambda i, k: (i, 0)),
        out_shape=jax.ShapeDtypeStruct((rows, D_MODEL), F32),
        scratch_shapes=[pltpu.VMEM((tm, D_MODEL), BF16), pltpu.VMEM((tm, D_MODEL), F32)],
        compiler_params=pltpu.CompilerParams(
            dimension_semantics=("parallel", "arbitrary"), vmem_limit_bytes=VMEM_LIMIT),
        name="mlp",
    )(h2, g1, w1, w2, g2)


def _inproj_pieces():
    return ((0, 1280), (1288, 1800), (1800, 1928), (2056, 2184), (2312, 2440),
            (1928, 2056), (2184, 2312), (2440, 2568),
            (1280, 1288), (None, GATE_COL - 8), (2568, 2580), (None, LANE - GATE_COL - 12),
            (None, GATE_COL), (2580, 2592), (None, LANE - GATE_COL - 12))


def _permute_w_in_kernel(w_ref, o_ref):
    x = w_ref[...]
    col = 0
    for a, b in _inproj_pieces():
        width = b if a is None else b - a
        piece = jnp.zeros((x.shape[0], width), BF16) if a is None else x[:, a:b].astype(BF16)
        o_ref[:, col:col + width] = piece
        col += width
    assert col == D_INP


def _permute_w_in(w_in):
    depth, rows, cols = w_in.shape
    tr = 256
    return pl.pallas_call(
        _permute_w_in_kernel,
        grid=(depth, rows // tr),
        in_specs=[pl.BlockSpec((None, tr, cols), lambda l, i: (l, i, 0))],
        out_specs=pl.BlockSpec((None, tr, D_INP), lambda l, i: (l, i, 0)),
        out_shape=jax.ShapeDtypeStruct((depth, rows, D_INP), BF16),
        compiler_params=pltpu.CompilerParams(
            dimension_semantics=("parallel", "parallel"), vmem_limit_bytes=VMEM_LIMIT),
        name="permute_w_in",
    )(w_in)


def _rope_tables(positions):
    inv = ROPE_THETA ** (-jnp.arange(0, ROPE_DIMS, 2, dtype=F32) / ROPE_DIMS)
    ang = positions.astype(F32)[..., None] * inv
    ones = jnp.ones(ang.shape[:-1] + (1,), F32)
    feats = jnp.concatenate([jnp.cos(ang), jnp.sin(ang), ones], axis=-1).reshape(-1, 2 * ROPE_HALF + 1)
    place = np.zeros((2 * ROPE_HALF + 1, 3 * LANE), np.float32)
    for lane in range(LANE):
        d = lane % NSA_HEAD_DIM
        if d < ROPE_HALF:
            place[d, lane] = 1.0
            place[ROPE_HALF + d, LANE + lane] = -1.0
        elif d < ROPE_DIMS:
            place[d - ROPE_HALF, lane] = 1.0
            place[d, 2 * LANE + lane] = 1.0
        else:
            place[2 * ROPE_HALF, lane] = 1.0
    return jnp.dot(feats, jnp.asarray(place), precision=HIGHEST)


def _s5_params(lam_re, lam_im, b_re, b_im, c_re, c_im, log_dt):
    G, P, Hc = SSM_GROUPS, SSM_STATE, SSM_GROUP
    dt = jnp.exp(log_dt)[:, None]
    mag = jnp.exp(lam_re * dt)
    ang = lam_im * dt
    ab_re = mag * jnp.cos(ang)
    ab_im = mag * jnp.sin(ang)
    den = lam_re * lam_re + lam_im * lam_im
    g_re = ((ab_re - 1.0) * lam_re + ab_im * lam_im) / den
    g_im = (ab_im * lam_re - (ab_re - 1.0) * lam_im) / den
    bb_re = g_re[..., None] * b_re - g_im[..., None] * b_im
    bb_im = g_re[..., None] * b_im + g_im[..., None] * b_re
    same_in = jnp.asarray(np.arange(G * Hc)[:, None] // Hc == np.arange(G * P)[None, :] // P)
    blockdiag_in = lambda t: jnp.where(
        same_in, jnp.tile(jnp.swapaxes(t, 1, 2).reshape(G * Hc, P), (1, G)), 0.0)
    blockdiag_out = lambda t: jnp.where(
        same_in.T, jnp.tile(jnp.swapaxes(t, 1, 2).reshape(G * P, Hc), (1, G)), 0.0)
    bb = jnp.concatenate([blockdiag_in(bb_re), blockdiag_in(bb_im)], axis=1).astype(BF16)
    cc = jnp.concatenate([blockdiag_out(c_re), -blockdiag_out(c_im)], axis=0).astype(BF16)
    a = jnp.stack([ab_re.reshape(-1), ab_im.reshape(-1)], axis=0)
    return bb, a, cc


def _group_mean_matrix(width, group):
    idx = np.arange(width) // group
    return jnp.asarray((idx[:, None] == idx[None, :]).astype(np.float32) / group, BF16)


def kernel(x, positions, ln_mix_pre, ln_mix_post, ln_mlp_pre, ln_mlp_post, w_in, w_out, ssm_lambda_re, ssm_lambda_im, ssm_b_re, ssm_b_im, ssm_c_re, ssm_c_im, ssm_d, ssm_log_dt, ssm_w_glu, mlstm_conv, mlstm_b_i, mlstm_b_f, cmp_pe_k, cmp_w1_k, cmp_w2_k, cmp_pe_v, cmp_w1_v, cmp_w2_v, gn_ssm, gn_mlstm, gn_nsa, mlp_w1, mlp_w2):
    B, S, D = x.shape
    depth = w_in.shape[0]
    assert D == D_MODEL and B == SUBLANE and S % 512 == 0 and S >= WINDOW + NSA_TQ
    G, H = NSA_KV_GROUPS, MLSTM_HEADS
    ts_proj = 1024
    ts_scan = 128

    rope = _rope_tables(positions)
    w_in_p = _permute_w_in(w_in)
    w_out_b = w_out.astype(BF16)
    wglu_b = ssm_w_glu.astype(BF16)
    gm_ssm = _group_mean_matrix(SSM_WIDTH, SSM_GROUP)
    hm_mls = _group_mean_matrix(MLSTM_WIDTH, MLSTM_HEAD_DIM)
    consts = _nsa_consts(S)
    half = CMP_STRIDE * NSA_HEAD_DIM
    w1ab = jnp.stack([jnp.concatenate([cmp_w1_k[:, :half], cmp_w1_k[:, half:]], axis=-1),
                      jnp.concatenate([cmp_w1_v[:, :half], cmp_w1_v[:, half:]], axis=-1)], axis=1).astype(BF16)
    w1f = jnp.stack([cmp_w1_k, cmp_w1_v], axis=1)
    pef = jnp.stack([cmp_pe_k.reshape(depth, 1, -1), cmp_pe_v.reshape(depth, 1, -1)], axis=1)
    w2c = jnp.stack([cmp_w2_k, cmp_w2_v], axis=1).astype(BF16)
    w2ct = jnp.swapaxes(w2c, -1, -2)
    bias_row = jnp.concatenate([mlstm_b_i, mlstm_b_f], axis=-1)[:, :, None]

    bb, a, cc = jax.vmap(_s5_params)(ssm_lambda_re, ssm_lambda_im, ssm_b_re, ssm_b_im, ssm_c_re, ssm_c_im,
                                     ssm_log_dt)
    sh3 = lambda t: t.reshape(B, S, t.shape[-1])

    h = x.reshape(B * S, D)
    for l in range(depth):
        (su, mq, mk, mv, mo, aq, ckv, sk, wk, svt, wvt, gates, gates_t) = _inproj(
            h, ln_mix_pre[l][None], w_in_p, l, rope, B, S, ts_proj, consts[2])

        y_ssm = _s5(sh3(su), bb, a, cc, ssm_d[l][None], wglu_b, l, gm_ssm, gn_ssm[l][None], B, S, ts_scan)

        y_mls = _mlstm(sh3(mq), sh3(mk), sh3(mv), sh3(mo), gates_t,
                       mlstm_conv[l][:, :MLSTM_WIDTH], mlstm_conv[l][:, MLSTM_WIDTH:],
                       bias_row[l], hm_mls, gn_mlstm[l][None], B, S)

        cmp_k, cmp_t = _compress(ckv, w1ab, w1f, pef, w2c, w2ct, l, B, S)
        y_nsa = _nsa(aq, cmp_k, cmp_t, sk, svt, wk, wvt, gates_t,
                     gn_nsa[l].reshape(NSA_HEADS, NSA_HEAD_DIM, 1), consts, B, S)

        h = _outproj(h, y_ssm.reshape(B * S, SSM_WIDTH), y_mls.reshape(B * S, MLSTM_WIDTH), y_nsa,
                     w_out_b, l, ln_mix_post[l][None], B, S, ts_proj)
        h = _mlp(h, ln_mlp_pre[l][None], mlp_w1, mlp_w2, l, ln_mlp_post[l][None], 1024, 1024)
    return h.reshape(B, S, D)
```
